```python
import jax, jax.numpy as jnp
from jax import lax
import numpy as np

D_MODEL = 1024
BATCH = 16
SEQ = 256
DEPTH = 1
DEC_BATCH = 2
DEC_SEQ = 1024
PAST_LEN = 512

GRID_W = 64
D_MIX = D_MODEL
D_RWKV = D_MIX // 2
D_CONV = D_MIX - D_RWKV
HEAD_DIM = 64
N_RWKV_HEADS = D_RWKV // HEAD_DIM
CONV_WIDTH = 31
LORA_W = 64
LORA_A = 64
LORA_G = 128
N_GROUPS = 4
N_EXP_PER_GROUP = 8
N_EXPERTS = N_GROUPS * N_EXP_PER_GROUP
TOP_K = 2
D_EXPERT = 256
N_MOD = 6
RMS_EPS = 1e-6
LN_EPS = 1e-5
GN_EPS = 64e-5
SHIFT_COLS = 3 * D_RWKV + LORA_W + LORA_A
IN_COLS = SHIFT_COLS + LORA_G + 2 * D_CONV

kernel_name = 'hymba_rwkv7_conformer_hmoe_diffusion_step'


def rms_norm(x, g):
    xf = x.astype(jnp.float32)
    y = xf * lax.rsqrt(jnp.mean(xf * xf, axis=-1, keepdims=True) + RMS_EPS)
    return (y * g.astype(jnp.float32)).astype(x.dtype)


def layer_norm(x, g, b, eps):
    xf = x.astype(jnp.float32)
    mu = jnp.mean(xf, axis=-1, keepdims=True)
    var = jnp.mean(jnp.square(xf - mu), axis=-1, keepdims=True)
    y = (xf - mu) * lax.rsqrt(var + eps)
    return (y * g.astype(jnp.float32) + b.astype(jnp.float32)).astype(x.dtype)


def grid_pos_embed(rows, dtype):
    t = jnp.arange(rows * GRID_W)
    row = (t // GRID_W).astype(jnp.float32)
    col = (t % GRID_W).astype(jnp.float32)
    quarter = D_MODEL // 4
    freqs = 1.0 / (10000.0 ** (jnp.arange(quarter, dtype=jnp.float32) / quarter))
    ang_r = row[:, None] * freqs[None, :]
    ang_c = col[:, None] * freqs[None, :]
    pe = jnp.concatenate([jnp.sin(ang_r), jnp.cos(ang_r), jnp.sin(ang_c), jnp.cos(ang_c)], axis=-1)
    return pe.astype(dtype)


def adaln_mod(cond, w, b):
    m = jax.nn.silu(cond) @ w + b
    return jnp.split(m[:, None, :], N_MOD, axis=-1)


def shift_prev(z):
    return jnp.pad(z, ((0, 0), (1, 0), (0, 0)))[:, :-1]


def rwkv7_scan_dir(proj, s0, mu, w0, w_b, a0, a_b, k_k, k_a, r_k):
    bsz, t_len, _ = proj.shape
    xs = proj + (shift_prev(proj) - proj) * mu
    r, k, v, zw, za = jnp.split(xs, [D_RWKV, 2 * D_RWKV, 3 * D_RWKV, 3 * D_RWKV + LORA_W], axis=-1)
    w_log = -jax.nn.softplus(-(w0 + jnp.tanh(zw) @ w_b)) - 0.5
    decay = jnp.exp(-jnp.exp(w_log.astype(jnp.float32)))
    a = jax.nn.sigmoid(a0 + za @ a_b)
    kk = (k * k_k).astype(jnp.float32).reshape(bsz, t_len, N_RWKV_HEADS, HEAD_DIM)
    kk = kk / jnp.maximum(jnp.sqrt(jnp.sum(kk * kk, axis=-1, keepdims=True)), 1e-12)
    k = k * (1 + (a - 1) * k_a)
    heads = lambda t: t.astype(jnp.float32).reshape(bsz, t_len, N_RWKV_HEADS, HEAD_DIM)
    r, k, v, a, decay = heads(r), heads(k), heads(v), heads(a), heads(decay)
    b_vec = kk * a

    def step(state, inp):
        r_t, k_t, v_t, w_t, kk_t, b_t = inp
        sa = jnp.einsum('bhvk,bhk->bhv', state, -kk_t)
        state = (state * w_t[:, :, None, :] + sa[..., None] * b_t[:, :, None, :]
                 + v_t[..., None] * k_t[:, :, None, :])
        return state, jnp.einsum('bhvk,bhk->bhv', state, r_t)

    seq = tuple(jnp.moveaxis(t, 1, 0) for t in (r, k, v, decay, kk, b_vec))
    s_final, ys = lax.scan(step, s0.astype(jnp.float32), seq)
    y = jnp.moveaxis(ys, 0, 1)
    bonus = jnp.sum(r * k * r_k.astype(jnp.float32), axis=-1, keepdims=True) * v
    return y + bonus, s_final


def rwkv7_bidir(z_shift, z_g, s0, p):
    bsz, t_len = z_shift.shape[:2]
    y_f, s_f = rwkv7_scan_dir(z_shift, s0[:, 0], p['tshift_mu'][0], p['decay_w0'][0], p['decay_lora_b'][0],
                              p['iclr_a0'][0], p['iclr_lora_b'][0], p['key_k'][0], p['key_a'][0], p['bonus_r_k'][0])
    y_b, s_b = rwkv7_scan_dir(z_shift[:, ::-1], s0[:, 1], p['tshift_mu'][1], p['decay_w0'][1], p['decay_lora_b'][1],
                              p['iclr_a0'][1], p['iclr_lora_b'][1], p['key_k'][1], p['key_a'][1], p['bonus_r_k'][1])
    y = y_f + y_b[:, ::-1]
    y = layer_norm(y, p['gn_g'].reshape(N_RWKV_HEADS, HEAD_DIM), p['gn_b'].reshape(N_RWKV_HEADS, HEAD_DIM), GN_EPS)
    y = y.reshape(bsz, t_len, D_RWKV).astype(z_shift.dtype)
    gate = jax.nn.sigmoid(z_g) @ p['gate_lora_b']
    return y * gate, jnp.stack([s_f, s_b], axis=1)


def conformer_conv(z_conv, p):
    u = z_conv[..., :D_CONV] * jax.nn.sigmoid(z_conv[..., D_CONV:])
    pad = CONV_WIDTH // 2
    h = lax.conv_general_dilated(u, p['conv_dw_w'][:, None, :], window_strides=(1,), padding=((pad, pad),),
                                 dimension_numbers=('NWC', 'WIO', 'NWC'), feature_group_count=D_CONV)
    h = h + p['conv_dw_b']
    h = layer_norm(h, p['conv_ln_g'], p['conv_ln_b'], LN_EPS)
    return jax.nn.silu(h)


def hier_moe(h, p):
    bsz, t_len, d = h.shape
    xt = h.reshape(-1, d)
    n_tok = xt.shape[0]
    g_prob = jax.nn.softmax((xt @ p['router_group_w'] + p['router_group_b']).astype(jnp.float32), axis=-1)
    g_p, g_idx = lax.top_k(g_prob, 1)
    e_logits = (xt @ p['router_expert_w'] + p['router_expert_b']).astype(jnp.float32)
    e_logits = e_logits.reshape(n_tok, N_GROUPS, N_EXP_PER_GROUP)
    e_sel = jnp.take_along_axis(e_logits, g_idx[:, :, None], axis=1)[:, 0]
    e_p, e_idx = lax.top_k(jax.nn.softmax(e_sel, axis=-1), TOP_K)
    e_p = e_p / jnp.sum(e_p, axis=-1, keepdims=True)
    weights = g_p * e_p
    flat_idx = g_idx * N_EXP_PER_GROUP + e_idx
    combine = jnp.sum(jax.nn.one_hot(flat_idx, N_EXPERTS, dtype=jnp.float32) * weights[..., None], axis=1)
    hid = jax.nn.silu(jnp.einsum('nd,edf->nef', xt, p['expert_w_gate'])) * jnp.einsum('nd,edf->nef', xt, p['expert_w_up'])
    hid = hid * combine[..., None].astype(hid.dtype)
    out = jnp.einsum('nef,efd->nd', hid, p['expert_w_down'])
    return out.reshape(bsz, t_len, d)


def trunk_layer(x, cond, s0, p):
    sh1, sc1, g1, sh2, sc2, g2 = adaln_mod(cond, p['ada_w'], p['ada_b'])
    h = rms_norm(x, p['norm1_g']) * (1 + sc1) + sh1
    z = h @ p['w_in']
    z_shift = z[..., :SHIFT_COLS]
    z_g = z[..., SHIFT_COLS:SHIFT_COLS + LORA_G]
    z_conv = z[..., SHIFT_COLS + LORA_G:]
    y_rwkv, s_new = rwkv7_bidir(z_shift, z_g, s0, p)
    y_conv = conformer_conv(z_conv, p)
    x = x + g1 * (jnp.concatenate([y_rwkv, y_conv], axis=-1) @ p['w_out'])
    h = rms_norm(x, p['norm2_g']) * (1 + sc2) + sh2
    x = x + g2 * hier_moe(h, p)
    return x, s_new


def setup_inputs(seed: int = 0) -> dict:
    key = jax.random.key(seed)
    ks = list(jax.random.split(key, 40))
    f32 = jnp.float32

    def nrm(i, shape, scale):
        return scale * jax.random.normal(ks[i], shape, f32)

    return {
        'x_prompt': nrm(0, (BATCH, SEQ, D_MODEL), 1.0),
        'x_sample': nrm(1, (DEC_BATCH, DEC_SEQ, D_MODEL), 1.0),
        'state_rwkv': nrm(2, (DEC_BATCH, DEPTH, 2, N_RWKV_HEADS, HEAD_DIM, HEAD_DIM), 0.5),
        'c': nrm(3, (DEC_BATCH, D_MODEL), 1.0),
        'c_ctx': nrm(4, (D_MODEL,), 1.0),
        'ada_w': nrm(5, (DEPTH, D_MODEL, N_MOD * D_MODEL), 0.5 * D_MODEL ** -0.5),
        'ada_b': nrm(6, (DEPTH, N_MOD * D_MODEL), 0.02),
        'norm1_g': 1.0 + nrm(7, (DEPTH, D_MODEL), 0.05),
        'w_in': nrm(8, (DEPTH, D_MODEL, IN_COLS), D_MODEL ** -0.5),
        'tshift_mu': jax.random.uniform(ks[9], (DEPTH, 2, SHIFT_COLS), f32),
        'decay_w0': nrm(10, (DEPTH, 2, D_RWKV), 0.5),
        'decay_lora_b': nrm(11, (DEPTH, 2, LORA_W, D_RWKV), 0.5 * LORA_W ** -0.5),
        'iclr_a0': nrm(12, (DEPTH, 2, D_RWKV), 0.5),
        'iclr_lora_b': nrm(13, (DEPTH, 2, LORA_A, D_RWKV), 0.5 * LORA_A ** -0.5),
        'key_k': 1.0 + nrm(14, (DEPTH, 2, D_RWKV), 0.1),
        'key_a': 1.0 + nrm(15, (DEPTH, 2, D_RWKV), 0.1),
        'bonus_r_k': nrm(16, (DEPTH, 2, N_RWKV_HEADS, HEAD_DIM), 0.1),
        'gate_lora_b': nrm(17, (DEPTH, LORA_G, D_RWKV), LORA_G ** -0.5),
        'gn_g': 1.0 + nrm(18, (DEPTH, D_RWKV), 0.05),
        'gn_b': nrm(19, (DEPTH, D_RWKV), 0.02),
        'conv_dw_w': nrm(20, (DEPTH, CONV_WIDTH, D_CONV), CONV_WIDTH ** -0.5),
        'conv_dw_b': nrm(21, (DEPTH, D_CONV), 0.02),
        'conv_ln_g': 1.0 + nrm(22, (DEPTH, D_CONV), 0.05),
        'conv_ln_b': nrm(23, (DEPTH, D_CONV), 0.02),
        'w_out': nrm(24, (DEPTH, D_MIX, D_MODEL), D_MIX ** -0.5),
        'norm2_g': 1.0 + nrm(25, (DEPTH, D_MODEL), 0.05),
        'router_group_w': nrm(26, (DEPTH, D_MODEL, N_GROUPS), D_MODEL ** -0.5),
        'router_group_b': nrm(27, (DEPTH, N_GROUPS), 0.01),
        'router_expert_w': nrm(28, (DEPTH, D_MODEL, N_EXPERTS), D_MODEL ** -0.5),
        'router_expert_b': nrm(29, (DEPTH, N_EXPERTS), 0.01),
        'expert_w_gate': nrm(30, (DEPTH, N_EXPERTS, D_MODEL, D_EXPERT), D_MODEL ** -0.5),
        'expert_w_up': nrm(31, (DEPTH, N_EXPERTS, D_MODEL, D_EXPERT), D_MODEL ** -0.5),
        'expert_w_down': nrm(32, (DEPTH, N_EXPERTS, D_EXPERT, D_MODEL), D_EXPERT ** -0.5),
        'final_norm_g': 1.0 + nrm(33, (D_MODEL,), 0.05),
    }


def reference(x_prompt, x_sample, state_rwkv, c, c_ctx, ada_w, ada_b, norm1_g, w_in, tshift_mu,
              decay_w0, decay_lora_b, iclr_a0, iclr_lora_b, key_k, key_a, bonus_r_k, gate_lora_b,
              gn_g, gn_b, conv_dw_w, conv_dw_b, conv_ln_g, conv_ln_b, w_out, norm2_g,
              router_group_w, router_group_b, router_expert_w, router_expert_b,
              expert_w_gate, expert_w_up, expert_w_down, final_norm_g):
    rows = x_sample.shape[1] // GRID_W
    xs = x_sample + grid_pos_embed(rows, x_sample.dtype)[None]
    xp = x_prompt
    ctx_cond = c_ctx[None, :]
    ctx_states = []
    for l in range(DEPTH):
        p = {
            'ada_w': ada_w[l], 'ada_b': ada_b[l], 'norm1_g': norm1_g[l], 'w_in': w_in[l],
            'tshift_mu': tshift_mu[l], 'decay_w0': decay_w0[l], 'decay_lora_b': decay_lora_b[l],
            'iclr_a0': iclr_a0[l], 'iclr_lora_b': iclr_lora_b[l], 'key_k': key_k[l], 'key_a': key_a[l],
            'bonus_r_k': bonus_r_k[l], 'gate_lora_b': gate_lora_b[l], 'gn_g': gn_g[l], 'gn_b': gn_b[l],
            'conv_dw_w': conv_dw_w[l], 'conv_dw_b': conv_dw_b[l], 'conv_ln_g': conv_ln_g[l],
            'conv_ln_b': conv_ln_b[l], 'w_out': w_out[l], 'norm2_g': norm2_g[l],
            'router_group_w': router_group_w[l], 'router_group_b': router_group_b[l],
            'router_expert_w': router_expert_w[l], 'router_expert_b': router_expert_b[l],
            'expert_w_gate': expert_w_gate[l], 'expert_w_up': expert_w_up[l], 'expert_w_down': expert_w_down[l],
        }
        s_zero = jnp.zeros((xp.shape[0], 2, N_RWKV_HEADS, HEAD_DIM, HEAD_DIM), jnp.float32)
        xp, s_ctx = trunk_layer(xp, ctx_cond, s_zero, p)
        ctx_states.append(s_ctx.astype(state_rwkv.dtype))
        xs, _ = trunk_layer(xs, c, state_rwkv[:, l], p)
    new_state_rwkv = jnp.stack(ctx_states, axis=1)
    y_prompt = rms_norm(xp, final_norm_g)
    y_sample = rms_norm(xs, final_norm_g)
    return (y_prompt, y_sample, new_state_rwkv)
```

```python
import functools

import numpy as np
import jax
import jax.numpy as jnp
from jax import lax
from jax.experimental import pallas as pl
from jax.experimental.pallas import tpu as pltpu

F32 = jnp.float32
BF16 = jnp.bfloat16

D = 1024
N_CTX_SEQ = 16
T_CTX = 256
N_LAT_SEQ = 2
T_LAT = 1024
TM = 256
N_CTX_TILES = N_CTX_SEQ * T_CTX // TM
LAT_CHUNKS = T_LAT // TM
N_LAT_TILES = N_LAT_SEQ * LAT_CHUNKS
N_TILES = N_CTX_TILES + N_LAT_TILES
NT = N_TILES * TM
GRID_W = 64
D_RWKV = 512
D_CONV = 512
HEAD = 64
N_HEADS = 8
CONV_W = 31
CONV_PAD = CONV_W // 2
LORA = 64
LORA_G = 128
SHIFT_COLS = 3 * D_RWKV + 2 * LORA
N_GROUPS = 4
N_EXP_PER_GROUP = 8
N_EXPERTS = 32
D_EXPERT = 256
N_MOD = 6
RMS_EPS = 1e-6
LN_EPS = 1e-5
GN_EPS = 64e-5
LANES = 128
SUB = 8
STEP_BLK = 64
BLK_PER_TILE = TM // STEP_BLK
N_COL = D_RWKV // LANES
ROUTE_LANES = 128
E_LANE0 = N_GROUPS
TM_MOE = 1024
VMEM_LIMIT = 56 * 1024 * 1024


def _cp(sem):
    return pltpu.CompilerParams(dimension_semantics=sem, vmem_limit_bytes=VMEM_LIMIT)


def _split2(a):
    hi = a.astype(BF16)
    lo = (a - hi.astype(F32)).astype(BF16)
    return hi, lo


def _split3(a):
    hi = a.astype(BF16)
    r1 = a - hi.astype(F32)
    mid = r1.astype(BF16)
    lo = (r1 - mid.astype(F32)).astype(BF16)
    return hi, mid, lo


def _dot(a, b):
    return jnp.dot(a, b, preferred_element_type=F32)


def _dot_hp(a, b):
    ah, al = _split2(a)
    bh, bl = _split2(b)
    return _dot(ah, bh) + _dot(ah, bl) + _dot(al, bh)


def _dot_sel(a, sel):
    h, m, l = _split3(a)
    return _dot(h, sel) + _dot(m, sel) + _dot(l, sel)


def _sigmoid(x):
    return 1.0 / (1.0 + jnp.exp(-x))


def _silu(x):
    return x * _sigmoid(x)


def _lat_js(i):
    il = jnp.maximum(i - N_CTX_TILES, 0)
    return il // N_LAT_SEQ, il % N_LAT_SEQ


def _xp_block(i):
    return jnp.minimum(i, N_CTX_TILES - 1)


def _xs_block(i):
    j, s = _lat_js(i)
    return s * LAT_CHUNKS + j


def _pe_block(i):
    j, _ = _lat_js(i)
    return j


def _mod_row(i):
    _, s = _lat_js(i)
    return jnp.where(i < N_CTX_TILES, 0, 1 + s)


def _first_last(i):
    j, _ = _lat_js(i)
    is_ctx = i < N_CTX_TILES
    first = jnp.logical_or(is_ctx, j == 0)
    last = jnp.logical_or(is_ctx, j == LAT_CHUNKS - 1)
    return first, last


def _prev_tile(i):
    first, _ = _first_last(i)
    return jnp.where(first, i, i - N_LAT_SEQ)


def _next_tile(i):
    _, last = _first_last(i)
    return jnp.where(last, i, i + N_LAT_SEQ)


def _adaln_kernel(c_ref, w_ref, b_ref, o_ref):
    c = c_ref[...]
    o_ref[...] = _dot_hp(_silu(c), w_ref[...]) + b_ref[...]


def _adaln(cond8, ada_w, ada_b):
    tn = 1536
    n = ada_w.shape[1]
    return pl.pallas_call(
        _adaln_kernel,
        grid=(n // tn,),
        in_specs=[pl.BlockSpec((8, D), lambda j: (0, 0)),
                  pl.BlockSpec((D, tn), lambda j: (0, j)),
                  pl.BlockSpec((1, tn), lambda j: (0, j))],
        out_specs=pl.BlockSpec((8, tn), lambda j: (0, j)),
        out_shape=jax.ShapeDtypeStruct((8, n), F32),
        compiler_params=_cp(("parallel",)),
        name="adaln",
    )(cond8, ada_w, ada_b)


def _load_x(i, xp_ref, xs_ref, pe_ref):
    f = (i >= N_CTX_TILES).astype(F32)
    return xp_ref[...] * (1.0 - f) + (xs_ref[...] + pe_ref[...]) * f


def _x_specs():
    return [pl.BlockSpec((TM, D), lambda i: (_xp_block(i), 0)),
            pl.BlockSpec((TM, D), lambda i: (_xs_block(i), 0)),
            pl.BlockSpec((TM, D), lambda i: (_pe_block(i), 0))]


def _rms(x, g):
    return x * lax.rsqrt(jnp.mean(x * x, axis=-1, keepdims=True) + RMS_EPS) * g


def _inproj_kernel(xp_ref, xs_ref, pe_ref, mod_ref, g_ref, w_ref, zs_ref, zg_ref, zc_ref):
    i = pl.program_id(0)
    x = _load_x(i, xp_ref, xs_ref, pe_ref)
    sh1 = mod_ref[:, 0:D]
    sc1 = mod_ref[:, D:2 * D]
    h = (_rms(x, g_ref[...]) * (1.0 + sc1) + sh1).astype(BF16)
    zs_ref[...] = _dot(h, w_ref[:, 0:SHIFT_COLS])
    zg_ref[...] = _dot(h, w_ref[:, SHIFT_COLS:SHIFT_COLS + LORA_G])
    zc_ref[...] = _dot(h, w_ref[:, SHIFT_COLS + LORA_G:])


def _inproj(xp, xs, pe, mod3, norm1_g, w_in_bf):
    in_cols = w_in_bf.shape[1]
    return pl.pallas_call(
        _inproj_kernel,
        grid=(N_TILES,),
        in_specs=_x_specs() + [
            pl.BlockSpec((None, 1, N_MOD * D), lambda i: (_mod_row(i), 0, 0)),
            pl.BlockSpec((1, D), lambda i: (0, 0)),
            pl.BlockSpec((D, in_cols), lambda i: (0, 0))],
        out_specs=[pl.BlockSpec((TM, SHIFT_COLS), lambda i: (i, 0)),
                   pl.BlockSpec((TM, LORA_G), lambda i: (i, 0)),
                   pl.BlockSpec((TM, 2 * D_CONV), lambda i: (i, 0))],
        out_shape=[jax.ShapeDtypeStruct((NT, SHIFT_COLS), F32),
                   jax.ShapeDtypeStruct((NT, LORA_G), F32),
                   jax.ShapeDtypeStruct((NT, 2 * D_CONV), F32)],
        compiler_params=_cp(("parallel",)),
        name="inproj",
    )(xp, xs, pe, mod3, norm1_g, w_in_bf)


def _prep_kernel(zs_ref, prev_ref, next_ref, mu_ref, w0_ref, a0_ref, lora_ref, kk_ref_, ka_ref,
                 rk_ref, ones_ref, r_out, k_out, w_out, kk_out, b_out, v_out, bon_out):
    i = pl.program_id(0)
    d = pl.program_id(1)
    first, last = _first_last(i)
    cur = zs_ref[...]
    prow = prev_ref[7:8, :] * (1.0 - first.astype(F32))
    nrow = next_ref[0:1, :] * (1.0 - last.astype(F32))
    rows = lax.broadcasted_iota(jnp.int32, (TM, 1), 0)
    down = jnp.where(rows == 0, prow, pltpu.roll(cur, 1, axis=0))
    up = jnp.where(rows == TM - 1, nrow, pltpu.roll(cur, TM - 1, axis=0))
    fd = (d == 0).astype(F32)
    prev = down * fd + up * (1.0 - fd)
    xs = cur + (prev - cur) * mu_ref[...]
    r = xs[:, 0:D_RWKV]
    k = xs[:, D_RWKV:2 * D_RWKV]
    v = xs[:, 2 * D_RWKV:3 * D_RWKV]
    z2 = xs[:, 3 * D_RWKV:SHIFT_COLS]
    lane = lax.broadcasted_iota(jnp.int32, (TM, 2 * LORA), 1)
    lin = jnp.where(lane < LORA, jnp.tanh(z2), z2)
    lo = _dot_hp(lin, lora_ref[...])
    u = -(w0_ref[...] + lo[:, 0:D_RWKV])
    softplus = jnp.maximum(u, 0.0) + jnp.log1p(jnp.exp(-jnp.abs(u)))
    w_log = -softplus - 0.5
    decay = jnp.exp(-jnp.exp(w_log))
    a = _sigmoid(a0_ref[...] + lo[:, D_RWKV:])
    kx = k * kk_ref_[...]
    ones = ones_ref[...]
    nrm = jnp.sqrt(_dot_sel(kx * kx, ones))
    kk = kx / jnp.maximum(nrm, 1e-12)
    k2 = k * (1.0 + (a - 1.0) * ka_ref[...])
    bonus = _dot_sel(r * k2 * rk_ref[...], ones) * v
    r_out[...] = r
    k_out[...] = k2
    w_out[...] = decay
    kk_out[...] = kk
    b_out[...] = kk * a
    v_out[...] = v
    bon_out[...] = bonus


def _prep(zs, mu, w0, a0, lora2, key_k, key_a, r_k, ones512):
    pvec = lambda n: pl.BlockSpec((None, 1, n), lambda i, d: (d, 0, 0))
    out_spec = pl.BlockSpec((None, TM, D_RWKV), lambda i, d: (d, i, 0))
    out_shape = jax.ShapeDtypeStruct((2, NT, D_RWKV), F32)
    rows8 = TM // 8
    return pl.pallas_call(
        _prep_kernel,
        grid=(N_TILES, 2),
        in_specs=[pl.BlockSpec((TM, SHIFT_COLS), lambda i, d: (i, 0)),
                  pl.BlockSpec((8, SHIFT_COLS), lambda i, d: (_prev_tile(i) * rows8 + rows8 - 1, 0)),
                  pl.BlockSpec((8, SHIFT_COLS), lambda i, d: (_next_tile(i) * rows8, 0)),
                  pvec(SHIFT_COLS), pvec(D_RWKV), pvec(D_RWKV),
                  pl.BlockSpec((None, 2 * LORA, 2 * D_RWKV), lambda i, d: (d, 0, 0)),
                  pvec(D_RWKV), pvec(D_RWKV), pvec(D_RWKV),
                  pl.BlockSpec((D_RWKV, D_RWKV), lambda i, d: (0, 0))],
        out_specs=[out_spec] * 7,
        out_shape=[out_shape] * 7,
        compiler_params=_cp(("parallel", "parallel")),
        name="rwkv_prep",
    )(zs, zs, zs, mu, w0, a0, lora2, key_k, key_a, r_k, ones512)


N_SCAN_STEPS = N_CTX_TILES // 2 + LAT_CHUNKS
LAT_STEP0 = N_CTX_TILES // 2
ROWS_D = N_COL * 2 * HEAD


def _scan_block(step, d):
    lat = step >= LAT_STEP0
    jl = step - LAT_STEP0
    j = jnp.where(d == 0, jl, LAT_CHUNKS - 1 - jl)
    return jnp.where(lat, LAT_STEP0 + j, step)


def _scan_state_block(step):
    return jnp.minimum(step, LAT_STEP0)


def _scan_kernel(*refs):
    (rf, kf, wf, kkf, bf, vf, rb, kb, wb, kkb, bb, vb, s0_ref, ones_ref, segt_ref, seg_ref,
     yf_ref, yb_ref, sout_ref, st, vhi, vlo, yacc) = refs
    step = pl.program_id(0)
    row_refs = ((wf, kkf, bf, kf, rf), (wb, kkb, bb, kb, rb))
    v_refs = (vf, vb)
    y_refs = (yf_ref, yb_ref)

    @pl.when(step <= LAT_STEP0)
    def _init():
        for dl in range(2):
            for col in range(N_COL):
                for sl in range(2):
                    g = col * 2 + sl
                    st[dl, g * HEAD:(g + 1) * HEAD, :] = s0_ref[sl, dl, :, col * LANES:(col + 1) * LANES]

    ones = ones_ref[...]

    def blk_body(q, carry):
        blks = (q, BLK_PER_TILE - 1 - q)
        for dl in range(2):
            for col in range(N_COL):
                for sl in range(2):
                    g = col * 2 + sl
                    vv = v_refs[dl][sl, blks[dl], col]
                    hi = vv.astype(BF16)
                    vhi[dl, g * HEAD:(g + 1) * HEAD, :] = hi
                    vlo[dl, g * HEAD:(g + 1) * HEAD, :] = (vv - hi.astype(F32)).astype(BF16)
        yacc[...] = jnp.zeros(yacc.shape, F32)

        def step_body(ub, c2):
            b8s = (ub, SUB - 1 - ub)
            tiles = []
            for dl in range(2):
                base = blks[dl] * STEP_BLK + b8s[dl] * SUB
                tiles.append([[[ref[pl.ds(pl.multiple_of(sl * TM + base, SUB), SUB),
                                    col * LANES:(col + 1) * LANES]
                                for sl in range(2)] for col in range(N_COL)] for ref in row_refs[dl]])
            for rr in range(SUB):
                for dl in range(2):
                    r8 = rr if dl == 0 else SUB - 1 - rr
                    tp = b8s[dl] * SUB + r8
                    vecs = []
                    for vi in range(len(row_refs[dl])):
                        pieces = [jnp.broadcast_to(tiles[dl][vi][col][sl][r8:r8 + 1, :], (HEAD, LANES))
                                  for col in range(N_COL) for sl in range(2)]
                        vecs.append(jnp.concatenate(pieces, axis=0))
                    w4, kk4, b4, k4, r4 = vecs
                    s4 = st[dl]
                    sa = _dot((s4 * kk4).astype(BF16), ones)
                    segt = segt_ref[tp]
                    vcol = _dot(vhi[dl], segt) + _dot(vlo[dl], segt)
                    s_new = s4 * w4 - sa * b4 + vcol * k4
                    st[dl] = s_new
                    yacc[dl] += _dot((s_new * r4).astype(BF16), seg_ref[tp])
            return c2

        lax.fori_loop(0, STEP_BLK // SUB, step_body, 0)
        for dl in range(2):
            for col in range(N_COL):
                for sl in range(2):
                    g = col * 2 + sl
                    y_refs[dl][sl, blks[dl], col] = yacc[dl, g * HEAD:(g + 1) * HEAD, :]
        return carry

    lax.fori_loop(0, BLK_PER_TILE, blk_body, 0)

    @pl.when(jnp.logical_or(step < LAT_STEP0, step == N_SCAN_STEPS - 1))
    def _final():
        for dl in range(2):
            for col in range(N_COL):
                for sl in range(2):
                    g = col * 2 + sl
                    sout_ref[sl, dl, :, col * LANES:(col + 1) * LANES] = st[dl, g * HEAD:(g + 1) * HEAD, :]


def _scan(r, k2, dec, kk, bvec, vt, s0_all, ones128, segt_all, seg_all):
    def row_spec(d):
        return pl.BlockSpec((None, 2 * TM, D_RWKV), lambda s: (d, _scan_block(s, d), 0))

    def vt_spec(d):
        return pl.BlockSpec((None, 2, BLK_PER_TILE, N_COL, HEAD, LANES),
                            lambda s: (d, _scan_block(s, d), 0, 0, 0, 0))

    def yt_spec(d):
        return pl.BlockSpec((2, BLK_PER_TILE, N_COL, HEAD, LANES),
                            lambda s: (_scan_block(s, d), 0, 0, 0, 0))

    rows = [r, k2, dec, kk, bvec]
    const = lambda shape: pl.BlockSpec(shape, lambda s: (0,) * len(shape))
    state_spec = pl.BlockSpec((2, 2, HEAD, D_RWKV), lambda s: (_scan_state_block(s), 0, 0, 0))
    n_pairs = N_TILES // 2
    yt_shape = jax.ShapeDtypeStruct((N_TILES, BLK_PER_TILE, N_COL, HEAD, LANES), F32)
    return pl.pallas_call(
        _scan_kernel,
        grid=(N_SCAN_STEPS,),
        in_specs=[row_spec(0)] * 5 + [vt_spec(0)] + [row_spec(1)] * 5 + [vt_spec(1)] + [
            state_spec, const((LANES, LANES)), const((STEP_BLK, LANES, LANES)),
            const((STEP_BLK, LANES, LANES))],
        out_specs=[yt_spec(0), yt_spec(1), state_spec],
        out_shape=[yt_shape, yt_shape,
                   jax.ShapeDtypeStruct(((LAT_STEP0 + 1) * 2, 2, HEAD, D_RWKV), F32)],
        scratch_shapes=[pltpu.VMEM((2, ROWS_D, LANES), F32),
                        pltpu.VMEM((2, ROWS_D, LANES), BF16),
                        pltpu.VMEM((2, ROWS_D, LANES), BF16),
                        pltpu.VMEM((2, ROWS_D, LANES), F32)],
        compiler_params=_cp(("arbitrary",)),
        name="rwkv_scan",
    )(*rows, vt, *rows, vt, s0_all, ones128, segt_all, seg_all)


HALO = 16


def _glu(z):
    return z[:, 0:D_CONV] * _sigmoid(z[:, D_CONV:])


def _conv_kernel(cur_ref, prev_ref, next_ref, w_ref, b_ref, g_ref, beta_ref, o_ref, ext):
    i = pl.program_id(0)
    first, last = _first_last(i)
    ext[0:HALO, :] = _glu(prev_ref[...]) * (1.0 - first.astype(F32))
    ext[HALO:HALO + TM, :] = _glu(cur_ref[...])
    ext[HALO + TM:, :] = _glu(next_ref[...]) * (1.0 - last.astype(F32))
    acc = jnp.zeros((TM, D_CONV), F32)
    for j in range(CONV_W):
        off = HALO - CONV_PAD + j
        acc = acc + ext[off:off + TM, :] * w_ref[j:j + 1, :]
    h = acc + b_ref[...]
    mu = jnp.mean(h, axis=-1, keepdims=True)
    hc = h - mu
    var = jnp.mean(hc * hc, axis=-1, keepdims=True)
    y = hc * lax.rsqrt(var + LN_EPS) * g_ref[...] + beta_ref[...]
    o_ref[...] = _silu(y)


def _conv(zc, conv_w, conv_b, ln_g, ln_b):
    nh = TM // HALO
    vec = pl.BlockSpec((1, D_CONV), lambda i: (0, 0))
    return pl.pallas_call(
        _conv_kernel,
        grid=(N_TILES,),
        in_specs=[pl.BlockSpec((TM, 2 * D_CONV), lambda i: (i, 0)),
                  pl.BlockSpec((HALO, 2 * D_CONV), lambda i: (_prev_tile(i) * nh + nh - 1, 0)),
                  pl.BlockSpec((HALO, 2 * D_CONV), lambda i: (_next_tile(i) * nh, 0)),
                  pl.BlockSpec((CONV_W + 1, D_CONV), lambda i: (0, 0)), vec, vec, vec],
        out_specs=pl.BlockSpec((TM, D_CONV), lambda i: (i, 0)),
        out_shape=jax.ShapeDtypeStruct((NT, D_CONV), F32),
        scratch_shapes=[pltpu.VMEM((TM + 2 * HALO, D_CONV), F32)],
        compiler_params=_cp(("parallel",)),
        name="conv_module",
    )(zc, zc, zc, conv_w, conv_b, ln_g, ln_b)


def _outproj_kernel(xp_ref, xs_ref, pe_ref, mod_ref, yf_ref, yb_ref, bf_ref, bb_ref, zg_ref, yc_ref,
                    gng_ref, gnb_ref, gl_ref, wo_ref, n2_ref, rw_ref, rb_ref, ones_ref,
                    x1_ref, h2_ref, comb_ref):
    i = pl.program_id(0)
    x = _load_x(i, xp_ref, xs_ref, pe_ref)
    g1 = mod_ref[:, 2 * D:3 * D]
    sh2 = mod_ref[:, 3 * D:4 * D]
    sc2 = mod_ref[:, 4 * D:5 * D]
    ones = ones_ref[...]
    y = (yf_ref[...] + bf_ref[...]) + (yb_ref[...] + bb_ref[...])
    mu = _dot_sel(y, ones) * (1.0 / HEAD)
    yc = y - mu
    var = _dot_sel(yc * yc, ones) * (1.0 / HEAD)
    yn = yc * lax.rsqrt(var + GN_EPS) * gng_ref[...] + gnb_ref[...]
    gate = _dot(_sigmoid(zg_ref[...]).astype(BF16), gl_ref[...])
    y_rwkv = (yn * gate).astype(BF16)
    mix = _dot(y_rwkv, wo_ref[0:D_RWKV, :]) + _dot(yc_ref[...].astype(BF16), wo_ref[D_RWKV:, :])
    x1 = x + g1 * mix
    x1_ref[...] = x1
    h2 = _rms(x1, n2_ref[...]) * (1.0 + sc2) + sh2
    h2_ref[...] = h2.astype(BF16)

    logits = _dot_hp(h2, rw_ref[...]) + rb_ref[...]
    lane = lax.broadcasted_iota(jnp.int32, (TM, ROUTE_LANES), 1)
    lanef = lane.astype(F32)
    neg = jnp.float32(-1e30)
    big = jnp.float32(1e9)
    gmask = lane < N_GROUPS
    gl = jnp.where(gmask, logits, neg)
    ge = jnp.where(gmask, jnp.exp(gl - jnp.max(gl, axis=-1, keepdims=True)), 0.0)
    gprob = ge / jnp.sum(ge, axis=-1, keepdims=True)
    gp = jnp.max(gprob, axis=-1, keepdims=True)
    gidx = jnp.min(jnp.where(jnp.logical_and(gmask, gprob == gp), lanef, big), axis=-1, keepdims=True)
    egrp = jnp.floor((lanef - float(E_LANE0)) * (1.0 / N_EXP_PER_GROUP))
    emask = jnp.logical_and(jnp.logical_and(lane >= E_LANE0, lane < E_LANE0 + N_EXPERTS), egrp == gidx)
    el = jnp.where(emask, logits, neg)
    ee = jnp.where(emask, jnp.exp(el - jnp.max(el, axis=-1, keepdims=True)), 0.0)
    ep = ee / jnp.sum(ee, axis=-1, keepdims=True)
    m1 = jnp.max(jnp.where(emask, ep, -1.0), axis=-1, keepdims=True)
    i1 = jnp.min(jnp.where(jnp.logical_and(emask, ep == m1), lanef, big), axis=-1, keepdims=True)
    mask2 = jnp.logical_and(emask, lanef != i1)
    m2 = jnp.max(jnp.where(mask2, ep, -1.0), axis=-1, keepdims=True)
    i2 = jnp.min(jnp.where(jnp.logical_and(mask2, ep == m2), lanef, big), axis=-1, keepdims=True)
    den = m1 + m2
    comb_ref[...] = (jnp.where(lanef == i1, gp * (m1 / den), 0.0)
                     + jnp.where(lanef == i2, gp * (m2 / den), 0.0))


def _outproj(xp, xs, pe, mod3, yf, yb, bonus, zg, yconv, gn_g, gn_b, gate_bf, w_out_bf, norm2_g,
             router_w, router_b, ones512):
    tile = lambda n: pl.BlockSpec((TM, n), lambda i: (i, 0))
    const = lambda shape: pl.BlockSpec(shape, lambda i: (0,) * len(shape))
    return pl.pallas_call(
        _outproj_kernel,
        grid=(N_TILES,),
        in_specs=_x_specs() + [
            pl.BlockSpec((None, 1, N_MOD * D), lambda i: (_mod_row(i), 0, 0)),
            tile(D_RWKV), tile(D_RWKV),
            pl.BlockSpec((None, TM, D_RWKV), lambda i: (0, i, 0)),
            pl.BlockSpec((None, TM, D_RWKV), lambda i: (1, i, 0)),
            tile(LORA_G), tile(D_CONV),
            const((1, D_RWKV)), const((1, D_RWKV)), const((LORA_G, D_RWKV)), const((D, D)),
            const((1, D)), const((D, ROUTE_LANES)), const((1, ROUTE_LANES)),
            const((D_RWKV, D_RWKV))],
        out_specs=[tile(D), tile(D), tile(ROUTE_LANES)],
        out_shape=[jax.ShapeDtypeStruct((NT, D), F32), jax.ShapeDtypeStruct((NT, D), BF16),
                   jax.ShapeDtypeStruct((NT, ROUTE_LANES), F32)],
        compiler_params=_cp(("parallel",)),
        name="outproj_router",
    )(xp, xs, pe, mod3, yf, yb, bonus, bonus, zg, yconv, gn_g, gn_b, gate_bf, w_out_bf, norm2_g,
      router_w, router_b, ones512)


def _moe_kernel(h_ref, comb_ref, wg_ref, wu_ref, wd_ref, o_ref):
    e = pl.program_id(1)

    @pl.when(e == 0)
    def _():
        o_ref[...] = jnp.zeros(o_ref.shape, F32)

    h = h_ref[...]
    lane = lax.broadcasted_iota(jnp.int32, (TM_MOE, ROUTE_LANES), 1)
    cw = jnp.sum(jnp.where(lane == e + E_LANE0, comb_ref[...], 0.0), axis=-1, keepdims=True)
    gate = _dot(h, wg_ref[...].astype(BF16))
    up = _dot(h, wu_ref[...].astype(BF16))
    hid = (_silu(gate) * up * cw).astype(BF16)
    o_ref[...] += _dot(hid, wd_ref[...].astype(BF16))


def _moe(h2, comb, wg, wu, wd):
    return pl.pallas_call(
        _moe_kernel,
        grid=(NT // TM_MOE, N_EXPERTS),
        in_specs=[pl.BlockSpec((TM_MOE, D), lambda t, e: (t, 0)),
                  pl.BlockSpec((TM_MOE, ROUTE_LANES), lambda t, e: (t, 0)),
                  pl.BlockSpec((None, D, D_EXPERT), lambda t, e: (e, 0, 0)),
                  pl.BlockSpec((None, D, D_EXPERT), lambda t, e: (e, 0, 0)),
                  pl.BlockSpec((None, D_EXPERT, D), lambda t, e: (e, 0, 0))],
        out_specs=pl.BlockSpec((TM_MOE, D), lambda t, e: (t, 0)),
        out_shape=jax.ShapeDtypeStruct((NT, D), F32),
        compiler_params=_cp(("parallel", "arbitrary")),
        name="moe_experts",
    )(h2, comb, wg, wu, wd)


def _final_kernel(x1_ref, moe_ref, mod_ref, g_ref, o_ref):
    g2 = mod_ref[:, 5 * D:6 * D]
    x2 = x1_ref[...] + g2 * moe_ref[...]
    o_ref[...] = _rms(x2, g_ref[...])


def _final(x1, moe, mod3, final_g):
    tile = pl.BlockSpec((TM, D), lambda i: (i, 0))
    return pl.pallas_call(
        _final_kernel,
        grid=(N_TILES,),
        in_specs=[tile, tile, pl.BlockSpec((None, 1, N_MOD * D), lambda i: (_mod_row(i), 0, 0)),
                  pl.BlockSpec((1, D), lambda i: (0, 0))],
        out_specs=tile,
        out_shape=jax.ShapeDtypeStruct((NT, D), F32),
        compiler_params=_cp(("parallel",)),
        name="final_norm",
    )(x1, moe, mod3, final_g)


def _pos_embed(rows):
    t = jnp.arange(rows * GRID_W)
    row = (t // GRID_W).astype(F32)
    col = (t % GRID_W).astype(F32)
    quarter = D // 4
    freqs = 1.0 / (10000.0 ** (jnp.arange(quarter, dtype=F32) / quarter))
    ang_r = row[:, None] * freqs[None, :]
    ang_c = col[:, None] * freqs[None, :]
    return jnp.concatenate([jnp.sin(ang_r), jnp.cos(ang_r), jnp.sin(ang_c), jnp.cos(ang_c)], axis=-1)


def _selection_constants():
    lane = np.arange(LANES)
    ones128 = (lane[:, None] // HEAD == lane[None, :] // HEAD).astype(np.float32)
    ch = np.arange(D_RWKV)
    ones512 = (ch[:, None] // HEAD == ch[None, :] // HEAD).astype(np.float32)
    tprime, hl = lane // 2, lane % 2
    base = (hl[:, None] == (lane[None, :] // HEAD)).astype(np.float32)
    segt = np.stack([base * (tprime[:, None] == t) for t in range(STEP_BLK)])
    seg = np.transpose(segt, (0, 2, 1))
    return (jnp.asarray(ones128, BF16), jnp.asarray(ones512, BF16), jnp.asarray(segt, BF16),
            jnp.asarray(seg, BF16))


def kernel(x_prompt, x_sample, state_rwkv, c, c_ctx, ada_w, ada_b, norm1_g, w_in, tshift_mu, decay_w0, decay_lora_b, iclr_a0, iclr_lora_b, key_k, key_a, bonus_r_k, gate_lora_b, gn_g, gn_b, conv_dw_w, conv_dw_b, conv_ln_g, conv_ln_b, w_out, norm2_g, router_group_w, router_group_b, router_expert_w, router_expert_b, expert_w_gate, expert_w_up, expert_w_down, final_norm_g):
    assert x_prompt.shape == (N_CTX_SEQ, T_CTX, D) and x_sample.shape == (N_LAT_SEQ, T_LAT, D)
    assert ada_w.shape[0] == 1, "one trunk layer"
    ones128, ones512, segt_all, seg_all = _selection_constants()
    xp = x_prompt.reshape(N_CTX_SEQ * T_CTX, D)
    xs = x_sample.reshape(N_LAT_SEQ * T_LAT, D)
    pe = _pos_embed(T_LAT // GRID_W)

    cond8 = jnp.concatenate([c_ctx[None, :], c, jnp.zeros((8 - 1 - N_LAT_SEQ, D), F32)], axis=0)
    mod3 = _adaln(cond8, ada_w[0], ada_b[0][None, :]).reshape(8, 1, N_MOD * D)

    zs, zg, zc = _inproj(xp, xs, pe, mod3, norm1_g, w_in[0].astype(BF16))

    zero = jnp.zeros((2, LORA, D_RWKV), F32)
    lora2 = jnp.concatenate([jnp.concatenate([decay_lora_b[0], zero], axis=2),
                             jnp.concatenate([zero, iclr_lora_b[0]], axis=2)], axis=1)
    vec = lambda p: p.reshape(2, 1, -1)
    r, k2, dec, kk, bvec, v, bonus = _prep(zs, vec(tshift_mu[0]), vec(decay_w0[0]), vec(iclr_a0[0]), lora2,
                                           vec(key_k[0]), vec(key_a[0]), vec(bonus_r_k[0]), ones512)

    vt = v.reshape(2, N_TILES, BLK_PER_TILE, STEP_BLK, N_COL, 2, HEAD)
    vt = vt.transpose(0, 1, 2, 4, 6, 3, 5).reshape(2, N_TILES, BLK_PER_TILE, N_COL, HEAD, LANES)
    s0 = state_rwkv[:, 0].transpose(0, 1, 3, 2, 4).reshape(N_LAT_SEQ, 2, HEAD, D_RWKV)
    s0_all = jnp.concatenate([jnp.zeros((N_CTX_SEQ, 2, HEAD, D_RWKV), F32), s0], axis=0)
    ytf, ytb, s_fin = _scan(r, k2, dec, kk, bvec, vt, s0_all, ones128, segt_all, seg_all)

    def untranspose(yt):
        yt = yt.reshape(N_TILES, BLK_PER_TILE, N_COL, HEAD, STEP_BLK, 2)
        return yt.transpose(0, 1, 4, 2, 5, 3).reshape(NT, D_RWKV)

    yconv = _conv(zc, jnp.concatenate([conv_dw_w[0], jnp.zeros((1, D_CONV), F32)], axis=0),
                  conv_dw_b, conv_ln_g, conv_ln_b)

    router_w = jnp.concatenate([router_group_w[0], router_expert_w[0],
                                jnp.zeros((D, ROUTE_LANES - N_GROUPS - N_EXPERTS), F32)], axis=1)
    router_b = jnp.concatenate([router_group_b[0], router_expert_b[0],
                                jnp.zeros((ROUTE_LANES - N_GROUPS - N_EXPERTS,), F32)])[None, :]
    x1, h2, comb = _outproj(xp, xs, pe, mod3, untranspose(ytf), untranspose(ytb), bonus, zg, yconv,
                            gn_g, gn_b, gate_lora_b[0].astype(BF16), w_out[0].astype(BF16), norm2_g,
                            router_w, router_b, ones512)

    moe = _moe(h2, comb, expert_w_gate[0], expert_w_up[0], expert_w_down[0])
    out = _final(x1, moe, mod3, final_norm_g[None, :])

    y_prompt = out[:N_CTX_SEQ * T_CTX].reshape(N_CTX_SEQ, T_CTX, D)
    y_sample = out[N_CTX_SEQ * T_CTX:].reshape(LAT_CHUNKS, N_LAT_SEQ, TM, D)
    y_sample = y_sample.transpose(1, 0, 2, 3).reshape(N_LAT_SEQ, T_LAT, D)
    s_ctx = s_fin[:N_CTX_SEQ].reshape(N_CTX_SEQ, 2, HEAD, N_HEADS, HEAD).transpose(0, 1, 3, 2, 4)
    new_state = s_ctx[:, None].astype(state_rwkv.dtype)
    return (y_prompt, y_sample, new_state)
```

```python
import numpy as np
import jax
import jax.numpy as jnp
from jax import lax
from jax.experimental import pallas as pl
from jax.experimental.pallas import tpu as pltpu

F32 = jnp.float32
BF16 = jnp.bfloat16

D = 1024
N_CTX_SEQ = 16
T_CTX = 256
N_LAT_SEQ = 2
T_LAT = 1024
TM = 256
N_CTX_TILES = N_CTX_SEQ * T_CTX // TM
LAT_CHUNKS = T_LAT // TM
N_LAT_TILES = N_LAT_SEQ * LAT_CHUNKS
N_TILES = N_CTX_TILES + N_LAT_TILES
NT = N_TILES * TM
GRID_W = 64
D_RWKV = 512
D_CONV = 512
HEAD = 64
N_HEADS = 8
CONV_W = 31
CONV_PAD = CONV_W // 2
LORA = 64
LORA_G = 128
SHIFT_COLS = 3 * D_RWKV + 2 * LORA
N_GROUPS = 4
N_EXP_PER_GROUP = 8
N_EXPERTS = 32
D_EXPERT = 256
N_MOD = 6
RMS_EPS = 1e-6
LN_EPS = 1e-5
GN_EPS = 64e-5
LANES = 128
SUB = 8
COLW = 256
N_COL = D_RWKV // COLW
HPC = COLW // HEAD
STEP_BLK = LANES // HPC
BLK_PER_TILE = TM // STEP_BLK
N_WIN = TM // LANES
BLK_PER_WIN = LANES // STEP_BLK
ROUTE_LANES = 128
E_LANE0 = N_GROUPS
TM_MOE = 1024
VMEM_LIMIT = 56 * 1024 * 1024


def _cp(sem, flags=None):
    return pltpu.CompilerParams(dimension_semantics=sem, vmem_limit_bytes=VMEM_LIMIT, flags=flags)


def _split2(a):
    hi = a.astype(BF16)
    lo = (a - hi.astype(F32)).astype(BF16)
    return hi, lo


def _split3(a):
    hi = a.astype(BF16)
    r1 = a - hi.astype(F32)
    mid = r1.astype(BF16)
    lo = (r1 - mid.astype(F32)).astype(BF16)
    return hi, mid, lo


def _dot(a, b):
    return jnp.dot(a, b, preferred_element_type=F32)


def _dot_hp(a, b):
    ah, al = _split2(a)
    bh, bl = _split2(b)
    return _dot(ah, bh) + _dot(ah, bl) + _dot(al, bh)


def _dot_sel(a, sel):
    h, m, l = _split3(a)
    return _dot(h, sel) + _dot(m, sel) + _dot(l, sel)


def _sigmoid(x):
    return 1.0 / (1.0 + jnp.exp(-x))


def _silu(x):
    return x * _sigmoid(x)


def _lat_js(i):
    il = jnp.maximum(i - N_CTX_TILES, 0)
    return il // N_LAT_SEQ, il % N_LAT_SEQ


def _xp_block(i):
    return jnp.minimum(i, N_CTX_TILES - 1)


def _xs_block(i):
    j, s = _lat_js(i)
    return s * LAT_CHUNKS + j


def _pe_block(i):
    j, _ = _lat_js(i)
    return j


def _mod_row(i):
    _, s = _lat_js(i)
    return jnp.where(i < N_CTX_TILES, 0, 1 + s)


def _first_last(i):
    j, _ = _lat_js(i)
    is_ctx = i < N_CTX_TILES
    first = jnp.logical_or(is_ctx, j == 0)
    last = jnp.logical_or(is_ctx, j == LAT_CHUNKS - 1)
    return first, last


def _prev_tile(i):
    first, _ = _first_last(i)
    return jnp.where(first, i, i - N_LAT_SEQ)


def _next_tile(i):
    _, last = _first_last(i)
    return jnp.where(last, i, i + N_LAT_SEQ)


def _adaln_kernel(c_ref, w_ref, b_ref, o_ref):
    c = c_ref[...]
    o_ref[...] = _dot_hp(_silu(c), w_ref[...]) + b_ref[...]


def _adaln(cond8, ada_w, ada_b):
    tn = 1536
    n = ada_w.shape[1]
    return pl.pallas_call(
        _adaln_kernel,
        grid=(n // tn,),
        in_specs=[pl.BlockSpec((8, D), lambda j: (0, 0)),
                  pl.BlockSpec((D, tn), lambda j: (0, j)),
                  pl.BlockSpec((1, tn), lambda j: (0, j))],
        out_specs=pl.BlockSpec((8, tn), lambda j: (0, j)),
        out_shape=jax.ShapeDtypeStruct((8, n), F32),
        compiler_params=_cp(("parallel",)),
        name="adaln",
    )(cond8, ada_w, ada_b)


def _load_x(i, xp_ref, xs_ref, pe_ref):
    f = (i >= N_CTX_TILES).astype(F32)
    return xp_ref[...] * (1.0 - f) + (xs_ref[...] + pe_ref[...]) * f


def _x_specs():
    return [pl.BlockSpec((TM, D), lambda i: (_xp_block(i), 0)),
            pl.BlockSpec((TM, D), lambda i: (_xs_block(i), 0)),
            pl.BlockSpec((TM, D), lambda i: (_pe_block(i), 0))]


def _rms(x, g):
    return x * lax.rsqrt(jnp.mean(x * x, axis=-1, keepdims=True) + RMS_EPS) * g


def _inproj_kernel(xp_ref, xs_ref, pe_ref, mod_ref, g_ref, w_ref, zs_ref, zg_ref, zc_ref):
    i = pl.program_id(0)
    x = _load_x(i, xp_ref, xs_ref, pe_ref)
    sh1 = mod_ref[:, 0:D]
    sc1 = mod_ref[:, D:2 * D]
    h = (_rms(x, g_ref[...]) * (1.0 + sc1) + sh1).astype(BF16)
    zs_ref[...] = _dot(h, w_ref[:, 0:SHIFT_COLS])
    zg_ref[...] = _dot(h, w_ref[:, SHIFT_COLS:SHIFT_COLS + LORA_G])
    zc_ref[...] = _dot(h, w_ref[:, SHIFT_COLS + LORA_G:])


def _inproj(xp, xs, pe, mod3, norm1_g, w_in_bf):
    in_cols = w_in_bf.shape[1]
    return pl.pallas_call(
        _inproj_kernel,
        grid=(N_TILES,),
        in_specs=_x_specs() + [
            pl.BlockSpec((None, 1, N_MOD * D), lambda i: (_mod_row(i), 0, 0)),
            pl.BlockSpec((1, D), lambda i: (0, 0)),
            pl.BlockSpec((D, in_cols), lambda i: (0, 0))],
        out_specs=[pl.BlockSpec((TM, SHIFT_COLS), lambda i: (i, 0)),
                   pl.BlockSpec((TM, LORA_G), lambda i: (i, 0)),
                   pl.BlockSpec((TM, 2 * D_CONV), lambda i: (i, 0))],
        out_shape=[jax.ShapeDtypeStruct((NT, SHIFT_COLS), F32),
                   jax.ShapeDtypeStruct((NT, LORA_G), F32),
                   jax.ShapeDtypeStruct((NT, 2 * D_CONV), F32)],
        compiler_params=_cp(("parallel",)),
        name="inproj",
    )(xp, xs, pe, mod3, norm1_g, w_in_bf)


def _prep_kernel(zs_ref, prev_ref, next_ref, mu_ref, w0_ref, a0_ref, lora_ref, kk_ref_, ka_ref,
                 rk_ref, ones_ref, r_out, k_out, w_out, kk_out, b_out, v_out, bon_out):
    i = pl.program_id(0)
    d = pl.program_id(1)
    first, last = _first_last(i)
    cur = zs_ref[...]
    prow = prev_ref[7:8, :] * (1.0 - first.astype(F32))
    nrow = next_ref[0:1, :] * (1.0 - last.astype(F32))
    rows = lax.broadcasted_iota(jnp.int32, (TM, 1), 0)
    down = jnp.where(rows == 0, prow, pltpu.roll(cur, 1, axis=0))
    up = jnp.where(rows == TM - 1, nrow, pltpu.roll(cur, TM - 1, axis=0))
    fd = (d == 0).astype(F32)
    prev = down * fd + up * (1.0 - fd)
    xs = cur + (prev - cur) * mu_ref[...]
    r = xs[:, 0:D_RWKV]
    k = xs[:, D_RWKV:2 * D_RWKV]
    v = xs[:, 2 * D_RWKV:3 * D_RWKV]
    z2 = xs[:, 3 * D_RWKV:SHIFT_COLS]
    lane = lax.broadcasted_iota(jnp.int32, (TM, 2 * LORA), 1)
    lin = jnp.where(lane < LORA, jnp.tanh(z2), z2)
    lo = _dot_hp(lin, lora_ref[...])
    u = -(w0_ref[...] + lo[:, 0:D_RWKV])
    softplus = jnp.maximum(u, 0.0) + jnp.log1p(jnp.exp(-jnp.abs(u)))
    w_log = -softplus - 0.5
    decay = jnp.exp(-jnp.exp(w_log))
    a = _sigmoid(a0_ref[...] + lo[:, D_RWKV:])
    kx = k * kk_ref_[...]
    ones = ones_ref[...]
    nrm = jnp.sqrt(_dot_sel(kx * kx, ones))
    kk = kx / jnp.maximum(nrm, 1e-12)
    k2 = k * (1.0 + (a - 1.0) * ka_ref[...])
    bonus = _dot_sel(r * k2 * rk_ref[...], ones) * v
    r_out[...] = r
    k_out[...] = k2
    w_out[...] = decay
    kk_out[...] = kk
    b_out[...] = kk * a
    vt = v.T
    for wdw in range(N_WIN):
        v_out[wdw] = vt[:, wdw * LANES:(wdw + 1) * LANES]
    bon_out[...] = bonus


def _prep(zs, mu, w0, a0, lora2, key_k, key_a, r_k, ones512):
    pvec = lambda n: pl.BlockSpec((None, 1, n), lambda i, d: (d, 0, 0))
    out_spec = pl.BlockSpec((None, TM, D_RWKV), lambda i, d: (d, i, 0))
    out_shape = jax.ShapeDtypeStruct((2, NT, D_RWKV), F32)
    vt_spec = pl.BlockSpec((None, None, N_WIN, D_RWKV, LANES), lambda i, d: (d, i, 0, 0, 0))
    vt_shape = jax.ShapeDtypeStruct((2, N_TILES, N_WIN, D_RWKV, LANES), F32)
    rows8 = TM // 8
    return pl.pallas_call(
        _prep_kernel,
        grid=(N_TILES, 2),
        in_specs=[pl.BlockSpec((TM, SHIFT_COLS), lambda i, d: (i, 0)),
                  pl.BlockSpec((8, SHIFT_COLS), lambda i, d: (_prev_tile(i) * rows8 + rows8 - 1, 0)),
                  pl.BlockSpec((8, SHIFT_COLS), lambda i, d: (_next_tile(i) * rows8, 0)),
                  pvec(SHIFT_COLS), pvec(D_RWKV), pvec(D_RWKV),
                  pl.BlockSpec((None, 2 * LORA, 2 * D_RWKV), lambda i, d: (d, 0, 0)),
                  pvec(D_RWKV), pvec(D_RWKV), pvec(D_RWKV),
                  pl.BlockSpec((D_RWKV, D_RWKV), lambda i, d: (0, 0))],
        out_specs=[out_spec] * 5 + [vt_spec, out_spec],
        out_shape=[out_shape] * 5 + [vt_shape, out_shape],
        compiler_params=_cp(("parallel", "parallel")),
        name="rwkv_prep",
    )(zs, zs, zs, mu, w0, a0, lora2, key_k, key_a, r_k, ones512)


N_SCAN_STEPS = N_CTX_TILES // 2 + LAT_CHUNKS
LAT_STEP0 = N_CTX_TILES // 2
ROWS_D = N_COL * 2 * HEAD


def _scan_block(step, d):
    lat = step >= LAT_STEP0
    jl = step - LAT_STEP0
    j = jnp.where(d == 0, jl, LAT_CHUNKS - 1 - jl)
    return jnp.where(lat, LAT_STEP0 + j, step)


def _scan_state_block(step):
    return jnp.minimum(step, LAT_STEP0)


def _scan_kernel(*refs):
    (rf, kf, wf, kkf, bf, vf, rb, kb, wb, kkb, bb, vb, s0_ref, ones_ref, segt_ref, seg_ref,
     yf_ref, yb_ref, sout_ref, st, vhl, yacc) = refs
    step = pl.program_id(0)
    row_refs = ((wf, kkf, bf, kf, rf), (wb, kkb, bb, kb, rb))
    v_refs = (vf, vb)
    y_refs = (yf_ref, yb_ref)

    @pl.when(step <= LAT_STEP0)
    def _init():
        for dl in range(2):
            for col in range(N_COL):
                for sl in range(2):
                    g = col * 2 + sl
                    st[dl, g * HEAD:(g + 1) * HEAD, :] = s0_ref[sl, dl, :, col * COLW:(col + 1) * COLW]

    ones = ones_ref[...]
    lane_blk = lax.broadcasted_iota(jnp.int32, (HEAD, LANES), 1) // STEP_BLK

    def blk_body(q, carry):
        blks = (q, BLK_PER_TILE - 1 - q)
        for dl in range(2):
            win = blks[dl] // BLK_PER_WIN
            off = (blks[dl] % BLK_PER_WIN) * STEP_BLK
            for col in range(N_COL):
                for sl in range(2):
                    g = col * 2 + sl
                    vv = jnp.zeros((HEAD, LANES), F32)
                    for h in range(HPC):
                        ch0 = (col * HPC + h) * HEAD
                        piece = v_refs[dl][sl, win, ch0:ch0 + HEAD, :]
                        moved = pltpu.roll(piece, (h * STEP_BLK + LANES - off) % LANES, axis=1)
                        vv = jnp.where(lane_blk == h, moved, vv)
                    hi = vv.astype(BF16)
                    vhl[dl, g * HEAD:(g + 1) * HEAD, 0:LANES] = hi
                    vhl[dl, g * HEAD:(g + 1) * HEAD, LANES:] = (vv - hi.astype(F32)).astype(BF16)
        yacc[...] = jnp.zeros(yacc.shape, F32)

        def step_body(ub, c2):
            b8s = (ub, STEP_BLK // SUB - 1 - ub)
            tiles = []
            for dl in range(2):
                base = blks[dl] * STEP_BLK + b8s[dl] * SUB
                tiles.append([[[ref[pl.ds(pl.multiple_of(sl * TM + base, SUB), SUB),
                                    col * COLW:(col + 1) * COLW]
                                for sl in range(2)] for col in range(N_COL)] for ref in row_refs[dl]])
            for rr in range(SUB):
                for dl in range(2):
                    r8 = rr if dl == 0 else SUB - 1 - rr
                    tp = b8s[dl] * SUB + r8
                    vecs = []
                    for vi in range(len(row_refs[dl])):
                        pieces = [jnp.broadcast_to(tiles[dl][vi][col][sl][r8:r8 + 1, :], (HEAD, COLW))
                                  for col in range(N_COL) for sl in range(2)]
                        vecs.append(jnp.concatenate(pieces, axis=0))
                    w4, kk4, b4, k4, r4 = vecs
                    s4 = st[dl]
                    sa = _dot((s4 * kk4).astype(BF16), ones)
                    vcol = _dot(vhl[dl], segt_ref[tp])
                    s_new = s4 * w4 - sa * b4 + vcol * k4
                    st[dl] = s_new
                    yacc[dl] += _dot((s_new * r4).astype(BF16), seg_ref[tp])
            return c2

        lax.fori_loop(0, STEP_BLK // SUB, step_body, 0)
        for dl in range(2):
            win = blks[dl] // BLK_PER_WIN
            wblk = blks[dl] % BLK_PER_WIN
            off = wblk * STEP_BLK
            for col in range(N_COL):
                for sl in range(2):
                    g = col * 2 + sl
                    ya = yacc[dl, g * HEAD:(g + 1) * HEAD, :]
                    for h in range(HPC):
                        ch0 = (col * HPC + h) * HEAD
                        moved = pltpu.roll(ya, (off + LANES - h * STEP_BLK) % LANES, axis=1)
                        cur = y_refs[dl][sl, win, ch0:ch0 + HEAD, :]
                        y_refs[dl][sl, win, ch0:ch0 + HEAD, :] = jnp.where(lane_blk == wblk, moved, cur)
        return carry

    for dl in range(2):
        y_refs[dl][...] = jnp.zeros(y_refs[dl].shape, F32)
    lax.fori_loop(0, BLK_PER_TILE, blk_body, 0)

    @pl.when(jnp.logical_or(step < LAT_STEP0, step == N_SCAN_STEPS - 1))
    def _final():
        for dl in range(2):
            for col in range(N_COL):
                for sl in range(2):
                    g = col * 2 + sl
                    sout_ref[sl, dl, :, col * COLW:(col + 1) * COLW] = st[dl, g * HEAD:(g + 1) * HEAD, :]


def _scan(r, k2, dec, kk, bvec, vt, s0_all, ones256, segt_all, seg_all):
    def row_spec(d):
        return pl.BlockSpec((None, 2 * TM, D_RWKV), lambda s: (d, _scan_block(s, d), 0))

    def vt_spec(d):
        return pl.BlockSpec((None, 2, N_WIN, D_RWKV, LANES), lambda s: (d, _scan_block(s, d), 0, 0, 0))

    def yt_spec(d):
        return pl.BlockSpec((2, N_WIN, D_RWKV, LANES), lambda s: (_scan_block(s, d), 0, 0, 0))

    rows = [r, k2, dec, kk, bvec]
    const = lambda shape: pl.BlockSpec(shape, lambda s: (0,) * len(shape))
    state_spec = pl.BlockSpec((2, 2, HEAD, D_RWKV), lambda s: (_scan_state_block(s), 0, 0, 0))
    yt_shape = jax.ShapeDtypeStruct((N_TILES, N_WIN, D_RWKV, LANES), F32)
    return pl.pallas_call(
        _scan_kernel,
        grid=(N_SCAN_STEPS,),
        in_specs=[row_spec(0)] * 5 + [vt_spec(0)] + [row_spec(1)] * 5 + [vt_spec(1)] + [
            state_spec, const((COLW, COLW)), const((STEP_BLK, 2 * LANES, COLW)),
            const((STEP_BLK, COLW, LANES))],
        out_specs=[yt_spec(0), yt_spec(1), state_spec],
        out_shape=[yt_shape, yt_shape,
                   jax.ShapeDtypeStruct(((LAT_STEP0 + 1) * 2, 2, HEAD, D_RWKV), F32)],
        scratch_shapes=[pltpu.VMEM((2, ROWS_D, COLW), F32),
                        pltpu.VMEM((2, ROWS_D, 2 * LANES), BF16),
                        pltpu.VMEM((2, ROWS_D, LANES), F32)],
        compiler_params=_cp(("arbitrary",)),
        name="rwkv_scan",
    )(*rows, vt, *rows, vt, s0_all, ones256, segt_all, seg_all)


HALO = 16


def _glu(z):
    return z[:, 0:D_CONV] * _sigmoid(z[:, D_CONV:])


def _conv_kernel(cur_ref, prev_ref, next_ref, w_ref, b_ref, g_ref, beta_ref, o_ref, ext):
    i = pl.program_id(0)
    first, last = _first_last(i)
    ext[0:HALO, :] = _glu(prev_ref[...]) * (1.0 - first.astype(F32))
    ext[HALO:HALO + TM, :] = _glu(cur_ref[...])
    ext[HALO + TM:, :] = _glu(next_ref[...]) * (1.0 - last.astype(F32))
    acc = jnp.zeros((TM, D_CONV), F32)
    for j in range(CONV_W):
        off = HALO - CONV_PAD + j
        acc = acc + ext[off:off + TM, :] * w_ref[j:j + 1, :]
    h = acc + b_ref[...]
    mu = jnp.mean(h, axis=-1, keepdims=True)
    hc = h - mu
    var = jnp.mean(hc * hc, axis=-1, keepdims=True)
    y = hc * lax.rsqrt(var + LN_EPS) * g_ref[...] + beta_ref[...]
    o_ref[...] = _silu(y)


def _conv(zc, conv_w, conv_b, ln_g, ln_b):
    nh = TM // HALO
    vec = pl.BlockSpec((1, D_CONV), lambda i: (0, 0))
    return pl.pallas_call(
        _conv_kernel,
        grid=(N_TILES,),
        in_specs=[pl.BlockSpec((TM, 2 * D_CONV), lambda i: (i, 0)),
                  pl.BlockSpec((HALO, 2 * D_CONV), lambda i: (_prev_tile(i) * nh + nh - 1, 0)),
                  pl.BlockSpec((HALO, 2 * D_CONV), lambda i: (_next_tile(i) * nh, 0)),
                  pl.BlockSpec((CONV_W + 1, D_CONV), lambda i: (0, 0)), vec, vec, vec],
        out_specs=pl.BlockSpec((TM, D_CONV), lambda i: (i, 0)),
        out_shape=jax.ShapeDtypeStruct((NT, D_CONV), F32),
        scratch_shapes=[pltpu.VMEM((TM + 2 * HALO, D_CONV), F32)],
        compiler_params=_cp(("parallel",)),
        name="conv_module",
    )(zc, zc, zc, conv_w, conv_b, ln_g, ln_b)


def _outproj_kernel(xp_ref, xs_ref, pe_ref, mod_ref, yf_ref, yb_ref, bf_ref, bb_ref, zg_ref, yc_ref,
                    gng_ref, gnb_ref, gl_ref, wo_ref, n2_ref, rw_ref, rb_ref, ones_ref,
                    x1_ref, h2_ref, comb_ref):
    i = pl.program_id(0)
    x = _load_x(i, xp_ref, xs_ref, pe_ref)
    g1 = mod_ref[:, 2 * D:3 * D]
    sh2 = mod_ref[:, 3 * D:4 * D]
    sc2 = mod_ref[:, 4 * D:5 * D]
    ones = ones_ref[...]
    yf = jnp.concatenate([yf_ref[wdw].T for wdw in range(N_WIN)], axis=0)
    yb = jnp.concatenate([yb_ref[wdw].T for wdw in range(N_WIN)], axis=0)
    y = (yf + bf_ref[...]) + (yb + bb_ref[...])
    mu = _dot_sel(y, ones) * (1.0 / HEAD)
    yc = y - mu
    var = _dot_sel(yc * yc, ones) * (1.0 / HEAD)
    yn = yc * lax.rsqrt(var + GN_EPS) * gng_ref[...] + gnb_ref[...]
    gate = _dot(_sigmoid(zg_ref[...]).astype(BF16), gl_ref[...])
    y_rwkv = (yn * gate).astype(BF16)
    mix = _dot(y_rwkv, wo_ref[0:D_RWKV, :]) + _dot(yc_ref[...].astype(BF16), wo_ref[D_RWKV:, :])
    x1 = x + g1 * mix
    x1_ref[...] = x1
    h2 = _rms(x1, n2_ref[...]) * (1.0 + sc2) + sh2
    h2_ref[...] = h2.astype(BF16)

    logits = _dot_hp(h2, rw_ref[...]) + rb_ref[...]
    lane = lax.broadcasted_iota(jnp.int32, (TM, ROUTE_LANES), 1)
    lanef = lane.astype(F32)
    neg = jnp.float32(-1e30)
    big = jnp.float32(1e9)
    gmask = lane < N_GROUPS
    gl = jnp.where(gmask, logits, neg)
    ge = jnp.where(gmask, jnp.exp(gl - jnp.max(gl, axis=-1, keepdims=True)), 0.0)
    gprob = ge / jnp.sum(ge, axis=-1, keepdims=True)
    gp = jnp.max(gprob, axis=-1, keepdims=True)
    gidx = jnp.min(jnp.where(jnp.logical_and(gmask, gprob == gp), lanef, big), axis=-1, keepdims=True)
    egrp = jnp.floor((lanef - float(E_LANE0)) * (1.0 / N_EXP_PER_GROUP))
    emask = jnp.logical_and(jnp.logical_and(lane >= E_LANE0, lane < E_LANE0 + N_EXPERTS), egrp == gidx)
    el = jnp.where(emask, logits, neg)
    ee = jnp.where(emask, jnp.exp(el - jnp.max(el, axis=-1, keepdims=True)), 0.0)
    ep = ee / jnp.sum(ee, axis=-1, keepdims=True)
    m1 = jnp.max(jnp.where(emask, ep, -1.0), axis=-1, keepdims=True)
    i1 = jnp.min(jnp.where(jnp.logical_and(emask, ep == m1), lanef, big), axis=-1, keepdims=True)
    mask2 = jnp.logical_and(emask, lanef != i1)
    m2 = jnp.max(jnp.where(mask2, ep, -1.0), axis=-1, keepdims=True)
    i2 = jnp.min(jnp.where(jnp.logical_and(mask2, ep == m2), lanef, big), axis=-1, keepdims=True)
    den = m1 + m2
    comb_ref[...] = (jnp.where(lanef == i1, gp * (m1 / den), 0.0)
                     + jnp.where(lanef == i2, gp * (m2 / den), 0.0))


def _outproj(xp, xs, pe, mod3, yf, yb, bonus, zg, yconv, gn_g, gn_b, gate_bf, w_out_bf, norm2_g,
             router_w, router_b, ones512):
    tile = lambda n: pl.BlockSpec((TM, n), lambda i: (i, 0))
    const = lambda shape: pl.BlockSpec(shape, lambda i: (0,) * len(shape))
    return pl.pallas_call(
        _outproj_kernel,
        grid=(N_TILES,),
        in_specs=_x_specs() + [
            pl.BlockSpec((None, 1, N_MOD * D), lambda i: (_mod_row(i), 0, 0)),
            pl.BlockSpec((None, N_WIN, D_RWKV, LANES), lambda i: (i, 0, 0, 0)),
            pl.BlockSpec((None, N_WIN, D_RWKV, LANES), lambda i: (i, 0, 0, 0)),
            pl.BlockSpec((None, TM, D_RWKV), lambda i: (0, i, 0)),
            pl.BlockSpec((None, TM, D_RWKV), lambda i: (1, i, 0)),
            tile(LORA_G), tile(D_CONV),
            const((1, D_RWKV)), const((1, D_RWKV)), const((LORA_G, D_RWKV)), const((D, D)),
            const((1, D)), const((D, ROUTE_LANES)), const((1, ROUTE_LANES)),
            const((D_RWKV, D_RWKV))],
        out_specs=[tile(D), tile(D), tile(ROUTE_LANES)],
        out_shape=[jax.ShapeDtypeStruct((NT, D), F32), jax.ShapeDtypeStruct((NT, D), BF16),
                   jax.ShapeDtypeStruct((NT, ROUTE_LANES), F32)],
        compiler_params=_cp(("parallel",)),
        name="outproj_router",
    )(xp, xs, pe, mod3, yf, yb, bonus, bonus, zg, yconv, gn_g, gn_b, gate_bf, w_out_bf, norm2_g,
      router_w, router_b, ones512)


def _moe_kernel(h_ref, comb_ref, wg_ref, wu_ref, wd_ref, o_ref):
    e = pl.program_id(1)

    @pl.when(e == 0)
    def _():
        o_ref[...] = jnp.zeros(o_ref.shape, F32)

    h = h_ref[...]
    lane = lax.broadcasted_iota(jnp.int32, (TM_MOE, ROUTE_LANES), 1)
    cw = jnp.sum(jnp.where(lane == e + E_LANE0, comb_ref[...], 0.0), axis=-1, keepdims=True)
    gate = _dot(h, wg_ref[...].astype(BF16))
    up = _dot(h, wu_ref[...].astype(BF16))
    hid = (_silu(gate) * up * cw).astype(BF16)
    o_ref[...] += _dot(hid, wd_ref[...].astype(BF16))


def _moe(h2, comb, wg, wu, wd):
    return pl.pallas_call(
        _moe_kernel,
        grid=(NT // TM_MOE, N_EXPERTS),
        in_specs=[pl.BlockSpec((TM_MOE, D), lambda t, e: (t, 0)),
                  pl.BlockSpec((TM_MOE, ROUTE_LANES), lambda t, e: (t, 0)),
                  pl.BlockSpec((None, D, D_EXPERT), lambda t, e: (e, 0, 0)),
                  pl.BlockSpec((None, D, D_EXPERT), lambda t, e: (e, 0, 0)),
                  pl.BlockSpec((None, D_EXPERT, D), lambda t, e: (e, 0, 0))],
        out_specs=pl.BlockSpec((TM_MOE, D), lambda t, e: (t, 0)),
        out_shape=jax.ShapeDtypeStruct((NT, D), F32),
        compiler_params=_cp(("parallel", "arbitrary")),
        name="moe_experts",
    )(h2, comb, wg, wu, wd)


def _final_kernel(x1_ref, moe_ref, mod_ref, g_ref, o_ref):
    g2 = mod_ref[:, 5 * D:6 * D]
    x2 = x1_ref[...] + g2 * moe_ref[...]
    o_ref[...] = _rms(x2, g_ref[...])


def _final(x1, moe, mod3, final_g, tile0, n_tiles, out_block, name):
    tile = pl.BlockSpec((TM, D), lambda i: (tile0 + i, 0))
    return pl.pallas_call(
        _final_kernel,
        grid=(n_tiles,),
        in_specs=[tile, tile, pl.BlockSpec((None, 1, N_MOD * D), lambda i: (_mod_row(tile0 + i), 0, 0)),
                  pl.BlockSpec((1, D), lambda i: (0, 0))],
        out_specs=pl.BlockSpec((TM, D), lambda i: (out_block(tile0 + i), 0)),
        out_shape=jax.ShapeDtypeStruct((n_tiles * TM, D), F32),
        compiler_params=_cp(("parallel",)),
        name=name,
    )(x1, moe, mod3, final_g)


def _pos_embed(rows):
    t = jnp.arange(rows * GRID_W)
    row = (t // GRID_W).astype(F32)
    col = (t % GRID_W).astype(F32)
    quarter = D // 4
    freqs = 1.0 / (10000.0 ** (jnp.arange(quarter, dtype=F32) / quarter))
    ang_r = row[:, None] * freqs[None, :]
    ang_c = col[:, None] * freqs[None, :]
    return jnp.concatenate([jnp.sin(ang_r), jnp.cos(ang_r), jnp.sin(ang_c), jnp.cos(ang_c)], axis=-1)


def _selection_constants():
    lane = np.arange(LANES)
    colw = np.arange(COLW)
    ones256 = (colw[:, None] // HEAD == colw[None, :] // HEAD).astype(np.float32)
    ch = np.arange(D_RWKV)
    ones512 = (ch[:, None] // HEAD == ch[None, :] // HEAD).astype(np.float32)
    hl, tprime = lane // STEP_BLK, lane % STEP_BLK
    base = (hl[:, None] == (colw[None, :] // HEAD)).astype(np.float32)
    segt = np.stack([base * (tprime[:, None] == t) for t in range(STEP_BLK)])
    seg = np.transpose(segt, (0, 2, 1))
    segt = np.concatenate([segt, segt], axis=1)
    return (jnp.asarray(ones256, BF16), jnp.asarray(ones512, BF16), jnp.asarray(segt, BF16),
            jnp.asarray(seg, BF16))


def kernel(x_prompt, x_sample, state_rwkv, c, c_ctx, ada_w, ada_b, norm1_g, w_in, tshift_mu, decay_w0, decay_lora_b, iclr_a0, iclr_lora_b, key_k, key_a, bonus_r_k, gate_lora_b, gn_g, gn_b, conv_dw_w, conv_dw_b, conv_ln_g, conv_ln_b, w_out, norm2_g, router_group_w, router_group_b, router_expert_w, router_expert_b, expert_w_gate, expert_w_up, expert_w_down, final_norm_g):
    assert x_prompt.shape == (N_CTX_SEQ, T_CTX, D) and x_sample.shape == (N_LAT_SEQ, T_LAT, D)
    assert ada_w.shape[0] == 1, "one trunk layer"
    ones256, ones512, segt_all, seg_all = _selection_constants()
    xp = x_prompt.reshape(N_CTX_SEQ * T_CTX, D)
    xs = x_sample.reshape(N_LAT_SEQ * T_LAT, D)
    pe = _pos_embed(T_LAT // GRID_W)

    cond8 = jnp.concatenate([c_ctx[None, :], c, jnp.zeros((8 - 1 - N_LAT_SEQ, D), F32)], axis=0)
    mod3 = _adaln(cond8, ada_w[0], ada_b[0][None, :]).reshape(8, 1, N_MOD * D)

    zs, zg, zc = _inproj(xp, xs, pe, mod3, norm1_g, w_in[0].astype(BF16))

    zero = jnp.zeros((2, LORA, D_RWKV), F32)
    lora2 = jnp.concatenate([jnp.concatenate([decay_lora_b[0], zero], axis=2),
                             jnp.concatenate([zero, iclr_lora_b[0]], axis=2)], axis=1)
    vec = lambda p: p.reshape(2, 1, -1)
    r, k2, dec, kk, bvec, vt, bonus = _prep(zs, vec(tshift_mu[0]), vec(decay_w0[0]), vec(iclr_a0[0]), lora2,
                                           vec(key_k[0]), vec(key_a[0]), vec(bonus_r_k[0]), ones512)

    s0 = state_rwkv[:, 0].transpose(0, 1, 3, 2, 4).reshape(N_LAT_SEQ, 2, HEAD, D_RWKV)
    s0_all = jnp.concatenate([jnp.zeros((N_CTX_SEQ, 2, HEAD, D_RWKV), F32), s0], axis=0)
    ytf, ytb, s_fin = _scan(r, k2, dec, kk, bvec, vt, s0_all, ones256, segt_all, seg_all)

    yconv = _conv(zc, jnp.concatenate([conv_dw_w[0], jnp.zeros((1, D_CONV), F32)], axis=0),
                  conv_dw_b, conv_ln_g, conv_ln_b)

    router_w = jnp.concatenate([router_group_w[0], router_expert_w[0],
                                jnp.zeros((D, ROUTE_LANES - N_GROUPS - N_EXPERTS), F32)], axis=1)
    router_b = jnp.concatenate([router_group_b[0], router_expert_b[0],
                                jnp.zeros((ROUTE_LANES - N_GROUPS - N_EXPERTS,), F32)])[None, :]
    x1, h2, comb = _outproj(xp, xs, pe, mod3, ytf, ytb, bonus, zg, yconv,
                            gn_g, gn_b, gate_lora_b[0].astype(BF16), w_out[0].astype(BF16), norm2_g,
                            router_w, router_b, ones512)

    moe = _moe(h2, comb, expert_w_gate[0], expert_w_up[0], expert_w_down[0])
    fg = final_norm_g[None, :]
    y_prompt = _final(x1, moe, mod3, fg, 0, N_CTX_TILES, lambda i: i, "final_norm_ctx")
    y_sample = _final(x1, moe, mod3, fg, N_CTX_TILES, N_LAT_TILES, _xs_block, "final_norm_lat")
    y_prompt = y_prompt.reshape(N_CTX_SEQ, T_CTX, D)
    y_sample = y_sample.reshape(N_LAT_SEQ, T_LAT, D)
    s_ctx = s_fin[:N_CTX_SEQ].reshape(N_CTX_SEQ, 2, HEAD, N_HEADS, HEAD).transpose(0, 1, 3, 2, 4)
    new_state = s_ctx[:, None].astype(state_rwkv.dtype)
    return (y_prompt, y_sample, new_state)
```

```python
import numpy as np
import jax
import jax.numpy as jnp
from jax import lax
from jax.experimental import pallas as pl
from jax.experimental.pallas import tpu as pltpu

F32 = jnp.float32
BF16 = jnp.bfloat16

D = 1024
N_CTX_SEQ = 16
T_CTX = 256
N_LAT_SEQ = 2
T_LAT = 1024
TM = 256
N_CTX_TILES = N_CTX_SEQ * T_CTX // TM
LAT_CHUNKS = T_LAT // TM
N_LAT_TILES = N_LAT_SEQ * LAT_CHUNKS
N_TILES = N_CTX_TILES + N_LAT_TILES
NT = N_TILES * TM
GRID_W = 64
D_RWKV = 512
D_CONV = 512
HEAD = 64
N_HEADS = 8
CONV_W = 31
CONV_PAD = CONV_W // 2
LORA = 64
LORA_G = 128
SHIFT_COLS = 3 * D_RWKV + 2 * LORA
N_GROUPS = 4
N_EXP_PER_GROUP = 8
N_EXPERTS = 32
D_EXPERT = 256
N_MOD = 6
RMS_EPS = 1e-6
LN_EPS = 1e-5
GN_EPS = 64e-5
LANES = 128
SUB = 8
CH = 32
NCH = TM // CH
HC = N_HEADS * CH
DOUBLINGS = 4
ROUTE_LANES = 128
E_LANE0 = N_GROUPS
TM_MOE = 1024
VMEM_LIMIT = 56 * 1024 * 1024


def _cp(sem, flags=None):
    return pltpu.CompilerParams(dimension_semantics=sem, vmem_limit_bytes=VMEM_LIMIT, flags=flags)


def _split2(a):
    hi = a.astype(BF16)
    lo = (a - hi.astype(F32)).astype(BF16)
    return hi, lo


def _split3(a):
    hi = a.astype(BF16)
    r1 = a - hi.astype(F32)
    mid = r1.astype(BF16)
    lo = (r1 - mid.astype(F32)).astype(BF16)
    return hi, mid, lo


def _dot(a, b):
    return jnp.dot(a, b, preferred_element_type=F32)


def _dot_hp(a, b):
    ah, al = _split2(a)
    bh, bl = _split2(b)
    return _dot(ah, bh) + _dot(ah, bl) + _dot(al, bh)


def _dot_sel(a, sel):
    h, m, l = _split3(a)
    return _dot(h, sel) + _dot(m, sel) + _dot(l, sel)


def _sel_dot(sel, a):
    h, m, l = _split3(a)
    return _dot(sel, h) + _dot(sel, m) + _dot(sel, l)


def _sigmoid(x):
    return 1.0 / (1.0 + jnp.exp(-x))


def _silu(x):
    return x * _sigmoid(x)


def _lat_js(i):
    il = jnp.maximum(i - N_CTX_TILES, 0)
    return il // N_LAT_SEQ, il % N_LAT_SEQ


def _xp_block(i):
    return jnp.minimum(i, N_CTX_TILES - 1)


def _xs_block(i):
    j, s = _lat_js(i)
    return s * LAT_CHUNKS + j


def _pe_block(i):
    j, _ = _lat_js(i)
    return j


def _mod_row(i):
    _, s = _lat_js(i)
    return jnp.where(i < N_CTX_TILES, 0, 1 + s)


def _first_last(i):
    j, _ = _lat_js(i)
    is_ctx = i < N_CTX_TILES
    first = jnp.logical_or(is_ctx, j == 0)
    last = jnp.logical_or(is_ctx, j == LAT_CHUNKS - 1)
    return first, last


def _prev_tile(i):
    first, _ = _first_last(i)
    return jnp.where(first, i, i - N_LAT_SEQ)


def _next_tile(i):
    _, last = _first_last(i)
    return jnp.where(last, i, i + N_LAT_SEQ)


def _adaln_kernel(c_ref, w_ref, b_ref, o_ref):
    c = c_ref[...]
    o_ref[...] = _dot_hp(_silu(c), w_ref[...]) + b_ref[...]


def _adaln(cond8, ada_w, ada_b):
    tn = 1536
    n = ada_w.shape[1]
    return pl.pallas_call(
        _adaln_kernel,
        grid=(n // tn,),
        in_specs=[pl.BlockSpec((8, D), lambda j: (0, 0)),
                  pl.BlockSpec((D, tn), lambda j: (0, j)),
                  pl.BlockSpec((1, tn), lambda j: (0, j))],
        out_specs=pl.BlockSpec((8, tn), lambda j: (0, j)),
        out_shape=jax.ShapeDtypeStruct((8, n), F32),
        compiler_params=_cp(("parallel",)),
        name="adaln",
    )(cond8, ada_w, ada_b)


def _load_x(i, xp_ref, xs_ref, pe_ref):
    f = (i >= N_CTX_TILES).astype(F32)
    return xp_ref[...] * (1.0 - f) + (xs_ref[...] + pe_ref[...]) * f


def _x_specs():
    return [pl.BlockSpec((TM, D), lambda i: (_xp_block(i), 0)),
            pl.BlockSpec((TM, D), lambda i: (_xs_block(i), 0)),
            pl.BlockSpec((TM, D), lambda i: (_pe_block(i), 0))]


def _rms(x, g):
    return x * lax.rsqrt(jnp.mean(x * x, axis=-1, keepdims=True) + RMS_EPS) * g


def _inproj_kernel(xp_ref, xs_ref, pe_ref, mod_ref, g_ref, w_ref, zs_ref, zg_ref, zc_ref):
    i = pl.program_id(0)
    x = _load_x(i, xp_ref, xs_ref, pe_ref)
    sh1 = mod_ref[:, 0:D]
    sc1 = mod_ref[:, D:2 * D]
    h = (_rms(x, g_ref[...]) * (1.0 + sc1) + sh1).astype(BF16)
    zs_ref[...] = _dot(h, w_ref[:, 0:SHIFT_COLS])
    zg_ref[...] = _dot(h, w_ref[:, SHIFT_COLS:SHIFT_COLS + LORA_G])
    zc_ref[...] = _dot(h, w_ref[:, SHIFT_COLS + LORA_G:])


def _inproj(xp, xs, pe, mod3, norm1_g, w_in_bf):
    in_cols = w_in_bf.shape[1]
    return pl.pallas_call(
        _inproj_kernel,
        grid=(N_TILES,),
        in_specs=_x_specs() + [
            pl.BlockSpec((None, 1, N_MOD * D), lambda i: (_mod_row(i), 0, 0)),
            pl.BlockSpec((1, D), lambda i: (0, 0)),
            pl.BlockSpec((D, in_cols), lambda i: (0, 0))],
        out_specs=[pl.BlockSpec((TM, SHIFT_COLS), lambda i: (i, 0)),
                   pl.BlockSpec((TM, LORA_G), lambda i: (i, 0)),
                   pl.BlockSpec((TM, 2 * D_CONV), lambda i: (i, 0))],
        out_shape=[jax.ShapeDtypeStruct((NT, SHIFT_COLS), F32),
                   jax.ShapeDtypeStruct((NT, LORA_G), F32),
                   jax.ShapeDtypeStruct((NT, 2 * D_CONV), F32)],
        compiler_params=_cp(("parallel",)),
        name="inproj",
    )(xp, xs, pe, mod3, norm1_g, w_in_bf)


NT_DIMS = (((1,), (1,)), ((), ()))


def _tile_rows(x):
    return jnp.concatenate([x] * N_HEADS, axis=0)


def _prep_kernel(zs_ref, prev_ref, next_ref, mu_ref, w0_ref, a0_ref, lora_ref, kk_ref_, ka_ref,
                 rk_ref, ones_ref, cum_ref, tot_ref, sel_ref, hm_ref, bm_ref, msk_ref, eye_ref,
                 at_out, rt_out, bh_out, kh_out, v_out, t_out, aak_out, arb_out, ark_out, gc_out, bon_out):
    i = pl.program_id(0)
    d = pl.program_id(1)
    first, last = _first_last(i)
    cur = zs_ref[...]
    prow = prev_ref[7:8, :] * (1.0 - first.astype(F32))
    nrow = next_ref[0:1, :] * (1.0 - last.astype(F32))
    rows = lax.broadcasted_iota(jnp.int32, (TM, 1), 0)
    down = jnp.where(rows == 0, prow, pltpu.roll(cur, 1, axis=0))
    up = jnp.where(rows == TM - 1, nrow, pltpu.roll(cur, TM - 1, axis=0))
    fd = (d == 0).astype(F32)
    prev = down * fd + up * (1.0 - fd)
    xs = cur + (prev - cur) * mu_ref[...]
    r = xs[:, 0:D_RWKV]
    k = xs[:, D_RWKV:2 * D_RWKV]
    v = xs[:, 2 * D_RWKV:3 * D_RWKV]
    z2 = xs[:, 3 * D_RWKV:SHIFT_COLS]
    lane = lax.broadcasted_iota(jnp.int32, (TM, 2 * LORA), 1)
    lin = jnp.where(lane < LORA, jnp.tanh(z2), z2)
    lo = _dot_hp(lin, lora_ref[...])
    u = -(w0_ref[...] + lo[:, 0:D_RWKV])
    softplus = jnp.maximum(u, 0.0) + jnp.log1p(jnp.exp(-jnp.abs(u)))
    w_log = -softplus - 0.5
    lw = -jnp.exp(w_log)
    a = _sigmoid(a0_ref[...] + lo[:, D_RWKV:])
    kx = k * kk_ref_[...]
    ones = ones_ref[...]
    nrm = jnp.sqrt(_dot_sel(kx * kx, ones))
    kk = kx / jnp.maximum(nrm, 1e-12)
    k2 = k * (1.0 + (a - 1.0) * ka_ref[...])
    bvec = kk * a
    bon_out[...] = _dot_sel(r * k2 * rk_ref[...], ones) * v

    lg = _sel_dot(cum_ref[...], lw)
    tot = _sel_dot(tot_ref[...], lw)
    e_tail = jnp.exp(tot - lg)
    e_inv = jnp.exp(-lg)
    at = (-(kk * jnp.exp(lg - lw))).astype(BF16)
    bt = (bvec * e_inv).astype(BF16)
    kt = (k2 * e_inv).astype(BF16)
    rt = (r * jnp.exp(lg)).astype(BF16)
    at_out[...] = at
    rt_out[...] = rt
    bh_out[...] = (bvec * e_tail).astype(BF16)
    kh_out[...] = (k2 * e_tail).astype(BF16)
    v_out[...] = v.astype(BF16)
    gc_out[...] = jnp.exp(_sel_dot(sel_ref[...], lw))

    hm = hm_ref[...]
    bm = bm_ref[...]
    m_strict = msk_ref[0]
    m_incl = msk_ref[1]

    def expand(z):
        return _tile_rows(z.astype(BF16)) * bm

    chunks = [slice(c * CH, (c + 1) * CH) for c in range(NCH)]
    pws = []
    for rows in chunks:
        ar = jnp.concatenate([at[rows], rt[rows]], axis=0)
        bk = jnp.concatenate([_tile_rows(bt[rows]) * hm, _tile_rows(kt[rows]) * hm], axis=0)
        p1 = lax.dot_general(ar, bk, NT_DIMS, preferred_element_type=F32)
        pws.append(p1[0:CH, 0:HC] * m_strict)
        aak_out[rows, :] = (p1[0:CH, HC:] * m_strict).astype(BF16)
        arb_out[rows, :] = (p1[CH:, 0:HC] * m_incl).astype(BF16)
        ark_out[rows, :] = (p1[CH:, HC:] * m_incl).astype(BF16)
    tms = [eye_ref[...] + pw for pw in pws]
    for _ in range(DOUBLINGS):
        pws = [_dot(pw.astype(BF16), expand(pw)) for pw in pws]
        tms = [tm + _dot(tm.astype(BF16), expand(pw)) for tm, pw in zip(tms, pws)]
    for rows, tm in zip(chunks, tms):
        t_out[rows, :] = tm.astype(BF16)


def _prep(zs, mu, w0, a0, lora2, key_k, key_a, r_k, ones512, cum2, tot, sel8, hm, bm, masks, eye):
    pvec = lambda n: pl.BlockSpec((None, 1, n), lambda i, d: (d, 0, 0))
    out_spec = pl.BlockSpec((None, TM, D_RWKV), lambda i, d: (d, i, 0))
    bf_shape = jax.ShapeDtypeStruct((2, NT, D_RWKV), BF16)
    tall_spec = pl.BlockSpec((None, TM, HC), lambda i, d: (d, i, 0))
    rows8 = TM // 8
    return pl.pallas_call(
        _prep_kernel,
        grid=(N_TILES, 2),
        in_specs=[pl.BlockSpec((TM, SHIFT_COLS), lambda i, d: (i, 0)),
                  pl.BlockSpec((8, SHIFT_COLS), lambda i, d: (_prev_tile(i) * rows8 + rows8 - 1, 0)),
                  pl.BlockSpec((8, SHIFT_COLS), lambda i, d: (_next_tile(i) * rows8, 0)),
                  pvec(SHIFT_COLS), pvec(D_RWKV), pvec(D_RWKV),
                  pl.BlockSpec((None, 2 * LORA, 2 * D_RWKV), lambda i, d: (d, 0, 0)),
                  pvec(D_RWKV), pvec(D_RWKV), pvec(D_RWKV),
                  pl.BlockSpec((D_RWKV, D_RWKV), lambda i, d: (0, 0)),
                  pl.BlockSpec((None, TM, TM), lambda i, d: (d, 0, 0)),
                  pl.BlockSpec((TM, TM), lambda i, d: (0, 0)),
                  pl.BlockSpec((NCH, TM), lambda i, d: (0, 0)),
                  pl.BlockSpec((HC, D_RWKV), lambda i, d: (0, 0)),
                  pl.BlockSpec((HC, HC), lambda i, d: (0, 0)),
                  pl.BlockSpec((None, 2, CH, HC), lambda i, d: (d, 0, 0, 0)),
                  pl.BlockSpec((CH, HC), lambda i, d: (0, 0))],
        out_specs=[out_spec] * 5 + [tall_spec] * 4 + [
            pl.BlockSpec((None, NCH, D_RWKV), lambda i, d: (d, i, 0)), out_spec],
        out_shape=[bf_shape] * 5 + [jax.ShapeDtypeStruct((2, NT, HC), BF16)] * 4 + [
            jax.ShapeDtypeStruct((2, N_TILES * NCH, D_RWKV), F32),
            jax.ShapeDtypeStruct((2, NT, D_RWKV), F32)],
        compiler_params=_cp(("parallel", "parallel")),
        name="rwkv_prep",
    )(zs, zs, zs, mu, w0, a0, lora2, key_k, key_a, r_k, ones512, cum2, tot, sel8, hm, bm, masks, eye)


N_SCAN_STEPS = N_CTX_TILES // 2 + LAT_CHUNKS
LAT_STEP0 = N_CTX_TILES // 2


def _scan_block(step, d):
    lat = step >= LAT_STEP0
    jl = step - LAT_STEP0
    j = jnp.where(d == 0, jl, LAT_CHUNKS - 1 - jl)
    return jnp.where(lat, LAT_STEP0 + j, step)


def _scan_state_block(step):
    return jnp.minimum(step, LAT_STEP0)


def _chunk_kernel(*refs):
    (atf, rtf, bhf, khf, vf, tf, aakf, arbf, arkf, gcf, atb, rtb, bhb, khb, vb, tb, aakb, arbb, arkb, gcb,
     s0_ref, hm_ref, bd_ref, fold_ref, yf_ref, yb_ref, sout_ref, mw) = refs
    step = pl.program_id(0)
    in_refs = ((atf, rtf, bhf, khf, vf, tf, aakf, arbf, arkf, gcf),
               (atb, rtb, bhb, khb, vb, tb, aakb, arbb, arkb, gcb))
    y_refs = (yf_ref, yb_ref)

    @pl.when(step < LAT_STEP0)
    def _zero_state():
        mw[...] = jnp.zeros(mw.shape, F32)

    @pl.when(step == LAT_STEP0)
    def _load_state():
        mw[...] = s0_ref[...]

    hm = hm_ref[...]
    bd = bd_ref[...]
    crow = lax.broadcasted_iota(jnp.int32, (NCH, 1), 0)

    def chunk_body(q, carry):
        chains = []
        for dl in range(2):
            cc = q if dl == 0 else NCH - 1 - q
            for sl in range(2):
                chains.append((dl, sl, cc, pl.ds(pl.multiple_of(sl * TM + cc * CH, CH), CH)))
        ld = lambda idx, dl, rows: in_refs[dl][idx][rows, :]
        m0s = [mw[dl, sl] for dl, sl, _, _ in chains]
        xy0s = [_dot(jnp.concatenate([ld(0, dl, rows), ld(1, dl, rows)], axis=0), m0.astype(BF16))
                for (dl, _, _, rows), m0 in zip(chains, m0s)]
        vvs = [ld(4, dl, rows) for dl, _, _, rows in chains]
        avs = [_dot(jnp.concatenate([ld(6, dl, rows), ld(8, dl, rows)], axis=0), _tile_rows(vv) * hm)
               for (dl, _, _, rows), vv in zip(chains, vvs)]
        ubs = [_dot(ld(5, dl, rows), _tile_rows((xy0[0:CH] + av[0:CH]).astype(BF16)) * hm).astype(BF16)
               for (dl, _, _, rows), xy0, av in zip(chains, xy0s, avs)]
        for (dl, _, _, rows), xy0, av, ub in zip(chains, xy0s, avs, ubs):
            y_refs[dl][rows, :] = xy0[CH:] + av[CH:] + _dot(ld(7, dl, rows), _tile_rows(ub) * hm)
        for (dl, sl, cc, rows), m0, ub, vv in zip(chains, m0s, ubs, vvs):
            gcrow = jnp.sum(jnp.where(crow == cc, in_refs[dl][9][sl * NCH:(sl + 1) * NCH, :], 0.0),
                            axis=0, keepdims=True)
            pad = jnp.zeros((LANES - 2 * CH - SUB, D_RWKV), F32)
            stack = jnp.concatenate([ld(2, dl, rows).astype(F32), ld(3, dl, rows).astype(F32),
                                     jnp.broadcast_to(gcrow, (SUB, D_RWKV)), pad], axis=0)
            stack_t = stack.T
            uv = jnp.concatenate([ub, vv, jnp.zeros((LANES - 2 * CH, D_RWKV), BF16)], axis=0)
            upd = _dot(stack_t.astype(BF16), uv)
            mw[dl, sl] = (m0 * stack_t[:, 2 * CH:2 * CH + 1] + upd) * bd
        return carry

    lax.fori_loop(0, NCH, chunk_body, 0)

    @pl.when(jnp.logical_or(step < LAT_STEP0, step == N_SCAN_STEPS - 1))
    def _final():
        for dl in range(2):
            for sl in range(2):
                sout_ref[sl, dl] = _dot_sel(mw[dl, sl], fold_ref[...])


def _scan(prep_out, s0_wide, consts):
    def row_spec(d, width):
        return pl.BlockSpec((None, 2 * TM, width), lambda s: (d, _scan_block(s, d), 0))

    def gc_spec(d):
        return pl.BlockSpec((None, 2 * NCH, D_RWKV), lambda s: (d, _scan_block(s, d), 0))

    def y_spec(d):
        return pl.BlockSpec((2 * TM, D_RWKV), lambda s: (_scan_block(s, d), 0))

    rows, gc = list(prep_out[:9]), prep_out[9]
    row_specs = lambda d: [row_spec(d, D_RWKV)] * 5 + [row_spec(d, HC)] * 4 + [gc_spec(d)]
    const = lambda a: pl.BlockSpec(a.shape, lambda s: (0,) * a.ndim)
    y_shape = jax.ShapeDtypeStruct((NT, D_RWKV), F32)
    return pl.pallas_call(
        _chunk_kernel,
        grid=(N_SCAN_STEPS,),
        in_specs=row_specs(0) + row_specs(1) + [const(s0_wide)] + [const(a) for a in consts],
        out_specs=[y_spec(0), y_spec(1),
                   pl.BlockSpec((2, 2, D_RWKV, HEAD), lambda s: (_scan_state_block(s), 0, 0, 0))],
        out_shape=[y_shape, y_shape,
                   jax.ShapeDtypeStruct(((LAT_STEP0 + 1) * 2, 2, D_RWKV, HEAD), F32)],
        scratch_shapes=[pltpu.VMEM((2, 2, D_RWKV, D_RWKV), F32)],
        compiler_params=_cp(("arbitrary",)),
        name="rwkv_scan",
    )(*rows, gc, *rows, gc, s0_wide, *consts)


HALO = 16


def _glu(z):
    return z[:, 0:D_CONV] * _sigmoid(z[:, D_CONV:])


def _conv_kernel(cur_ref, prev_ref, next_ref, w_ref, b_ref, g_ref, beta_ref, o_ref, ext):
    i = pl.program_id(0)
    first, last = _first_last(i)
    ext[0:HALO, :] = _glu(prev_ref[...]) * (1.0 - first.astype(F32))
    ext[HALO:HALO + TM, :] = _glu(cur_ref[...])
    ext[HALO + TM:, :] = _glu(next_ref[...]) * (1.0 - last.astype(F32))
    acc = jnp.zeros((TM, D_CONV), F32)
    for j in range(CONV_W):
        off = HALO - CONV_PAD + j
        acc = acc + ext[off:off + TM, :] * w_ref[j:j + 1, :]
    h = acc + b_ref[...]
    mu = jnp.mean(h, axis=-1, keepdims=True)
    hc = h - mu
    var = jnp.mean(hc * hc, axis=-1, keepdims=True)
    y = hc * lax.rsqrt(var + LN_EPS) * g_ref[...] + beta_ref[...]
    o_ref[...] = _silu(y)


def _conv(zc, conv_w, conv_b, ln_g, ln_b):
    nh = TM // HALO
    vec = pl.BlockSpec((1, D_CONV), lambda i: (0, 0))
    return pl.pallas_call(
        _conv_kernel,
        grid=(N_TILES,),
        in_specs=[pl.BlockSpec((TM, 2 * D_CONV), lambda i: (i, 0)),
                  pl.BlockSpec((HALO, 2 * D_CONV), lambda i: (_prev_tile(i) * nh + nh - 1, 0)),
                  pl.BlockSpec((HALO, 2 * D_CONV), lambda i: (_next_tile(i) * nh, 0)),
                  pl.BlockSpec((CONV_W + 1, D_CONV), lambda i: (0, 0)), vec, vec, vec],
        out_specs=pl.BlockSpec((TM, D_CONV), lambda i: (i, 0)),
        out_shape=jax.ShapeDtypeStruct((NT, D_CONV), F32),
        scratch_shapes=[pltpu.VMEM((TM + 2 * HALO, D_CONV), F32)],
        compiler_params=_cp(("parallel",)),
        name="conv_module",
    )(zc, zc, zc, conv_w, conv_b, ln_g, ln_b)


def _outproj_kernel(xp_ref, xs_ref, pe_ref, mod_ref, yf_ref, yb_ref, bf_ref, bb_ref, zg_ref, yc_ref,
                    gng_ref, gnb_ref, gl_ref, wo_ref, n2_ref, rw_ref, rb_ref, ones_ref,
                    x1_ref, h2_ref, comb_ref):
    i = pl.program_id(0)
    x = _load_x(i, xp_ref, xs_ref, pe_ref)
    g1 = mod_ref[:, 2 * D:3 * D]
    sh2 = mod_ref[:, 3 * D:4 * D]
    sc2 = mod_ref[:, 4 * D:5 * D]
    ones = ones_ref[...]
    y = (yf_ref[...] + bf_ref[...]) + (yb_ref[...] + bb_ref[...])
    mu = _dot_sel(y, ones) * (1.0 / HEAD)
    yc = y - mu
    var = _dot_sel(yc * yc, ones) * (1.0 / HEAD)
    yn = yc * lax.rsqrt(var + GN_EPS) * gng_ref[...] + gnb_ref[...]
    gate = _dot(_sigmoid(zg_ref[...]).astype(BF16), gl_ref[...])
    y_rwkv = (yn * gate).astype(BF16)
    mix = _dot(y_rwkv, wo_ref[0:D_RWKV, :]) + _dot(yc_ref[...].astype(BF16), wo_ref[D_RWKV:, :])
    x1 = x + g1 * mix
    x1_ref[...] = x1
    h2 = _rms(x1, n2_ref[...]) * (1.0 + sc2) + sh2
    h2_ref[...] = h2.astype(BF16)

    logits = _dot_hp(h2, rw_ref[...]) + rb_ref[...]
    lane = lax.broadcasted_iota(jnp.int32, (TM, ROUTE_LANES), 1)
    lanef = lane.astype(F32)
    neg = jnp.float32(-1e30)
    big = jnp.float32(1e9)
    gmask = lane < N_GROUPS
    gl = jnp.where(gmask, logits, neg)
    ge = jnp.where(gmask, jnp.exp(gl - jnp.max(gl, axis=-1, keepdims=True)), 0.0)
    gprob = ge / jnp.sum(ge, axis=-1, keepdims=True)
    gp = jnp.max(gprob, axis=-1, keepdims=True)
    gidx = jnp.min(jnp.where(jnp.logical_and(gmask, gprob == gp), lanef, big), axis=-1, keepdims=True)
    egrp = jnp.floor((lanef - float(E_LANE0)) * (1.0 / N_EXP_PER_GROUP))
    emask = jnp.logical_and(jnp.logical_and(lane >= E_LANE0, lane < E_LANE0 + N_EXPERTS), egrp == gidx)
    el = jnp.where(emask, logits, neg)
    ee = jnp.where(emask, jnp.exp(el - jnp.max(el, axis=-1, keepdims=True)), 0.0)
    ep = ee / jnp.sum(ee, axis=-1, keepdims=True)
    m1 = jnp.max(jnp.where(emask, ep, -1.0), axis=-1, keepdims=True)
    i1 = jnp.min(jnp.where(jnp.logical_and(emask, ep == m1), lanef, big), axis=-1, keepdims=True)
    mask2 = jnp.logical_and(emask, lanef != i1)
    m2 = jnp.max(jnp.where(mask2, ep, -1.0), axis=-1, keepdims=True)
    i2 = jnp.min(jnp.where(jnp.logical_and(mask2, ep == m2), lanef, big), axis=-1, keepdims=True)
    den = m1 + m2
    comb_ref[...] = (jnp.where(lanef == i1, gp * (m1 / den), 0.0)
                     + jnp.where(lanef == i2, gp * (m2 / den), 0.0))


def _outproj(xp, xs, pe, mod3, yf, yb, bonus, zg, yconv, gn_g, gn_b, gate_bf, w_out_bf, norm2_g,
             router_w, router_b, ones512):
    tile = lambda n: pl.BlockSpec((TM, n), lambda i: (i, 0))
    const = lambda shape: pl.BlockSpec(shape, lambda i: (0,) * len(shape))
    return pl.pallas_call(
        _outproj_kernel,
        grid=(N_TILES,),
        in_specs=_x_specs() + [
            pl.BlockSpec((None, 1, N_MOD * D), lambda i: (_mod_row(i), 0, 0)),
            tile(D_RWKV), tile(D_RWKV),
            pl.BlockSpec((None, TM, D_RWKV), lambda i: (0, i, 0)),
            pl.BlockSpec((None, TM, D_RWKV), lambda i: (1, i, 0)),
            tile(LORA_G), tile(D_CONV),
            const((1, D_RWKV)), const((1, D_RWKV)), const((LORA_G, D_RWKV)), const((D, D)),
            const((1, D)), const((D, ROUTE_LANES)), const((1, ROUTE_LANES)),
            const((D_RWKV, D_RWKV))],
        out_specs=[tile(D), tile(D), tile(ROUTE_LANES)],
        out_shape=[jax.ShapeDtypeStruct((NT, D), F32), jax.ShapeDtypeStruct((NT, D), BF16),
                   jax.ShapeDtypeStruct((NT, ROUTE_LANES), F32)],
        compiler_params=_cp(("parallel",)),
        name="outproj_router",
    )(xp, xs, pe, mod3, yf, yb, bonus, bonus, zg, yconv, gn_g, gn_b, gate_bf, w_out_bf, norm2_g,
      router_w, router_b, ones512)


def _moe_kernel(h_ref, comb_ref, wg_ref, wu_ref, wd_ref, o_ref):
    e = pl.program_id(1)

    @pl.when(e == 0)
    def _():
        o_ref[...] = jnp.zeros(o_ref.shape, F32)

    h = h_ref[...]
    lane = lax.broadcasted_iota(jnp.int32, (TM_MOE, ROUTE_LANES), 1)
    cw = jnp.sum(jnp.where(lane == e + E_LANE0, comb_ref[...], 0.0), axis=-1, keepdims=True)
    gate = _dot(h, wg_ref[...].astype(BF16))
    up = _dot(h, wu_ref[...].astype(BF16))
    hid = (_silu(gate) * up * cw).astype(BF16)
    o_ref[...] += _dot(hid, wd_ref[...].astype(BF16))


def _moe(h2, comb, wg, wu, wd):
    return pl.pallas_call(
        _moe_kernel,
        grid=(NT // TM_MOE, N_EXPERTS),
        in_specs=[pl.BlockSpec((TM_MOE, D), lambda t, e: (t, 0)),
                  pl.BlockSpec((TM_MOE, ROUTE_LANES), lambda t, e: (t, 0)),
                  pl.BlockSpec((None, D, D_EXPERT), lambda t, e: (e, 0, 0)),
                  pl.BlockSpec((None, D, D_EXPERT), lambda t, e: (e, 0, 0)),
                  pl.BlockSpec((None, D_EXPERT, D), lambda t, e: (e, 0, 0))],
        out_specs=pl.BlockSpec((TM_MOE, D), lambda t, e: (t, 0)),
        out_shape=jax.ShapeDtypeStruct((NT, D), F32),
        compiler_params=_cp(("parallel", "arbitrary")),
        name="moe_experts",
    )(h2, comb, wg, wu, wd)


def _final_kernel(x1_ref, moe_ref, mod_ref, g_ref, o_ref):
    g2 = mod_ref[:, 5 * D:6 * D]
    x2 = x1_ref[...] + g2 * moe_ref[...]
    o_ref[...] = _rms(x2, g_ref[...])


def _final(x1, moe, mod3, final_g, tile0, n_tiles, out_block, name):
    tile = pl.BlockSpec((TM, D), lambda i: (tile0 + i, 0))
    return pl.pallas_call(
        _final_kernel,
        grid=(n_tiles,),
        in_specs=[tile, tile, pl.BlockSpec((None, 1, N_MOD * D), lambda i: (_mod_row(tile0 + i), 0, 0)),
                  pl.BlockSpec((1, D), lambda i: (0, 0))],
        out_specs=pl.BlockSpec((TM, D), lambda i: (out_block(tile0 + i), 0)),
        out_shape=jax.ShapeDtypeStruct((n_tiles * TM, D), F32),
        compiler_params=_cp(("parallel",)),
        name=name,
    )(x1, moe, mod3, final_g)


def _pos_embed(rows):
    t = jnp.arange(rows * GRID_W)
    row = (t // GRID_W).astype(F32)
    col = (t % GRID_W).astype(F32)
    quarter = D // 4
    freqs = 1.0 / (10000.0 ** (jnp.arange(quarter, dtype=F32) / quarter))
    ang_r = row[:, None] * freqs[None, :]
    ang_c = col[:, None] * freqs[None, :]
    return jnp.concatenate([jnp.sin(ang_r), jnp.cos(ang_r), jnp.sin(ang_c), jnp.cos(ang_c)], axis=-1)


def _selection_constants():
    ch = np.arange(D_RWKV)
    ones512 = (ch[:, None] // HEAD == ch[None, :] // HEAD).astype(np.float32)
    t = np.arange(TM)
    same_chunk = t[:, None] // CH == t[None, :] // CH
    cum_f = same_chunk & (t[None, :] <= t[:, None])
    cum_b = same_chunk & (t[None, :] >= t[:, None])
    sel8 = np.arange(NCH)[:, None] == t[None, :] // CH
    col = np.arange(HC)
    hm = col[:, None] // CH == ch[None, :] // HEAD
    bm = col[:, None] // CH == col[None, :] // CH
    tt, jj = np.arange(CH)[:, None], col[None, :] % CH
    masks = np.stack([jj < tt, jj <= tt, jj > tt, jj >= tt])
    eye = jj == tt
    fold = ch[:, None] % HEAD == np.arange(HEAD)[None, :]
    bf = lambda x: jnp.asarray(x, BF16)
    f32 = lambda x: jnp.asarray(x, F32)
    prep_consts = (bf(ones512), bf(np.stack([cum_f, cum_b])), bf(same_chunk), bf(sel8), bf(hm), bf(bm),
                   f32(masks.reshape(2, 2, CH, HC)), f32(eye))
    scan_consts = (bf(hm), f32(ones512), bf(fold))
    return prep_consts, scan_consts


def kernel(x_prompt, x_sample, state_rwkv, c, c_ctx, ada_w, ada_b, norm1_g, w_in, tshift_mu, decay_w0, decay_lora_b, iclr_a0, iclr_lora_b, key_k, key_a, bonus_r_k, gate_lora_b, gn_g, gn_b, conv_dw_w, conv_dw_b, conv_ln_g, conv_ln_b, w_out, norm2_g, router_group_w, router_group_b, router_expert_w, router_expert_b, expert_w_gate, expert_w_up, expert_w_down, final_norm_g):
    assert x_prompt.shape == (N_CTX_SEQ, T_CTX, D) and x_sample.shape == (N_LAT_SEQ, T_LAT, D)
    assert ada_w.shape[0] == 1, "one trunk layer"
    prep_consts, scan_consts = _selection_constants()
    ones512 = prep_consts[0]
    xp = x_prompt.reshape(N_CTX_SEQ * T_CTX, D)
    xs = x_sample.reshape(N_LAT_SEQ * T_LAT, D)
    pe = _pos_embed(T_LAT // GRID_W)

    cond8 = jnp.concatenate([c_ctx[None, :], c, jnp.zeros((8 - 1 - N_LAT_SEQ, D), F32)], axis=0)
    mod3 = _adaln(cond8, ada_w[0], ada_b[0][None, :]).reshape(8, 1, N_MOD * D)

    zs, zg, zc = _inproj(xp, xs, pe, mod3, norm1_g, w_in[0].astype(BF16))

    zero = jnp.zeros((2, LORA, D_RWKV), F32)
    lora2 = jnp.concatenate([jnp.concatenate([decay_lora_b[0], zero], axis=2),
                             jnp.concatenate([zero, iclr_lora_b[0]], axis=2)], axis=1)
    vec = lambda p: p.reshape(2, 1, -1)
    prep_out = _prep(zs, vec(tshift_mu[0]), vec(decay_w0[0]), vec(iclr_a0[0]), lora2,
                     vec(key_k[0]), vec(key_a[0]), vec(bonus_r_k[0]), *prep_consts)
    bonus = prep_out[10]

    s0t = state_rwkv[:, 0].transpose(1, 0, 2, 4, 3)
    eye_h = jnp.eye(N_HEADS, dtype=F32)
    s0_wide = (s0t[:, :, :, :, None, :] * eye_h[None, None, :, None, :, None]).reshape(
        2, N_LAT_SEQ, D_RWKV, D_RWKV)
    yf, yb, s_fin = _scan(prep_out, s0_wide, scan_consts)

    yconv = _conv(zc, jnp.concatenate([conv_dw_w[0], jnp.zeros((1, D_CONV), F32)], axis=0),
                  conv_dw_b, conv_ln_g, conv_ln_b)

    router_w = jnp.concatenate([router_group_w[0], router_expert_w[0],
                                jnp.zeros((D, ROUTE_LANES - N_GROUPS - N_EXPERTS), F32)], axis=1)
    router_b = jnp.concatenate([router_group_b[0], router_expert_b[0],
                                jnp.zeros((ROUTE_LANES - N_GROUPS - N_EXPERTS,), F32)])[None, :]
    x1, h2, comb = _outproj(xp, xs, pe, mod3, yf, yb, bonus, zg, yconv,
                            gn_g, gn_b, gate_lora_b[0].astype(BF16), w_out[0].astype(BF16), norm2_g,
                            router_w, router_b, ones512)

    moe = _moe(h2, comb, expert_w_gate[0], expert_w_up[0], expert_w_down[0])
    fg = final_norm_g[None, :]
    y_prompt = _final(x1, moe, mod3, fg, 0, N_CTX_TILES, lambda i: i, "final_norm_ctx")
    y_sample = _final(x1, moe, mod3, fg, N_CTX_TILES, N_LAT_TILES, _xs_block, "final_norm_lat")
    y_prompt = y_prompt.reshape(N_CTX_SEQ, T_CTX, D)
    y_sample = y_sample.reshape(N_LAT_SEQ, T_LAT, D)
    s_ctx = s_fin[:N_CTX_SEQ].reshape(N_CTX_SEQ, 2, N_HEADS, HEAD, HEAD).transpose(0, 1, 2, 4, 3)
    new_state = s_ctx[:, None].astype(state_rwkv.dtype)
    return (y_prompt, y_sample, new_state)
```

```python
import numpy as np
import jax
import jax.numpy as jnp
from jax import lax
from jax.experimental import pallas as pl
from jax.experimental.pallas import tpu as pltpu

F32 = jnp.float32
BF16 = jnp.bfloat16

D = 1024
N_CTX_SEQ = 16
T_CTX = 256
N_LAT_SEQ = 2
T_LAT = 1024
TM = 256
N_CTX_TILES = N_CTX_SEQ * T_CTX // TM
LAT_CHUNKS = T_LAT // TM
N_LAT_TILES = N_LAT_SEQ * LAT_CHUNKS
N_TILES = N_CTX_TILES + N_LAT_TILES
NT = N_TILES * TM
GRID_W = 64
D_RWKV = 512
D_CONV = 512
HEAD = 64
N_HEADS = 8
CONV_W = 31
CONV_PAD = CONV_W // 2
LORA = 64
LORA_G = 128
SHIFT_COLS = 3 * D_RWKV + 2 * LORA
N_GROUPS = 4
N_EXP_PER_GROUP = 8
N_EXPERTS = 32
D_EXPERT = 256
N_MOD = 6
RMS_EPS = 1e-6
LN_EPS = 1e-5
GN_EPS = 64e-5
LANES = 128
SUB = 8
CH = 32
NCH = TM // CH
HC = N_HEADS * CH
DOUBLINGS = 4
ROUTE_LANES = 128
E_LANE0 = N_GROUPS
GROUP_LANE = 36
TM_MOE = 1024
SEG = 128
MOE_WINDOW = 3
MOE_ROWS = TM_MOE + N_GROUPS * SEG + (MOE_WINDOW - 1) * SEG
N_MOE_TILES = NT // TM_MOE
VMEM_LIMIT = 56 * 1024 * 1024


def _cp(sem, flags=None):
    return pltpu.CompilerParams(dimension_semantics=sem, vmem_limit_bytes=VMEM_LIMIT, flags=flags)


def _split2(a):
    hi = a.astype(BF16)
    lo = (a - hi.astype(F32)).astype(BF16)
    return hi, lo


def _split3(a):
    hi = a.astype(BF16)
    r1 = a - hi.astype(F32)
    mid = r1.astype(BF16)
    lo = (r1 - mid.astype(F32)).astype(BF16)
    return hi, mid, lo


def _dot(a, b):
    return jnp.dot(a, b, preferred_element_type=F32)


def _dot_hp(a, b):
    ah, al = _split2(a)
    bh, bl = _split2(b)
    return _dot(ah, bh) + _dot(ah, bl) + _dot(al, bh)


def _dot_sel(a, sel):
    h, m, l = _split3(a)
    return _dot(h, sel) + _dot(m, sel) + _dot(l, sel)


def _sel_dot(sel, a):
    h, m, l = _split3(a)
    return _dot(sel, h) + _dot(sel, m) + _dot(sel, l)


def _sigmoid(x):
    return 1.0 / (1.0 + jnp.exp(-x))


def _silu(x):
    return x * _sigmoid(x)


def _lat_js(i):
    il = jnp.maximum(i - N_CTX_TILES, 0)
    return il // N_LAT_SEQ, il % N_LAT_SEQ


def _xp_block(i):
    return jnp.minimum(i, N_CTX_TILES - 1)


def _xs_block(i):
    j, s = _lat_js(i)
    return s * LAT_CHUNKS + j


def _pe_block(i):
    j, _ = _lat_js(i)
    return j


def _mod_row(i):
    _, s = _lat_js(i)
    return jnp.where(i < N_CTX_TILES, 0, 1 + s)


def _first_last(i):
    j, _ = _lat_js(i)
    is_ctx = i < N_CTX_TILES
    first = jnp.logical_or(is_ctx, j == 0)
    last = jnp.logical_or(is_ctx, j == LAT_CHUNKS - 1)
    return first, last


def _prev_tile(i):
    first, _ = _first_last(i)
    return jnp.where(first, i, i - N_LAT_SEQ)


def _next_tile(i):
    _, last = _first_last(i)
    return jnp.where(last, i, i + N_LAT_SEQ)


def _adaln_kernel(c_ref, w_ref, b_ref, o_ref):
    c = c_ref[...]
    o_ref[...] = _dot_hp(_silu(c), w_ref[...]) + b_ref[...]


def _adaln(cond8, ada_w, ada_b):
    tn = 1536
    n = ada_w.shape[1]
    return pl.pallas_call(
        _adaln_kernel,
        grid=(n // tn,),
        in_specs=[pl.BlockSpec((8, D), lambda j: (0, 0)),
                  pl.BlockSpec((D, tn), lambda j: (0, j)),
                  pl.BlockSpec((1, tn), lambda j: (0, j))],
        out_specs=pl.BlockSpec((8, tn), lambda j: (0, j)),
        out_shape=jax.ShapeDtypeStruct((8, n), F32),
        compiler_params=_cp(("parallel",)),
        name="adaln",
    )(cond8, ada_w, ada_b)


def _load_x(i, xp_ref, xs_ref, pe_ref):
    f = (i >= N_CTX_TILES).astype(F32)
    return xp_ref[...] * (1.0 - f) + (xs_ref[...] + pe_ref[...]) * f


def _x_specs():
    return [pl.BlockSpec((TM, D), lambda i: (_xp_block(i), 0)),
            pl.BlockSpec((TM, D), lambda i: (_xs_block(i), 0)),
            pl.BlockSpec((TM, D), lambda i: (_pe_block(i), 0))]


def _rms(x, g):
    return x * lax.rsqrt(jnp.mean(x * x, axis=-1, keepdims=True) + RMS_EPS) * g


def _inproj_kernel(xp_ref, xs_ref, pe_ref, mod_ref, g_ref, w_ref, zs_ref, zg_ref, zc_ref):
    i = pl.program_id(0)
    x = _load_x(i, xp_ref, xs_ref, pe_ref)
    sh1 = mod_ref[:, 0:D]
    sc1 = mod_ref[:, D:2 * D]
    h = (_rms(x, g_ref[...]) * (1.0 + sc1) + sh1).astype(BF16)
    zs_ref[...] = _dot(h, w_ref[:, 0:SHIFT_COLS])
    zg_ref[...] = _dot(h, w_ref[:, SHIFT_COLS:SHIFT_COLS + LORA_G])
    zc_ref[...] = _dot(h, w_ref[:, SHIFT_COLS + LORA_G:])


def _inproj(xp, xs, pe, mod3, norm1_g, w_in_bf):
    in_cols = w_in_bf.shape[1]
    return pl.pallas_call(
        _inproj_kernel,
        grid=(N_TILES,),
        in_specs=_x_specs() + [
            pl.BlockSpec((None, 1, N_MOD * D), lambda i: (_mod_row(i), 0, 0)),
            pl.BlockSpec((1, D), lambda i: (0, 0)),
            pl.BlockSpec((D, in_cols), lambda i: (0, 0))],
        out_specs=[pl.BlockSpec((TM, SHIFT_COLS), lambda i: (i, 0)),
                   pl.BlockSpec((TM, LORA_G), lambda i: (i, 0)),
                   pl.BlockSpec((TM, 2 * D_CONV), lambda i: (i, 0))],
        out_shape=[jax.ShapeDtypeStruct((NT, SHIFT_COLS), F32),
                   jax.ShapeDtypeStruct((NT, LORA_G), F32),
                   jax.ShapeDtypeStruct((NT, 2 * D_CONV), F32)],
        compiler_params=_cp(("parallel",)),
        name="inproj",
    )(xp, xs, pe, mod3, norm1_g, w_in_bf)


NT_DIMS = (((1,), (1,)), ((), ()))


def _tile_rows(x):
    return jnp.concatenate([x] * N_HEADS, axis=0)


def _prep_kernel(zs_ref, prev_ref, next_ref, mu_ref, w0_ref, a0_ref, lora_ref, kk_ref_, ka_ref,
                 rk_ref, ones_ref, cum_ref, tot_ref, sel_ref, hm_ref, bm_ref, msk_ref, eye_ref,
                 at_out, rt_out, bh_out, kh_out, v_out, t_out, aak_out, arb_out, ark_out, gc_out, bon_out):
    i = pl.program_id(0)
    d = pl.program_id(1)
    first, last = _first_last(i)
    cur = zs_ref[...]
    prow = prev_ref[7:8, :] * (1.0 - first.astype(F32))
    nrow = next_ref[0:1, :] * (1.0 - last.astype(F32))
    rows = lax.broadcasted_iota(jnp.int32, (TM, 1), 0)
    down = jnp.where(rows == 0, prow, pltpu.roll(cur, 1, axis=0))
    up = jnp.where(rows == TM - 1, nrow, pltpu.roll(cur, TM - 1, axis=0))
    fd = (d == 0).astype(F32)
    prev = down * fd + up * (1.0 - fd)
    xs = cur + (prev - cur) * mu_ref[...]
    r = xs[:, 0:D_RWKV]
    k = xs[:, D_RWKV:2 * D_RWKV]
    v = xs[:, 2 * D_RWKV:3 * D_RWKV]
    z2 = xs[:, 3 * D_RWKV:SHIFT_COLS]
    lane = lax.broadcasted_iota(jnp.int32, (TM, 2 * LORA), 1)
    lin = jnp.where(lane < LORA, jnp.tanh(z2), z2)
    lo = _dot_hp(lin, lora_ref[...])
    u = -(w0_ref[...] + lo[:, 0:D_RWKV])
    softplus = jnp.maximum(u, 0.0) + jnp.log1p(jnp.exp(-jnp.abs(u)))
    w_log = -softplus - 0.5
    lw = -jnp.exp(w_log)
    a = _sigmoid(a0_ref[...] + lo[:, D_RWKV:])
    kx = k * kk_ref_[...]
    ones = ones_ref[...]
    nrm = jnp.sqrt(_dot_sel(kx * kx, ones))
    kk = kx / jnp.maximum(nrm, 1e-12)
    k2 = k * (1.0 + (a - 1.0) * ka_ref[...])
    bvec = kk * a
    bon_out[...] = _dot_sel(r * k2 * rk_ref[...], ones) * v

    lg = _sel_dot(cum_ref[...], lw)
    tot = _sel_dot(tot_ref[...], lw)
    e_tail = jnp.exp(tot - lg)
    e_inv = jnp.exp(-lg)
    at = (-(kk * jnp.exp(lg - lw))).astype(BF16)
    bt = (bvec * e_inv).astype(BF16)
    kt = (k2 * e_inv).astype(BF16)
    rt = (r * jnp.exp(lg)).astype(BF16)
    at_out[...] = at
    rt_out[...] = rt
    bh_out[...] = (bvec * e_tail).astype(BF16)
    kh_out[...] = (k2 * e_tail).astype(BF16)
    v_out[...] = v.astype(BF16)
    gc_out[...] = jnp.exp(_sel_dot(sel_ref[...], lw))

    hm = hm_ref[...]
    bm = bm_ref[...]
    m_strict = msk_ref[0]
    m_incl = msk_ref[1]

    def expand(z):
        return _tile_rows(z.astype(BF16)) * bm

    chunks = [slice(c * CH, (c + 1) * CH) for c in range(NCH)]
    pws = []
    for rows in chunks:
        ar = jnp.concatenate([at[rows], rt[rows]], axis=0)
        bk = jnp.concatenate([_tile_rows(bt[rows]) * hm, _tile_rows(kt[rows]) * hm], axis=0)
        p1 = lax.dot_general(ar, bk, NT_DIMS, preferred_element_type=F32)
        pws.append(p1[0:CH, 0:HC] * m_strict)
        aak_out[rows, :] = (p1[0:CH, HC:] * m_strict).astype(BF16)
        arb_out[rows, :] = (p1[CH:, 0:HC] * m_incl).astype(BF16)
        ark_out[rows, :] = (p1[CH:, HC:] * m_incl).astype(BF16)
    tms = [eye_ref[...] + pw for pw in pws]
    for _ in range(DOUBLINGS):
        pws = [_dot(pw.astype(BF16), expand(pw)) for pw in pws]
        tms = [tm + _dot(tm.astype(BF16), expand(pw)) for tm, pw in zip(tms, pws)]
    for rows, tm in zip(chunks, tms):
        t_out[rows, :] = tm.astype(BF16)


def _prep(zs, mu, w0, a0, lora2, key_k, key_a, r_k, ones512, cum2, tot, sel8, hm, bm, masks, eye):
    pvec = lambda n: pl.BlockSpec((None, 1, n), lambda i, d: (d, 0, 0))
    out_spec = pl.BlockSpec((None, TM, D_RWKV), lambda i, d: (d, i, 0))
    bf_shape = jax.ShapeDtypeStruct((2, NT, D_RWKV), BF16)
    tall_spec = pl.BlockSpec((None, TM, HC), lambda i, d: (d, i, 0))
    rows8 = TM // 8
    return pl.pallas_call(
        _prep_kernel,
        grid=(N_TILES, 2),
        in_specs=[pl.BlockSpec((TM, SHIFT_COLS), lambda i, d: (i, 0)),
                  pl.BlockSpec((8, SHIFT_COLS), lambda i, d: (_prev_tile(i) * rows8 + rows8 - 1, 0)),
                  pl.BlockSpec((8, SHIFT_COLS), lambda i, d: (_next_tile(i) * rows8, 0)),
                  pvec(SHIFT_COLS), pvec(D_RWKV), pvec(D_RWKV),
                  pl.BlockSpec((None, 2 * LORA, 2 * D_RWKV), lambda i, d: (d, 0, 0)),
                  pvec(D_RWKV), pvec(D_RWKV), pvec(D_RWKV),
                  pl.BlockSpec((D_RWKV, D_RWKV), lambda i, d: (0, 0)),
                  pl.BlockSpec((None, TM, TM), lambda i, d: (d, 0, 0)),
                  pl.BlockSpec((TM, TM), lambda i, d: (0, 0)),
                  pl.BlockSpec((NCH, TM), lambda i, d: (0, 0)),
                  pl.BlockSpec((HC, D_RWKV), lambda i, d: (0, 0)),
                  pl.BlockSpec((HC, HC), lambda i, d: (0, 0)),
                  pl.BlockSpec((None, 2, CH, HC), lambda i, d: (d, 0, 0, 0)),
                  pl.BlockSpec((CH, HC), lambda i, d: (0, 0))],
        out_specs=[out_spec] * 5 + [tall_spec] * 4 + [
            pl.BlockSpec((None, NCH, D_RWKV), lambda i, d: (d, i, 0)), out_spec],
        out_shape=[bf_shape] * 5 + [jax.ShapeDtypeStruct((2, NT, HC), BF16)] * 4 + [
            jax.ShapeDtypeStruct((2, N_TILES * NCH, D_RWKV), F32),
            jax.ShapeDtypeStruct((2, NT, D_RWKV), F32)],
        compiler_params=_cp(("parallel", "parallel")),
        name="rwkv_prep",
    )(zs, zs, zs, mu, w0, a0, lora2, key_k, key_a, r_k, ones512, cum2, tot, sel8, hm, bm, masks, eye)


N_SCAN_STEPS = N_CTX_TILES // 2 + LAT_CHUNKS
LAT_STEP0 = N_CTX_TILES // 2


def _scan_block(step, d):
    lat = step >= LAT_STEP0
    jl = step - LAT_STEP0
    j = jnp.where(d == 0, jl, LAT_CHUNKS - 1 - jl)
    return jnp.where(lat, LAT_STEP0 + j, step)


def _scan_state_block(step):
    return jnp.minimum(step, LAT_STEP0)


def _chunk_kernel(*refs):
    (atf, rtf, bhf, khf, vf, tf, aakf, arbf, arkf, gcf, atb, rtb, bhb, khb, vb, tb, aakb, arbb, arkb, gcb,
     s0_ref, hm_ref, bd_ref, fold_ref, yf_ref, yb_ref, sout_ref, mw) = refs
    step = pl.program_id(0)
    in_refs = ((atf, rtf, bhf, khf, vf, tf, aakf, arbf, arkf, gcf),
               (atb, rtb, bhb, khb, vb, tb, aakb, arbb, arkb, gcb))
    y_refs = (yf_ref, yb_ref)

    @pl.when(step < LAT_STEP0)
    def _zero_state():
        mw[...] = jnp.zeros(mw.shape, F32)

    @pl.when(step == LAT_STEP0)
    def _load_state():
        mw[...] = s0_ref[...]

    hm = hm_ref[...]
    bd = bd_ref[...]
    crow = lax.broadcasted_iota(jnp.int32, (NCH, 1), 0)

    def chunk_body(q, carry):
        chains = []
        for dl in range(2):
            cc = q if dl == 0 else NCH - 1 - q
            for sl in range(2):
                chains.append((dl, sl, cc, pl.ds(pl.multiple_of(sl * TM + cc * CH, CH), CH)))
        ld = lambda idx, dl, rows: in_refs[dl][idx][rows, :]
        m0s = [mw[dl, sl] for dl, sl, _, _ in chains]
        xy0s = [_dot(jnp.concatenate([ld(0, dl, rows), ld(1, dl, rows)], axis=0), m0.astype(BF16))
                for (dl, _, _, rows), m0 in zip(chains, m0s)]
        vvs = [ld(4, dl, rows) for dl, _, _, rows in chains]
        avs = [_dot(jnp.concatenate([ld(6, dl, rows), ld(8, dl, rows)], axis=0), _tile_rows(vv) * hm)
               for (dl, _, _, rows), vv in zip(chains, vvs)]
        ubs = [_dot(ld(5, dl, rows), _tile_rows((xy0[0:CH] + av[0:CH]).astype(BF16)) * hm).astype(BF16)
               for (dl, _, _, rows), xy0, av in zip(chains, xy0s, avs)]
        for (dl, _, _, rows), xy0, av, ub in zip(chains, xy0s, avs, ubs):
            y_refs[dl][rows, :] = xy0[CH:] + av[CH:] + _dot(ld(7, dl, rows), _tile_rows(ub) * hm)
        for (dl, sl, cc, rows), m0, ub, vv in zip(chains, m0s, ubs, vvs):
            gcrow = jnp.sum(jnp.where(crow == cc, in_refs[dl][9][sl * NCH:(sl + 1) * NCH, :], 0.0),
                            axis=0, keepdims=True)
            pad = jnp.zeros((LANES - 2 * CH - SUB, D_RWKV), F32)
            stack = jnp.concatenate([ld(2, dl, rows).astype(F32), ld(3, dl, rows).astype(F32),
                                     jnp.broadcast_to(gcrow, (SUB, D_RWKV)), pad], axis=0)
            stack_t = stack.T
            uv = jnp.concatenate([ub, vv, jnp.zeros((LANES - 2 * CH, D_RWKV), BF16)], axis=0)
            upd = _dot(stack_t.astype(BF16), uv)
            mw[dl, sl] = (m0 * stack_t[:, 2 * CH:2 * CH + 1] + upd) * bd
        return carry

    lax.fori_loop(0, NCH, chunk_body, 0)

    @pl.when(jnp.logical_or(step < LAT_STEP0, step == N_SCAN_STEPS - 1))
    def _final():
        for dl in range(2):
            for sl in range(2):
                sout_ref[sl, dl] = _dot_sel(mw[dl, sl], fold_ref[...])


def _scan(prep_out, s0_wide, consts):
    def row_spec(d, width):
        return pl.BlockSpec((None, 2 * TM, width), lambda s: (d, _scan_block(s, d), 0))

    def gc_spec(d):
        return pl.BlockSpec((None, 2 * NCH, D_RWKV), lambda s: (d, _scan_block(s, d), 0))

    def y_spec(d):
        return pl.BlockSpec((2 * TM, D_RWKV), lambda s: (_scan_block(s, d), 0))

    rows, gc = list(prep_out[:9]), prep_out[9]
    row_specs = lambda d: [row_spec(d, D_RWKV)] * 5 + [row_spec(d, HC)] * 4 + [gc_spec(d)]
    const = lambda a: pl.BlockSpec(a.shape, lambda s: (0,) * a.ndim)
    y_shape = jax.ShapeDtypeStruct((NT, D_RWKV), F32)
    return pl.pallas_call(
        _chunk_kernel,
        grid=(N_SCAN_STEPS,),
        in_specs=row_specs(0) + row_specs(1) + [const(s0_wide)] + [const(a) for a in consts],
        out_specs=[y_spec(0), y_spec(1),
                   pl.BlockSpec((2, 2, D_RWKV, HEAD), lambda s: (_scan_state_block(s), 0, 0, 0))],
        out_shape=[y_shape, y_shape,
                   jax.ShapeDtypeStruct(((LAT_STEP0 + 1) * 2, 2, D_RWKV, HEAD), F32)],
        scratch_shapes=[pltpu.VMEM((2, 2, D_RWKV, D_RWKV), F32)],
        compiler_params=_cp(("arbitrary",)),
        name="rwkv_scan",
    )(*rows, gc, *rows, gc, s0_wide, *consts)


HALO = 16


def _glu(z):
    return z[:, 0:D_CONV] * _sigmoid(z[:, D_CONV:])


def _conv_kernel(cur_ref, prev_ref, next_ref, w_ref, b_ref, g_ref, beta_ref, o_ref, ext):
    i = pl.program_id(0)
    first, last = _first_last(i)
    ext[0:HALO, :] = _glu(prev_ref[...]) * (1.0 - first.astype(F32))
    ext[HALO:HALO + TM, :] = _glu(cur_ref[...])
    ext[HALO + TM:, :] = _glu(next_ref[...]) * (1.0 - last.astype(F32))
    acc = jnp.zeros((TM, D_CONV), F32)
    for j in range(CONV_W):
        off = HALO - CONV_PAD + j
        acc = acc + ext[off:off + TM, :] * w_ref[j:j + 1, :]
    h = acc + b_ref[...]
    mu = jnp.mean(h, axis=-1, keepdims=True)
    hc = h - mu
    var = jnp.mean(hc * hc, axis=-1, keepdims=True)
    y = hc * lax.rsqrt(var + LN_EPS) * g_ref[...] + beta_ref[...]
    o_ref[...] = _silu(y)


def _conv(zc, conv_w, conv_b, ln_g, ln_b):
    nh = TM // HALO
    vec = pl.BlockSpec((1, D_CONV), lambda i: (0, 0))
    return pl.pallas_call(
        _conv_kernel,
        grid=(N_TILES,),
        in_specs=[pl.BlockSpec((TM, 2 * D_CONV), lambda i: (i, 0)),
                  pl.BlockSpec((HALO, 2 * D_CONV), lambda i: (_prev_tile(i) * nh + nh - 1, 0)),
                  pl.BlockSpec((HALO, 2 * D_CONV), lambda i: (_next_tile(i) * nh, 0)),
                  pl.BlockSpec((CONV_W + 1, D_CONV), lambda i: (0, 0)), vec, vec, vec],
        out_specs=pl.BlockSpec((TM, D_CONV), lambda i: (i, 0)),
        out_shape=jax.ShapeDtypeStruct((NT, D_CONV), F32),
        scratch_shapes=[pltpu.VMEM((TM + 2 * HALO, D_CONV), F32)],
        compiler_params=_cp(("parallel",)),
        name="conv_module",
    )(zc, zc, zc, conv_w, conv_b, ln_g, ln_b)


def _outproj_kernel(xp_ref, xs_ref, pe_ref, mod_ref, yf_ref, yb_ref, bf_ref, bb_ref, zg_ref, yc_ref,
                    gng_ref, gnb_ref, gl_ref, wo_ref, n2_ref, rw_ref, rb_ref, ones_ref,
                    x1_ref, h2_ref, comb_ref, combt_ref):
    i = pl.program_id(0)
    x = _load_x(i, xp_ref, xs_ref, pe_ref)
    g1 = mod_ref[:, 2 * D:3 * D]
    sh2 = mod_ref[:, 3 * D:4 * D]
    sc2 = mod_ref[:, 4 * D:5 * D]
    ones = ones_ref[...]
    y = (yf_ref[...] + bf_ref[...]) + (yb_ref[...] + bb_ref[...])
    mu = _dot_sel(y, ones) * (1.0 / HEAD)
    yc = y - mu
    var = _dot_sel(yc * yc, ones) * (1.0 / HEAD)
    yn = yc * lax.rsqrt(var + GN_EPS) * gng_ref[...] + gnb_ref[...]
    gate = _dot(_sigmoid(zg_ref[...]).astype(BF16), gl_ref[...])
    y_rwkv = (yn * gate).astype(BF16)
    mix = _dot(y_rwkv, wo_ref[0:D_RWKV, :]) + _dot(yc_ref[...].astype(BF16), wo_ref[D_RWKV:, :])
    x1 = x + g1 * mix
    x1_ref[...] = x1
    h2 = _rms(x1, n2_ref[...]) * (1.0 + sc2) + sh2
    h2_ref[...] = h2.astype(BF16)

    logits = _dot_hp(h2, rw_ref[...]) + rb_ref[...]
    lane = lax.broadcasted_iota(jnp.int32, (TM, ROUTE_LANES), 1)
    lanef = lane.astype(F32)
    neg = jnp.float32(-1e30)
    big = jnp.float32(1e9)
    gmask = lane < N_GROUPS
    gl = jnp.where(gmask, logits, neg)
    ge = jnp.where(gmask, jnp.exp(gl - jnp.max(gl, axis=-1, keepdims=True)), 0.0)
    gprob = ge / jnp.sum(ge, axis=-1, keepdims=True)
    gp = jnp.max(gprob, axis=-1, keepdims=True)
    gidx = jnp.min(jnp.where(jnp.logical_and(gmask, gprob == gp), lanef, big), axis=-1, keepdims=True)
    egrp = jnp.floor((lanef - float(E_LANE0)) * (1.0 / N_EXP_PER_GROUP))
    emask = jnp.logical_and(jnp.logical_and(lane >= E_LANE0, lane < E_LANE0 + N_EXPERTS), egrp == gidx)
    el = jnp.where(emask, logits, neg)
    ee = jnp.where(emask, jnp.exp(el - jnp.max(el, axis=-1, keepdims=True)), 0.0)
    ep = ee / jnp.sum(ee, axis=-1, keepdims=True)
    m1 = jnp.max(jnp.where(emask, ep, -1.0), axis=-1, keepdims=True)
    i1 = jnp.min(jnp.where(jnp.logical_and(emask, ep == m1), lanef, big), axis=-1, keepdims=True)
    mask2 = jnp.logical_and(emask, lanef != i1)
    m2 = jnp.max(jnp.where(mask2, ep, -1.0), axis=-1, keepdims=True)
    i2 = jnp.min(jnp.where(jnp.logical_and(mask2, ep == m2), lanef, big), axis=-1, keepdims=True)
    den = m1 + m2
    comb = (jnp.where(lanef == i1, gp * (m1 / den), 0.0)
            + jnp.where(lanef == i2, gp * (m2 / den), 0.0)
            + jnp.where(lane == GROUP_LANE, gidx, 0.0))
    comb_ref[...] = comb
    combt_ref[...] = comb.T


def _outproj(xp, xs, pe, mod3, yf, yb, bonus, zg, yconv, gn_g, gn_b, gate_bf, w_out_bf, norm2_g,
             router_w, router_b, ones512):
    tile = lambda n: pl.BlockSpec((TM, n), lambda i: (i, 0))
    const = lambda shape: pl.BlockSpec(shape, lambda i: (0,) * len(shape))
    return pl.pallas_call(
        _outproj_kernel,
        grid=(N_TILES,),
        in_specs=_x_specs() + [
            pl.BlockSpec((None, 1, N_MOD * D), lambda i: (_mod_row(i), 0, 0)),
            tile(D_RWKV), tile(D_RWKV),
            pl.BlockSpec((None, TM, D_RWKV), lambda i: (0, i, 0)),
            pl.BlockSpec((None, TM, D_RWKV), lambda i: (1, i, 0)),
            tile(LORA_G), tile(D_CONV),
            const((1, D_RWKV)), const((1, D_RWKV)), const((LORA_G, D_RWKV)), const((D, D)),
            const((1, D)), const((D, ROUTE_LANES)), const((1, ROUTE_LANES)),
            const((D_RWKV, D_RWKV))],
        out_specs=[tile(D), tile(D), tile(ROUTE_LANES),
                   pl.BlockSpec((ROUTE_LANES, TM), lambda i: (0, i))],
        out_shape=[jax.ShapeDtypeStruct((NT, D), F32), jax.ShapeDtypeStruct((NT, D), BF16),
                   jax.ShapeDtypeStruct((NT, ROUTE_LANES), F32),
                   jax.ShapeDtypeStruct((ROUTE_LANES, NT), F32)],
        compiler_params=_cp(("parallel",)),
        name="outproj_router",
    )(xp, xs, pe, mod3, yf, yb, bonus, bonus, zg, yconv, gn_g, gn_b, gate_bf, w_out_bf, norm2_g,
      router_w, router_b, ones512)


def _plan_kernel(meta_ref, triu_ref, drow_ref, dcol_ref, start_ref, ntile_ref):
    gidx = meta_ref[GROUP_LANE % SUB:GROUP_LANE % SUB + 1, :]
    grow = lax.broadcasted_iota(jnp.int32, (SUB, 1), 0)
    growf = grow.astype(F32)
    onehot = jnp.where(jnp.logical_and(gidx == growf, grow < N_GROUPS), 1.0, 0.0)
    before = _dot(onehot.astype(BF16), triu_ref[...])
    count = jnp.sum(onehot, axis=-1, keepdims=True)
    padded = jnp.floor((count + (SEG - 1.0)) * (1.0 / SEG)) * SEG
    start = jnp.zeros((SUB, 1), F32)
    for g in range(N_GROUPS - 1):
        start = start + jnp.where(grow > g, padded[g:g + 1, :], 0.0)
    dest = jnp.sum(onehot * (start + before), axis=0, keepdims=True)
    drow_ref[...] = jnp.broadcast_to(dest, (SUB, TM_MOE))
    dcol_ref[...] = jnp.broadcast_to(dest, (LANES, TM_MOE)).T
    start_ref[...] = jnp.broadcast_to(start * (1.0 / SEG), (SUB, LANES)).astype(jnp.int32)
    ntile_ref[...] = jnp.broadcast_to(padded * (1.0 / SEG), (SUB, LANES)).astype(jnp.int32)


def _plan(comb_t, triu):
    seg_shape = jax.ShapeDtypeStruct((N_MOE_TILES, SUB, LANES), jnp.int32)
    seg_spec = pl.BlockSpec((None, SUB, LANES), lambda t: (t, 0, 0))
    return pl.pallas_call(
        _plan_kernel,
        grid=(N_MOE_TILES,),
        in_specs=[pl.BlockSpec((SUB, TM_MOE), lambda t: (GROUP_LANE // SUB, t)),
                  pl.BlockSpec((TM_MOE, TM_MOE), lambda t: (0, 0))],
        out_specs=[pl.BlockSpec((SUB, TM_MOE), lambda t: (0, t)),
                   pl.BlockSpec((TM_MOE, LANES), lambda t: (t, 0)), seg_spec, seg_spec],
        out_shape=[jax.ShapeDtypeStruct((SUB, NT), F32), jax.ShapeDtypeStruct((NT, LANES), F32),
                   seg_shape, seg_shape],
        compiler_params=_cp(("parallel",)),
        name="moe_plan",
    )(comb_t, triu)


def _moe_kernel(start_ref, ntile_ref, h_ref, comb_ref, drow_ref, dcol_ref, wg_ref, wu_ref, wd_ref,
                o_ref, xs, cws, ys):
    t = pl.program_id(0)
    e = pl.program_id(1)

    @pl.when(e == 0)
    def _sort_in():
        slot = lax.broadcasted_iota(jnp.int32, (MOE_ROWS, TM_MOE), 0).astype(F32)
        perm = jnp.where(slot == drow_ref[0:1, :], 1.0, 0.0).astype(BF16)
        xs[...] = _dot(perm, h_ref[...]).astype(BF16)
        ch, cm, cl = _split3(comb_ref[...])
        cws[...] = _dot(perm, ch) + _dot(perm, cm) + _dot(perm, cl)
        ys[...] = jnp.zeros(ys.shape, F32)

    seg = t * N_GROUPS + e // N_EXP_PER_GROUP
    first = start_ref[seg]
    wg = wg_ref[...].astype(BF16)
    wu = wu_ref[...].astype(BF16)
    wd = wd_ref[...].astype(BF16)

    def visit(row0, n_rows):
        rows = pl.ds(pl.multiple_of(row0 * SEG, SEG), n_rows)
        x = xs[rows, :]
        lane = lax.broadcasted_iota(jnp.int32, (n_rows, ROUTE_LANES), 1)
        cw = jnp.sum(jnp.where(lane == e + E_LANE0, cws[rows, :], 0.0), axis=-1, keepdims=True)
        hid = (_silu(_dot(x, wg)) * _dot(x, wu) * cw).astype(BF16)
        ys[rows, :] += _dot(hid, wd)

    visit(jnp.minimum(first, MOE_ROWS // SEG - MOE_WINDOW), MOE_WINDOW * SEG)

    def sub_tile(i, carry):
        visit(first + i, SEG)
        return carry

    lax.fori_loop(MOE_WINDOW, ntile_ref[seg], sub_tile, 0)

    @pl.when(e == N_EXPERTS - 1)
    def _sort_out():
        slot = lax.broadcasted_iota(jnp.int32, (TM_MOE, MOE_ROWS), 1).astype(F32)
        perm_t = jnp.where(slot == dcol_ref[:, 0:1], 1.0, 0.0).astype(BF16)
        yh, yl = _split2(ys[...])
        o_ref[...] = _dot(perm_t, yh) + _dot(perm_t, yl)


def _moe(h2, comb, drow, dcol, seg_start, seg_ntile, wg, wu, wd):
    grid_spec = pltpu.PrefetchScalarGridSpec(
        num_scalar_prefetch=2,
        grid=(N_MOE_TILES, N_EXPERTS),
        in_specs=[pl.BlockSpec((TM_MOE, D), lambda t, e, s, n: (t, 0)),
                  pl.BlockSpec((TM_MOE, ROUTE_LANES), lambda t, e, s, n: (t, 0)),
                  pl.BlockSpec((SUB, TM_MOE), lambda t, e, s, n: (0, t)),
                  pl.BlockSpec((TM_MOE, LANES), lambda t, e, s, n: (t, 0)),
                  pl.BlockSpec((None, D, D_EXPERT), lambda t, e, s, n: (e, 0, 0)),
                  pl.BlockSpec((None, D, D_EXPERT), lambda t, e, s, n: (e, 0, 0)),
                  pl.BlockSpec((None, D_EXPERT, D), lambda t, e, s, n: (e, 0, 0))],
        out_specs=pl.BlockSpec((TM_MOE, D), lambda t, e, s, n: (t, 0)),
        scratch_shapes=[pltpu.VMEM((MOE_ROWS, D), BF16), pltpu.VMEM((MOE_ROWS, ROUTE_LANES), F32),
                        pltpu.VMEM((MOE_ROWS, D), F32)])
    return pl.pallas_call(
        _moe_kernel,
        grid_spec=grid_spec,
        out_shape=jax.ShapeDtypeStruct((NT, D), F32),
        compiler_params=_cp(("parallel", "arbitrary")),
        name="moe_experts",
    )(seg_start, seg_ntile, h2, comb, drow, dcol, wg, wu, wd)


def _final_kernel(x1_ref, moe_ref, mod_ref, g_ref, o_ref):
    g2 = mod_ref[:, 5 * D:6 * D]
    x2 = x1_ref[...] + g2 * moe_ref[...]
    o_ref[...] = _rms(x2, g_ref[...])


def _final(x1, moe, mod3, final_g, tile0, n_tiles, out_block, name):
    tile = pl.BlockSpec((TM, D), lambda i: (tile0 + i, 0))
    return pl.pallas_call(
        _final_kernel,
        grid=(n_tiles,),
        in_specs=[tile, tile, pl.BlockSpec((None, 1, N_MOD * D), lambda i: (_mod_row(tile0 + i), 0, 0)),
                  pl.BlockSpec((1, D), lambda i: (0, 0))],
        out_specs=pl.BlockSpec((TM, D), lambda i: (out_block(tile0 + i), 0)),
        out_shape=jax.ShapeDtypeStruct((n_tiles * TM, D), F32),
        compiler_params=_cp(("parallel",)),
        name=name,
    )(x1, moe, mod3, final_g)


def _pos_embed(rows):
    t = jnp.arange(rows * GRID_W)
    row = (t // GRID_W).astype(F32)
    col = (t % GRID_W).astype(F32)
    quarter = D // 4
    freqs = 1.0 / (10000.0 ** (jnp.arange(quarter, dtype=F32) / quarter))
    ang_r = row[:, None] * freqs[None, :]
    ang_c = col[:, None] * freqs[None, :]
    return jnp.concatenate([jnp.sin(ang_r), jnp.cos(ang_r), jnp.sin(ang_c), jnp.cos(ang_c)], axis=-1)


def _selection_constants():
    ch = np.arange(D_RWKV)
    ones512 = (ch[:, None] // HEAD == ch[None, :] // HEAD).astype(np.float32)
    t = np.arange(TM)
    same_chunk = t[:, None] // CH == t[None, :] // CH
    cum_f = same_chunk & (t[None, :] <= t[:, None])
    cum_b = same_chunk & (t[None, :] >= t[:, None])
    sel8 = np.arange(NCH)[:, None] == t[None, :] // CH
    col = np.arange(HC)
    hm = col[:, None] // CH == ch[None, :] // HEAD
    bm = col[:, None] // CH == col[None, :] // CH
    tt, jj = np.arange(CH)[:, None], col[None, :] % CH
    masks = np.stack([jj < tt, jj <= tt, jj > tt, jj >= tt])
    eye = jj == tt
    fold = ch[:, None] % HEAD == np.arange(HEAD)[None, :]
    bf = lambda x: jnp.asarray(x, BF16)
    f32 = lambda x: jnp.asarray(x, F32)
    prep_consts = (bf(ones512), bf(np.stack([cum_f, cum_b])), bf(same_chunk), bf(sel8), bf(hm), bf(bm),
                   f32(masks.reshape(2, 2, CH, HC)), f32(eye))
    scan_consts = (bf(hm), f32(ones512), bf(fold))
    return prep_consts, scan_consts


def kernel(x_prompt, x_sample, state_rwkv, c, c_ctx, ada_w, ada_b, norm1_g, w_in, tshift_mu, decay_w0, decay_lora_b, iclr_a0, iclr_lora_b, key_k, key_a, bonus_r_k, gate_lora_b, gn_g, gn_b, conv_dw_w, conv_dw_b, conv_ln_g, conv_ln_b, w_out, norm2_g, router_group_w, router_group_b, router_expert_w, router_expert_b, expert_w_gate, expert_w_up, expert_w_down, final_norm_g):
    assert x_prompt.shape == (N_CTX_SEQ, T_CTX, D) and x_sample.shape == (N_LAT_SEQ, T_LAT, D)
    assert ada_w.shape[0] == 1, "one trunk layer"
    prep_consts, scan_consts = _selection_constants()
    ones512 = prep_consts[0]
    xp = x_prompt.reshape(N_CTX_SEQ * T_CTX, D)
    xs = x_sample.reshape(N_LAT_SEQ * T_LAT, D)
    pe = _pos_embed(T_LAT // GRID_W)

    cond8 = jnp.concatenate([c_ctx[None, :], c, jnp.zeros((8 - 1 - N_LAT_SEQ, D), F32)], axis=0)
    mod3 = _adaln(cond8, ada_w[0], ada_b[0][None, :]).reshape(8, 1, N_MOD * D)

    zs, zg, zc = _inproj(xp, xs, pe, mod3, norm1_g, w_in[0].astype(BF16))

    zero = jnp.zeros((2, LORA, D_RWKV), F32)
    lora2 = jnp.concatenate([jnp.concatenate([decay_lora_b[0], zero], axis=2),
                             jnp.concatenate([zero, iclr_lora_b[0]], axis=2)], axis=1)
    vec = lambda p: p.reshape(2, 1, -1)
    prep_out = _prep(zs, vec(tshift_mu[0]), vec(decay_w0[0]), vec(iclr_a0[0]), lora2,
                     vec(key_k[0]), vec(key_a[0]), vec(bonus_r_k[0]), *prep_consts)
    bonus = prep_out[10]

    s0t = state_rwkv[:, 0].transpose(1, 0, 2, 4, 3)
    eye_h = jnp.eye(N_HEADS, dtype=F32)
    s0_wide = (s0t[:, :, :, :, None, :] * eye_h[None, None, :, None, :, None]).reshape(
        2, N_LAT_SEQ, D_RWKV, D_RWKV)
    yf, yb, s_fin = _scan(prep_out, s0_wide, scan_consts)

    yconv = _conv(zc, jnp.concatenate([conv_dw_w[0], jnp.zeros((1, D_CONV), F32)], axis=0),
                  conv_dw_b, conv_ln_g, conv_ln_b)

    router_w = jnp.concatenate([router_group_w[0], router_expert_w[0],
                                jnp.zeros((D, ROUTE_LANES - N_GROUPS - N_EXPERTS), F32)], axis=1)
    router_b = jnp.concatenate([router_group_b[0], router_expert_b[0],
                                jnp.zeros((ROUTE_LANES - N_GROUPS - N_EXPERTS,), F32)])[None, :]
    x1, h2, comb, comb_t = _outproj(xp, xs, pe, mod3, yf, yb, bonus, zg, yconv,
                                    gn_g, gn_b, gate_lora_b[0].astype(BF16), w_out[0].astype(BF16), norm2_g,
                                    router_w, router_b, ones512)

    tok = np.arange(TM_MOE)
    triu = jnp.asarray(tok[:, None] < tok[None, :], BF16)
    drow, dcol, seg_start, seg_ntile = _plan(comb_t, triu)
    seg_start = seg_start[:, :N_GROUPS, 0].reshape(-1)
    seg_ntile = seg_ntile[:, :N_GROUPS, 0].reshape(-1)
    moe = _moe(h2, comb, drow, dcol, seg_start, seg_ntile,
               expert_w_gate[0], expert_w_up[0], expert_w_down[0])
    fg = final_norm_g[None, :]
    y_prompt = _final(x1, moe, mod3, fg, 0, N_CTX_TILES, lambda i: i, "final_norm_ctx")
    y_sample = _final(x1, moe, mod3, fg, N_CTX_TILES, N_LAT_TILES, _xs_block, "final_norm_lat")
    y_prompt = y_prompt.reshape(N_CTX_SEQ, T_CTX, D)
    y_sample = y_sample.reshape(N_LAT_SEQ, T_LAT, D)
    s_ctx = s_fin[:N_CTX_SEQ].reshape(N_CTX_SEQ, 2, N_HEADS, HEAD, HEAD).transpose(0, 1, 2, 4, 3)
    new_state = s_ctx[:, None].astype(state_rwkv.dtype)
    return (y_prompt, y_sample, new_state)
```

```python
import numpy as np
import jax
import jax.numpy as jnp
from jax import lax
from jax.experimental import pallas as pl
from jax.experimental.pallas import tpu as pltpu

F32 = jnp.float32
BF16 = jnp.bfloat16

D = 1024
N_CTX_SEQ = 16
T_CTX = 256
N_LAT_SEQ = 2
T_LAT = 1024
TM = 256
N_CTX_TILES = N_CTX_SEQ * T_CTX // TM
LAT_CHUNKS = T_LAT // TM
N_LAT_TILES = N_LAT_SEQ * LAT_CHUNKS
N_TILES = N_CTX_TILES + N_LAT_TILES
NT = N_TILES * TM
GRID_W = 64
D_RWKV = 512
D_CONV = 512
HEAD = 64
N_HEADS = 8
CONV_W = 31
CONV_PAD = CONV_W // 2
LORA = 64
LORA_G = 128
SHIFT_COLS = 3 * D_RWKV + 2 * LORA
N_GROUPS = 4
N_EXP_PER_GROUP = 8
N_EXPERTS = 32
D_EXPERT = 256
N_MOD = 6
RMS_EPS = 1e-6
LN_EPS = 1e-5
GN_EPS = 64e-5
LANES = 128
SUB = 8
CH = 32
NCH = TM // CH
HC = N_HEADS * CH
DOUBLINGS = 4
ROUTE_LANES = 128
E_LANE0 = N_GROUPS
GROUP_LANE = 36
TM_MOE = 1024
SEG = 128
EXP_PER_STEP = 4
MOE_WINDOW = 3
MOE_ROWS = TM_MOE + N_GROUPS * SEG + (MOE_WINDOW - 1) * SEG
N_MOE_TILES = NT // TM_MOE
VMEM_LIMIT = 56 * 1024 * 1024


def _cp(sem, flags=None):
    return pltpu.CompilerParams(dimension_semantics=sem, vmem_limit_bytes=VMEM_LIMIT, flags=flags)


def _split2(a):
    hi = a.astype(BF16)
    lo = (a - hi.astype(F32)).astype(BF16)
    return hi, lo


def _dot(a, b):
    return jnp.dot(a, b, preferred_element_type=F32)


def _dot_hp(a, b):
    ah, al = _split2(a)
    bh, bl = _split2(b)
    return _dot(ah, bh) + _dot(ah, bl) + _dot(al, bh)


def _dot_sel(a, sel):
    h, l = _split2(a)
    return _dot(h, sel) + _dot(l, sel)


def _sel_dot(sel, a):
    h, l = _split2(a)
    return _dot(sel, h) + _dot(sel, l)


def _sigmoid(x):
    return 1.0 / (1.0 + jnp.exp(-x))


def _silu(x):
    return x * _sigmoid(x)


def _lat_js(i):
    il = jnp.maximum(i - N_CTX_TILES, 0)
    return il // N_LAT_SEQ, il % N_LAT_SEQ


def _xp_block(i):
    return jnp.minimum(i, N_CTX_TILES - 1)


def _xs_block(i):
    j, s = _lat_js(i)
    return s * LAT_CHUNKS + j


def _pe_block(i):
    j, _ = _lat_js(i)
    return j


def _mod_row(i):
    _, s = _lat_js(i)
    return jnp.where(i < N_CTX_TILES, 0, 1 + s)


def _first_last(i):
    j, _ = _lat_js(i)
    is_ctx = i < N_CTX_TILES
    first = jnp.logical_or(is_ctx, j == 0)
    last = jnp.logical_or(is_ctx, j == LAT_CHUNKS - 1)
    return first, last


def _prev_tile(i):
    first, _ = _first_last(i)
    return jnp.where(first, i, i - N_LAT_SEQ)


def _next_tile(i):
    _, last = _first_last(i)
    return jnp.where(last, i, i + N_LAT_SEQ)


def _adaln_kernel(c_ref, w_ref, b_ref, o_ref):
    c = c_ref[...]
    o_ref[...] = _dot_hp(_silu(c), w_ref[...]) + b_ref[...]


def _adaln(cond8, ada_w, ada_b):
    tn = 1536
    n = ada_w.shape[1]
    return pl.pallas_call(
        _adaln_kernel,
        grid=(n // tn,),
        in_specs=[pl.BlockSpec((8, D), lambda j: (0, 0)),
                  pl.BlockSpec((D, tn), lambda j: (0, j)),
                  pl.BlockSpec((1, tn), lambda j: (0, j))],
        out_specs=pl.BlockSpec((8, tn), lambda j: (0, j)),
        out_shape=jax.ShapeDtypeStruct((8, n), F32),
        compiler_params=_cp(("parallel",)),
        name="adaln",
    )(cond8, ada_w, ada_b)


def _load_x(i, xp_ref, xs_ref, pe_ref):
    f = (i >= N_CTX_TILES).astype(F32)
    return xp_ref[...] * (1.0 - f) + (xs_ref[...] + pe_ref[...]) * f


def _x_specs():
    return [pl.BlockSpec((TM, D), lambda i: (_xp_block(i), 0)),
            pl.BlockSpec((TM, D), lambda i: (_xs_block(i), 0)),
            pl.BlockSpec((TM, D), lambda i: (_pe_block(i), 0))]


def _rms(x, g):
    return x * lax.rsqrt(jnp.mean(x * x, axis=-1, keepdims=True) + RMS_EPS) * g


def _inproj_kernel(xp_ref, xs_ref, pe_ref, mod_ref, g_ref, w_ref, zs_ref, zg_ref, zc_ref):
    i = pl.program_id(0)
    x = _load_x(i, xp_ref, xs_ref, pe_ref)
    sh1 = mod_ref[:, 0:D]
    sc1 = mod_ref[:, D:2 * D]
    h = (_rms(x, g_ref[...]) * (1.0 + sc1) + sh1).astype(BF16)
    zs_ref[...] = _dot(h, w_ref[:, 0:SHIFT_COLS])
    zg_ref[...] = _dot(h, w_ref[:, SHIFT_COLS:SHIFT_COLS + LORA_G])
    zc_ref[...] = _dot(h, w_ref[:, SHIFT_COLS + LORA_G:])


def _inproj(xp, xs, pe, mod3, norm1_g, w_in_bf):
    in_cols = w_in_bf.shape[1]
    return pl.pallas_call(
        _inproj_kernel,
        grid=(N_TILES,),
        in_specs=_x_specs() + [
            pl.BlockSpec((None, 1, N_MOD * D), lambda i: (_mod_row(i), 0, 0)),
            pl.BlockSpec((1, D), lambda i: (0, 0)),
            pl.BlockSpec((D, in_cols), lambda i: (0, 0))],
        out_specs=[pl.BlockSpec((TM, SHIFT_COLS), lambda i: (i, 0)),
                   pl.BlockSpec((TM, LORA_G), lambda i: (i, 0)),
                   pl.BlockSpec((TM, 2 * D_CONV), lambda i: (i, 0))],
        out_shape=[jax.ShapeDtypeStruct((NT, SHIFT_COLS), F32),
                   jax.ShapeDtypeStruct((NT, LORA_G), F32),
                   jax.ShapeDtypeStruct((NT, 2 * D_CONV), F32)],
        compiler_params=_cp(("parallel",)),
        name="inproj",
    )(xp, xs, pe, mod3, norm1_g, w_in_bf)


NT_DIMS = (((1,), (1,)), ((), ()))


def _tile_rows(x):
    return jnp.concatenate([x] * N_HEADS, axis=0)


def _prep_kernel(zs_ref, prev_ref, next_ref, mu_ref, w0_ref, a0_ref, lora_ref, kk_ref_, ka_ref,
                 rk_ref, ones_ref, cum_ref, tot_ref, sel_ref, hm_ref, bm_ref, msk_ref, eye_ref,
                 at_out, rt_out, bh_out, kh_out, v_out, t_out, aak_out, arb_out, ark_out, gc_out, bon_out):
    i = pl.program_id(0)
    d = pl.program_id(1)
    first, last = _first_last(i)
    cur = zs_ref[...]
    prow = prev_ref[7:8, :] * (1.0 - first.astype(F32))
    nrow = next_ref[0:1, :] * (1.0 - last.astype(F32))
    rows = lax.broadcasted_iota(jnp.int32, (TM, 1), 0)
    down = jnp.where(rows == 0, prow, pltpu.roll(cur, 1, axis=0))
    up = jnp.where(rows == TM - 1, nrow, pltpu.roll(cur, TM - 1, axis=0))
    fd = (d == 0).astype(F32)
    prev = down * fd + up * (1.0 - fd)
    xs = cur + (prev - cur) * mu_ref[...]
    r = xs[:, 0:D_RWKV]
    k = xs[:, D_RWKV:2 * D_RWKV]
    v = xs[:, 2 * D_RWKV:3 * D_RWKV]
    z2 = xs[:, 3 * D_RWKV:SHIFT_COLS]
    lane = lax.broadcasted_iota(jnp.int32, (TM, 2 * LORA), 1)
    lin = jnp.where(lane < LORA, jnp.tanh(z2), z2)
    lo = _dot_hp(lin, lora_ref[...])
    u = -(w0_ref[...] + lo[:, 0:D_RWKV])
    softplus = jnp.maximum(u, 0.0) + jnp.log1p(jnp.exp(-jnp.abs(u)))
    w_log = -softplus - 0.5
    lw = -jnp.exp(w_log)
    a = _sigmoid(a0_ref[...] + lo[:, D_RWKV:])
    kx = k * kk_ref_[...]
    ones = ones_ref[...]
    nrm = jnp.sqrt(_dot_sel(kx * kx, ones))
    kk = kx / jnp.maximum(nrm, 1e-12)
    k2 = k * (1.0 + (a - 1.0) * ka_ref[...])
    bvec = kk * a
    bon_out[...] = _dot_sel(r * k2 * rk_ref[...], ones) * v

    lg = _sel_dot(cum_ref[...], lw)
    tot = _sel_dot(tot_ref[...], lw)
    e_tail = jnp.exp(tot - lg)
    e_inv = jnp.exp(-lg)
    at = (-(kk * jnp.exp(lg - lw))).astype(BF16)
    bt = (bvec * e_inv).astype(BF16)
    kt = (k2 * e_inv).astype(BF16)
    rt = (r * jnp.exp(lg)).astype(BF16)
    at_out[...] = at
    rt_out[...] = rt
    bh_out[...] = (bvec * e_tail).astype(BF16)
    kh_out[...] = (k2 * e_tail).astype(BF16)
    v_out[...] = v.astype(BF16)
    gc_out[...] = jnp.exp(_sel_dot(sel_ref[...], lw))

    hm = hm_ref[...]
    bm = bm_ref[...]
    m_strict = msk_ref[0]
    m_incl = msk_ref[1]

    def expand(z):
        return _tile_rows(z.astype(BF16)) * bm

    chunks = [slice(c * CH, (c + 1) * CH) for c in range(NCH)]
    pws = []
    for rows in chunks:
        ar = jnp.concatenate([at[rows], rt[rows]], axis=0)
        bk = jnp.concatenate([_tile_rows(bt[rows]) * hm, _tile_rows(kt[rows]) * hm], axis=0)
        p1 = lax.dot_general(ar, bk, NT_DIMS, preferred_element_type=F32)
        pws.append(p1[0:CH, 0:HC] * m_strict)
        aak_out[rows, :] = (p1[0:CH, HC:] * m_strict).astype(BF16)
        arb_out[rows, :] = (p1[CH:, 0:HC] * m_incl).astype(BF16)
        ark_out[rows, :] = (p1[CH:, HC:] * m_incl).astype(BF16)
    tms = [eye_ref[...] + pw for pw in pws]
    for _ in range(DOUBLINGS):
        pws = [_dot(pw.astype(BF16), expand(pw)) for pw in pws]
        tms = [tm + _dot(tm.astype(BF16), expand(pw)) for tm, pw in zip(tms, pws)]
    for rows, tm in zip(chunks, tms):
        t_out[rows, :] = tm.astype(BF16)


def _prep(zs, mu, w0, a0, lora2, key_k, key_a, r_k, ones512, cum2, tot, sel8, hm, bm, masks, eye):
    pvec = lambda n: pl.BlockSpec((None, 1, n), lambda i, d: (d, 0, 0))
    out_spec = pl.BlockSpec((None, TM, D_RWKV), lambda i, d: (d, i, 0))
    bf_shape = jax.ShapeDtypeStruct((2, NT, D_RWKV), BF16)
    tall_spec = pl.BlockSpec((None, TM, HC), lambda i, d: (d, i, 0))
    rows8 = TM // 8
    return pl.pallas_call(
        _prep_kernel,
        grid=(N_TILES, 2),
        in_specs=[pl.BlockSpec((TM, SHIFT_COLS), lambda i, d: (i, 0)),
                  pl.BlockSpec((8, SHIFT_COLS), lambda i, d: (_prev_tile(i) * rows8 + rows8 - 1, 0)),
                  pl.BlockSpec((8, SHIFT_COLS), lambda i, d: (_next_tile(i) * rows8, 0)),
                  pvec(SHIFT_COLS), pvec(D_RWKV), pvec(D_RWKV),
                  pl.BlockSpec((None, 2 * LORA, 2 * D_RWKV), lambda i, d: (d, 0, 0)),
                  pvec(D_RWKV), pvec(D_RWKV), pvec(D_RWKV),
                  pl.BlockSpec((D_RWKV, D_RWKV), lambda i, d: (0, 0)),
                  pl.BlockSpec((None, TM, TM), lambda i, d: (d, 0, 0)),
                  pl.BlockSpec((TM, TM), lambda i, d: (0, 0)),
                  pl.BlockSpec((NCH, TM), lambda i, d: (0, 0)),
                  pl.BlockSpec((HC, D_RWKV), lambda i, d: (0, 0)),
                  pl.BlockSpec((HC, HC), lambda i, d: (0, 0)),
                  pl.BlockSpec((None, 2, CH, HC), lambda i, d: (d, 0, 0, 0)),
                  pl.BlockSpec((CH, HC), lambda i, d: (0, 0))],
        out_specs=[out_spec] * 5 + [tall_spec] * 4 + [
            pl.BlockSpec((None, NCH, D_RWKV), lambda i, d: (d, i, 0)), out_spec],
        out_shape=[bf_shape] * 5 + [jax.ShapeDtypeStruct((2, NT, HC), BF16)] * 4 + [
            jax.ShapeDtypeStruct((2, N_TILES * NCH, D_RWKV), F32),
            jax.ShapeDtypeStruct((2, NT, D_RWKV), F32)],
        compiler_params=_cp(("parallel", "parallel")),
        name="rwkv_prep",
    )(zs, zs, zs, mu, w0, a0, lora2, key_k, key_a, r_k, ones512, cum2, tot, sel8, hm, bm, masks, eye)


N_SCAN_STEPS = N_CTX_TILES // 2 + LAT_CHUNKS
LAT_STEP0 = N_CTX_TILES // 2


def _scan_block(step, d):
    lat = step >= LAT_STEP0
    jl = step - LAT_STEP0
    j = jnp.where(d == 0, jl, LAT_CHUNKS - 1 - jl)
    return jnp.where(lat, LAT_STEP0 + j, step)


def _scan_state_block(step):
    return jnp.minimum(step, LAT_STEP0)


def _chunk_kernel(*refs):
    (atf, rtf, bhf, khf, vf, tf, aakf, arbf, arkf, gcf, atb, rtb, bhb, khb, vb, tb, aakb, arbb, arkb, gcb,
     s0_ref, hm_ref, bd_ref, fold_ref, yf_ref, yb_ref, sout_ref, mw) = refs
    step = pl.program_id(0)
    in_refs = ((atf, rtf, bhf, khf, vf, tf, aakf, arbf, arkf, gcf),
               (atb, rtb, bhb, khb, vb, tb, aakb, arbb, arkb, gcb))
    y_refs = (yf_ref, yb_ref)

    @pl.when(step < LAT_STEP0)
    def _zero_state():
        mw[...] = jnp.zeros(mw.shape, F32)

    @pl.when(step == LAT_STEP0)
    def _load_state():
        mw[...] = s0_ref[...]

    hm = hm_ref[...]
    bd = bd_ref[...]
    crow = lax.broadcasted_iota(jnp.int32, (NCH, 1), 0)

    def chunk_body(q, carry):
        chains = []
        for dl in range(2):
            cc = q if dl == 0 else NCH - 1 - q
            for sl in range(2):
                chains.append((dl, sl, cc, pl.ds(pl.multiple_of(sl * TM + cc * CH, CH), CH)))
        ld = lambda idx, dl, rows: in_refs[dl][idx][rows, :]
        m0s = [mw[dl, sl] for dl, sl, _, _ in chains]
        xy0s = [_dot(jnp.concatenate([ld(0, dl, rows), ld(1, dl, rows)], axis=0), m0.astype(BF16))
                for (dl, _, _, rows), m0 in zip(chains, m0s)]
        vvs = [ld(4, dl, rows) for dl, _, _, rows in chains]
        avs = [_dot(jnp.concatenate([ld(6, dl, rows), ld(8, dl, rows)], axis=0), _tile_rows(vv) * hm)
               for (dl, _, _, rows), vv in zip(chains, vvs)]
        ubs = [_dot(ld(5, dl, rows), _tile_rows((xy0[0:CH] + av[0:CH]).astype(BF16)) * hm).astype(BF16)
               for (dl, _, _, rows), xy0, av in zip(chains, xy0s, avs)]
        for (dl, _, _, rows), xy0, av, ub in zip(chains, xy0s, avs, ubs):
            y_refs[dl][rows, :] = xy0[CH:] + av[CH:] + _dot(ld(7, dl, rows), _tile_rows(ub) * hm)
        for (dl, sl, cc, rows), m0, ub, vv in zip(chains, m0s, ubs, vvs):
            gcrow = jnp.sum(jnp.where(crow == cc, in_refs[dl][9][sl * NCH:(sl + 1) * NCH, :], 0.0),
                            axis=0, keepdims=True)
            pad = jnp.zeros((LANES - 2 * CH - SUB, D_RWKV), F32)
            stack = jnp.concatenate([ld(2, dl, rows).astype(F32), ld(3, dl, rows).astype(F32),
                                     jnp.broadcast_to(gcrow, (SUB, D_RWKV)), pad], axis=0)
            stack_t = stack.T
            uv = jnp.concatenate([ub, vv, jnp.zeros((LANES - 2 * CH, D_RWKV), BF16)], axis=0)
            upd = _dot(stack_t.astype(BF16), uv)
            mw[dl, sl] = (m0 * stack_t[:, 2 * CH:2 * CH + 1] + upd) * bd
        return carry

    lax.fori_loop(0, NCH, chunk_body, 0)

    @pl.when(jnp.logical_or(step < LAT_STEP0, step == N_SCAN_STEPS - 1))
    def _final():
        for dl in range(2):
            for sl in range(2):
                sout_ref[sl, dl] = _dot_sel(mw[dl, sl], fold_ref[...])


def _scan(prep_out, s0_wide, consts):
    def row_spec(d, width):
        return pl.BlockSpec((None, 2 * TM, width), lambda s: (d, _scan_block(s, d), 0))

    def gc_spec(d):
        return pl.BlockSpec((None, 2 * NCH, D_RWKV), lambda s: (d, _scan_block(s, d), 0))

    def y_spec(d):
        return pl.BlockSpec((2 * TM, D_RWKV), lambda s: (_scan_block(s, d), 0))

    rows, gc = list(prep_out[:9]), prep_out[9]
    row_specs = lambda d: [row_spec(d, D_RWKV)] * 5 + [row_spec(d, HC)] * 4 + [gc_spec(d)]
    const = lambda a: pl.BlockSpec(a.shape, lambda s: (0,) * a.ndim)
    y_shape = jax.ShapeDtypeStruct((NT, D_RWKV), F32)
    return pl.pallas_call(
        _chunk_kernel,
        grid=(N_SCAN_STEPS,),
        in_specs=row_specs(0) + row_specs(1) + [const(s0_wide)] + [const(a) for a in consts],
        out_specs=[y_spec(0), y_spec(1),
                   pl.BlockSpec((2, 2, D_RWKV, HEAD), lambda s: (_scan_state_block(s), 0, 0, 0))],
        out_shape=[y_shape, y_shape,
                   jax.ShapeDtypeStruct(((LAT_STEP0 + 1) * 2, 2, D_RWKV, HEAD), F32)],
        scratch_shapes=[pltpu.VMEM((2, 2, D_RWKV, D_RWKV), F32)],
        compiler_params=_cp(("arbitrary",)),
        name="rwkv_scan",
    )(*rows, gc, *rows, gc, s0_wide, *consts)


HALO = 16


def _glu(z):
    return z[:, 0:D_CONV] * _sigmoid(z[:, D_CONV:])


def _conv_kernel(cur_ref, prev_ref, next_ref, w_ref, b_ref, g_ref, beta_ref, o_ref, ext):
    i = pl.program_id(0)
    first, last = _first_last(i)
    ext[0:HALO, :] = _glu(prev_ref[...]) * (1.0 - first.astype(F32))
    ext[HALO:HALO + TM, :] = _glu(cur_ref[...])
    ext[HALO + TM:, :] = _glu(next_ref[...]) * (1.0 - last.astype(F32))
    acc = jnp.zeros((TM, D_CONV), F32)
    for j in range(CONV_W):
        off = HALO - CONV_PAD + j
        acc = acc + ext[off:off + TM, :] * w_ref[j:j + 1, :]
    h = acc + b_ref[...]
    mu = jnp.mean(h, axis=-1, keepdims=True)
    hc = h - mu
    var = jnp.mean(hc * hc, axis=-1, keepdims=True)
    y = hc * lax.rsqrt(var + LN_EPS) * g_ref[...] + beta_ref[...]
    o_ref[...] = _silu(y)


def _conv(zc, conv_w, conv_b, ln_g, ln_b):
    nh = TM // HALO
    vec = pl.BlockSpec((1, D_CONV), lambda i: (0, 0))
    return pl.pallas_call(
        _conv_kernel,
        grid=(N_TILES,),
        in_specs=[pl.BlockSpec((TM, 2 * D_CONV), lambda i: (i, 0)),
                  pl.BlockSpec((HALO, 2 * D_CONV), lambda i: (_prev_tile(i) * nh + nh - 1, 0)),
                  pl.BlockSpec((HALO, 2 * D_CONV), lambda i: (_next_tile(i) * nh, 0)),
                  pl.BlockSpec((CONV_W + 1, D_CONV), lambda i: (0, 0)), vec, vec, vec],
        out_specs=pl.BlockSpec((TM, D_CONV), lambda i: (i, 0)),
        out_shape=jax.ShapeDtypeStruct((NT, D_CONV), F32),
        scratch_shapes=[pltpu.VMEM((TM + 2 * HALO, D_CONV), F32)],
        compiler_params=_cp(("parallel",)),
        name="conv_module",
    )(zc, zc, zc, conv_w, conv_b, ln_g, ln_b)


def _outproj_kernel(xp_ref, xs_ref, pe_ref, mod_ref, yf_ref, yb_ref, bf_ref, bb_ref, zg_ref, yc_ref,
                    gng_ref, gnb_ref, gl_ref, wo_ref, n2_ref, rw_ref, rb_ref, ones_ref,
                    x1_ref, h2_ref, comb_ref, combt_ref):
    i = pl.program_id(0)
    x = _load_x(i, xp_ref, xs_ref, pe_ref)
    g1 = mod_ref[:, 2 * D:3 * D]
    sh2 = mod_ref[:, 3 * D:4 * D]
    sc2 = mod_ref[:, 4 * D:5 * D]
    ones = ones_ref[...]
    y = (yf_ref[...] + bf_ref[...]) + (yb_ref[...] + bb_ref[...])
    mu = _dot_sel(y, ones) * (1.0 / HEAD)
    yc = y - mu
    var = _dot_sel(yc * yc, ones) * (1.0 / HEAD)
    yn = yc * lax.rsqrt(var + GN_EPS) * gng_ref[...] + gnb_ref[...]
    gate = _dot(_sigmoid(zg_ref[...]).astype(BF16), gl_ref[...])
    y_rwkv = (yn * gate).astype(BF16)
    mix = _dot(y_rwkv, wo_ref[0:D_RWKV, :]) + _dot(yc_ref[...].astype(BF16), wo_ref[D_RWKV:, :])
    x1 = x + g1 * mix
    x1_ref[...] = x1
    h2 = _rms(x1, n2_ref[...]) * (1.0 + sc2) + sh2
    h2_ref[...] = h2.astype(BF16)

    logits = _dot_hp(h2, rw_ref[...]) + rb_ref[...]
    lane = lax.broadcasted_iota(jnp.int32, (TM, ROUTE_LANES), 1)
    lanef = lane.astype(F32)
    neg = jnp.float32(-1e30)
    big = jnp.float32(1e9)
    gmask = lane < N_GROUPS
    gl = jnp.where(gmask, logits, neg)
    ge = jnp.where(gmask, jnp.exp(gl - jnp.max(gl, axis=-1, keepdims=True)), 0.0)
    gprob = ge / jnp.sum(ge, axis=-1, keepdims=True)
    gp = jnp.max(gprob, axis=-1, keepdims=True)
    gidx = jnp.min(jnp.where(jnp.logical_and(gmask, gprob == gp), lanef, big), axis=-1, keepdims=True)
    egrp = jnp.floor((lanef - float(E_LANE0)) * (1.0 / N_EXP_PER_GROUP))
    emask = jnp.logical_and(jnp.logical_and(lane >= E_LANE0, lane < E_LANE0 + N_EXPERTS), egrp == gidx)
    el = jnp.where(emask, logits, neg)
    ee = jnp.where(emask, jnp.exp(el - jnp.max(el, axis=-1, keepdims=True)), 0.0)
    ep = ee / jnp.sum(ee, axis=-1, keepdims=True)
    m1 = jnp.max(jnp.where(emask, ep, -1.0), axis=-1, keepdims=True)
    i1 = jnp.min(jnp.where(jnp.logical_and(emask, ep == m1), lanef, big), axis=-1, keepdims=True)
    mask2 = jnp.logical_and(emask, lanef != i1)
    m2 = jnp.max(jnp.where(mask2, ep, -1.0), axis=-1, keepdims=True)
    i2 = jnp.min(jnp.where(jnp.logical_and(mask2, ep == m2), lanef, big), axis=-1, keepdims=True)
    den = m1 + m2
    comb = (jnp.where(lanef == i1, gp * (m1 / den), 0.0)
            + jnp.where(lanef == i2, gp * (m2 / den), 0.0)
            + jnp.where(lane == GROUP_LANE, gidx, 0.0))
    comb_ref[...] = comb
    combt_ref[...] = comb.T


def _outproj(xp, xs, pe, mod3, yf, yb, bonus, zg, yconv, gn_g, gn_b, gate_bf, w_out_bf, norm2_g,
             router_w, router_b, ones512):
    tile = lambda n: pl.BlockSpec((TM, n), lambda i: (i, 0))
    const = lambda shape: pl.BlockSpec(shape, lambda i: (0,) * len(shape))
    return pl.pallas_call(
        _outproj_kernel,
        grid=(N_TILES,),
        in_specs=_x_specs() + [
            pl.BlockSpec((None, 1, N_MOD * D), lambda i: (_mod_row(i), 0, 0)),
            tile(D_RWKV), tile(D_RWKV),
            pl.BlockSpec((None, TM, D_RWKV), lambda i: (0, i, 0)),
            pl.BlockSpec((None, TM, D_RWKV), lambda i: (1, i, 0)),
            tile(LORA_G), tile(D_CONV),
            const((1, D_RWKV)), const((1, D_RWKV)), const((LORA_G, D_RWKV)), const((D, D)),
            const((1, D)), const((D, ROUTE_LANES)), const((1, ROUTE_LANES)),
            const((D_RWKV, D_RWKV))],
        out_specs=[tile(D), tile(D), tile(ROUTE_LANES),
                   pl.BlockSpec((ROUTE_LANES, TM), lambda i: (0, i))],
        out_shape=[jax.ShapeDtypeStruct((NT, D), F32), jax.ShapeDtypeStruct((NT, D), BF16),
                   jax.ShapeDtypeStruct((NT, ROUTE_LANES), F32),
                   jax.ShapeDtypeStruct((ROUTE_LANES, NT), F32)],
        compiler_params=_cp(("parallel",)),
        name="outproj_router",
    )(xp, xs, pe, mod3, yf, yb, bonus, bonus, zg, yconv, gn_g, gn_b, gate_bf, w_out_bf, norm2_g,
      router_w, router_b, ones512)


def _plan_kernel(meta_ref, triu_ref, drow_ref, dcol_ref, start_ref, ntile_ref):
    gidx = meta_ref[GROUP_LANE % SUB:GROUP_LANE % SUB + 1, :]
    grow = lax.broadcasted_iota(jnp.int32, (SUB, 1), 0)
    growf = grow.astype(F32)
    onehot = jnp.where(jnp.logical_and(gidx == growf, grow < N_GROUPS), 1.0, 0.0)
    before = _dot(onehot.astype(BF16), triu_ref[...])
    count = jnp.sum(onehot, axis=-1, keepdims=True)
    padded = jnp.floor((count + (SEG - 1.0)) * (1.0 / SEG)) * SEG
    start = jnp.zeros((SUB, 1), F32)
    for g in range(N_GROUPS - 1):
        start = start + jnp.where(grow > g, padded[g:g + 1, :], 0.0)
    dest = jnp.sum(onehot * (start + before), axis=0, keepdims=True)
    drow_ref[...] = jnp.broadcast_to(dest, (SUB, TM_MOE))
    dcol_ref[...] = jnp.broadcast_to(dest, (LANES, TM_MOE)).T
    start_ref[...] = jnp.broadcast_to(start * (1.0 / SEG), (SUB, LANES)).astype(jnp.int32)
    ntile_ref[...] = jnp.broadcast_to(padded * (1.0 / SEG), (SUB, LANES)).astype(jnp.int32)


def _plan(comb_t, triu):
    seg_shape = jax.ShapeDtypeStruct((N_MOE_TILES, SUB, LANES), jnp.int32)
    seg_spec = pl.BlockSpec((None, SUB, LANES), lambda t: (t, 0, 0))
    return pl.pallas_call(
        _plan_kernel,
        grid=(N_MOE_TILES,),
        in_specs=[pl.BlockSpec((SUB, TM_MOE), lambda t: (GROUP_LANE // SUB, t)),
                  pl.BlockSpec((TM_MOE, TM_MOE), lambda t: (0, 0))],
        out_specs=[pl.BlockSpec((SUB, TM_MOE), lambda t: (0, t)),
                   pl.BlockSpec((TM_MOE, LANES), lambda t: (t, 0)), seg_spec, seg_spec],
        out_shape=[jax.ShapeDtypeStruct((SUB, NT), F32), jax.ShapeDtypeStruct((NT, LANES), F32),
                   seg_shape, seg_shape],
        compiler_params=_cp(("parallel",)),
        name="moe_plan",
    )(comb_t, triu)


def _moe_kernel(start_ref, ntile_ref, h_ref, comb_ref, drow_ref, dcol_ref, wg_ref, wu_ref, wd_ref,
                o_ref, xs, cws, ys):
    t = pl.program_id(0)
    q = pl.program_id(1)
    g = q // (N_EXP_PER_GROUP // EXP_PER_STEP)

    @pl.when(q == 0)
    def _sort_in():
        slot = lax.broadcasted_iota(jnp.int32, (MOE_ROWS, TM_MOE), 0).astype(F32)
        perm = jnp.where(slot == drow_ref[0:1, :], 1.0, 0.0).astype(BF16)
        xs[...] = _dot(perm, h_ref[...]).astype(BF16)
        ch, cl = _split2(comb_ref[...])
        cws[...] = _dot(perm, ch) + _dot(perm, cl)
        ys[...] = jnp.zeros(ys.shape, F32)

    seg = t * N_GROUPS + g
    first = start_ref[seg]

    def visit(row0, n_rows):
        rows = pl.ds(pl.multiple_of(row0 * SEG, SEG), n_rows)
        x = xs[rows, :]
        cw_all = cws[rows, :]
        lane = lax.broadcasted_iota(jnp.int32, (n_rows, ROUTE_LANES), 1)
        acc = ys[rows, :]
        for j in range(EXP_PER_STEP):
            e_lane = q * EXP_PER_STEP + j + E_LANE0
            cw = jnp.sum(jnp.where(lane == e_lane, cw_all, 0.0), axis=-1, keepdims=True)
            hid = (_silu(_dot(x, wg_ref[j])) * _dot(x, wu_ref[j]) * cw).astype(BF16)
            acc = acc + _dot(hid, wd_ref[j])
        ys[rows, :] = acc

    visit(jnp.minimum(first, MOE_ROWS // SEG - MOE_WINDOW), MOE_WINDOW * SEG)

    def sub_tile(i, carry):
        visit(first + i, SEG)
        return carry

    lax.fori_loop(MOE_WINDOW, ntile_ref[seg], sub_tile, 0)

    @pl.when(q == N_EXPERTS // EXP_PER_STEP - 1)
    def _sort_out():
        slot = lax.broadcasted_iota(jnp.int32, (TM_MOE, MOE_ROWS), 1).astype(F32)
        perm_t = jnp.where(slot == dcol_ref[:, 0:1], 1.0, 0.0).astype(BF16)
        o_ref[...] = _dot(perm_t, ys[...].astype(BF16))


def _moe(h2, comb, drow, dcol, seg_start, seg_ntile, wg, wu, wd):
    grid_spec = pltpu.PrefetchScalarGridSpec(
        num_scalar_prefetch=2,
        grid=(N_MOE_TILES, N_EXPERTS // EXP_PER_STEP),
        in_specs=[pl.BlockSpec((TM_MOE, D), lambda t, q, s, n: (t, 0)),
                  pl.BlockSpec((TM_MOE, ROUTE_LANES), lambda t, q, s, n: (t, 0)),
                  pl.BlockSpec((SUB, TM_MOE), lambda t, q, s, n: (0, t)),
                  pl.BlockSpec((TM_MOE, LANES), lambda t, q, s, n: (t, 0)),
                  pl.BlockSpec((EXP_PER_STEP, D, D_EXPERT), lambda t, q, s, n: (q, 0, 0)),
                  pl.BlockSpec((EXP_PER_STEP, D, D_EXPERT), lambda t, q, s, n: (q, 0, 0)),
                  pl.BlockSpec((EXP_PER_STEP, D_EXPERT, D), lambda t, q, s, n: (q, 0, 0))],
        out_specs=pl.BlockSpec((TM_MOE, D), lambda t, q, s, n: (t, 0)),
        scratch_shapes=[pltpu.VMEM((MOE_ROWS, D), BF16), pltpu.VMEM((MOE_ROWS, ROUTE_LANES), F32),
                        pltpu.VMEM((MOE_ROWS, D), F32)])
    return pl.pallas_call(
        _moe_kernel,
        grid_spec=grid_spec,
        out_shape=jax.ShapeDtypeStruct((NT, D), F32),
        compiler_params=_cp(("parallel", "arbitrary")),
        name="moe_experts",
    )(seg_start, seg_ntile, h2, comb, drow, dcol, wg, wu, wd)


def _final_kernel(x1_ref, moe_ref, mod_ref, g_ref, o_ref):
    g2 = mod_ref[:, 5 * D:6 * D]
    x2 = x1_ref[...] + g2 * moe_ref[...]
    o_ref[...] = _rms(x2, g_ref[...])


def _final(x1, moe, mod3, final_g, tile0, n_tiles, out_block, name):
    tile = pl.BlockSpec((TM, D), lambda i: (tile0 + i, 0))
    return pl.pallas_call(
        _final_kernel,
        grid=(n_tiles,),
        in_specs=[tile, tile, pl.BlockSpec((None, 1, N_MOD * D), lambda i: (_mod_row(tile0 + i), 0, 0)),
                  pl.BlockSpec((1, D), lambda i: (0, 0))],
        out_specs=pl.BlockSpec((TM, D), lambda i: (out_block(tile0 + i), 0)),
        out_shape=jax.ShapeDtypeStruct((n_tiles * TM, D), F32),
        compiler_params=_cp(("parallel",)),
        name=name,
    )(x1, moe, mod3, final_g)


def _pos_embed(rows):
    t = jnp.arange(rows * GRID_W)
    row = (t // GRID_W).astype(F32)
    col = (t % GRID_W).astype(F32)
    quarter = D // 4
    freqs = 1.0 / (10000.0 ** (jnp.arange(quarter, dtype=F32) / quarter))
    ang_r = row[:, None] * freqs[None, :]
    ang_c = col[:, None] * freqs[None, :]
    return jnp.concatenate([jnp.sin(ang_r), jnp.cos(ang_r), jnp.sin(ang_c), jnp.cos(ang_c)], axis=-1)


def _selection_constants():
    ch = np.arange(D_RWKV)
    ones512 = (ch[:, None] // HEAD == ch[None, :] // HEAD).astype(np.float32)
    t = np.arange(TM)
    same_chunk = t[:, None] // CH == t[None, :] // CH
    cum_f = same_chunk & (t[None, :] <= t[:, None])
    cum_b = same_chunk & (t[None, :] >= t[:, None])
    sel8 = np.arange(NCH)[:, None] == t[None, :] // CH
    col = np.arange(HC)
    hm = col[:, None] // CH == ch[None, :] // HEAD
    bm = col[:, None] // CH == col[None, :] // CH
    tt, jj = np.arange(CH)[:, None], col[None, :] % CH
    masks = np.stack([jj < tt, jj <= tt, jj > tt, jj >= tt])
    eye = jj == tt
    fold = ch[:, None] % HEAD == np.arange(HEAD)[None, :]
    bf = lambda x: jnp.asarray(x, BF16)
    f32 = lambda x: jnp.asarray(x, F32)
    prep_consts = (bf(ones512), bf(np.stack([cum_f, cum_b])), bf(same_chunk), bf(sel8), bf(hm), bf(bm),
                   f32(masks.reshape(2, 2, CH, HC)), f32(eye))
    scan_consts = (bf(hm), f32(ones512), bf(fold))
    return prep_consts, scan_consts


def kernel(x_prompt, x_sample, state_rwkv, c, c_ctx, ada_w, ada_b, norm1_g, w_in, tshift_mu, decay_w0, decay_lora_b, iclr_a0, iclr_lora_b, key_k, key_a, bonus_r_k, gate_lora_b, gn_g, gn_b, conv_dw_w, conv_dw_b, conv_ln_g, conv_ln_b, w_out, norm2_g, router_group_w, router_group_b, router_expert_w, router_expert_b, expert_w_gate, expert_w_up, expert_w_down, final_norm_g):
    assert x_prompt.shape == (N_CTX_SEQ, T_CTX, D) and x_sample.shape == (N_LAT_SEQ, T_LAT, D)
    assert ada_w.shape[0] == 1, "one trunk layer"
    prep_consts, scan_consts = _selection_constants()
    ones512 = prep_consts[0]
    xp = x_prompt.reshape(N_CTX_SEQ * T_CTX, D)
    xs = x_sample.reshape(N_LAT_SEQ * T_LAT, D)
    pe = _pos_embed(T_LAT // GRID_W)

    cond8 = jnp.concatenate([c_ctx[None, :], c, jnp.zeros((8 - 1 - N_LAT_SEQ, D), F32)], axis=0)
    mod3 = _adaln(cond8, ada_w[0], ada_b[0][None, :]).reshape(8, 1, N_MOD * D)

    zs, zg, zc = _inproj(xp, xs, pe, mod3, norm1_g, w_in[0].astype(BF16))

    zero = jnp.zeros((2, LORA, D_RWKV), F32)
    lora2 = jnp.concatenate([jnp.concatenate([decay_lora_b[0], zero], axis=2),
                             jnp.concatenate([zero, iclr_lora_b[0]], axis=2)], axis=1)
    vec = lambda p: p.reshape(2, 1, -1)
    prep_out = _prep(zs, vec(tshift_mu[0]), vec(decay_w0[0]), vec(iclr_a0[0]), lora2,
                     vec(key_k[0]), vec(key_a[0]), vec(bonus_r_k[0]), *prep_consts)
    bonus = prep_out[10]

    s0t = state_rwkv[:, 0].transpose(1, 0, 2, 4, 3)
    eye_h = jnp.eye(N_HEADS, dtype=F32)
    s0_wide = (s0t[:, :, :, :, None, :] * eye_h[None, None, :, None, :, None]).reshape(
        2, N_LAT_SEQ, D_RWKV, D_RWKV)
    yf, yb, s_fin = _scan(prep_out, s0_wide, scan_consts)

    yconv = _conv(zc, jnp.concatenate([conv_dw_w[0], jnp.zeros((1, D_CONV), F32)], axis=0),
                  conv_dw_b, conv_ln_g, conv_ln_b)

    router_w = jnp.concatenate([router_group_w[0], router_expert_w[0],
                                jnp.zeros((D, ROUTE_LANES - N_GROUPS - N_EXPERTS), F32)], axis=1)
    router_b = jnp.concatenate([router_group_b[0], router_expert_b[0],
                                jnp.zeros((ROUTE_LANES - N_GROUPS - N_EXPERTS,), F32)])[None, :]
    x1, h2, comb, comb_t = _outproj(xp, xs, pe, mod3, yf, yb, bonus, zg, yconv,
                                    gn_g, gn_b, gate_lora_b[0].astype(BF16), w_out[0].astype(BF16), norm2_g,
                                    router_w, router_b, ones512)

    tok = np.arange(TM_MOE)
    triu = jnp.asarray(tok[:, None] < tok[None, :], BF16)
    drow, dcol, seg_start, seg_ntile = _plan(comb_t, triu)
    seg_start = seg_start[:, :N_GROUPS, 0].reshape(-1)
    seg_ntile = seg_ntile[:, :N_GROUPS, 0].reshape(-1)
    moe = _moe(h2, comb, drow, dcol, seg_start, seg_ntile,
               expert_w_gate[0].astype(BF16), expert_w_up[0].astype(BF16), expert_w_down[0].astype(BF16))
    fg = final_norm_g[None, :]
    y_prompt = _final(x1, moe, mod3, fg, 0, N_CTX_TILES, lambda i: i, "final_norm_ctx")
    y_sample = _final(x1, moe, mod3, fg, N_CTX_TILES, N_LAT_TILES, _xs_block, "final_norm_lat")
    y_prompt = y_prompt.reshape(N_CTX_SEQ, T_CTX, D)
    y_sample = y_sample.reshape(N_LAT_SEQ, T_LAT, D)
    s_ctx = s_fin[:N_CTX_SEQ].reshape(N_CTX_SEQ, 2, N_HEADS, HEAD, HEAD).transpose(0, 1, 2, 4, 3)
    new_state = s_ctx[:, None].astype(state_rwkv.dtype)
    return (y_prompt, y_sample, new_state)
```

```python
import numpy as np
import jax
import jax.numpy as jnp
from jax import lax
from jax.experimental import pallas as pl
from jax.experimental.pallas import tpu as pltpu

F32 = jnp.float32
BF16 = jnp.bfloat16

D = 1024
N_CTX_SEQ = 16
T_CTX = 256
N_LAT_SEQ = 2
T_LAT = 1024
TM = 256
N_CTX_TILES = N_CTX_SEQ * T_CTX // TM
LAT_CHUNKS = T_LAT // TM
N_LAT_TILES = N_LAT_SEQ * LAT_CHUNKS
N_TILES = N_CTX_TILES + N_LAT_TILES
NT = N_TILES * TM
GRID_W = 64
D_RWKV = 512
D_CONV = 512
HEAD = 64
N_HEADS = 8
CONV_W = 31
CONV_PAD = CONV_W // 2
LORA = 64
LORA_G = 128
SHIFT_COLS = 3 * D_RWKV + 2 * LORA
N_GROUPS = 4
N_EXP_PER_GROUP = 8
N_EXPERTS = 32
D_EXPERT = 256
N_MOD = 6
RMS_EPS = 1e-6
LN_EPS = 1e-5
GN_EPS = 64e-5
LANES = 128
SUB = 8
CH = 32
NCH = TM // CH
HC = N_HEADS * CH
DOUBLINGS = 4
ROUTE_LANES = 128
E_LANE0 = N_GROUPS
GROUP_LANE = 36
TM_MOE = 1024
SEG = 128
EXP_PER_STEP = 4
MOE_WINDOW = 3
MOE_ROWS = TM_MOE + N_GROUPS * SEG + (MOE_WINDOW - 1) * SEG
N_MOE_TILES = NT // TM_MOE
VMEM_LIMIT = 56 * 1024 * 1024


def _cp(sem, flags=None):
    return pltpu.CompilerParams(dimension_semantics=sem, vmem_limit_bytes=VMEM_LIMIT, flags=flags)


def _split2(a):
    hi = a.astype(BF16)
    lo = (a - hi.astype(F32)).astype(BF16)
    return hi, lo


def _dot(a, b):
    return jnp.dot(a, b, preferred_element_type=F32)


def _dot_hp(a, b):
    ah, al = _split2(a)
    bh, bl = _split2(b)
    return _dot(ah, bh) + _dot(ah, bl) + _dot(al, bh)


def _dot_sel(a, sel):
    h, l = _split2(a)
    return _dot(h, sel) + _dot(l, sel)


def _sel_dot(sel, a):
    h, l = _split2(a)
    return _dot(sel, h) + _dot(sel, l)


def _sigmoid(x):
    return 1.0 / (1.0 + jnp.exp(-x))


def _silu(x):
    return x * _sigmoid(x)


def _lat_js(i):
    il = jnp.maximum(i - N_CTX_TILES, 0)
    return il // N_LAT_SEQ, il % N_LAT_SEQ


def _xp_block(i):
    return jnp.minimum(i, N_CTX_TILES - 1)


def _xs_block(i):
    j, s = _lat_js(i)
    return s * LAT_CHUNKS + j


def _pe_block(i):
    j, _ = _lat_js(i)
    return j


def _mod_row(i):
    _, s = _lat_js(i)
    return jnp.where(i < N_CTX_TILES, 0, 1 + s)


def _first_last(i):
    j, _ = _lat_js(i)
    is_ctx = i < N_CTX_TILES
    first = jnp.logical_or(is_ctx, j == 0)
    last = jnp.logical_or(is_ctx, j == LAT_CHUNKS - 1)
    return first, last


def _prev_tile(i):
    first, _ = _first_last(i)
    return jnp.where(first, i, i - N_LAT_SEQ)


def _next_tile(i):
    _, last = _first_last(i)
    return jnp.where(last, i, i + N_LAT_SEQ)


def _adaln_kernel(c_ref, w_ref, b_ref, o_ref):
    c = c_ref[...]
    o_ref[...] = _dot_hp(_silu(c), w_ref[...]) + b_ref[...]


def _adaln(cond8, ada_w, ada_b):
    tn = 1536
    n = ada_w.shape[1]
    return pl.pallas_call(
        _adaln_kernel,
        grid=(n // tn,),
        in_specs=[pl.BlockSpec((8, D), lambda j: (0, 0)),
                  pl.BlockSpec((D, tn), lambda j: (0, j)),
                  pl.BlockSpec((1, tn), lambda j: (0, j))],
        out_specs=pl.BlockSpec((8, tn), lambda j: (0, j)),
        out_shape=jax.ShapeDtypeStruct((8, n), F32),
        compiler_params=_cp(("parallel",)),
        name="adaln",
    )(cond8, ada_w, ada_b)


def _load_x(i, xp_ref, xs_ref, pe_ref):
    f = (i >= N_CTX_TILES).astype(F32)
    return xp_ref[...] * (1.0 - f) + (xs_ref[...] + pe_ref[...]) * f


def _x_specs():
    return [pl.BlockSpec((TM, D), lambda i: (_xp_block(i), 0)),
            pl.BlockSpec((TM, D), lambda i: (_xs_block(i), 0)),
            pl.BlockSpec((TM, D), lambda i: (_pe_block(i), 0))]


def _rms(x, g):
    return x * lax.rsqrt(jnp.mean(x * x, axis=-1, keepdims=True) + RMS_EPS) * g


def _inproj_kernel(xp_ref, xs_ref, pe_ref, mod_ref, g_ref, w_ref, zs_ref, zg_ref, zc_ref):
    i = pl.program_id(0)
    x = _load_x(i, xp_ref, xs_ref, pe_ref)
    sh1 = mod_ref[:, 0:D]
    sc1 = mod_ref[:, D:2 * D]
    h = (_rms(x, g_ref[...]) * (1.0 + sc1) + sh1).astype(BF16)
    zs_ref[...] = _dot(h, w_ref[:, 0:SHIFT_COLS])
    zg_ref[...] = _dot(h, w_ref[:, SHIFT_COLS:SHIFT_COLS + LORA_G])
    zc_ref[...] = _dot(h, w_ref[:, SHIFT_COLS + LORA_G:])


def _inproj(xp, xs, pe, mod3, norm1_g, w_in_bf):
    in_cols = w_in_bf.shape[1]
    return pl.pallas_call(
        _inproj_kernel,
        grid=(N_TILES,),
        in_specs=_x_specs() + [
            pl.BlockSpec((None, 1, N_MOD * D), lambda i: (_mod_row(i), 0, 0)),
            pl.BlockSpec((1, D), lambda i: (0, 0)),
            pl.BlockSpec((D, in_cols), lambda i: (0, 0))],
        out_specs=[pl.BlockSpec((TM, SHIFT_COLS), lambda i: (i, 0)),
                   pl.BlockSpec((TM, LORA_G), lambda i: (i, 0)),
                   pl.BlockSpec((TM, 2 * D_CONV), lambda i: (i, 0))],
        out_shape=[jax.ShapeDtypeStruct((NT, SHIFT_COLS), F32),
                   jax.ShapeDtypeStruct((NT, LORA_G), F32),
                   jax.ShapeDtypeStruct((NT, 2 * D_CONV), F32)],
        compiler_params=_cp(("parallel",)),
        name="inproj",
    )(xp, xs, pe, mod3, norm1_g, w_in_bf)


NT_DIMS = (((1,), (1,)), ((), ()))


def _tile_rows(x):
    return jnp.concatenate([x] * N_HEADS, axis=0)


def _prep_kernel(zs_ref, prev_ref, next_ref, mu_ref, w0_ref, a0_ref, lora_ref, kk_ref_, ka_ref,
                 rk_ref, ones_ref, cum_ref, tot_ref, sel_ref, hm_ref, bm_ref, msk_ref, eye_ref,
                 at_out, rt_out, bh_out, kh_out, v_out, t_out, aak_out, arb_out, ark_out, gc_out, bon_out):
    i = pl.program_id(0)
    first, last = _first_last(i)
    cur = zs_ref[...]
    prow = prev_ref[7:8, :] * (1.0 - first.astype(F32))
    nrow = next_ref[0:1, :] * (1.0 - last.astype(F32))
    rows = lax.broadcasted_iota(jnp.int32, (TM, 1), 0)
    shifted = (jnp.where(rows == 0, prow, pltpu.roll(cur, 1, axis=0)),
               jnp.where(rows == TM - 1, nrow, pltpu.roll(cur, TM - 1, axis=0)))
    ones = ones_ref[...]
    lane = lax.broadcasted_iota(jnp.int32, (TM, 2 * LORA), 1)
    scaled = []
    for d in range(2):
        xs = cur + (shifted[d] - cur) * mu_ref[d]
        r = xs[:, 0:D_RWKV]
        k = xs[:, D_RWKV:2 * D_RWKV]
        v = xs[:, 2 * D_RWKV:3 * D_RWKV]
        z2 = xs[:, 3 * D_RWKV:SHIFT_COLS]
        lin = jnp.where(lane < LORA, jnp.tanh(z2), z2)
        lo = _dot_hp(lin, lora_ref[d])
        u = -(w0_ref[d] + lo[:, 0:D_RWKV])
        softplus = jnp.maximum(u, 0.0) + jnp.log1p(jnp.exp(-jnp.abs(u)))
        w_log = -softplus - 0.5
        lw = -jnp.exp(w_log)
        a = _sigmoid(a0_ref[d] + lo[:, D_RWKV:])
        kx = k * kk_ref_[d]
        nrm = jnp.sqrt(_dot_sel(kx * kx, ones))
        kk = kx / jnp.maximum(nrm, 1e-12)
        k2 = k * (1.0 + (a - 1.0) * ka_ref[d])
        bvec = kk * a
        bon_out[d] = _dot_sel(r * k2 * rk_ref[d], ones) * v

        lg = _sel_dot(cum_ref[d], lw)
        tot = _sel_dot(tot_ref[...], lw)
        e_tail = jnp.exp(tot - lg)
        e_inv = jnp.exp(-lg)
        at = (-(kk * jnp.exp(lg - lw))).astype(BF16)
        bt = (bvec * e_inv).astype(BF16)
        kt = (k2 * e_inv).astype(BF16)
        rt = (r * jnp.exp(lg)).astype(BF16)
        at_out[d] = at
        rt_out[d] = rt
        bh_out[d] = (bvec * e_tail).astype(BF16)
        kh_out[d] = (k2 * e_tail).astype(BF16)
        v_out[d] = v.astype(BF16)
        gc_out[d] = jnp.exp(_sel_dot(sel_ref[...], lw))
        scaled.append((at, bt, kt, rt))

    hm = hm_ref[...]
    bm = bm_ref[...]

    def expand(z):
        return _tile_rows(z.astype(BF16)) * bm

    chunks = [(d, slice(c * CH, (c + 1) * CH)) for d in range(2) for c in range(NCH)]
    pws = []
    for d, rows in chunks:
        at, bt, kt, rt = scaled[d]
        ar = jnp.concatenate([at[rows], rt[rows]], axis=0)
        bk = jnp.concatenate([_tile_rows(bt[rows]) * hm, _tile_rows(kt[rows]) * hm], axis=0)
        p1 = lax.dot_general(ar, bk, NT_DIMS, preferred_element_type=F32)
        m_strict, m_incl = msk_ref[d, 0], msk_ref[d, 1]
        pws.append(p1[0:CH, 0:HC] * m_strict)
        aak_out[d, rows, :] = (p1[0:CH, HC:] * m_strict).astype(BF16)
        arb_out[d, rows, :] = (p1[CH:, 0:HC] * m_incl).astype(BF16)
        ark_out[d, rows, :] = (p1[CH:, HC:] * m_incl).astype(BF16)
    tms = [eye_ref[...] + pw for pw in pws]
    for _ in range(DOUBLINGS):
        pws = [_dot(pw.astype(BF16), expand(pw)) for pw in pws]
        tms = [tm + _dot(tm.astype(BF16), expand(pw)) for tm, pw in zip(tms, pws)]
    for (d, rows), tm in zip(chunks, tms):
        t_out[d, rows, :] = tm.astype(BF16)


def _prep(zs, mu, w0, a0, lora2, key_k, key_a, r_k, ones512, cum2, tot, sel8, hm, bm, masks, eye):
    full = lambda a: pl.BlockSpec(a.shape, lambda i: (0,) * a.ndim)
    out_spec = pl.BlockSpec((2, TM, D_RWKV), lambda i: (0, i, 0))
    bf_shape = jax.ShapeDtypeStruct((2, NT, D_RWKV), BF16)
    tall_spec = pl.BlockSpec((2, TM, HC), lambda i: (0, i, 0))
    rows8 = TM // 8
    params = (mu, w0, a0, lora2, key_k, key_a, r_k, ones512, cum2, tot, sel8, hm, bm, masks, eye)
    return pl.pallas_call(
        _prep_kernel,
        grid=(N_TILES,),
        in_specs=[pl.BlockSpec((TM, SHIFT_COLS), lambda i: (i, 0)),
                  pl.BlockSpec((8, SHIFT_COLS), lambda i: (_prev_tile(i) * rows8 + rows8 - 1, 0)),
                  pl.BlockSpec((8, SHIFT_COLS), lambda i: (_next_tile(i) * rows8, 0))]
                 + [full(p) for p in params],
        out_specs=[out_spec] * 5 + [tall_spec] * 4 + [
            pl.BlockSpec((2, NCH, D_RWKV), lambda i: (0, i, 0)), out_spec],
        out_shape=[bf_shape] * 5 + [jax.ShapeDtypeStruct((2, NT, HC), BF16)] * 4 + [
            jax.ShapeDtypeStruct((2, N_TILES * NCH, D_RWKV), F32),
            jax.ShapeDtypeStruct((2, NT, D_RWKV), F32)],
        compiler_params=_cp(("parallel",)),
        name="rwkv_prep",
    )(zs, zs, zs, *params)


N_SCAN_STEPS = N_CTX_TILES // 2 + LAT_CHUNKS
LAT_STEP0 = N_CTX_TILES // 2


def _scan_block(step, d):
    lat = step >= LAT_STEP0
    jl = step - LAT_STEP0
    j = jnp.where(d == 0, jl, LAT_CHUNKS - 1 - jl)
    return jnp.where(lat, LAT_STEP0 + j, step)


def _scan_state_block(step):
    return jnp.minimum(step, LAT_STEP0)


def _chunk_kernel(*refs):
    (atf, rtf, bhf, khf, vf, tf, aakf, arbf, arkf, gcf, atb, rtb, bhb, khb, vb, tb, aakb, arbb, arkb, gcb,
     s0_ref, hm_ref, bd_ref, fold_ref, yf_ref, yb_ref, sout_ref, mw) = refs
    step = pl.program_id(0)
    in_refs = ((atf, rtf, bhf, khf, vf, tf, aakf, arbf, arkf, gcf),
               (atb, rtb, bhb, khb, vb, tb, aakb, arbb, arkb, gcb))
    y_refs = (yf_ref, yb_ref)

    @pl.when(step < LAT_STEP0)
    def _zero_state():
        mw[...] = jnp.zeros(mw.shape, F32)

    @pl.when(step == LAT_STEP0)
    def _load_state():
        mw[...] = s0_ref[...]

    hm = hm_ref[...]
    bd = bd_ref[...]
    crow = lax.broadcasted_iota(jnp.int32, (NCH, 1), 0)

    def chunk_body(q, carry):
        chains = []
        for dl in range(2):
            cc = q if dl == 0 else NCH - 1 - q
            for sl in range(2):
                chains.append((dl, sl, cc, pl.ds(pl.multiple_of(sl * TM + cc * CH, CH), CH)))
        ld = lambda idx, dl, rows: in_refs[dl][idx][rows, :]
        m0s = [mw[dl, sl] for dl, sl, _, _ in chains]
        xy0s = [_dot(jnp.concatenate([ld(0, dl, rows), ld(1, dl, rows)], axis=0), m0.astype(BF16))
                for (dl, _, _, rows), m0 in zip(chains, m0s)]
        vvs = [ld(4, dl, rows) for dl, _, _, rows in chains]
        avs = [_dot(jnp.concatenate([ld(6, dl, rows), ld(8, dl, rows)], axis=0), _tile_rows(vv) * hm)
               for (dl, _, _, rows), vv in zip(chains, vvs)]
        ubs = [_dot(ld(5, dl, rows), _tile_rows((xy0[0:CH] + av[0:CH]).astype(BF16)) * hm).astype(BF16)
               for (dl, _, _, rows), xy0, av in zip(chains, xy0s, avs)]
        for (dl, _, _, rows), xy0, av, ub in zip(chains, xy0s, avs, ubs):
            y_refs[dl][rows, :] = xy0[CH:] + av[CH:] + _dot(ld(7, dl, rows), _tile_rows(ub) * hm)
        for (dl, sl, cc, rows), m0, ub, vv in zip(chains, m0s, ubs, vvs):
            gcrow = jnp.sum(jnp.where(crow == cc, in_refs[dl][9][sl * NCH:(sl + 1) * NCH, :], 0.0),
                            axis=0, keepdims=True)
            pad = jnp.zeros((LANES - 2 * CH - SUB, D_RWKV), F32)
            stack = jnp.concatenate([ld(2, dl, rows).astype(F32), ld(3, dl, rows).astype(F32),
                                     jnp.broadcast_to(gcrow, (SUB, D_RWKV)), pad], axis=0)
            stack_t = stack.T
            uv = jnp.concatenate([ub, vv, jnp.zeros((LANES - 2 * CH, D_RWKV), BF16)], axis=0)
            upd = _dot(stack_t.astype(BF16), uv)
            mw[dl, sl] = (m0 * stack_t[:, 2 * CH:2 * CH + 1] + upd) * bd
        return carry

    lax.fori_loop(0, NCH, chunk_body, 0)

    @pl.when(jnp.logical_or(step < LAT_STEP0, step == N_SCAN_STEPS - 1))
    def _final():
        for dl in range(2):
            for sl in range(2):
                sout_ref[sl, dl] = _dot_sel(mw[dl, sl], fold_ref[...])


def _scan(prep_out, s0_wide, consts):
    def row_spec(d, width):
        return pl.BlockSpec((None, 2 * TM, width), lambda s: (d, _scan_block(s, d), 0))

    def gc_spec(d):
        return pl.BlockSpec((None, 2 * NCH, D_RWKV), lambda s: (d, _scan_block(s, d), 0))

    def y_spec(d):
        return pl.BlockSpec((2 * TM, D_RWKV), lambda s: (_scan_block(s, d), 0))

    rows, gc = list(prep_out[:9]), prep_out[9]
    row_specs = lambda d: [row_spec(d, D_RWKV)] * 5 + [row_spec(d, HC)] * 4 + [gc_spec(d)]
    const = lambda a: pl.BlockSpec(a.shape, lambda s: (0,) * a.ndim)
    y_shape = jax.ShapeDtypeStruct((NT, D_RWKV), F32)
    return pl.pallas_call(
        _chunk_kernel,
        grid=(N_SCAN_STEPS,),
        in_specs=row_specs(0) + row_specs(1) + [const(s0_wide)] + [const(a) for a in consts],
        out_specs=[y_spec(0), y_spec(1),
                   pl.BlockSpec((2, 2, D_RWKV, HEAD), lambda s: (_scan_state_block(s), 0, 0, 0))],
        out_shape=[y_shape, y_shape,
                   jax.ShapeDtypeStruct(((LAT_STEP0 + 1) * 2, 2, D_RWKV, HEAD), F32)],
        scratch_shapes=[pltpu.VMEM((2, 2, D_RWKV, D_RWKV), F32)],
        compiler_params=_cp(("arbitrary",)),
        name="rwkv_scan",
    )(*rows, gc, *rows, gc, s0_wide, *consts)


HALO = 16


def _glu(z):
    return z[:, 0:D_CONV] * _sigmoid(z[:, D_CONV:])


def _conv_kernel(cur_ref, prev_ref, next_ref, w_ref, b_ref, g_ref, beta_ref, o_ref, ext):
    i = pl.program_id(0)
    first, last = _first_last(i)
    n_ext = TM + 2 * HALO
    u = jnp.concatenate([_glu(prev_ref[...]) * (1.0 - first.astype(F32)), _glu(cur_ref[...]),
                         _glu(next_ref[...]) * (1.0 - last.astype(F32))], axis=0)
    ext[0] = u
    for b in range(1, SUB):
        ext[b] = pltpu.roll(u, n_ext - b, axis=0)
    acc = jnp.zeros((TM, D_CONV), F32)
    for j in range(CONV_W):
        off = HALO - CONV_PAD + j
        acc = acc + ext[off % SUB, off - off % SUB:off - off % SUB + TM, :] * w_ref[j:j + 1, :]
    h = acc + b_ref[...]
    mu = jnp.mean(h, axis=-1, keepdims=True)
    hc = h - mu
    var = jnp.mean(hc * hc, axis=-1, keepdims=True)
    y = hc * lax.rsqrt(var + LN_EPS) * g_ref[...] + beta_ref[...]
    o_ref[...] = _silu(y)


def _conv(zc, conv_w, conv_b, ln_g, ln_b):
    nh = TM // HALO
    vec = pl.BlockSpec((1, D_CONV), lambda i: (0, 0))
    return pl.pallas_call(
        _conv_kernel,
        grid=(N_TILES,),
        in_specs=[pl.BlockSpec((TM, 2 * D_CONV), lambda i: (i, 0)),
                  pl.BlockSpec((HALO, 2 * D_CONV), lambda i: (_prev_tile(i) * nh + nh - 1, 0)),
                  pl.BlockSpec((HALO, 2 * D_CONV), lambda i: (_next_tile(i) * nh, 0)),
                  pl.BlockSpec((CONV_W + 1, D_CONV), lambda i: (0, 0)), vec, vec, vec],
        out_specs=pl.BlockSpec((TM, D_CONV), lambda i: (i, 0)),
        out_shape=jax.ShapeDtypeStruct((NT, D_CONV), F32),
        scratch_shapes=[pltpu.VMEM((SUB, TM + 2 * HALO, D_CONV), F32)],
        compiler_params=_cp(("parallel",)),
        name="conv_module",
    )(zc, zc, zc, conv_w, conv_b, ln_g, ln_b)


def _outproj_kernel(xp_ref, xs_ref, pe_ref, mod_ref, yf_ref, yb_ref, bf_ref, bb_ref, zg_ref, yc_ref,
                    gng_ref, gnb_ref, gl_ref, wo_ref, n2_ref, rw_ref, rb_ref, ones_ref,
                    x1_ref, h2_ref, comb_ref, combt_ref):
    i = pl.program_id(0)
    x = _load_x(i, xp_ref, xs_ref, pe_ref)
    g1 = mod_ref[:, 2 * D:3 * D]
    sh2 = mod_ref[:, 3 * D:4 * D]
    sc2 = mod_ref[:, 4 * D:5 * D]
    ones = ones_ref[...]
    y = (yf_ref[...] + bf_ref[...]) + (yb_ref[...] + bb_ref[...])
    mu = _dot_sel(y, ones) * (1.0 / HEAD)
    yc = y - mu
    var = _dot_sel(yc * yc, ones) * (1.0 / HEAD)
    yn = yc * lax.rsqrt(var + GN_EPS) * gng_ref[...] + gnb_ref[...]
    gate = _dot(_sigmoid(zg_ref[...]).astype(BF16), gl_ref[...])
    y_rwkv = (yn * gate).astype(BF16)
    mix = _dot(y_rwkv, wo_ref[0:D_RWKV, :]) + _dot(yc_ref[...].astype(BF16), wo_ref[D_RWKV:, :])
    x1 = x + g1 * mix
    x1_ref[...] = x1
    h2 = _rms(x1, n2_ref[...]) * (1.0 + sc2) + sh2
    h2_ref[...] = h2.astype(BF16)

    logits = _dot_hp(h2, rw_ref[...]) + rb_ref[...]
    lane = lax.broadcasted_iota(jnp.int32, (TM, ROUTE_LANES), 1)
    lanef = lane.astype(F32)
    neg = jnp.float32(-1e30)
    big = jnp.float32(1e9)
    gmask = lane < N_GROUPS
    gl = jnp.where(gmask, logits, neg)
    ge = jnp.where(gmask, jnp.exp(gl - jnp.max(gl, axis=-1, keepdims=True)), 0.0)
    gprob = ge / jnp.sum(ge, axis=-1, keepdims=True)
    gp = jnp.max(gprob, axis=-1, keepdims=True)
    gidx = jnp.min(jnp.where(jnp.logical_and(gmask, gprob == gp), lanef, big), axis=-1, keepdims=True)
    egrp = jnp.floor((lanef - float(E_LANE0)) * (1.0 / N_EXP_PER_GROUP))
    emask = jnp.logical_and(jnp.logical_and(lane >= E_LANE0, lane < E_LANE0 + N_EXPERTS), egrp == gidx)
    el = jnp.where(emask, logits, neg)
    ee = jnp.where(emask, jnp.exp(el - jnp.max(el, axis=-1, keepdims=True)), 0.0)
    ep = ee / jnp.sum(ee, axis=-1, keepdims=True)
    m1 = jnp.max(jnp.where(emask, ep, -1.0), axis=-1, keepdims=True)
    i1 = jnp.min(jnp.where(jnp.logical_and(emask, ep == m1), lanef, big), axis=-1, keepdims=True)
    mask2 = jnp.logical_and(emask, lanef != i1)
    m2 = jnp.max(jnp.where(mask2, ep, -1.0), axis=-1, keepdims=True)
    i2 = jnp.min(jnp.where(jnp.logical_and(mask2, ep == m2), lanef, big), axis=-1, keepdims=True)
    den = m1 + m2
    comb = (jnp.where(lanef == i1, gp * (m1 / den), 0.0)
            + jnp.where(lanef == i2, gp * (m2 / den), 0.0)
            + jnp.where(lane == GROUP_LANE, gidx, 0.0))
    comb_ref[...] = comb
    combt_ref[...] = comb.T


def _outproj(xp, xs, pe, mod3, yf, yb, bonus, zg, yconv, gn_g, gn_b, gate_bf, w_out_bf, norm2_g,
             router_w, router_b, ones512):
    tile = lambda n: pl.BlockSpec((TM, n), lambda i: (i, 0))
    const = lambda shape: pl.BlockSpec(shape, lambda i: (0,) * len(shape))
    return pl.pallas_call(
        _outproj_kernel,
        grid=(N_TILES,),
        in_specs=_x_specs() + [
            pl.BlockSpec((None, 1, N_MOD * D), lambda i: (_mod_row(i), 0, 0)),
            tile(D_RWKV), tile(D_RWKV),
            pl.BlockSpec((None, TM, D_RWKV), lambda i: (0, i, 0)),
            pl.BlockSpec((None, TM, D_RWKV), lambda i: (1, i, 0)),
            tile(LORA_G), tile(D_CONV),
            const((1, D_RWKV)), const((1, D_RWKV)), const((LORA_G, D_RWKV)), const((D, D)),
            const((1, D)), const((D, ROUTE_LANES)), const((1, ROUTE_LANES)),
            const((D_RWKV, D_RWKV))],
        out_specs=[tile(D), tile(D), tile(ROUTE_LANES),
                   pl.BlockSpec((ROUTE_LANES, TM), lambda i: (0, i))],
        out_shape=[jax.ShapeDtypeStruct((NT, D), F32), jax.ShapeDtypeStruct((NT, D), BF16),
                   jax.ShapeDtypeStruct((NT, ROUTE_LANES), F32),
                   jax.ShapeDtypeStruct((ROUTE_LANES, NT), F32)],
        compiler_params=_cp(("parallel",)),
        name="outproj_router",
    )(xp, xs, pe, mod3, yf, yb, bonus, bonus, zg, yconv, gn_g, gn_b, gate_bf, w_out_bf, norm2_g,
      router_w, router_b, ones512)


def _plan_kernel(meta_ref, triu_ref, drow_ref, dcol_ref, start_ref, ntile_ref):
    gidx = meta_ref[GROUP_LANE % SUB:GROUP_LANE % SUB + 1, :]
    grow = lax.broadcasted_iota(jnp.int32, (SUB, 1), 0)
    growf = grow.astype(F32)
    onehot = jnp.where(jnp.logical_and(gidx == growf, grow < N_GROUPS), 1.0, 0.0)
    before = _dot(onehot.astype(BF16), triu_ref[...])
    count = jnp.sum(onehot, axis=-1, keepdims=True)
    padded = jnp.floor((count + (SEG - 1.0)) * (1.0 / SEG)) * SEG
    start = jnp.zeros((SUB, 1), F32)
    for g in range(N_GROUPS - 1):
        start = start + jnp.where(grow > g, padded[g:g + 1, :], 0.0)
    dest = jnp.sum(onehot * (start + before), axis=0, keepdims=True)
    drow_ref[...] = jnp.broadcast_to(dest, (SUB, TM_MOE))
    dcol_ref[...] = jnp.broadcast_to(dest, (LANES, TM_MOE)).T
    start_ref[...] = jnp.broadcast_to(start * (1.0 / SEG), (SUB, LANES)).astype(jnp.int32)
    ntile_ref[...] = jnp.broadcast_to(padded * (1.0 / SEG), (SUB, LANES)).astype(jnp.int32)


def _plan(comb_t, triu):
    seg_shape = jax.ShapeDtypeStruct((N_MOE_TILES, SUB, LANES), jnp.int32)
    seg_spec = pl.BlockSpec((None, SUB, LANES), lambda t: (t, 0, 0))
    return pl.pallas_call(
        _plan_kernel,
        grid=(N_MOE_TILES,),
        in_specs=[pl.BlockSpec((SUB, TM_MOE), lambda t: (GROUP_LANE // SUB, t)),
                  pl.BlockSpec((TM_MOE, TM_MOE), lambda t: (0, 0))],
        out_specs=[pl.BlockSpec((SUB, TM_MOE), lambda t: (0, t)),
                   pl.BlockSpec((TM_MOE, LANES), lambda t: (t, 0)), seg_spec, seg_spec],
        out_shape=[jax.ShapeDtypeStruct((SUB, NT), F32), jax.ShapeDtypeStruct((NT, LANES), F32),
                   seg_shape, seg_shape],
        compiler_params=_cp(("parallel",)),
        name="moe_plan",
    )(comb_t, triu)


def _moe_kernel(start_ref, ntile_ref, h_ref, comb_ref, drow_ref, dcol_ref, wg_ref, wu_ref, wd_ref,
                o_ref, xs, cws, ys):
    t = pl.program_id(0)
    q = pl.program_id(1)
    g = q // (N_EXP_PER_GROUP // EXP_PER_STEP)

    @pl.when(q == 0)
    def _sort_in():
        slot = lax.broadcasted_iota(jnp.int32, (MOE_ROWS, TM_MOE), 0).astype(F32)
        perm = jnp.where(slot == drow_ref[0:1, :], 1.0, 0.0).astype(BF16)
        xs[...] = _dot(perm, h_ref[...]).astype(BF16)
        ch, cl = _split2(comb_ref[...])
        cws[...] = _dot(perm, ch) + _dot(perm, cl)
        ys[...] = jnp.zeros(ys.shape, F32)

    seg = t * N_GROUPS + g
    first = start_ref[seg]

    def visit(row0, n_rows):
        rows = pl.ds(pl.multiple_of(row0 * SEG, SEG), n_rows)
        x = xs[rows, :]
        cw_all = cws[rows, :]
        lane = lax.broadcasted_iota(jnp.int32, (n_rows, ROUTE_LANES), 1)
        acc = ys[rows, :]
        for j in range(EXP_PER_STEP):
            e_lane = q * EXP_PER_STEP + j + E_LANE0
            cw = jnp.sum(jnp.where(lane == e_lane, cw_all, 0.0), axis=-1, keepdims=True)
            hid = (_silu(_dot(x, wg_ref[j])) * _dot(x, wu_ref[j]) * cw).astype(BF16)
            acc = acc + _dot(hid, wd_ref[j])
        ys[rows, :] = acc

    visit(jnp.minimum(first, MOE_ROWS // SEG - MOE_WINDOW), MOE_WINDOW * SEG)

    def sub_tile(i, carry):
        visit(first + i, SEG)
        return carry

    lax.fori_loop(MOE_WINDOW, ntile_ref[seg], sub_tile, 0)

    @pl.when(q == N_EXPERTS // EXP_PER_STEP - 1)
    def _sort_out():
        slot = lax.broadcasted_iota(jnp.int32, (TM_MOE, MOE_ROWS), 1).astype(F32)
        perm_t = jnp.where(slot == dcol_ref[:, 0:1], 1.0, 0.0).astype(BF16)
        o_ref[...] = _dot(perm_t, ys[...].astype(BF16))


def _moe(h2, comb, drow, dcol, seg_start, seg_ntile, wg, wu, wd):
    grid_spec = pltpu.PrefetchScalarGridSpec(
        num_scalar_prefetch=2,
        grid=(N_MOE_TILES, N_EXPERTS // EXP_PER_STEP),
        in_specs=[pl.BlockSpec((TM_MOE, D), lambda t, q, s, n: (t, 0)),
                  pl.BlockSpec((TM_MOE, ROUTE_LANES), lambda t, q, s, n: (t, 0)),
                  pl.BlockSpec((SUB, TM_MOE), lambda t, q, s, n: (0, t)),
                  pl.BlockSpec((TM_MOE, LANES), lambda t, q, s, n: (t, 0)),
                  pl.BlockSpec((EXP_PER_STEP, D, D_EXPERT), lambda t, q, s, n: (q, 0, 0)),
                  pl.BlockSpec((EXP_PER_STEP, D, D_EXPERT), lambda t, q, s, n: (q, 0, 0)),
                  pl.BlockSpec((EXP_PER_STEP, D_EXPERT, D), lambda t, q, s, n: (q, 0, 0))],
        out_specs=pl.BlockSpec((TM_MOE, D), lambda t, q, s, n: (t, 0)),
        scratch_shapes=[pltpu.VMEM((MOE_ROWS, D), BF16), pltpu.VMEM((MOE_ROWS, ROUTE_LANES), F32),
                        pltpu.VMEM((MOE_ROWS, D), F32)])
    return pl.pallas_call(
        _moe_kernel,
        grid_spec=grid_spec,
        out_shape=jax.ShapeDtypeStruct((NT, D), F32),
        compiler_params=_cp(("parallel", "arbitrary")),
        name="moe_experts",
    )(seg_start, seg_ntile, h2, comb, drow, dcol, wg, wu, wd)


def _final_kernel(x1_ref, moe_ref, mod_ref, g_ref, o_ref):
    g2 = mod_ref[:, 5 * D:6 * D]
    x2 = x1_ref[...] + g2 * moe_ref[...]
    o_ref[...] = _rms(x2, g_ref[...])


def _final(x1, moe, mod3, final_g, tile0, n_tiles, out_block, name):
    tile = pl.BlockSpec((TM, D), lambda i: (tile0 + i, 0))
    return pl.pallas_call(
        _final_kernel,
        grid=(n_tiles,),
        in_specs=[tile, tile, pl.BlockSpec((None, 1, N_MOD * D), lambda i: (_mod_row(tile0 + i), 0, 0)),
                  pl.BlockSpec((1, D), lambda i: (0, 0))],
        out_specs=pl.BlockSpec((TM, D), lambda i: (out_block(tile0 + i), 0)),
        out_shape=jax.ShapeDtypeStruct((n_tiles * TM, D), F32),
        compiler_params=_cp(("parallel",)),
        name=name,
    )(x1, moe, mod3, final_g)


def _pos_embed(rows):
    t = np.arange(rows * GRID_W)
    row = (t // GRID_W).astype(np.float32)
    col = (t % GRID_W).astype(np.float32)
    quarter = D // 4
    freqs = (1.0 / (10000.0 ** (np.arange(quarter, dtype=np.float32) / quarter))).astype(np.float32)
    ang_r = row[:, None] * freqs[None, :]
    ang_c = col[:, None] * freqs[None, :]
    pe = np.concatenate([np.sin(ang_r), np.cos(ang_r), np.sin(ang_c), np.cos(ang_c)], axis=-1)
    return jnp.asarray(pe, F32)


def _selection_constants():
    ch = np.arange(D_RWKV)
    ones512 = (ch[:, None] // HEAD == ch[None, :] // HEAD).astype(np.float32)
    t = np.arange(TM)
    same_chunk = t[:, None] // CH == t[None, :] // CH
    cum_f = same_chunk & (t[None, :] <= t[:, None])
    cum_b = same_chunk & (t[None, :] >= t[:, None])
    sel8 = np.arange(NCH)[:, None] == t[None, :] // CH
    col = np.arange(HC)
    hm = col[:, None] // CH == ch[None, :] // HEAD
    bm = col[:, None] // CH == col[None, :] // CH
    tt, jj = np.arange(CH)[:, None], col[None, :] % CH
    masks = np.stack([jj < tt, jj <= tt, jj > tt, jj >= tt])
    eye = jj == tt
    fold = ch[:, None] % HEAD == np.arange(HEAD)[None, :]
    bf = lambda x: jnp.asarray(x, BF16)
    f32 = lambda x: jnp.asarray(x, F32)
    prep_consts = (bf(ones512), bf(np.stack([cum_f, cum_b])), bf(same_chunk), bf(sel8), bf(hm), bf(bm),
                   f32(masks.reshape(2, 2, CH, HC)), f32(eye))
    scan_consts = (bf(hm), f32(ones512), bf(fold))
    return prep_consts, scan_consts


def kernel(x_prompt, x_sample, state_rwkv, c, c_ctx, ada_w, ada_b, norm1_g, w_in, tshift_mu, decay_w0, decay_lora_b, iclr_a0, iclr_lora_b, key_k, key_a, bonus_r_k, gate_lora_b, gn_g, gn_b, conv_dw_w, conv_dw_b, conv_ln_g, conv_ln_b, w_out, norm2_g, router_group_w, router_group_b, router_expert_w, router_expert_b, expert_w_gate, expert_w_up, expert_w_down, final_norm_g):
    assert x_prompt.shape == (N_CTX_SEQ, T_CTX, D) and x_sample.shape == (N_LAT_SEQ, T_LAT, D)
    assert ada_w.shape[0] == 1, "one trunk layer"
    prep_consts, scan_consts = _selection_constants()
    ones512 = prep_consts[0]
    xp = x_prompt.reshape(N_CTX_SEQ * T_CTX, D)
    xs = x_sample.reshape(N_LAT_SEQ * T_LAT, D)
    pe = _pos_embed(T_LAT // GRID_W)

    cond8 = jnp.concatenate([c_ctx[None, :], c, jnp.zeros((8 - 1 - N_LAT_SEQ, D), F32)], axis=0)
    mod3 = _adaln(cond8, ada_w[0], ada_b[0][None, :]).reshape(8, 1, N_MOD * D)

    zs, zg, zc = _inproj(xp, xs, pe, mod3, norm1_g, w_in[0].astype(BF16))

    zero = jnp.zeros((2, LORA, D_RWKV), F32)
    lora2 = jnp.concatenate([jnp.concatenate([decay_lora_b[0], zero], axis=2),
                             jnp.concatenate([zero, iclr_lora_b[0]], axis=2)], axis=1)
    vec = lambda p: p.reshape(2, 1, -1)
    prep_out = _prep(zs, vec(tshift_mu[0]), vec(decay_w0[0]), vec(iclr_a0[0]), lora2,
                     vec(key_k[0]), vec(key_a[0]), vec(bonus_r_k[0]), *prep_consts)
    bonus = prep_out[10]

    s0t = state_rwkv[:, 0].transpose(1, 0, 2, 4, 3)
    eye_h = jnp.eye(N_HEADS, dtype=F32)
    s0_wide = (s0t[:, :, :, :, None, :] * eye_h[None, None, :, None, :, None]).reshape(
        2, N_LAT_SEQ, D_RWKV, D_RWKV)
    yf, yb, s_fin = _scan(prep_out, s0_wide, scan_consts)

    yconv = _conv(zc, jnp.concatenate([conv_dw_w[0], jnp.zeros((1, D_CONV), F32)], axis=0),
                  conv_dw_b, conv_ln_g, conv_ln_b)

    router_w = jnp.concatenate([router_group_w[0], router_expert_w[0],
                                jnp.zeros((D, ROUTE_LANES - N_GROUPS - N_EXPERTS), F32)], axis=1)
    router_b = jnp.concatenate([router_group_b[0], router_expert_b[0],
                                jnp.zeros((ROUTE_LANES - N_GROUPS - N_EXPERTS,), F32)])[None, :]
    x1, h2, comb, comb_t = _outproj(xp, xs, pe, mod3, yf, yb, bonus, zg, yconv,
                                    gn_g, gn_b, gate_lora_b[0].astype(BF16), w_out[0].astype(BF16), norm2_g,
                                    router_w, router_b, ones512)

    tok = np.arange(TM_MOE)
    triu = jnp.asarray(tok[:, None] < tok[None, :], BF16)
    drow, dcol, seg_start, seg_ntile = _plan(comb_t, triu)
    seg_start = seg_start[:, :N_GROUPS, 0].reshape(-1)
    seg_ntile = seg_ntile[:, :N_GROUPS, 0].reshape(-1)
    moe = _moe(h2, comb, drow, dcol, seg_start, seg_ntile,
               expert_w_gate[0].astype(BF16), expert_w_up[0].astype(BF16), expert_w_down[0].astype(BF16))
    fg = final_norm_g[None, :]
    y_prompt = _final(x1, moe, mod3, fg, 0, N_CTX_TILES, lambda i: i, "final_norm_ctx")
    y_sample = _final(x1, moe, mod3, fg, N_CTX_TILES, N_LAT_TILES, _xs_block, "final_norm_lat")
    y_prompt = y_prompt.reshape(N_CTX_SEQ, T_CTX, D)
    y_sample = y_sample.reshape(N_LAT_SEQ, T_LAT, D)
    s_ctx = s_fin[:N_CTX_SEQ].reshape(N_CTX_SEQ, 2, N_HEADS, HEAD, HEAD).transpose(0, 1, 2, 4, 3)
    new_state = s_ctx[:, None].astype(state_rwkv.dtype)
    return (y_prompt, y_sample, new_state)
```

```python
import numpy as np
import jax
import jax.numpy as jnp
from jax import lax
from jax.experimental import pallas as pl
from jax.experimental.pallas import tpu as pltpu

F32 = jnp.float32
BF16 = jnp.bfloat16

D = 1024
N_CTX_SEQ = 16
T_CTX = 256
N_LAT_SEQ = 2
T_LAT = 1024
TM = 256
N_CTX_TILES = N_CTX_SEQ * T_CTX // TM
LAT_CHUNKS = T_LAT // TM
N_LAT_TILES = N_LAT_SEQ * LAT_CHUNKS
N_TILES = N_CTX_TILES + N_LAT_TILES
NT = N_TILES * TM
GRID_W = 64
D_RWKV = 512
D_CONV = 512
HEAD = 64
N_HEADS = 8
CONV_W = 31
CONV_PAD = CONV_W // 2
LORA = 64
LORA_G = 128
SHIFT_COLS = 3 * D_RWKV + 2 * LORA
N_GROUPS = 4
N_EXP_PER_GROUP = 8
N_EXPERTS = 32
D_EXPERT = 256
N_MOD = 6
RMS_EPS = 1e-6
LN_EPS = 1e-5
GN_EPS = 64e-5
LANES = 128
SUB = 8
CH = 32
NCH = TM // CH
HC = N_HEADS * CH
DOUBLINGS = 4
ROUTE_LANES = 128
E_LANE0 = N_GROUPS
GROUP_LANE = 36
TM_MOE = 1024
SEG = 64
EXP_PER_STEP = 4
MOE_WINDOW = 5
MOE_ROWS = TM_MOE + N_GROUPS * SEG + (MOE_WINDOW - 1) * SEG
N_MOE_TILES = NT // TM_MOE
VMEM_LIMIT = 56 * 1024 * 1024


def _cp(sem, flags=None):
    return pltpu.CompilerParams(dimension_semantics=sem, vmem_limit_bytes=VMEM_LIMIT, flags=flags)


def _split2(a):
    hi = a.astype(BF16)
    lo = (a - hi.astype(F32)).astype(BF16)
    return hi, lo


def _dot(a, b):
    return jnp.dot(a, b, preferred_element_type=F32)


def _dot_hp(a, b):
    ah, al = _split2(a)
    bh, bl = _split2(b)
    return _dot(ah, bh) + _dot(ah, bl) + _dot(al, bh)


def _dot_sel(a, sel):
    h, l = _split2(a)
    return _dot(h, sel) + _dot(l, sel)


def _sel_dot(sel, a):
    h, l = _split2(a)
    return _dot(sel, h) + _dot(sel, l)


def _sigmoid(x):
    return 1.0 / (1.0 + jnp.exp(-x))


def _silu(x):
    return x * _sigmoid(x)


def _lat_js(i):
    il = jnp.maximum(i - N_CTX_TILES, 0)
    return il // N_LAT_SEQ, il % N_LAT_SEQ


def _xp_block(i):
    return jnp.minimum(i, N_CTX_TILES - 1)


def _xs_block(i):
    j, s = _lat_js(i)
    return s * LAT_CHUNKS + j


def _pe_block(i):
    j, _ = _lat_js(i)
    return j


def _mod_row(i):
    _, s = _lat_js(i)
    return jnp.where(i < N_CTX_TILES, 0, 1 + s)


def _first_last(i):
    j, _ = _lat_js(i)
    is_ctx = i < N_CTX_TILES
    first = jnp.logical_or(is_ctx, j == 0)
    last = jnp.logical_or(is_ctx, j == LAT_CHUNKS - 1)
    return first, last


def _prev_tile(i):
    first, _ = _first_last(i)
    return jnp.where(first, i, i - N_LAT_SEQ)


def _next_tile(i):
    _, last = _first_last(i)
    return jnp.where(last, i, i + N_LAT_SEQ)


def _adaln_kernel(c_ref, w_ref, b_ref, o_ref):
    c = c_ref[...]
    o_ref[...] = _dot_hp(_silu(c), w_ref[...]) + b_ref[...]


def _adaln(cond8, ada_w, ada_b):
    tn = 1536
    n = ada_w.shape[1]
    return pl.pallas_call(
        _adaln_kernel,
        grid=(n // tn,),
        in_specs=[pl.BlockSpec((8, D), lambda j: (0, 0)),
                  pl.BlockSpec((D, tn), lambda j: (0, j)),
                  pl.BlockSpec((1, tn), lambda j: (0, j))],
        out_specs=pl.BlockSpec((8, tn), lambda j: (0, j)),
        out_shape=jax.ShapeDtypeStruct((8, n), F32),
        compiler_params=_cp(("parallel",)),
        name="adaln",
    )(cond8, ada_w, ada_b)


def _load_x(i, xp_ref, xs_ref, pe_ref):
    f = (i >= N_CTX_TILES).astype(F32)
    return xp_ref[...] * (1.0 - f) + (xs_ref[...] + pe_ref[...]) * f


def _x_specs():
    return [pl.BlockSpec((TM, D), lambda i: (_xp_block(i), 0)),
            pl.BlockSpec((TM, D), lambda i: (_xs_block(i), 0)),
            pl.BlockSpec((TM, D), lambda i: (_pe_block(i), 0))]


def _rms(x, g):
    return x * lax.rsqrt(jnp.mean(x * x, axis=-1, keepdims=True) + RMS_EPS) * g


def _inproj_kernel(xp_ref, xs_ref, pe_ref, mod_ref, g_ref, w_ref, zs_ref, zg_ref, zc_ref):
    i = pl.program_id(0)
    x = _load_x(i, xp_ref, xs_ref, pe_ref)
    sh1 = mod_ref[:, 0:D]
    sc1 = mod_ref[:, D:2 * D]
    h = (_rms(x, g_ref[...]) * (1.0 + sc1) + sh1).astype(BF16)
    zs_ref[...] = _dot(h, w_ref[:, 0:SHIFT_COLS])
    zg_ref[...] = _dot(h, w_ref[:, SHIFT_COLS:SHIFT_COLS + LORA_G])
    zc_ref[...] = _dot(h, w_ref[:, SHIFT_COLS + LORA_G:])


def _inproj(xp, xs, pe, mod3, norm1_g, w_in_bf):
    in_cols = w_in_bf.shape[1]
    return pl.pallas_call(
        _inproj_kernel,
        grid=(N_TILES,),
        in_specs=_x_specs() + [
            pl.BlockSpec((None, 1, N_MOD * D), lambda i: (_mod_row(i), 0, 0)),
            pl.BlockSpec((1, D), lambda i: (0, 0)),
            pl.BlockSpec((D, in_cols), lambda i: (0, 0))],
        out_specs=[pl.BlockSpec((TM, SHIFT_COLS), lambda i: (i, 0)),
                   pl.BlockSpec((TM, LORA_G), lambda i: (i, 0)),
                   pl.BlockSpec((TM, 2 * D_CONV), lambda i: (i, 0))],
        out_shape=[jax.ShapeDtypeStruct((NT, SHIFT_COLS), F32),
                   jax.ShapeDtypeStruct((NT, LORA_G), F32),
                   jax.ShapeDtypeStruct((NT, 2 * D_CONV), F32)],
        compiler_params=_cp(("parallel",)),
        name="inproj",
    )(xp, xs, pe, mod3, norm1_g, w_in_bf)


NT_DIMS = (((1,), (1,)), ((), ()))


def _tile_rows(x):
    return jnp.concatenate([x] * N_HEADS, axis=0)


def _prep_kernel(zs_ref, prev_ref, next_ref, mu_ref, w0_ref, a0_ref, lora_ref, kk_ref_, ka_ref,
                 rk_ref, ones_ref, cum_ref, tot_ref, sel_ref, hm_ref, bm_ref, msk_ref, eye_ref,
                 at_out, rt_out, bh_out, kh_out, v_out, t_out, aak_out, arb_out, ark_out, gc_out, bon_out):
    i = pl.program_id(0)
    first, last = _first_last(i)
    cur = zs_ref[...]
    prow = prev_ref[7:8, :] * (1.0 - first.astype(F32))
    nrow = next_ref[0:1, :] * (1.0 - last.astype(F32))
    rows = lax.broadcasted_iota(jnp.int32, (TM, 1), 0)
    shifted = (jnp.where(rows == 0, prow, pltpu.roll(cur, 1, axis=0)),
               jnp.where(rows == TM - 1, nrow, pltpu.roll(cur, TM - 1, axis=0)))
    ones = ones_ref[...]
    lane = lax.broadcasted_iota(jnp.int32, (TM, 2 * LORA), 1)
    scaled = []
    for d in range(2):
        xs = cur + (shifted[d] - cur) * mu_ref[d]
        r = xs[:, 0:D_RWKV]
        k = xs[:, D_RWKV:2 * D_RWKV]
        v = xs[:, 2 * D_RWKV:3 * D_RWKV]
        z2 = xs[:, 3 * D_RWKV:SHIFT_COLS]
        lin = jnp.where(lane < LORA, jnp.tanh(z2), z2)
        lo = _dot_hp(lin, lora_ref[d])
        u = -(w0_ref[d] + lo[:, 0:D_RWKV])
        softplus = jnp.maximum(u, 0.0) + jnp.log1p(jnp.exp(-jnp.abs(u)))
        w_log = -softplus - 0.5
        lw = -jnp.exp(w_log)
        a = _sigmoid(a0_ref[d] + lo[:, D_RWKV:])
        kx = k * kk_ref_[d]
        nrm = jnp.sqrt(_dot_sel(kx * kx, ones))
        kk = kx / jnp.maximum(nrm, 1e-12)
        k2 = k * (1.0 + (a - 1.0) * ka_ref[d])
        bvec = kk * a
        bon_out[d] = _dot_sel(r * k2 * rk_ref[d], ones) * v

        lg = _sel_dot(cum_ref[d], lw)
        tot = _sel_dot(tot_ref[...], lw)
        e_tail = jnp.exp(tot - lg)
        e_inv = jnp.exp(-lg)
        at = (-(kk * jnp.exp(lg - lw))).astype(BF16)
        bt = (bvec * e_inv).astype(BF16)
        kt = (k2 * e_inv).astype(BF16)
        rt = (r * jnp.exp(lg)).astype(BF16)
        at_out[d] = at
        rt_out[d] = rt
        bh_out[d] = (bvec * e_tail).astype(BF16)
        kh_out[d] = (k2 * e_tail).astype(BF16)
        v_out[d] = v.astype(BF16)
        gc_out[d] = jnp.exp(_sel_dot(sel_ref[...], lw))
        scaled.append((at, bt, kt, rt))

    hm = hm_ref[...]
    bm = bm_ref[...]

    def expand(z):
        return _tile_rows(z.astype(BF16)) * bm

    chunks = [(d, slice(c * CH, (c + 1) * CH)) for d in range(2) for c in range(NCH)]
    pws = []
    for d, rows in chunks:
        at, bt, kt, rt = scaled[d]
        ar = jnp.concatenate([at[rows], rt[rows]], axis=0)
        bk = jnp.concatenate([_tile_rows(bt[rows]) * hm, _tile_rows(kt[rows]) * hm], axis=0)
        p1 = lax.dot_general(ar, bk, NT_DIMS, preferred_element_type=F32)
        m_strict, m_incl = msk_ref[d, 0], msk_ref[d, 1]
        pws.append(p1[0:CH, 0:HC] * m_strict)
        aak_out[d, rows, :] = (p1[0:CH, HC:] * m_strict).astype(BF16)
        arb_out[d, rows, :] = (p1[CH:, 0:HC] * m_incl).astype(BF16)
        ark_out[d, rows, :] = (p1[CH:, HC:] * m_incl).astype(BF16)
    tms = [eye_ref[...] + pw for pw in pws]
    pws = [_dot(pw.astype(BF16), expand(pw)) for pw in pws]
    for k in range(1, DOUBLINGS + 1):
        if k < DOUBLINGS:
            prods = [_dot(jnp.concatenate([pw, tm], axis=0).astype(BF16), expand(pw))
                     for pw, tm in zip(pws, tms)]
            pws = [p[0:CH] for p in prods]
            tms = [tm + p[CH:] for tm, p in zip(tms, prods)]
        else:
            tms = [tm + _dot(tm.astype(BF16), expand(pw)) for tm, pw in zip(tms, pws)]
    for (d, rows), tm in zip(chunks, tms):
        t_out[d, rows, :] = tm.astype(BF16)


def _prep(zs, mu, w0, a0, lora2, key_k, key_a, r_k, ones512, cum2, tot, sel8, hm, bm, masks, eye):
    full = lambda a: pl.BlockSpec(a.shape, lambda i: (0,) * a.ndim)
    out_spec = pl.BlockSpec((2, TM, D_RWKV), lambda i: (0, i, 0))
    bf_shape = jax.ShapeDtypeStruct((2, NT, D_RWKV), BF16)
    tall_spec = pl.BlockSpec((2, TM, HC), lambda i: (0, i, 0))
    rows8 = TM // 8
    params = (mu, w0, a0, lora2, key_k, key_a, r_k, ones512, cum2, tot, sel8, hm, bm, masks, eye)
    return pl.pallas_call(
        _prep_kernel,
        grid=(N_TILES,),
        in_specs=[pl.BlockSpec((TM, SHIFT_COLS), lambda i: (i, 0)),
                  pl.BlockSpec((8, SHIFT_COLS), lambda i: (_prev_tile(i) * rows8 + rows8 - 1, 0)),
                  pl.BlockSpec((8, SHIFT_COLS), lambda i: (_next_tile(i) * rows8, 0))]
                 + [full(p) for p in params],
        out_specs=[out_spec] * 5 + [tall_spec] * 4 + [
            pl.BlockSpec((2, NCH, D_RWKV), lambda i: (0, i, 0)), out_spec],
        out_shape=[bf_shape] * 5 + [jax.ShapeDtypeStruct((2, NT, HC), BF16)] * 4 + [
            jax.ShapeDtypeStruct((2, N_TILES * NCH, D_RWKV), F32),
            jax.ShapeDtypeStruct((2, NT, D_RWKV), F32)],
        compiler_params=_cp(("parallel",)),
        name="rwkv_prep",
    )(zs, zs, zs, *params)


N_SCAN_STEPS = N_CTX_TILES // 2 + LAT_CHUNKS
LAT_STEP0 = N_CTX_TILES // 2


def _scan_block(step, d):
    lat = step >= LAT_STEP0
    jl = step - LAT_STEP0
    j = jnp.where(d == 0, jl, LAT_CHUNKS - 1 - jl)
    return jnp.where(lat, LAT_STEP0 + j, step)


def _scan_state_block(step):
    return jnp.minimum(step, LAT_STEP0)


def _chunk_kernel(*refs):
    (atf, rtf, bhf, khf, vf, tf, aakf, arbf, arkf, gcf, atb, rtb, bhb, khb, vb, tb, aakb, arbb, arkb, gcb,
     s0_ref, hm_ref, bd_ref, fold_ref, yf_ref, yb_ref, sout_ref, mw) = refs
    step = pl.program_id(0)
    in_refs = ((atf, rtf, bhf, khf, vf, tf, aakf, arbf, arkf, gcf),
               (atb, rtb, bhb, khb, vb, tb, aakb, arbb, arkb, gcb))
    y_refs = (yf_ref, yb_ref)

    @pl.when(step < LAT_STEP0)
    def _zero_state():
        mw[...] = jnp.zeros(mw.shape, F32)

    @pl.when(step == LAT_STEP0)
    def _load_state():
        mw[...] = s0_ref[...]

    hm = hm_ref[...]
    bd = bd_ref[...]
    crow = lax.broadcasted_iota(jnp.int32, (NCH, 1), 0)

    def chunk_body(q, carry):
        chains = []
        for dl in range(2):
            cc = q if dl == 0 else NCH - 1 - q
            for sl in range(2):
                chains.append((dl, sl, cc, pl.ds(pl.multiple_of(sl * TM + cc * CH, CH), CH)))
        ld = lambda idx, dl, rows: in_refs[dl][idx][rows, :]
        m0s = [mw[dl, sl] for dl, sl, _, _ in chains]
        xy0s = [_dot(jnp.concatenate([ld(0, dl, rows), ld(1, dl, rows)], axis=0), m0.astype(BF16))
                for (dl, _, _, rows), m0 in zip(chains, m0s)]
        vvs = [ld(4, dl, rows) for dl, _, _, rows in chains]
        avs = [_dot(jnp.concatenate([ld(6, dl, rows), ld(8, dl, rows)], axis=0), _tile_rows(vv) * hm)
               for (dl, _, _, rows), vv in zip(chains, vvs)]
        ubs = [_dot(ld(5, dl, rows), _tile_rows((xy0[0:CH] + av[0:CH]).astype(BF16)) * hm).astype(BF16)
               for (dl, _, _, rows), xy0, av in zip(chains, xy0s, avs)]
        for (dl, _, _, rows), xy0, av, ub in zip(chains, xy0s, avs, ubs):
            y_refs[dl][rows, :] = xy0[CH:] + av[CH:] + _dot(ld(7, dl, rows), _tile_rows(ub) * hm)
        for (dl, sl, cc, rows), m0, ub, vv in zip(chains, m0s, ubs, vvs):
            gcrow = jnp.sum(jnp.where(crow == cc, in_refs[dl][9][sl * NCH:(sl + 1) * NCH, :], 0.0),
                            axis=0, keepdims=True)
            pad = jnp.zeros((LANES - 2 * CH - SUB, D_RWKV), F32)
            stack = jnp.concatenate([ld(2, dl, rows).astype(F32), ld(3, dl, rows).astype(F32),
                                     jnp.broadcast_to(gcrow, (SUB, D_RWKV)), pad], axis=0)
            stack_t = stack.T
            uv = jnp.concatenate([ub, vv, jnp.zeros((LANES - 2 * CH, D_RWKV), BF16)], axis=0)
            upd = _dot(stack_t.astype(BF16), uv)
            mw[dl, sl] = (m0 * stack_t[:, 2 * CH:2 * CH + 1] + upd) * bd
        return carry

    lax.fori_loop(0, NCH, chunk_body, 0)

    @pl.when(jnp.logical_or(step < LAT_STEP0, step == N_SCAN_STEPS - 1))
    def _final():
        for dl in range(2):
            for sl in range(2):
                sout_ref[sl, dl] = _dot_sel(mw[dl, sl], fold_ref[...])


def _scan(prep_out, s0_wide, consts):
    def row_spec(d, width):
        return pl.BlockSpec((None, 2 * TM, width), lambda s: (d, _scan_block(s, d), 0))

    def gc_spec(d):
        return pl.BlockSpec((None, 2 * NCH, D_RWKV), lambda s: (d, _scan_block(s, d), 0))

    def y_spec(d):
        return pl.BlockSpec((2 * TM, D_RWKV), lambda s: (_scan_block(s, d), 0))

    rows, gc = list(prep_out[:9]), prep_out[9]
    row_specs = lambda d: [row_spec(d, D_RWKV)] * 5 + [row_spec(d, HC)] * 4 + [gc_spec(d)]
    const = lambda a: pl.BlockSpec(a.shape, lambda s: (0,) * a.ndim)
    y_shape = jax.ShapeDtypeStruct((NT, D_RWKV), F32)
    return pl.pallas_call(
        _chunk_kernel,
        grid=(N_SCAN_STEPS,),
        in_specs=row_specs(0) + row_specs(1) + [const(s0_wide)] + [const(a) for a in consts],
        out_specs=[y_spec(0), y_spec(1),
                   pl.BlockSpec((2, 2, D_RWKV, HEAD), lambda s: (_scan_state_block(s), 0, 0, 0))],
        out_shape=[y_shape, y_shape,
                   jax.ShapeDtypeStruct(((LAT_STEP0 + 1) * 2, 2, D_RWKV, HEAD), F32)],
        scratch_shapes=[pltpu.VMEM((2, 2, D_RWKV, D_RWKV), F32)],
        compiler_params=_cp(("arbitrary",)),
        name="rwkv_scan",
    )(*rows, gc, *rows, gc, s0_wide, *consts)


HALO = 16


def _glu(z):
    return z[:, 0:D_CONV] * _sigmoid(z[:, D_CONV:])


def _conv_kernel(cur_ref, prev_ref, next_ref, w_ref, b_ref, g_ref, beta_ref, o_ref, ext):
    i = pl.program_id(0)
    first, last = _first_last(i)
    n_ext = TM + 2 * HALO
    u = jnp.concatenate([_glu(prev_ref[...]) * (1.0 - first.astype(F32)), _glu(cur_ref[...]),
                         _glu(next_ref[...]) * (1.0 - last.astype(F32))], axis=0)
    ext[0] = u
    for b in range(1, SUB):
        ext[b] = pltpu.roll(u, n_ext - b, axis=0)
    acc = jnp.zeros((TM, D_CONV), F32)
    for j in range(CONV_W):
        off = HALO - CONV_PAD + j
        acc = acc + ext[off % SUB, off - off % SUB:off - off % SUB + TM, :] * w_ref[j:j + 1, :]
    h = acc + b_ref[...]
    mu = jnp.mean(h, axis=-1, keepdims=True)
    hc = h - mu
    var = jnp.mean(hc * hc, axis=-1, keepdims=True)
    y = hc * lax.rsqrt(var + LN_EPS) * g_ref[...] + beta_ref[...]
    o_ref[...] = _silu(y)


def _conv(zc, conv_w, conv_b, ln_g, ln_b):
    nh = TM // HALO
    vec = pl.BlockSpec((1, D_CONV), lambda i: (0, 0))
    return pl.pallas_call(
        _conv_kernel,
        grid=(N_TILES,),
        in_specs=[pl.BlockSpec((TM, 2 * D_CONV), lambda i: (i, 0)),
                  pl.BlockSpec((HALO, 2 * D_CONV), lambda i: (_prev_tile(i) * nh + nh - 1, 0)),
                  pl.BlockSpec((HALO, 2 * D_CONV), lambda i: (_next_tile(i) * nh, 0)),
                  pl.BlockSpec((CONV_W + 1, D_CONV), lambda i: (0, 0)), vec, vec, vec],
        out_specs=pl.BlockSpec((TM, D_CONV), lambda i: (i, 0)),
        out_shape=jax.ShapeDtypeStruct((NT, D_CONV), F32),
        scratch_shapes=[pltpu.VMEM((SUB, TM + 2 * HALO, D_CONV), F32)],
        compiler_params=_cp(("parallel",)),
        name="conv_module",
    )(zc, zc, zc, conv_w, conv_b, ln_g, ln_b)


def _outproj_kernel(xp_ref, xs_ref, pe_ref, mod_ref, yf_ref, yb_ref, bf_ref, bb_ref, zg_ref, yc_ref,
                    gng_ref, gnb_ref, gl_ref, wo_ref, n2_ref, rw_ref, rb_ref, ones_ref,
                    x1_ref, h2_ref, logit_ref):
    i = pl.program_id(0)
    x = _load_x(i, xp_ref, xs_ref, pe_ref)
    g1 = mod_ref[:, 2 * D:3 * D]
    sh2 = mod_ref[:, 3 * D:4 * D]
    sc2 = mod_ref[:, 4 * D:5 * D]
    ones = ones_ref[...]
    y = (yf_ref[...] + bf_ref[...]) + (yb_ref[...] + bb_ref[...])
    mu = _dot_sel(y, ones) * (1.0 / HEAD)
    yc = y - mu
    var = _dot_sel(yc * yc, ones) * (1.0 / HEAD)
    yn = yc * lax.rsqrt(var + GN_EPS) * gng_ref[...] + gnb_ref[...]
    gate = _dot(_sigmoid(zg_ref[...]).astype(BF16), gl_ref[...])
    y_rwkv = (yn * gate).astype(BF16)
    mix = _dot(y_rwkv, wo_ref[0:D_RWKV, :]) + _dot(yc_ref[...].astype(BF16), wo_ref[D_RWKV:, :])
    x1 = x + g1 * mix
    x1_ref[...] = x1
    h2 = _rms(x1, n2_ref[...]) * (1.0 + sc2) + sh2
    h2_ref[...] = h2.astype(BF16)

    logit_ref[...] = _dot_hp(h2, rw_ref[...]) + rb_ref[...]


def _route(logits):
    lane = lax.broadcasted_iota(jnp.int32, logits.shape, 1)
    lanef = lane.astype(F32)
    neg = jnp.float32(-1e30)
    big = jnp.float32(1e9)
    gmask = lane < N_GROUPS
    gl = jnp.where(gmask, logits, neg)
    ge = jnp.where(gmask, jnp.exp(gl - jnp.max(gl, axis=-1, keepdims=True)), 0.0)
    gprob = ge / jnp.sum(ge, axis=-1, keepdims=True)
    gp = jnp.max(gprob, axis=-1, keepdims=True)
    gidx = jnp.min(jnp.where(jnp.logical_and(gmask, gprob == gp), lanef, big), axis=-1, keepdims=True)
    egrp = jnp.floor((lanef - float(E_LANE0)) * (1.0 / N_EXP_PER_GROUP))
    emask = jnp.logical_and(jnp.logical_and(lane >= E_LANE0, lane < E_LANE0 + N_EXPERTS), egrp == gidx)
    el = jnp.where(emask, logits, neg)
    ee = jnp.where(emask, jnp.exp(el - jnp.max(el, axis=-1, keepdims=True)), 0.0)
    ep = ee / jnp.sum(ee, axis=-1, keepdims=True)
    m1 = jnp.max(jnp.where(emask, ep, -1.0), axis=-1, keepdims=True)
    i1 = jnp.min(jnp.where(jnp.logical_and(emask, ep == m1), lanef, big), axis=-1, keepdims=True)
    mask2 = jnp.logical_and(emask, lanef != i1)
    m2 = jnp.max(jnp.where(mask2, ep, -1.0), axis=-1, keepdims=True)
    i2 = jnp.min(jnp.where(jnp.logical_and(mask2, ep == m2), lanef, big), axis=-1, keepdims=True)
    den = m1 + m2
    return (jnp.where(lanef == i1, gp * (m1 / den), 0.0)
            + jnp.where(lanef == i2, gp * (m2 / den), 0.0)
            + jnp.where(lane == GROUP_LANE, gidx, 0.0))


def _outproj(xp, xs, pe, mod3, yf, yb, bonus, zg, yconv, gn_g, gn_b, gate_bf, w_out_bf, norm2_g,
             router_w, router_b, ones512):
    tile = lambda n: pl.BlockSpec((TM, n), lambda i: (i, 0))
    const = lambda shape: pl.BlockSpec(shape, lambda i: (0,) * len(shape))
    return pl.pallas_call(
        _outproj_kernel,
        grid=(N_TILES,),
        in_specs=_x_specs() + [
            pl.BlockSpec((None, 1, N_MOD * D), lambda i: (_mod_row(i), 0, 0)),
            tile(D_RWKV), tile(D_RWKV),
            pl.BlockSpec((None, TM, D_RWKV), lambda i: (0, i, 0)),
            pl.BlockSpec((None, TM, D_RWKV), lambda i: (1, i, 0)),
            tile(LORA_G), tile(D_CONV),
            const((1, D_RWKV)), const((1, D_RWKV)), const((LORA_G, D_RWKV)), const((D, D)),
            const((1, D)), const((D, ROUTE_LANES)), const((1, ROUTE_LANES)),
            const((D_RWKV, D_RWKV))],
        out_specs=[tile(D), tile(D), tile(ROUTE_LANES)],
        out_shape=[jax.ShapeDtypeStruct((NT, D), F32), jax.ShapeDtypeStruct((NT, D), BF16),
                   jax.ShapeDtypeStruct((NT, ROUTE_LANES), F32)],
        compiler_params=_cp(("parallel",)),
        name="outproj_router",
    )(xp, xs, pe, mod3, yf, yb, bonus, bonus, zg, yconv, gn_g, gn_b, gate_bf, w_out_bf, norm2_g,
      router_w, router_b, ones512)


def _plan_kernel(logit_ref, triu_ref, comb_ref, drow_ref, dcol_ref, start_ref, ntile_ref):
    comb = _route(logit_ref[...])
    comb_ref[...] = comb
    gidx = comb.T[GROUP_LANE:GROUP_LANE + 1, :]
    grow = lax.broadcasted_iota(jnp.int32, (SUB, 1), 0)
    growf = grow.astype(F32)
    onehot = jnp.where(jnp.logical_and(gidx == growf, grow < N_GROUPS), 1.0, 0.0)
    before = _dot(onehot.astype(BF16), triu_ref[...])
    count = jnp.sum(onehot, axis=-1, keepdims=True)
    padded = jnp.floor((count + (SEG - 1.0)) * (1.0 / SEG)) * SEG
    start = jnp.zeros((SUB, 1), F32)
    for g in range(N_GROUPS - 1):
        start = start + jnp.where(grow > g, padded[g:g + 1, :], 0.0)
    dest = jnp.sum(onehot * (start + before), axis=0, keepdims=True)
    drow_ref[...] = jnp.broadcast_to(dest, (SUB, TM_MOE))
    dcol_ref[...] = jnp.broadcast_to(dest, (LANES, TM_MOE)).T
    start_ref[...] = jnp.broadcast_to(start * (1.0 / SEG), (SUB, LANES)).astype(jnp.int32)
    ntile_ref[...] = jnp.broadcast_to(padded * (1.0 / SEG), (SUB, LANES)).astype(jnp.int32)


def _plan(logits, triu):
    seg_shape = jax.ShapeDtypeStruct((N_MOE_TILES, SUB, LANES), jnp.int32)
    seg_spec = pl.BlockSpec((None, SUB, LANES), lambda t: (t, 0, 0))
    tok_spec = pl.BlockSpec((TM_MOE, LANES), lambda t: (t, 0))
    return pl.pallas_call(
        _plan_kernel,
        grid=(N_MOE_TILES,),
        in_specs=[tok_spec, pl.BlockSpec((TM_MOE, TM_MOE), lambda t: (0, 0))],
        out_specs=[tok_spec, pl.BlockSpec((SUB, TM_MOE), lambda t: (0, t)), tok_spec, seg_spec, seg_spec],
        out_shape=[jax.ShapeDtypeStruct((NT, ROUTE_LANES), F32), jax.ShapeDtypeStruct((SUB, NT), F32),
                   jax.ShapeDtypeStruct((NT, LANES), F32), seg_shape, seg_shape],
        compiler_params=_cp(("parallel",)),
        name="moe_plan",
    )(logits, triu)


def _moe_kernel(start_ref, ntile_ref, h_ref, comb_ref, drow_ref, dcol_ref, wg_ref, wu_ref, wd_ref,
                o_ref, xs, cws, ys):
    t = pl.program_id(0)
    q = pl.program_id(1)
    g = q // (N_EXP_PER_GROUP // EXP_PER_STEP)

    @pl.when(q == 0)
    def _sort_in():
        slot = lax.broadcasted_iota(jnp.int32, (MOE_ROWS, TM_MOE), 0).astype(F32)
        perm = jnp.where(slot == drow_ref[0:1, :], 1.0, 0.0).astype(BF16)
        xs[...] = _dot(perm, h_ref[...]).astype(BF16)
        ch, cl = _split2(comb_ref[...])
        cws[...] = _dot(perm, ch) + _dot(perm, cl)
        ys[...] = jnp.zeros(ys.shape, F32)

    seg = t * N_GROUPS + g
    first = start_ref[seg]

    def visit(row0, n_rows):
        rows = pl.ds(pl.multiple_of(row0 * SEG, SEG), n_rows)
        x = xs[rows, :]
        cw_all = cws[rows, :]
        lane = lax.broadcasted_iota(jnp.int32, (n_rows, ROUTE_LANES), 1)
        acc = ys[rows, :]
        for j in range(EXP_PER_STEP):
            e_lane = q * EXP_PER_STEP + j + E_LANE0
            cw = jnp.sum(jnp.where(lane == e_lane, cw_all, 0.0), axis=-1, keepdims=True)
            hid = (_silu(_dot(x, wg_ref[j])) * _dot(x, wu_ref[j]) * cw).astype(BF16)
            acc = acc + _dot(hid, wd_ref[j])
        ys[rows, :] = acc

    visit(jnp.minimum(first, MOE_ROWS // SEG - MOE_WINDOW), MOE_WINDOW * SEG)

    def sub_tile(i, carry):
        visit(first + i, SEG)
        return carry

    lax.fori_loop(MOE_WINDOW, ntile_ref[seg], sub_tile, 0)

    @pl.when(q == N_EXPERTS // EXP_PER_STEP - 1)
    def _sort_out():
        slot = lax.broadcasted_iota(jnp.int32, (TM_MOE, MOE_ROWS), 1).astype(F32)
        perm_t = jnp.where(slot == dcol_ref[:, 0:1], 1.0, 0.0).astype(BF16)
        o_ref[...] = _dot(perm_t, ys[...].astype(BF16))


def _moe(h2, comb, drow, dcol, seg_start, seg_ntile, wg, wu, wd):
    grid_spec = pltpu.PrefetchScalarGridSpec(
        num_scalar_prefetch=2,
        grid=(N_MOE_TILES, N_EXPERTS // EXP_PER_STEP),
        in_specs=[pl.BlockSpec((TM_MOE, D), lambda t, q, s, n: (t, 0)),
                  pl.BlockSpec((TM_MOE, ROUTE_LANES), lambda t, q, s, n: (t, 0)),
                  pl.BlockSpec((SUB, TM_MOE), lambda t, q, s, n: (0, t)),
                  pl.BlockSpec((TM_MOE, LANES), lambda t, q, s, n: (t, 0)),
                  pl.BlockSpec((EXP_PER_STEP, D, D_EXPERT), lambda t, q, s, n: (q, 0, 0)),
                  pl.BlockSpec((EXP_PER_STEP, D, D_EXPERT), lambda t, q, s, n: (q, 0, 0)),
                  pl.BlockSpec((EXP_PER_STEP, D_EXPERT, D), lambda t, q, s, n: (q, 0, 0))],
        out_specs=pl.BlockSpec((TM_MOE, D), lambda t, q, s, n: (t, 0)),
        scratch_shapes=[pltpu.VMEM((MOE_ROWS, D), BF16), pltpu.VMEM((MOE_ROWS, ROUTE_LANES), F32),
                        pltpu.VMEM((MOE_ROWS, D), F32)])
    return pl.pallas_call(
        _moe_kernel,
        grid_spec=grid_spec,
        out_shape=jax.ShapeDtypeStruct((NT, D), F32),
        compiler_params=_cp(("parallel", "arbitrary")),
        name="moe_experts",
    )(seg_start, seg_ntile, h2, comb, drow, dcol, wg, wu, wd)


def _final_kernel(x1_ref, moe_ref, mod_ref, g_ref, o_ref):
    g2 = mod_ref[:, 5 * D:6 * D]
    x2 = x1_ref[...] + g2 * moe_ref[...]
    o_ref[...] = _rms(x2, g_ref[...])


def _final(x1, moe, mod3, final_g, tile0, n_tiles, out_block, name):
    tile = pl.BlockSpec((TM, D), lambda i: (tile0 + i, 0))
    return pl.pallas_call(
        _final_kernel,
        grid=(n_tiles,),
        in_specs=[tile, tile, pl.BlockSpec((None, 1, N_MOD * D), lambda i: (_mod_row(tile0 + i), 0, 0)),
                  pl.BlockSpec((1, D), lambda i: (0, 0))],
        out_specs=pl.BlockSpec((TM, D), lambda i: (out_block(tile0 + i), 0)),
        out_shape=jax.ShapeDtypeStruct((n_tiles * TM, D), F32),
        compiler_params=_cp(("parallel",)),
        name=name,
    )(x1, moe, mod3, final_g)


def _pos_embed(rows):
    t = np.arange(rows * GRID_W)
    row = (t // GRID_W).astype(np.float32)
    col = (t % GRID_W).astype(np.float32)
    quarter = D // 4
    freqs = (1.0 / (10000.0 ** (np.arange(quarter, dtype=np.float32) / quarter))).astype(np.float32)
    ang_r = row[:, None] * freqs[None, :]
    ang_c = col[:, None] * freqs[None, :]
    pe = np.concatenate([np.sin(ang_r), np.cos(ang_r), np.sin(ang_c), np.cos(ang_c)], axis=-1)
    return jnp.asarray(pe, F32)


def _selection_constants():
    ch = np.arange(D_RWKV)
    ones512 = (ch[:, None] // HEAD == ch[None, :] // HEAD).astype(np.float32)
    t = np.arange(TM)
    same_chunk = t[:, None] // CH == t[None, :] // CH
    cum_f = same_chunk & (t[None, :] <= t[:, None])
    cum_b = same_chunk & (t[None, :] >= t[:, None])
    sel8 = np.arange(NCH)[:, None] == t[None, :] // CH
    col = np.arange(HC)
    hm = col[:, None] // CH == ch[None, :] // HEAD
    bm = col[:, None] // CH == col[None, :] // CH
    tt, jj = np.arange(CH)[:, None], col[None, :] % CH
    masks = np.stack([jj < tt, jj <= tt, jj > tt, jj >= tt])
    eye = jj == tt
    fold = ch[:, None] % HEAD == np.arange(HEAD)[None, :]
    bf = lambda x: jnp.asarray(x, BF16)
    f32 = lambda x: jnp.asarray(x, F32)
    prep_consts = (bf(ones512), bf(np.stack([cum_f, cum_b])), bf(same_chunk), bf(sel8), bf(hm), bf(bm),
                   f32(masks.reshape(2, 2, CH, HC)), f32(eye))
    scan_consts = (bf(hm), f32(ones512), bf(fold))
    return prep_consts, scan_consts


def kernel(x_prompt, x_sample, state_rwkv, c, c_ctx, ada_w, ada_b, norm1_g, w_in, tshift_mu, decay_w0, decay_lora_b, iclr_a0, iclr_lora_b, key_k, key_a, bonus_r_k, gate_lora_b, gn_g, gn_b, conv_dw_w, conv_dw_b, conv_ln_g, conv_ln_b, w_out, norm2_g, router_group_w, router_group_b, router_expert_w, router_expert_b, expert_w_gate, expert_w_up, expert_w_down, final_norm_g):
    assert x_prompt.shape == (N_CTX_SEQ, T_CTX, D) and x_sample.shape == (N_LAT_SEQ, T_LAT, D)
    assert ada_w.shape[0] == 1, "one trunk layer"
    prep_consts, scan_consts = _selection_constants()
    ones512 = prep_consts[0]
    xp = x_prompt.reshape(N_CTX_SEQ * T_CTX, D)
    xs = x_sample.reshape(N_LAT_SEQ * T_LAT, D)
    pe = _pos_embed(T_LAT // GRID_W)

    cond8 = jnp.concatenate([c_ctx[None, :], c, jnp.zeros((8 - 1 - N_LAT_SEQ, D), F32)], axis=0)
    mod3 = _adaln(cond8, ada_w[0], ada_b[0][None, :]).reshape(8, 1, N_MOD * D)

    zs, zg, zc = _inproj(xp, xs, pe, mod3, norm1_g, w_in[0].astype(BF16))

    zero = jnp.zeros((2, LORA, D_RWKV), F32)
    lora2 = jnp.concatenate([jnp.concatenate([decay_lora_b[0], zero], axis=2),
                             jnp.concatenate([zero, iclr_lora_b[0]], axis=2)], axis=1)
    vec = lambda p: p.reshape(2, 1, -1)
    prep_out = _prep(zs, vec(tshift_mu[0]), vec(decay_w0[0]), vec(iclr_a0[0]), lora2,
                     vec(key_k[0]), vec(key_a[0]), vec(bonus_r_k[0]), *prep_consts)
    bonus = prep_out[10]

    s0t = state_rwkv[:, 0].transpose(1, 0, 2, 4, 3)
    eye_h = jnp.eye(N_HEADS, dtype=F32)
    s0_wide = (s0t[:, :, :, :, None, :] * eye_h[None, None, :, None, :, None]).reshape(
        2, N_LAT_SEQ, D_RWKV, D_RWKV)
    yf, yb, s_fin = _scan(prep_out, s0_wide, scan_consts)

    yconv = _conv(zc, jnp.concatenate([conv_dw_w[0], jnp.zeros((1, D_CONV), F32)], axis=0),
                  conv_dw_b, conv_ln_g, conv_ln_b)

    router_w = jnp.concatenate([router_group_w[0], router_expert_w[0],
                                jnp.zeros((D, ROUTE_LANES - N_GROUPS - N_EXPERTS), F32)], axis=1)
    router_b = jnp.concatenate([router_group_b[0], router_expert_b[0],
                                jnp.zeros((ROUTE_LANES - N_GROUPS - N_EXPERTS,), F32)])[None, :]
    x1, h2, logits = _outproj(xp, xs, pe, mod3, yf, yb, bonus, zg, yconv,
                              gn_g, gn_b, gate_lora_b[0].astype(BF16), w_out[0].astype(BF16), norm2_g,
                              router_w, router_b, ones512)

    tok = np.arange(TM_MOE)
    triu = jnp.asarray(tok[:, None] < tok[None, :], BF16)
    comb, drow, dcol, seg_start, seg_ntile = _plan(logits, triu)
    seg_start = seg_start[:, :N_GROUPS, 0].reshape(-1)
    seg_ntile = seg_ntile[:, :N_GROUPS, 0].reshape(-1)
    moe = _moe(h2, comb, drow, dcol, seg_start, seg_ntile,
               expert_w_gate[0].astype(BF16), expert_w_up[0].astype(BF16), expert_w_down[0].astype(BF16))
    fg = final_norm_g[None, :]
    y_prompt = _final(x1, moe, mod3, fg, 0, N_CTX_TILES, lambda i: i, "final_norm_ctx")
    y_sample = _final(x1, moe, mod3, fg, N_CTX_TILES, N_LAT_TILES, _xs_block, "final_norm_lat")
    y_prompt = y_prompt.reshape(N_CTX_SEQ, T_CTX, D)
    y_sample = y_sample.reshape(N_LAT_SEQ, T_LAT, D)
    s_ctx = s_fin[:N_CTX_SEQ].reshape(N_CTX_SEQ, 2, N_HEADS, HEAD, HEAD).transpose(0, 1, 2, 4, 3)
    new_state = s_ctx[:, None].astype(state_rwkv.dtype)
    return (y_prompt, y_sample, new_state)
```

```python
import numpy as np
import jax
import jax.numpy as jnp
from jax import lax
from jax.experimental import pallas as pl
from jax.experimental.pallas import tpu as pltpu

F32 = jnp.float32
BF16 = jnp.bfloat16

D = 1024
N_CTX_SEQ = 16
T_CTX = 256
N_LAT_SEQ = 2
T_LAT = 1024
TM = 256
N_CTX_TILES = N_CTX_SEQ * T_CTX // TM
LAT_CHUNKS = T_LAT // TM
N_LAT_TILES = N_LAT_SEQ * LAT_CHUNKS
N_TILES = N_CTX_TILES + N_LAT_TILES
NT = N_TILES * TM
GRID_W = 64
D_RWKV = 512
D_CONV = 512
HEAD = 64
N_HEADS = 8
CONV_W = 31
CONV_PAD = CONV_W // 2
LORA = 64
LORA_G = 128
SHIFT_COLS = 3 * D_RWKV + 2 * LORA
N_GROUPS = 4
N_EXP_PER_GROUP = 8
N_EXPERTS = 32
D_EXPERT = 256
N_MOD = 6
RMS_EPS = 1e-6
LN_EPS = 1e-5
GN_EPS = 64e-5
LANES = 128
SUB = 8
CH = 32
NCH = TM // CH
HC = N_HEADS * CH
DOUBLINGS = 4
ROUTE_LANES = 128
E_LANE0 = N_GROUPS
GROUP_LANE = 36
TM_MOE = 1024
SEG = 64
EXP_PER_STEP = 4
TILES_PER_PASS = 2
MOE_WINDOW = 5
MOE_ROWS = TM_MOE + N_GROUPS * SEG + (MOE_WINDOW - 1) * SEG
N_MOE_TILES = NT // TM_MOE
VMEM_LIMIT = 56 * 1024 * 1024


def _cp(sem, flags=None):
    return pltpu.CompilerParams(dimension_semantics=sem, vmem_limit_bytes=VMEM_LIMIT, flags=flags)


def _split2(a):
    hi = a.astype(BF16)
    lo = (a - hi.astype(F32)).astype(BF16)
    return hi, lo


def _dot(a, b):
    return jnp.dot(a, b, preferred_element_type=F32)


def _dot_hp(a, b):
    ah, al = _split2(a)
    bh, bl = _split2(b)
    return _dot(ah, bh) + _dot(ah, bl) + _dot(al, bh)


def _dot_sel(a, sel):
    h, l = _split2(a)
    return _dot(h, sel) + _dot(l, sel)


def _sel_dot(sel, a):
    h, l = _split2(a)
    return _dot(sel, h) + _dot(sel, l)


def _sigmoid(x):
    return 1.0 / (1.0 + jnp.exp(-x))


def _silu(x):
    return x * _sigmoid(x)


def _lat_js(i):
    il = jnp.maximum(i - N_CTX_TILES, 0)
    return il // N_LAT_SEQ, il % N_LAT_SEQ


def _xp_block(i):
    return jnp.minimum(i, N_CTX_TILES - 1)


def _xs_block(i):
    j, s = _lat_js(i)
    return s * LAT_CHUNKS + j


def _pe_block(i):
    j, _ = _lat_js(i)
    return j


def _mod_row(i):
    _, s = _lat_js(i)
    return jnp.where(i < N_CTX_TILES, 0, 1 + s)


def _first_last(i):
    j, _ = _lat_js(i)
    is_ctx = i < N_CTX_TILES
    first = jnp.logical_or(is_ctx, j == 0)
    last = jnp.logical_or(is_ctx, j == LAT_CHUNKS - 1)
    return first, last


def _prev_tile(i):
    first, _ = _first_last(i)
    return jnp.where(first, i, i - N_LAT_SEQ)


def _next_tile(i):
    _, last = _first_last(i)
    return jnp.where(last, i, i + N_LAT_SEQ)


def _adaln_kernel(c_ref, w_ref, b_ref, o_ref):
    c = c_ref[...]
    o_ref[...] = _dot_hp(_silu(c), w_ref[...]) + b_ref[...]


def _adaln(cond8, ada_w, ada_b):
    tn = 1536
    n = ada_w.shape[1]
    return pl.pallas_call(
        _adaln_kernel,
        grid=(n // tn,),
        in_specs=[pl.BlockSpec((8, D), lambda j: (0, 0)),
                  pl.BlockSpec((D, tn), lambda j: (0, j)),
                  pl.BlockSpec((1, tn), lambda j: (0, j))],
        out_specs=pl.BlockSpec((8, tn), lambda j: (0, j)),
        out_shape=jax.ShapeDtypeStruct((8, n), F32),
        compiler_params=_cp(("parallel",)),
        name="adaln",
    )(cond8, ada_w, ada_b)


def _load_x(i, xp_ref, xs_ref, pe_ref):
    f = (i >= N_CTX_TILES).astype(F32)
    return xp_ref[...] * (1.0 - f) + (xs_ref[...] + pe_ref[...]) * f


def _x_specs():
    return [pl.BlockSpec((TM, D), lambda i: (_xp_block(i), 0)),
            pl.BlockSpec((TM, D), lambda i: (_xs_block(i), 0)),
            pl.BlockSpec((TM, D), lambda i: (_pe_block(i), 0))]


def _rms(x, g):
    return x * lax.rsqrt(jnp.mean(x * x, axis=-1, keepdims=True) + RMS_EPS) * g


def _inproj_kernel(xp_ref, xs_ref, pe_ref, mod_ref, g_ref, w_ref, zs_ref, zg_ref, zc_ref):
    i = pl.program_id(0)
    x = _load_x(i, xp_ref, xs_ref, pe_ref)
    sh1 = mod_ref[:, 0:D]
    sc1 = mod_ref[:, D:2 * D]
    h = (_rms(x, g_ref[...]) * (1.0 + sc1) + sh1).astype(BF16)
    zs_ref[...] = _dot(h, w_ref[:, 0:SHIFT_COLS])
    zg_ref[...] = _dot(h, w_ref[:, SHIFT_COLS:SHIFT_COLS + LORA_G])
    zc_ref[...] = _dot(h, w_ref[:, SHIFT_COLS + LORA_G:])


def _inproj(xp, xs, pe, mod3, norm1_g, w_in_bf):
    in_cols = w_in_bf.shape[1]
    return pl.pallas_call(
        _inproj_kernel,
        grid=(N_TILES,),
        in_specs=_x_specs() + [
            pl.BlockSpec((None, 1, N_MOD * D), lambda i: (_mod_row(i), 0, 0)),
            pl.BlockSpec((1, D), lambda i: (0, 0)),
            pl.BlockSpec((D, in_cols), lambda i: (0, 0))],
        out_specs=[pl.BlockSpec((TM, SHIFT_COLS), lambda i: (i, 0)),
                   pl.BlockSpec((TM, LORA_G), lambda i: (i, 0)),
                   pl.BlockSpec((TM, 2 * D_CONV), lambda i: (i, 0))],
        out_shape=[jax.ShapeDtypeStruct((NT, SHIFT_COLS), F32),
                   jax.ShapeDtypeStruct((NT, LORA_G), F32),
                   jax.ShapeDtypeStruct((NT, 2 * D_CONV), F32)],
        compiler_params=_cp(("parallel",)),
        name="inproj",
    )(xp, xs, pe, mod3, norm1_g, w_in_bf)


NT_DIMS = (((1,), (1,)), ((), ()))


def _tile_rows(x):
    return jnp.concatenate([x] * N_HEADS, axis=0)


def _prep_kernel(zs_ref, prev_ref, next_ref, mu_ref, w0_ref, a0_ref, lora_ref, kk_ref_, ka_ref,
                 rk_ref, ones_ref, cum_ref, tot_ref, sel_ref, hm_ref, bm_ref, msk_ref, eye_ref,
                 at_out, rt_out, bh_out, kh_out, v_out, t_out, aak_out, arb_out, ark_out, gc_out, bon_out):
    i = pl.program_id(0)
    first, last = _first_last(i)
    cur = zs_ref[...]
    prow = prev_ref[7:8, :] * (1.0 - first.astype(F32))
    nrow = next_ref[0:1, :] * (1.0 - last.astype(F32))
    rows = lax.broadcasted_iota(jnp.int32, (TM, 1), 0)
    shifted = (jnp.where(rows == 0, prow, pltpu.roll(cur, 1, axis=0)),
               jnp.where(rows == TM - 1, nrow, pltpu.roll(cur, TM - 1, axis=0)))
    ones = ones_ref[...]
    lane = lax.broadcasted_iota(jnp.int32, (TM, 2 * LORA), 1)
    scaled = []
    for d in range(2):
        xs = cur + (shifted[d] - cur) * mu_ref[d]
        r = xs[:, 0:D_RWKV]
        k = xs[:, D_RWKV:2 * D_RWKV]
        v = xs[:, 2 * D_RWKV:3 * D_RWKV]
        z2 = xs[:, 3 * D_RWKV:SHIFT_COLS]
        lin = jnp.where(lane < LORA, jnp.tanh(z2), z2)
        lo = _dot_hp(lin, lora_ref[d])
        u = -(w0_ref[d] + lo[:, 0:D_RWKV])
        softplus = jnp.maximum(u, 0.0) + jnp.log1p(jnp.exp(-jnp.abs(u)))
        w_log = -softplus - 0.5
        lw = -jnp.exp(w_log)
        a = _sigmoid(a0_ref[d] + lo[:, D_RWKV:])
        kx = k * kk_ref_[d]
        nrm = jnp.sqrt(_dot_sel(kx * kx, ones))
        kk = kx / jnp.maximum(nrm, 1e-12)
        k2 = k * (1.0 + (a - 1.0) * ka_ref[d])
        bvec = kk * a
        bon_out[d] = _dot_sel(r * k2 * rk_ref[d], ones) * v

        lg = _sel_dot(cum_ref[d], lw)
        tot = _sel_dot(tot_ref[...], lw)
        e_tail = jnp.exp(tot - lg)
        e_inv = jnp.exp(-lg)
        at = (-(kk * jnp.exp(lg - lw))).astype(BF16)
        bt = (bvec * e_inv).astype(BF16)
        kt = (k2 * e_inv).astype(BF16)
        rt = (r * jnp.exp(lg)).astype(BF16)
        at_out[d] = at
        rt_out[d] = rt
        bh_out[d] = (bvec * e_tail).astype(BF16)
        kh_out[d] = (k2 * e_tail).astype(BF16)
        v_out[d] = v.astype(BF16)
        gc_out[d] = jnp.exp(_sel_dot(sel_ref[...], lw))
        scaled.append((at, bt, kt, rt))

    hm = hm_ref[...]
    bm = bm_ref[...]

    def expand(z):
        return _tile_rows(z.astype(BF16)) * bm

    chunks = [(d, slice(c * CH, (c + 1) * CH)) for d in range(2) for c in range(NCH)]
    pws = []
    for d, rows in chunks:
        at, bt, kt, rt = scaled[d]
        ar = jnp.concatenate([at[rows], rt[rows]], axis=0)
        bk = jnp.concatenate([_tile_rows(bt[rows]) * hm, _tile_rows(kt[rows]) * hm], axis=0)
        p1 = lax.dot_general(ar, bk, NT_DIMS, preferred_element_type=F32)
        m_strict, m_incl = msk_ref[d, 0], msk_ref[d, 1]
        pws.append(p1[0:CH, 0:HC] * m_strict)
        aak_out[d, rows, :] = (p1[0:CH, HC:] * m_strict).astype(BF16)
        arb_out[d, rows, :] = (p1[CH:, 0:HC] * m_incl).astype(BF16)
        ark_out[d, rows, :] = (p1[CH:, HC:] * m_incl).astype(BF16)
    tms = [eye_ref[...] + pw for pw in pws]
    pws = [_dot(pw.astype(BF16), expand(pw)) for pw in pws]
    for k in range(1, DOUBLINGS + 1):
        if k < DOUBLINGS:
            prods = [_dot(jnp.concatenate([pw, tm], axis=0).astype(BF16), expand(pw))
                     for pw, tm in zip(pws, tms)]
            pws = [p[0:CH] for p in prods]
            tms = [tm + p[CH:] for tm, p in zip(tms, prods)]
        else:
            tms = [tm + _dot(tm.astype(BF16), expand(pw)) for tm, pw in zip(tms, pws)]
    for (d, rows), tm in zip(chunks, tms):
        t_out[d, rows, :] = tm.astype(BF16)


def _prep(zs, mu, w0, a0, lora2, key_k, key_a, r_k, ones512, cum2, tot, sel8, hm, bm, masks, eye):
    full = lambda a: pl.BlockSpec(a.shape, lambda i: (0,) * a.ndim)
    out_spec = pl.BlockSpec((2, TM, D_RWKV), lambda i: (0, i, 0))
    bf_shape = jax.ShapeDtypeStruct((2, NT, D_RWKV), BF16)
    tall_spec = pl.BlockSpec((2, TM, HC), lambda i: (0, i, 0))
    rows8 = TM // 8
    params = (mu, w0, a0, lora2, key_k, key_a, r_k, ones512, cum2, tot, sel8, hm, bm, masks, eye)
    return pl.pallas_call(
        _prep_kernel,
        grid=(N_TILES,),
        in_specs=[pl.BlockSpec((TM, SHIFT_COLS), lambda i: (i, 0)),
                  pl.BlockSpec((8, SHIFT_COLS), lambda i: (_prev_tile(i) * rows8 + rows8 - 1, 0)),
                  pl.BlockSpec((8, SHIFT_COLS), lambda i: (_next_tile(i) * rows8, 0))]
                 + [full(p) for p in params],
        out_specs=[out_spec] * 5 + [tall_spec] * 4 + [
            pl.BlockSpec((2, NCH, D_RWKV), lambda i: (0, i, 0)), out_spec],
        out_shape=[bf_shape] * 5 + [jax.ShapeDtypeStruct((2, NT, HC), BF16)] * 4 + [
            jax.ShapeDtypeStruct((2, N_TILES * NCH, D_RWKV), F32),
            jax.ShapeDtypeStruct((2, NT, D_RWKV), F32)],
        compiler_params=_cp(("parallel",)),
        name="rwkv_prep",
    )(zs, zs, zs, *params)


N_SCAN_STEPS = N_CTX_TILES // 2 + LAT_CHUNKS
LAT_STEP0 = N_CTX_TILES // 2


def _scan_block(step, d):
    lat = step >= LAT_STEP0
    jl = step - LAT_STEP0
    j = jnp.where(d == 0, jl, LAT_CHUNKS - 1 - jl)
    return jnp.where(lat, LAT_STEP0 + j, step)


def _scan_state_block(step):
    return jnp.minimum(step, LAT_STEP0)


def _chunk_kernel(*refs):
    (atf, rtf, bhf, khf, vf, tf, aakf, arbf, arkf, gcf, atb, rtb, bhb, khb, vb, tb, aakb, arbb, arkb, gcb,
     s0_ref, hm_ref, bd_ref, fold_ref, yf_ref, yb_ref, sout_ref, mw) = refs
    step = pl.program_id(0)
    in_refs = ((atf, rtf, bhf, khf, vf, tf, aakf, arbf, arkf, gcf),
               (atb, rtb, bhb, khb, vb, tb, aakb, arbb, arkb, gcb))
    y_refs = (yf_ref, yb_ref)

    @pl.when(step < LAT_STEP0)
    def _zero_state():
        mw[...] = jnp.zeros(mw.shape, F32)

    @pl.when(step == LAT_STEP0)
    def _load_state():
        mw[...] = s0_ref[...]

    hm = hm_ref[...]
    bd = bd_ref[...]
    crow = lax.broadcasted_iota(jnp.int32, (NCH, 1), 0)

    def chunk_body(q, carry):
        chains = []
        for dl in range(2):
            cc = q if dl == 0 else NCH - 1 - q
            for sl in range(2):
                chains.append((dl, sl, cc, pl.ds(pl.multiple_of(sl * TM + cc * CH, CH), CH)))
        ld = lambda idx, dl, rows: in_refs[dl][idx][rows, :]
        m0s = [mw[dl, sl] for dl, sl, _, _ in chains]
        xy0s = [_dot(jnp.concatenate([ld(0, dl, rows), ld(1, dl, rows)], axis=0), m0.astype(BF16))
                for (dl, _, _, rows), m0 in zip(chains, m0s)]
        vvs = [ld(4, dl, rows) for dl, _, _, rows in chains]
        avs = [_dot(jnp.concatenate([ld(6, dl, rows), ld(8, dl, rows)], axis=0), _tile_rows(vv) * hm)
               for (dl, _, _, rows), vv in zip(chains, vvs)]
        ubs = [_dot(ld(5, dl, rows), _tile_rows((xy0[0:CH] + av[0:CH]).astype(BF16)) * hm).astype(BF16)
               for (dl, _, _, rows), xy0, av in zip(chains, xy0s, avs)]
        for (dl, _, _, rows), xy0, av, ub in zip(chains, xy0s, avs, ubs):
            y_refs[dl][rows, :] = xy0[CH:] + av[CH:] + _dot(ld(7, dl, rows), _tile_rows(ub) * hm)
        for (dl, sl, cc, rows), m0, ub, vv in zip(chains, m0s, ubs, vvs):
            gcrow = jnp.sum(jnp.where(crow == cc, in_refs[dl][9][sl * NCH:(sl + 1) * NCH, :], 0.0),
                            axis=0, keepdims=True)
            pad = jnp.zeros((LANES - 2 * CH - SUB, D_RWKV), F32)
            stack = jnp.concatenate([ld(2, dl, rows).astype(F32), ld(3, dl, rows).astype(F32),
                                     jnp.broadcast_to(gcrow, (SUB, D_RWKV)), pad], axis=0)
            stack_t = stack.T
            uv = jnp.concatenate([ub, vv, jnp.zeros((LANES - 2 * CH, D_RWKV), BF16)], axis=0)
            upd = _dot(stack_t.astype(BF16), uv)
            mw[dl, sl] = (m0 * stack_t[:, 2 * CH:2 * CH + 1] + upd) * bd
        return carry

    lax.fori_loop(0, NCH, chunk_body, 0)

    @pl.when(jnp.logical_or(step < LAT_STEP0, step == N_SCAN_STEPS - 1))
    def _final():
        for dl in range(2):
            for sl in range(2):
                sout_ref[sl, dl] = _dot_sel(mw[dl, sl], fold_ref[...])


def _scan(prep_out, s0_wide, consts):
    def row_spec(d, width):
        return pl.BlockSpec((None, 2 * TM, width), lambda s: (d, _scan_block(s, d), 0))

    def gc_spec(d):
        return pl.BlockSpec((None, 2 * NCH, D_RWKV), lambda s: (d, _scan_block(s, d), 0))

    def y_spec(d):
        return pl.BlockSpec((2 * TM, D_RWKV), lambda s: (_scan_block(s, d), 0))

    rows, gc = list(prep_out[:9]), prep_out[9]
    row_specs = lambda d: [row_spec(d, D_RWKV)] * 5 + [row_spec(d, HC)] * 4 + [gc_spec(d)]
    const = lambda a: pl.BlockSpec(a.shape, lambda s: (0,) * a.ndim)
    y_shape = jax.ShapeDtypeStruct((NT, D_RWKV), F32)
    return pl.pallas_call(
        _chunk_kernel,
        grid=(N_SCAN_STEPS,),
        in_specs=row_specs(0) + row_specs(1) + [const(s0_wide)] + [const(a) for a in consts],
        out_specs=[y_spec(0), y_spec(1),
                   pl.BlockSpec((2, 2, D_RWKV, HEAD), lambda s: (_scan_state_block(s), 0, 0, 0))],
        out_shape=[y_shape, y_shape,
                   jax.ShapeDtypeStruct(((LAT_STEP0 + 1) * 2, 2, D_RWKV, HEAD), F32)],
        scratch_shapes=[pltpu.VMEM((2, 2, D_RWKV, D_RWKV), F32)],
        compiler_params=_cp(("arbitrary",)),
        name="rwkv_scan",
    )(*rows, gc, *rows, gc, s0_wide, *consts)


HALO = 16


def _glu(z):
    return z[:, 0:D_CONV] * _sigmoid(z[:, D_CONV:])


def _conv_kernel(cur_ref, prev_ref, next_ref, w_ref, b_ref, g_ref, beta_ref, o_ref, ext):
    i = pl.program_id(0)
    first, last = _first_last(i)
    n_ext = TM + 2 * HALO
    u = jnp.concatenate([_glu(prev_ref[...]) * (1.0 - first.astype(F32)), _glu(cur_ref[...]),
                         _glu(next_ref[...]) * (1.0 - last.astype(F32))], axis=0)
    ext[0] = u
    for b in range(1, SUB):
        ext[b] = pltpu.roll(u, n_ext - b, axis=0)
    acc = jnp.zeros((TM, D_CONV), F32)
    for j in range(CONV_W):
        off = HALO - CONV_PAD + j
        acc = acc + ext[off % SUB, off - off % SUB:off - off % SUB + TM, :] * w_ref[j:j + 1, :]
    h = acc + b_ref[...]
    mu = jnp.mean(h, axis=-1, keepdims=True)
    hc = h - mu
    var = jnp.mean(hc * hc, axis=-1, keepdims=True)
    y = hc * lax.rsqrt(var + LN_EPS) * g_ref[...] + beta_ref[...]
    o_ref[...] = _silu(y)


def _conv(zc, conv_w, conv_b, ln_g, ln_b):
    nh = TM // HALO
    vec = pl.BlockSpec((1, D_CONV), lambda i: (0, 0))
    return pl.pallas_call(
        _conv_kernel,
        grid=(N_TILES,),
        in_specs=[pl.BlockSpec((TM, 2 * D_CONV), lambda i: (i, 0)),
                  pl.BlockSpec((HALO, 2 * D_CONV), lambda i: (_prev_tile(i) * nh + nh - 1, 0)),
                  pl.BlockSpec((HALO, 2 * D_CONV), lambda i: (_next_tile(i) * nh, 0)),
                  pl.BlockSpec((CONV_W + 1, D_CONV), lambda i: (0, 0)), vec, vec, vec],
        out_specs=pl.BlockSpec((TM, D_CONV), lambda i: (i, 0)),
        out_shape=jax.ShapeDtypeStruct((NT, D_CONV), F32),
        scratch_shapes=[pltpu.VMEM((SUB, TM + 2 * HALO, D_CONV), F32)],
        compiler_params=_cp(("parallel",)),
        name="conv_module",
    )(zc, zc, zc, conv_w, conv_b, ln_g, ln_b)


def _outproj_kernel(xp_ref, xs_ref, pe_ref, mod_ref, yf_ref, yb_ref, bf_ref, bb_ref, zg_ref, yc_ref,
                    gng_ref, gnb_ref, gl_ref, wo_ref, n2_ref, rw_ref, rb_ref, ones_ref,
                    x1_ref, h2_ref, logit_ref):
    i = pl.program_id(0)
    x = _load_x(i, xp_ref, xs_ref, pe_ref)
    g1 = mod_ref[:, 2 * D:3 * D]
    sh2 = mod_ref[:, 3 * D:4 * D]
    sc2 = mod_ref[:, 4 * D:5 * D]
    ones = ones_ref[...]
    y = (yf_ref[...] + bf_ref[...]) + (yb_ref[...] + bb_ref[...])
    mu = _dot_sel(y, ones) * (1.0 / HEAD)
    yc = y - mu
    var = _dot_sel(yc * yc, ones) * (1.0 / HEAD)
    yn = yc * lax.rsqrt(var + GN_EPS) * gng_ref[...] + gnb_ref[...]
    gate = _dot(_sigmoid(zg_ref[...]).astype(BF16), gl_ref[...])
    y_rwkv = (yn * gate).astype(BF16)
    mix = _dot(y_rwkv, wo_ref[0:D_RWKV, :]) + _dot(yc_ref[...].astype(BF16), wo_ref[D_RWKV:, :])
    x1 = x + g1 * mix
    x1_ref[...] = x1
    h2 = _rms(x1, n2_ref[...]) * (1.0 + sc2) + sh2
    h2_ref[...] = h2.astype(BF16)

    logit_ref[...] = _dot_hp(h2, rw_ref[...]) + rb_ref[...]


def _route(logits):
    lane = lax.broadcasted_iota(jnp.int32, logits.shape, 1)
    lanef = lane.astype(F32)
    neg = jnp.float32(-1e30)
    big = jnp.float32(1e9)
    gmask = lane < N_GROUPS
    gl = jnp.where(gmask, logits, neg)
    ge = jnp.where(gmask, jnp.exp(gl - jnp.max(gl, axis=-1, keepdims=True)), 0.0)
    gprob = ge / jnp.sum(ge, axis=-1, keepdims=True)
    gp = jnp.max(gprob, axis=-1, keepdims=True)
    gidx = jnp.min(jnp.where(jnp.logical_and(gmask, gprob == gp), lanef, big), axis=-1, keepdims=True)
    egrp = jnp.floor((lanef - float(E_LANE0)) * (1.0 / N_EXP_PER_GROUP))
    emask = jnp.logical_and(jnp.logical_and(lane >= E_LANE0, lane < E_LANE0 + N_EXPERTS), egrp == gidx)
    el = jnp.where(emask, logits, neg)
    ee = jnp.where(emask, jnp.exp(el - jnp.max(el, axis=-1, keepdims=True)), 0.0)
    ep = ee / jnp.sum(ee, axis=-1, keepdims=True)
    m1 = jnp.max(jnp.where(emask, ep, -1.0), axis=-1, keepdims=True)
    i1 = jnp.min(jnp.where(jnp.logical_and(emask, ep == m1), lanef, big), axis=-1, keepdims=True)
    mask2 = jnp.logical_and(emask, lanef != i1)
    m2 = jnp.max(jnp.where(mask2, ep, -1.0), axis=-1, keepdims=True)
    i2 = jnp.min(jnp.where(jnp.logical_and(mask2, ep == m2), lanef, big), axis=-1, keepdims=True)
    den = m1 + m2
    return (jnp.where(lanef == i1, gp * (m1 / den), 0.0)
            + jnp.where(lanef == i2, gp * (m2 / den), 0.0)
            + jnp.where(lane == GROUP_LANE, gidx, 0.0))


def _outproj(xp, xs, pe, mod3, yf, yb, bonus, zg, yconv, gn_g, gn_b, gate_bf, w_out_bf, norm2_g,
             router_w, router_b, ones512):
    tile = lambda n: pl.BlockSpec((TM, n), lambda i: (i, 0))
    const = lambda shape: pl.BlockSpec(shape, lambda i: (0,) * len(shape))
    return pl.pallas_call(
        _outproj_kernel,
        grid=(N_TILES,),
        in_specs=_x_specs() + [
            pl.BlockSpec((None, 1, N_MOD * D), lambda i: (_mod_row(i), 0, 0)),
            tile(D_RWKV), tile(D_RWKV),
            pl.BlockSpec((None, TM, D_RWKV), lambda i: (0, i, 0)),
            pl.BlockSpec((None, TM, D_RWKV), lambda i: (1, i, 0)),
            tile(LORA_G), tile(D_CONV),
            const((1, D_RWKV)), const((1, D_RWKV)), const((LORA_G, D_RWKV)), const((D, D)),
            const((1, D)), const((D, ROUTE_LANES)), const((1, ROUTE_LANES)),
            const((D_RWKV, D_RWKV))],
        out_specs=[tile(D), tile(D), tile(ROUTE_LANES)],
        out_shape=[jax.ShapeDtypeStruct((NT, D), F32), jax.ShapeDtypeStruct((NT, D), BF16),
                   jax.ShapeDtypeStruct((NT, ROUTE_LANES), F32)],
        compiler_params=_cp(("parallel",)),
        name="outproj_router",
    )(xp, xs, pe, mod3, yf, yb, bonus, bonus, zg, yconv, gn_g, gn_b, gate_bf, w_out_bf, norm2_g,
      router_w, router_b, ones512)


def _plan_kernel(logit_ref, triu_ref, comb_ref, drow_ref, dcol_ref, start_ref, ntile_ref):
    comb = _route(logit_ref[...])
    comb_ref[...] = comb
    gidx = comb.T[GROUP_LANE:GROUP_LANE + 1, :]
    grow = lax.broadcasted_iota(jnp.int32, (SUB, 1), 0)
    growf = grow.astype(F32)
    onehot = jnp.where(jnp.logical_and(gidx == growf, grow < N_GROUPS), 1.0, 0.0)
    before = _dot(onehot.astype(BF16), triu_ref[...])
    count = jnp.sum(onehot, axis=-1, keepdims=True)
    padded = jnp.floor((count + (SEG - 1.0)) * (1.0 / SEG)) * SEG
    start = jnp.zeros((SUB, 1), F32)
    for g in range(N_GROUPS - 1):
        start = start + jnp.where(grow > g, padded[g:g + 1, :], 0.0)
    dest = jnp.sum(onehot * (start + before), axis=0, keepdims=True)
    drow_ref[...] = jnp.broadcast_to(dest, (SUB, TM_MOE))
    dcol_ref[...] = jnp.broadcast_to(dest, (LANES, TM_MOE)).T
    start_ref[...] = jnp.broadcast_to(start * (1.0 / SEG), (SUB, LANES)).astype(jnp.int32)
    ntile_ref[...] = jnp.broadcast_to(padded * (1.0 / SEG), (SUB, LANES)).astype(jnp.int32)


def _plan(logits, triu):
    seg_shape = jax.ShapeDtypeStruct((N_MOE_TILES, SUB, LANES), jnp.int32)
    seg_spec = pl.BlockSpec((None, SUB, LANES), lambda t: (t, 0, 0))
    tok_spec = pl.BlockSpec((TM_MOE, LANES), lambda t: (t, 0))
    return pl.pallas_call(
        _plan_kernel,
        grid=(N_MOE_TILES,),
        in_specs=[tok_spec, pl.BlockSpec((TM_MOE, TM_MOE), lambda t: (0, 0))],
        out_specs=[tok_spec, pl.BlockSpec((None, SUB, TM_MOE), lambda t: (t, 0, 0)), tok_spec,
                   seg_spec, seg_spec],
        out_shape=[jax.ShapeDtypeStruct((NT, ROUTE_LANES), F32), jax.ShapeDtypeStruct((N_MOE_TILES, SUB, TM_MOE), F32),
                   jax.ShapeDtypeStruct((NT, LANES), F32), seg_shape, seg_shape],
        compiler_params=_cp(("parallel",)),
        name="moe_plan",
    )(logits, triu)


def _moe_kernel(start_ref, ntile_ref, h_ref, comb_ref, drow_ref, dcol_ref, wg_ref, wu_ref, wd_ref,
                o_ref, xs, cws, ys):
    q = pl.program_id(1)
    j = pl.program_id(2)
    t = pl.program_id(0) * TILES_PER_PASS + j
    g = q // (N_EXP_PER_GROUP // EXP_PER_STEP)
    toks = pl.ds(pl.multiple_of(j * TM_MOE, TM_MOE), TM_MOE)

    @pl.when(q == 0)
    def _sort_in():
        slot = lax.broadcasted_iota(jnp.int32, (MOE_ROWS, TM_MOE), 0).astype(F32)
        perm = jnp.where(slot == drow_ref[j, 0:1, :], 1.0, 0.0).astype(BF16)
        xs[j] = _dot(perm, h_ref[toks, :]).astype(BF16)
        ch, cl = _split2(comb_ref[toks, :])
        cws[j] = _dot(perm, ch) + _dot(perm, cl)
        ys[j] = jnp.zeros((MOE_ROWS, D), F32)

    seg = t * N_GROUPS + g
    first = start_ref[seg]

    def visit(row0, n_rows):
        rows = pl.ds(pl.multiple_of(row0 * SEG, SEG), n_rows)
        x = xs[j, rows, :]
        cw_all = cws[j, rows, :]
        lane = lax.broadcasted_iota(jnp.int32, (n_rows, ROUTE_LANES), 1)
        acc = ys[j, rows, :]
        for k in range(EXP_PER_STEP):
            e_lane = q * EXP_PER_STEP + k + E_LANE0
            cw = jnp.sum(jnp.where(lane == e_lane, cw_all, 0.0), axis=-1, keepdims=True)
            hid = (_silu(_dot(x, wg_ref[k])) * _dot(x, wu_ref[k]) * cw).astype(BF16)
            acc = acc + _dot(hid, wd_ref[k])
        ys[j, rows, :] = acc

    visit(jnp.minimum(first, MOE_ROWS // SEG - MOE_WINDOW), MOE_WINDOW * SEG)

    def sub_tile(i, carry):
        visit(first + i, SEG)
        return carry

    lax.fori_loop(MOE_WINDOW, ntile_ref[seg], sub_tile, 0)

    @pl.when(q == N_EXPERTS // EXP_PER_STEP - 1)
    def _sort_out():
        slot = lax.broadcasted_iota(jnp.int32, (TM_MOE, MOE_ROWS), 1).astype(F32)
        perm_t = jnp.where(slot == dcol_ref[toks, 0:1], 1.0, 0.0).astype(BF16)
        o_ref[toks, :] = _dot(perm_t, ys[j].astype(BF16)).astype(BF16)


def _moe(h2, comb, drow, dcol, seg_start, seg_ntile, wg, wu, wd):
    rows = TILES_PER_PASS * TM_MOE
    once = dict(pipeline_mode=pl.Buffered(1))
    weights = lambda shape: pl.BlockSpec(shape, lambda p, q, j, s, n: (q, 0, 0))
    grid_spec = pltpu.PrefetchScalarGridSpec(
        num_scalar_prefetch=2,
        grid=(N_MOE_TILES // TILES_PER_PASS, N_EXPERTS // EXP_PER_STEP, TILES_PER_PASS),
        in_specs=[pl.BlockSpec((rows, D), lambda p, q, j, s, n: (p, 0), **once),
                  pl.BlockSpec((rows, ROUTE_LANES), lambda p, q, j, s, n: (p, 0), **once),
                  pl.BlockSpec((TILES_PER_PASS, SUB, TM_MOE), lambda p, q, j, s, n: (p, 0, 0), **once),
                  pl.BlockSpec((rows, LANES), lambda p, q, j, s, n: (p, 0), **once),
                  weights((EXP_PER_STEP, D, D_EXPERT)), weights((EXP_PER_STEP, D, D_EXPERT)),
                  weights((EXP_PER_STEP, D_EXPERT, D))],
        out_specs=pl.BlockSpec((rows, D), lambda p, q, j, s, n: (p, 0)),
        scratch_shapes=[pltpu.VMEM((TILES_PER_PASS, MOE_ROWS, D), BF16),
                        pltpu.VMEM((TILES_PER_PASS, MOE_ROWS, ROUTE_LANES), F32),
                        pltpu.VMEM((TILES_PER_PASS, MOE_ROWS, D), F32)])
    return pl.pallas_call(
        _moe_kernel,
        grid_spec=grid_spec,
        out_shape=jax.ShapeDtypeStruct((NT, D), BF16),
        compiler_params=_cp(("parallel", "arbitrary", "arbitrary")),
        name="moe_experts",
    )(seg_start, seg_ntile, h2, comb, drow, dcol, wg, wu, wd)


def _final_kernel(x1_ref, moe_ref, mod_ref, g_ref, o_ref):
    g2 = mod_ref[:, 5 * D:6 * D]
    x2 = x1_ref[...] + g2 * moe_ref[...].astype(F32)
    o_ref[...] = _rms(x2, g_ref[...])


def _final(x1, moe, mod3, final_g, tile0, n_tiles, out_block, name):
    tile = pl.BlockSpec((TM, D), lambda i: (tile0 + i, 0))
    return pl.pallas_call(
        _final_kernel,
        grid=(n_tiles,),
        in_specs=[tile, tile, pl.BlockSpec((None, 1, N_MOD * D), lambda i: (_mod_row(tile0 + i), 0, 0)),
                  pl.BlockSpec((1, D), lambda i: (0, 0))],
        out_specs=pl.BlockSpec((TM, D), lambda i: (out_block(tile0 + i), 0)),
        out_shape=jax.ShapeDtypeStruct((n_tiles * TM, D), F32),
        compiler_params=_cp(("parallel",)),
        name=name,
    )(x1, moe, mod3, final_g)


def _pos_embed(rows):
    t = np.arange(rows * GRID_W)
    row = (t // GRID_W).astype(np.float32)
    col = (t % GRID_W).astype(np.float32)
    quarter = D // 4
    freqs = (1.0 / (10000.0 ** (np.arange(quarter, dtype=np.float32) / quarter))).astype(np.float32)
    ang_r = row[:, None] * freqs[None, :]
    ang_c = col[:, None] * freqs[None, :]
    pe = np.concatenate([np.sin(ang_r), np.cos(ang_r), np.sin(ang_c), np.cos(ang_c)], axis=-1)
    return jnp.asarray(pe, F32)


def _selection_constants():
    ch = np.arange(D_RWKV)
    ones512 = (ch[:, None] // HEAD == ch[None, :] // HEAD).astype(np.float32)
    t = np.arange(TM)
    same_chunk = t[:, None] // CH == t[None, :] // CH
    cum_f = same_chunk & (t[None, :] <= t[:, None])
    cum_b = same_chunk & (t[None, :] >= t[:, None])
    sel8 = np.arange(NCH)[:, None] == t[None, :] // CH
    col = np.arange(HC)
    hm = col[:, None] // CH == ch[None, :] // HEAD
    bm = col[:, None] // CH == col[None, :] // CH
    tt, jj = np.arange(CH)[:, None], col[None, :] % CH
    masks = np.stack([jj < tt, jj <= tt, jj > tt, jj >= tt])
    eye = jj == tt
    fold = ch[:, None] % HEAD == np.arange(HEAD)[None, :]
    bf = lambda x: jnp.asarray(x, BF16)
    f32 = lambda x: jnp.asarray(x, F32)
    prep_consts = (bf(ones512), bf(np.stack([cum_f, cum_b])), bf(same_chunk), bf(sel8), bf(hm), bf(bm),
                   f32(masks.reshape(2, 2, CH, HC)), f32(eye))
    scan_consts = (bf(hm), f32(ones512), bf(fold))
    return prep_consts, scan_consts


def kernel(x_prompt, x_sample, state_rwkv, c, c_ctx, ada_w, ada_b, norm1_g, w_in, tshift_mu, decay_w0, decay_lora_b, iclr_a0, iclr_lora_b, key_k, key_a, bonus_r_k, gate_lora_b, gn_g, gn_b, conv_dw_w, conv_dw_b, conv_ln_g, conv_ln_b, w_out, norm2_g, router_group_w, router_group_b, router_expert_w, router_expert_b, expert_w_gate, expert_w_up, expert_w_down, final_norm_g):
    assert x_prompt.shape == (N_CTX_SEQ, T_CTX, D) and x_sample.shape == (N_LAT_SEQ, T_LAT, D)
    assert ada_w.shape[0] == 1, "one trunk layer"
    prep_consts, scan_consts = _selection_constants()
    ones512 = prep_consts[0]
    xp = x_prompt.reshape(N_CTX_SEQ * T_CTX, D)
    xs = x_sample.reshape(N_LAT_SEQ * T_LAT, D)
    pe = _pos_embed(T_LAT // GRID_W)

    cond8 = jnp.concatenate([c_ctx[None, :], c, jnp.zeros((8 - 1 - N_LAT_SEQ, D), F32)], axis=0)
    mod3 = _adaln(cond8, ada_w[0], ada_b[0][None, :]).reshape(8, 1, N_MOD * D)

    zs, zg, zc = _inproj(xp, xs, pe, mod3, norm1_g, w_in[0].astype(BF16))

    zero = jnp.zeros((2, LORA, D_RWKV), F32)
    lora2 = jnp.concatenate([jnp.concatenate([decay_lora_b[0], zero], axis=2),
                             jnp.concatenate([zero, iclr_lora_b[0]], axis=2)], axis=1)
    vec = lambda p: p.reshape(2, 1, -1)
    prep_out = _prep(zs, vec(tshift_mu[0]), vec(decay_w0[0]), vec(iclr_a0[0]), lora2,
                     vec(key_k[0]), vec(key_a[0]), vec(bonus_r_k[0]), *prep_consts)
    bonus = prep_out[10]

    s0t = state_rwkv[:, 0].transpose(1, 0, 2, 4, 3)
    eye_h = jnp.eye(N_HEADS, dtype=F32)
    s0_wide = (s0t[:, :, :, :, None, :] * eye_h[None, None, :, None, :, None]).reshape(
        2, N_LAT_SEQ, D_RWKV, D_RWKV)
    yf, yb, s_fin = _scan(prep_out, s0_wide, scan_consts)

    yconv = _conv(zc, jnp.concatenate([conv_dw_w[0], jnp.zeros((1, D_CONV), F32)], axis=0),
                  conv_dw_b, conv_ln_g, conv_ln_b)

    router_w = jnp.concatenate([router_group_w[0], router_expert_w[0],
                                jnp.zeros((D, ROUTE_LANES - N_GROUPS - N_EXPERTS), F32)], axis=1)
    router_b = jnp.concatenate([router_group_b[0], router_expert_b[0],
                                jnp.zeros((ROUTE_LANES - N_GROUPS - N_EXPERTS,), F32)])[None, :]
    x1, h2, logits = _outproj(xp, xs, pe, mod3, yf, yb, bonus, zg, yconv,
                              gn_g, gn_b, gate_lora_b[0].astype(BF16), w_out[0].astype(BF16), norm2_g,
                              router_w, router_b, ones512)

    tok = np.arange(TM_MOE)
    triu = jnp.asarray(tok[:, None] < tok[None, :], BF16)
    comb, drow, dcol, seg_start, seg_ntile = _plan(logits, triu)
    seg_start = seg_start[:, :N_GROUPS, 0].reshape(-1)
    seg_ntile = seg_ntile[:, :N_GROUPS, 0].reshape(-1)
    moe = _moe(h2, comb, drow, dcol, seg_start, seg_ntile,
               expert_w_gate[0].astype(BF16), expert_w_up[0].astype(BF16), expert_w_down[0].astype(BF16))
    fg = final_norm_g[None, :]
    y_prompt = _final(x1, moe, mod3, fg, 0, N_CTX_TILES, lambda i: i, "final_norm_ctx")
    y_sample = _final(x1, moe, mod3, fg, N_CTX_TILES, N_LAT_TILES, _xs_block, "final_norm_lat")
    y_prompt = y_prompt.reshape(N_CTX_SEQ, T_CTX, D)
    y_sample = y_sample.reshape(N_LAT_SEQ, T_LAT, D)
    s_ctx = s_fin[:N_CTX_SEQ].reshape(N_CTX_SEQ, 2, N_HEADS, HEAD, HEAD).transpose(0, 1, 2, 4, 3)
    new_state = s_ctx[:, None].astype(state_rwkv.dtype)
    return (y_prompt, y_sample, new_state)
```

```python
import numpy as np
import jax
import jax.numpy as jnp
from jax import lax
from jax.experimental import pallas as pl
from jax.experimental.pallas import tpu as pltpu

F32 = jnp.float32
BF16 = jnp.bfloat16

D = 1024
N_CTX_SEQ = 16
T_CTX = 256
N_LAT_SEQ = 2
T_LAT = 1024
TM = 256
N_CTX_TILES = N_CTX_SEQ * T_CTX // TM
LAT_CHUNKS = T_LAT // TM
N_LAT_TILES = N_LAT_SEQ * LAT_CHUNKS
N_TILES = N_CTX_TILES + N_LAT_TILES
NT = N_TILES * TM
GRID_W = 64
D_RWKV = 512
D_CONV = 512
HEAD = 64
N_HEADS = 8
CONV_W = 31
CONV_PAD = CONV_W // 2
LORA = 64
LORA_G = 128
SHIFT_COLS = 3 * D_RWKV + 2 * LORA
N_GROUPS = 4
N_EXP_PER_GROUP = 8
N_EXPERTS = 32
D_EXPERT = 256
N_MOD = 6
RMS_EPS = 1e-6
LN_EPS = 1e-5
GN_EPS = 64e-5
LANES = 128
SUB = 8
CH = 32
NCH = TM // CH
HC = N_HEADS * CH
DOUBLINGS = 4
ROUTE_LANES = 128
E_LANE0 = N_GROUPS
GROUP_LANE = 36
TM_MOE = 1024
SEG = 64
EXP_PER_STEP = 4
TILES_PER_PASS = 2
MOE_WINDOW = 5
MOE_ROWS = TM_MOE + N_GROUPS * SEG + (MOE_WINDOW - 1) * SEG
N_MOE_TILES = NT // TM_MOE
VMEM_LIMIT = 56 * 1024 * 1024


def _cp(sem, flags=None):
    return pltpu.CompilerParams(dimension_semantics=sem, vmem_limit_bytes=VMEM_LIMIT, flags=flags)


def _split2(a):
    hi = a.astype(BF16)
    lo = (a - hi.astype(F32)).astype(BF16)
    return hi, lo


def _dot(a, b):
    return jnp.dot(a, b, preferred_element_type=F32)


def _dot_hp(a, b):
    ah, al = _split2(a)
    bh, bl = _split2(b)
    return _dot(ah, bh) + _dot(ah, bl) + _dot(al, bh)


def _dot_sel(a, sel):
    h, l = _split2(a)
    return _dot(h, sel) + _dot(l, sel)


def _sel_dot(sel, a):
    h, l = _split2(a)
    return _dot(sel, h) + _dot(sel, l)


def _sigmoid(x):
    return 1.0 / (1.0 + jnp.exp(-x))


def _silu(x):
    return x * _sigmoid(x)


def _lat_js(i):
    il = jnp.maximum(i - N_CTX_TILES, 0)
    return il // N_LAT_SEQ, il % N_LAT_SEQ


def _xp_block(i):
    return jnp.minimum(i, N_CTX_TILES - 1)


def _xs_block(i):
    j, s = _lat_js(i)
    return s * LAT_CHUNKS + j


def _pe_block(i):
    j, _ = _lat_js(i)
    return j


def _mod_row(i):
    _, s = _lat_js(i)
    return jnp.where(i < N_CTX_TILES, 0, 1 + s)


def _first_last(i):
    j, _ = _lat_js(i)
    is_ctx = i < N_CTX_TILES
    first = jnp.logical_or(is_ctx, j == 0)
    last = jnp.logical_or(is_ctx, j == LAT_CHUNKS - 1)
    return first, last


def _prev_tile(i):
    first, _ = _first_last(i)
    return jnp.where(first, i, i - N_LAT_SEQ)


def _next_tile(i):
    _, last = _first_last(i)
    return jnp.where(last, i, i + N_LAT_SEQ)


def _adaln_kernel(c_ref, w_ref, b_ref, o_ref):
    c = c_ref[...]
    o_ref[...] = _dot_hp(_silu(c), w_ref[...]) + b_ref[...]


def _adaln(cond8, ada_w, ada_b):
    tn = 1536
    n = ada_w.shape[1]
    return pl.pallas_call(
        _adaln_kernel,
        grid=(n // tn,),
        in_specs=[pl.BlockSpec((8, D), lambda j: (0, 0)),
                  pl.BlockSpec((D, tn), lambda j: (0, j)),
                  pl.BlockSpec((1, tn), lambda j: (0, j))],
        out_specs=pl.BlockSpec((8, tn), lambda j: (0, j)),
        out_shape=jax.ShapeDtypeStruct((8, n), F32),
        compiler_params=_cp(("parallel",)),
        name="adaln",
    )(cond8, ada_w, ada_b)


def _load_x(i, xp_ref, xs_ref, pe_ref):
    f = (i >= N_CTX_TILES).astype(F32)
    return xp_ref[...] * (1.0 - f) + (xs_ref[...] + pe_ref[...]) * f


def _x_specs():
    return [pl.BlockSpec((TM, D), lambda i: (_xp_block(i), 0)),
            pl.BlockSpec((TM, D), lambda i: (_xs_block(i), 0)),
            pl.BlockSpec((TM, D), lambda i: (_pe_block(i), 0))]


def _rms(x, g):
    return x * lax.rsqrt(jnp.mean(x * x, axis=-1, keepdims=True) + RMS_EPS) * g


def _inproj_kernel(xp_ref, xs_ref, pe_ref, mod_ref, g_ref, w_ref, zs_ref, zg_ref, zc_ref):
    i = pl.program_id(0)
    x = _load_x(i, xp_ref, xs_ref, pe_ref)
    sh1 = mod_ref[:, 0:D]
    sc1 = mod_ref[:, D:2 * D]
    h = (_rms(x, g_ref[...]) * (1.0 + sc1) + sh1).astype(BF16)
    zs_ref[...] = _dot(h, w_ref[:, 0:SHIFT_COLS])
    zg_ref[...] = _dot(h, w_ref[:, SHIFT_COLS:SHIFT_COLS + LORA_G])
    zc_ref[...] = _dot(h, w_ref[:, SHIFT_COLS + LORA_G:])


def _inproj(xp, xs, pe, mod3, norm1_g, w_in_bf):
    in_cols = w_in_bf.shape[1]
    return pl.pallas_call(
        _inproj_kernel,
        grid=(N_TILES,),
        in_specs=_x_specs() + [
            pl.BlockSpec((None, 1, N_MOD * D), lambda i: (_mod_row(i), 0, 0)),
            pl.BlockSpec((1, D), lambda i: (0, 0)),
            pl.BlockSpec((D, in_cols), lambda i: (0, 0))],
        out_specs=[pl.BlockSpec((TM, SHIFT_COLS), lambda i: (i, 0)),
                   pl.BlockSpec((TM, LORA_G), lambda i: (i, 0)),
                   pl.BlockSpec((TM, 2 * D_CONV), lambda i: (i, 0))],
        out_shape=[jax.ShapeDtypeStruct((NT, SHIFT_COLS), F32),
                   jax.ShapeDtypeStruct((NT, LORA_G), F32),
                   jax.ShapeDtypeStruct((NT, 2 * D_CONV), F32)],
        compiler_params=_cp(("parallel",)),
        name="inproj",
    )(xp, xs, pe, mod3, norm1_g, w_in_bf)


NT_DIMS = (((1,), (1,)), ((), ()))


def _tile_rows(x):
    return jnp.concatenate([x] * N_HEADS, axis=0)


def _prep_kernel(zs_ref, prev_ref, next_ref, mu_ref, w0_ref, a0_ref, lora_ref, kk_ref_, ka_ref,
                 rk_ref, ones_ref, cum_ref, tot_ref, sel_ref, hm_ref, bm_ref, msk_ref, eye_ref,
                 at_out, rt_out, bh_out, kh_out, v_out, t_out, aak_out, arb_out, ark_out, gc_out, bon_out):
    i = pl.program_id(0)
    first, last = _first_last(i)
    cur = zs_ref[...]
    prow = prev_ref[7:8, :] * (1.0 - first.astype(F32))
    nrow = next_ref[0:1, :] * (1.0 - last.astype(F32))
    rows = lax.broadcasted_iota(jnp.int32, (TM, 1), 0)
    shifted = (jnp.where(rows == 0, prow, pltpu.roll(cur, 1, axis=0)),
               jnp.where(rows == TM - 1, nrow, pltpu.roll(cur, TM - 1, axis=0)))
    ones = ones_ref[...]
    lane = lax.broadcasted_iota(jnp.int32, (TM, 2 * LORA), 1)
    scaled = []
    for d in range(2):
        xs = cur + (shifted[d] - cur) * mu_ref[d]
        r = xs[:, 0:D_RWKV]
        k = xs[:, D_RWKV:2 * D_RWKV]
        v = xs[:, 2 * D_RWKV:3 * D_RWKV]
        z2 = xs[:, 3 * D_RWKV:SHIFT_COLS]
        lin = jnp.where(lane < LORA, jnp.tanh(z2), z2)
        lo = _dot_hp(lin, lora_ref[d])
        u = -(w0_ref[d] + lo[:, 0:D_RWKV])
        softplus = jnp.maximum(u, 0.0) + jnp.log1p(jnp.exp(-jnp.abs(u)))
        w_log = -softplus - 0.5
        lw = -jnp.exp(w_log)
        a = _sigmoid(a0_ref[d] + lo[:, D_RWKV:])
        kx = k * kk_ref_[d]
        nrm = jnp.sqrt(_dot_sel(kx * kx, ones))
        kk = kx / jnp.maximum(nrm, 1e-12)
        k2 = k * (1.0 + (a - 1.0) * ka_ref[d])
        bvec = kk * a
        bon_out[d] = _dot_sel(r * k2 * rk_ref[d], ones) * v

        lg = _sel_dot(cum_ref[d], lw)
        tot = _sel_dot(tot_ref[...], lw)
        e_tail = jnp.exp(tot - lg)
        e_inv = jnp.exp(-lg)
        at = (-(kk * jnp.exp(lg - lw))).astype(BF16)
        bt = (bvec * e_inv).astype(BF16)
        kt = (k2 * e_inv).astype(BF16)
        rt = (r * jnp.exp(lg)).astype(BF16)
        at_out[d] = at
        rt_out[d] = rt
        bh_out[d] = (bvec * e_tail).astype(BF16)
        kh_out[d] = (k2 * e_tail).astype(BF16)
        v_out[d] = v.astype(BF16)
        gc_out[d] = jnp.exp(_sel_dot(sel_ref[...], lw))
        scaled.append((at, bt, kt, rt))

    hm = hm_ref[...]
    bm = bm_ref[...]

    def expand(z):
        return _tile_rows(z.astype(BF16)) * bm

    chunks = [(d, slice(c * CH, (c + 1) * CH)) for d in range(2) for c in range(NCH)]
    pws = []
    for d, rows in chunks:
        at, bt, kt, rt = scaled[d]
        ar = jnp.concatenate([at[rows], rt[rows]], axis=0)
        bk = jnp.concatenate([_tile_rows(bt[rows]) * hm, _tile_rows(kt[rows]) * hm], axis=0)
        p1 = lax.dot_general(ar, bk, NT_DIMS, preferred_element_type=F32)
        m_strict, m_incl = msk_ref[d, 0], msk_ref[d, 1]
        pws.append(p1[0:CH, 0:HC] * m_strict)
        aak_out[d, rows, :] = (p1[0:CH, HC:] * m_strict).astype(BF16)
        arb_out[d, rows, :] = (p1[CH:, 0:HC] * m_incl).astype(BF16)
        ark_out[d, rows, :] = (p1[CH:, HC:] * m_incl).astype(BF16)
    tms = [eye_ref[...] + pw for pw in pws]
    pws = [_dot(pw.astype(BF16), expand(pw)) for pw in pws]
    for k in range(1, DOUBLINGS + 1):
        if k < DOUBLINGS:
            prods = [_dot(jnp.concatenate([pw, tm], axis=0).astype(BF16), expand(pw))
                     for pw, tm in zip(pws, tms)]
            pws = [p[0:CH] for p in prods]
            tms = [tm + p[CH:] for tm, p in zip(tms, prods)]
        else:
            tms = [tm + _dot(tm.astype(BF16), expand(pw)) for tm, pw in zip(tms, pws)]
    for (d, rows), tm in zip(chunks, tms):
        t_out[d, rows, :] = tm.astype(BF16)


def _prep(zs, mu, w0, a0, lora2, key_k, key_a, r_k, ones512, cum2, tot, sel8, hm, bm, masks, eye):
    full = lambda a: pl.BlockSpec(a.shape, lambda i: (0,) * a.ndim)
    out_spec = pl.BlockSpec((2, TM, D_RWKV), lambda i: (0, i, 0))
    bf_shape = jax.ShapeDtypeStruct((2, NT, D_RWKV), BF16)
    tall_spec = pl.BlockSpec((2, TM, HC), lambda i: (0, i, 0))
    rows8 = TM // 8
    params = (mu, w0, a0, lora2, key_k, key_a, r_k, ones512, cum2, tot, sel8, hm, bm, masks, eye)
    return pl.pallas_call(
        _prep_kernel,
        grid=(N_TILES,),
        in_specs=[pl.BlockSpec((TM, SHIFT_COLS), lambda i: (i, 0)),
                  pl.BlockSpec((8, SHIFT_COLS), lambda i: (_prev_tile(i) * rows8 + rows8 - 1, 0)),
                  pl.BlockSpec((8, SHIFT_COLS), lambda i: (_next_tile(i) * rows8, 0))]
                 + [full(p) for p in params],
        out_specs=[out_spec] * 5 + [tall_spec] * 4 + [
            pl.BlockSpec((2, NCH, D_RWKV), lambda i: (0, i, 0)), out_spec],
        out_shape=[bf_shape] * 5 + [jax.ShapeDtypeStruct((2, NT, HC), BF16)] * 4 + [
            jax.ShapeDtypeStruct((2, N_TILES * NCH, D_RWKV), F32),
            jax.ShapeDtypeStruct((2, NT, D_RWKV), F32)],
        compiler_params=_cp(("parallel",)),
        name="rwkv_prep",
    )(zs, zs, zs, *params)


N_SCAN_STEPS = N_CTX_TILES // 2 + LAT_CHUNKS
LAT_STEP0 = N_CTX_TILES // 2


def _scan_block(step, d):
    lat = step >= LAT_STEP0
    jl = step - LAT_STEP0
    j = jnp.where(d == 0, jl, LAT_CHUNKS - 1 - jl)
    return jnp.where(lat, LAT_STEP0 + j, step)


def _scan_state_block(step):
    return jnp.minimum(step, LAT_STEP0 - 1)


def _chunk_kernel(*refs):
    (atf, rtf, bhf, khf, vf, tf, aakf, arbf, arkf, gcf, atb, rtb, bhb, khb, vb, tb, aakb, arbb, arkb, gcb,
     s0_ref, hm_ref, bd_ref, fold_ref, foldt_ref, yf_ref, yb_ref, sout_ref, mw) = refs
    step = pl.program_id(0)
    in_refs = ((atf, rtf, bhf, khf, vf, tf, aakf, arbf, arkf, gcf),
               (atb, rtb, bhb, khb, vb, tb, aakb, arbb, arkb, gcb))
    y_refs = (yf_ref, yb_ref)

    @pl.when(step < LAT_STEP0)
    def _zero_state():
        mw[...] = jnp.zeros(mw.shape, F32)

    bd = bd_ref[...]

    @pl.when(step == LAT_STEP0)
    def _load_state():
        for dl in range(2):
            for sl in range(2):
                mw[dl, sl] = _dot_sel(s0_ref[dl, sl], foldt_ref[...]) * bd

    hm = hm_ref[...]
    crow = lax.broadcasted_iota(jnp.int32, (NCH, 1), 0)

    def chunk_body(q, carry):
        chains = []
        for dl in range(2):
            cc = q if dl == 0 else NCH - 1 - q
            for sl in range(2):
                chains.append((dl, sl, cc, pl.ds(pl.multiple_of(sl * TM + cc * CH, CH), CH)))
        ld = lambda idx, dl, rows: in_refs[dl][idx][rows, :]
        m0s = [mw[dl, sl] for dl, sl, _, _ in chains]
        xy0s = [_dot(jnp.concatenate([ld(0, dl, rows), ld(1, dl, rows)], axis=0), m0.astype(BF16))
                for (dl, _, _, rows), m0 in zip(chains, m0s)]
        vvs = [ld(4, dl, rows) for dl, _, _, rows in chains]
        avs = [_dot(jnp.concatenate([ld(6, dl, rows), ld(8, dl, rows)], axis=0), _tile_rows(vv) * hm)
               for (dl, _, _, rows), vv in zip(chains, vvs)]
        ubs = [_dot(ld(5, dl, rows), _tile_rows((xy0[0:CH] + av[0:CH]).astype(BF16)) * hm).astype(BF16)
               for (dl, _, _, rows), xy0, av in zip(chains, xy0s, avs)]
        for (dl, _, _, rows), xy0, av, ub in zip(chains, xy0s, avs, ubs):
            y_refs[dl][rows, :] = xy0[CH:] + av[CH:] + _dot(ld(7, dl, rows), _tile_rows(ub) * hm)
        for (dl, sl, cc, rows), m0, ub, vv in zip(chains, m0s, ubs, vvs):
            gcrow = jnp.sum(jnp.where(crow == cc, in_refs[dl][9][sl * NCH:(sl + 1) * NCH, :], 0.0),
                            axis=0, keepdims=True)
            pad = jnp.zeros((LANES - 2 * CH - SUB, D_RWKV), F32)
            stack = jnp.concatenate([ld(2, dl, rows).astype(F32), ld(3, dl, rows).astype(F32),
                                     jnp.broadcast_to(gcrow, (SUB, D_RWKV)), pad], axis=0)
            stack_t = stack.T
            uv = jnp.concatenate([ub, vv, jnp.zeros((LANES - 2 * CH, D_RWKV), BF16)], axis=0)
            upd = _dot(stack_t.astype(BF16), uv)
            mw[dl, sl] = (m0 * stack_t[:, 2 * CH:2 * CH + 1] + upd) * bd
        return carry

    lax.fori_loop(0, NCH, chunk_body, 0)

    @pl.when(step < LAT_STEP0)
    def _final():
        for dl in range(2):
            for sl in range(2):
                compact = _dot_sel(mw[dl, sl], fold_ref[...])
                by_v = jnp.concatenate([compact, jnp.zeros_like(compact)], axis=1).T
                for h in range(N_HEADS):
                    sout_ref[sl, dl, h] = by_v[0:HEAD, h * HEAD:(h + 1) * HEAD]


def _scan(prep_out, s0_lat, consts):
    def row_spec(d, width):
        return pl.BlockSpec((None, 2 * TM, width), lambda s: (d, _scan_block(s, d), 0))

    def gc_spec(d):
        return pl.BlockSpec((None, 2 * NCH, D_RWKV), lambda s: (d, _scan_block(s, d), 0))

    def y_spec(d):
        return pl.BlockSpec((2 * TM, D_RWKV), lambda s: (_scan_block(s, d), 0))

    rows, gc = list(prep_out[:9]), prep_out[9]
    row_specs = lambda d: [row_spec(d, D_RWKV)] * 5 + [row_spec(d, HC)] * 4 + [gc_spec(d)]
    const = lambda a: pl.BlockSpec(a.shape, lambda s: (0,) * a.ndim)
    y_shape = jax.ShapeDtypeStruct((NT, D_RWKV), F32)
    return pl.pallas_call(
        _chunk_kernel,
        grid=(N_SCAN_STEPS,),
        in_specs=row_specs(0) + row_specs(1) + [const(s0_lat)] + [const(a) for a in consts],
        out_specs=[y_spec(0), y_spec(1),
                   pl.BlockSpec((2, 2, N_HEADS, HEAD, HEAD), lambda s: (_scan_state_block(s), 0, 0, 0, 0))],
        out_shape=[y_shape, y_shape,
                   jax.ShapeDtypeStruct((N_CTX_SEQ, 2, N_HEADS, HEAD, HEAD), F32)],
        scratch_shapes=[pltpu.VMEM((2, 2, D_RWKV, D_RWKV), F32)],
        compiler_params=_cp(("arbitrary",)),
        name="rwkv_scan",
    )(*rows, gc, *rows, gc, s0_lat, *consts)


HALO = 16


def _glu(z):
    return z[:, 0:D_CONV] * _sigmoid(z[:, D_CONV:])


def _conv_kernel(cur_ref, prev_ref, next_ref, w_ref, b_ref, g_ref, beta_ref, o_ref, ext):
    i = pl.program_id(0)
    first, last = _first_last(i)
    n_ext = TM + 2 * HALO
    u = jnp.concatenate([_glu(prev_ref[...]) * (1.0 - first.astype(F32)), _glu(cur_ref[...]),
                         _glu(next_ref[...]) * (1.0 - last.astype(F32))], axis=0)
    ext[0] = u
    for b in range(1, SUB):
        ext[b] = pltpu.roll(u, n_ext - b, axis=0)
    acc = jnp.zeros((TM, D_CONV), F32)
    for j in range(CONV_W):
        off = HALO - CONV_PAD + j
        acc = acc + ext[off % SUB, off - off % SUB:off - off % SUB + TM, :] * w_ref[j:j + 1, :]
    h = acc + b_ref[...]
    mu = jnp.mean(h, axis=-1, keepdims=True)
    hc = h - mu
    var = jnp.mean(hc * hc, axis=-1, keepdims=True)
    y = hc * lax.rsqrt(var + LN_EPS) * g_ref[...] + beta_ref[...]
    o_ref[...] = _silu(y)


def _conv(zc, conv_w, conv_b, ln_g, ln_b):
    nh = TM // HALO
    vec = pl.BlockSpec((1, D_CONV), lambda i: (0, 0))
    return pl.pallas_call(
        _conv_kernel,
        grid=(N_TILES,),
        in_specs=[pl.BlockSpec((TM, 2 * D_CONV), lambda i: (i, 0)),
                  pl.BlockSpec((HALO, 2 * D_CONV), lambda i: (_prev_tile(i) * nh + nh - 1, 0)),
                  pl.BlockSpec((HALO, 2 * D_CONV), lambda i: (_next_tile(i) * nh, 0)),
                  pl.BlockSpec((CONV_W + 1, D_CONV), lambda i: (0, 0)), vec, vec, vec],
        out_specs=pl.BlockSpec((TM, D_CONV), lambda i: (i, 0)),
        out_shape=jax.ShapeDtypeStruct((NT, D_CONV), F32),
        scratch_shapes=[pltpu.VMEM((SUB, TM + 2 * HALO, D_CONV), F32)],
        compiler_params=_cp(("parallel",)),
        name="conv_module",
    )(zc, zc, zc, conv_w, conv_b, ln_g, ln_b)


def _outproj_kernel(xp_ref, xs_ref, pe_ref, mod_ref, yf_ref, yb_ref, bf_ref, bb_ref, zg_ref, yc_ref,
                    gng_ref, gnb_ref, gl_ref, wo_ref, n2_ref, rw_ref, rb_ref, ones_ref,
                    x1_ref, h2_ref, logit_ref):
    i = pl.program_id(0)
    x = _load_x(i, xp_ref, xs_ref, pe_ref)
    g1 = mod_ref[:, 2 * D:3 * D]
    sh2 = mod_ref[:, 3 * D:4 * D]
    sc2 = mod_ref[:, 4 * D:5 * D]
    ones = ones_ref[...]
    y = (yf_ref[...] + bf_ref[...]) + (yb_ref[...] + bb_ref[...])
    mu = _dot_sel(y, ones) * (1.0 / HEAD)
    yc = y - mu
    var = _dot_sel(yc * yc, ones) * (1.0 / HEAD)
    yn = yc * lax.rsqrt(var + GN_EPS) * gng_ref[...] + gnb_ref[...]
    gate = _dot(_sigmoid(zg_ref[...]).astype(BF16), gl_ref[...])
    y_rwkv = (yn * gate).astype(BF16)
    mix = _dot(y_rwkv, wo_ref[0:D_RWKV, :]) + _dot(yc_ref[...].astype(BF16), wo_ref[D_RWKV:, :])
    x1 = x + g1 * mix
    x1_ref[...] = x1
    h2 = _rms(x1, n2_ref[...]) * (1.0 + sc2) + sh2
    h2_ref[...] = h2.astype(BF16)

    logit_ref[...] = _dot_hp(h2, rw_ref[...]) + rb_ref[...]


def _route(logits):
    lane = lax.broadcasted_iota(jnp.int32, logits.shape, 1)
    lanef = lane.astype(F32)
    neg = jnp.float32(-1e30)
    big = jnp.float32(1e9)
    gmask = lane < N_GROUPS
    gl = jnp.where(gmask, logits, neg)
    ge = jnp.where(gmask, jnp.exp(gl - jnp.max(gl, axis=-1, keepdims=True)), 0.0)
    gprob = ge / jnp.sum(ge, axis=-1, keepdims=True)
    gp = jnp.max(gprob, axis=-1, keepdims=True)
    gidx = jnp.min(jnp.where(jnp.logical_and(gmask, gprob == gp), lanef, big), axis=-1, keepdims=True)
    egrp = jnp.floor((lanef - float(E_LANE0)) * (1.0 / N_EXP_PER_GROUP))
    emask = jnp.logical_and(jnp.logical_and(lane >= E_LANE0, lane < E_LANE0 + N_EXPERTS), egrp == gidx)
    el = jnp.where(emask, logits, neg)
    ee = jnp.where(emask, jnp.exp(el - jnp.max(el, axis=-1, keepdims=True)), 0.0)
    ep = ee / jnp.sum(ee, axis=-1, keepdims=True)
    m1 = jnp.max(jnp.where(emask, ep, -1.0), axis=-1, keepdims=True)
    i1 = jnp.min(jnp.where(jnp.logical_and(emask, ep == m1), lanef, big), axis=-1, keepdims=True)
    mask2 = jnp.logical_and(emask, lanef != i1)
    m2 = jnp.max(jnp.where(mask2, ep, -1.0), axis=-1, keepdims=True)
    i2 = jnp.min(jnp.where(jnp.logical_and(mask2, ep == m2), lanef, big), axis=-1, keepdims=True)
    den = m1 + m2
    return (jnp.where(lanef == i1, gp * (m1 / den), 0.0)
            + jnp.where(lanef == i2, gp * (m2 / den), 0.0)
            + jnp.where(lane == GROUP_LANE, gidx, 0.0))


def _outproj(xp, xs, pe, mod3, yf, yb, bonus, zg, yconv, gn_g, gn_b, gate_bf, w_out_bf, norm2_g,
             router_w, router_b, ones512):
    tile = lambda n: pl.BlockSpec((TM, n), lambda i: (i, 0))
    const = lambda shape: pl.BlockSpec(shape, lambda i: (0,) * len(shape))
    return pl.pallas_call(
        _outproj_kernel,
        grid=(N_TILES,),
        in_specs=_x_specs() + [
            pl.BlockSpec((None, 1, N_MOD * D), lambda i: (_mod_row(i), 0, 0)),
            tile(D_RWKV), tile(D_RWKV),
            pl.BlockSpec((None, TM, D_RWKV), lambda i: (0, i, 0)),
            pl.BlockSpec((None, TM, D_RWKV), lambda i: (1, i, 0)),
            tile(LORA_G), tile(D_CONV),
            const((1, D_RWKV)), const((1, D_RWKV)), const((LORA_G, D_RWKV)), const((D, D)),
            const((1, D)), const((D, ROUTE_LANES)), const((1, ROUTE_LANES)),
            const((D_RWKV, D_RWKV))],
        out_specs=[tile(D), tile(D), tile(ROUTE_LANES)],
        out_shape=[jax.ShapeDtypeStruct((NT, D), F32), jax.ShapeDtypeStruct((NT, D), BF16),
                   jax.ShapeDtypeStruct((NT, ROUTE_LANES), F32)],
        compiler_params=_cp(("parallel",)),
        name="outproj_router",
    )(xp, xs, pe, mod3, yf, yb, bonus, bonus, zg, yconv, gn_g, gn_b, gate_bf, w_out_bf, norm2_g,
      router_w, router_b, ones512)


def _plan_kernel(logit_ref, triu_ref, comb_ref, drow_ref, dcol_ref, start_ref, ntile_ref):
    comb = _route(logit_ref[...])
    comb_ref[...] = comb
    gidx = comb.T[GROUP_LANE:GROUP_LANE + 1, :]
    grow = lax.broadcasted_iota(jnp.int32, (SUB, 1), 0)
    growf = grow.astype(F32)
    onehot = jnp.where(jnp.logical_and(gidx == growf, grow < N_GROUPS), 1.0, 0.0)
    before = _dot(onehot.astype(BF16), triu_ref[...])
    count = jnp.sum(onehot, axis=-1, keepdims=True)
    padded = jnp.floor((count + (SEG - 1.0)) * (1.0 / SEG)) * SEG
    start = jnp.zeros((SUB, 1), F32)
    for g in range(N_GROUPS - 1):
        start = start + jnp.where(grow > g, padded[g:g + 1, :], 0.0)
    dest = jnp.sum(onehot * (start + before), axis=0, keepdims=True)
    drow_ref[...] = jnp.broadcast_to(dest, (SUB, TM_MOE))
    dcol_ref[...] = jnp.broadcast_to(dest, (LANES, TM_MOE)).T
    start_ref[...] = jnp.broadcast_to(start * (1.0 / SEG), (SUB, LANES)).astype(jnp.int32)
    ntile_ref[...] = jnp.broadcast_to(padded * (1.0 / SEG), (SUB, LANES)).astype(jnp.int32)


def _plan(logits, triu):
    seg_shape = jax.ShapeDtypeStruct((N_MOE_TILES, SUB, LANES), jnp.int32)
    seg_spec = pl.BlockSpec((None, SUB, LANES), lambda t: (t, 0, 0))
    tok_spec = pl.BlockSpec((TM_MOE, LANES), lambda t: (t, 0))
    return pl.pallas_call(
        _plan_kernel,
        grid=(N_MOE_TILES,),
        in_specs=[tok_spec, pl.BlockSpec((TM_MOE, TM_MOE), lambda t: (0, 0))],
        out_specs=[tok_spec, pl.BlockSpec((None, SUB, TM_MOE), lambda t: (t, 0, 0)), tok_spec,
                   seg_spec, seg_spec],
        out_shape=[jax.ShapeDtypeStruct((NT, ROUTE_LANES), F32), jax.ShapeDtypeStruct((N_MOE_TILES, SUB, TM_MOE), F32),
                   jax.ShapeDtypeStruct((NT, LANES), F32), seg_shape, seg_shape],
        compiler_params=_cp(("parallel",)),
        name="moe_plan",
    )(logits, triu)


def _moe_kernel(start_ref, ntile_ref, h_ref, comb_ref, drow_ref, dcol_ref, wg_ref, wu_ref, wd_ref,
                o_ref, xs, cws, ys):
    q = pl.program_id(1)
    g = q // (N_EXP_PER_GROUP // EXP_PER_STEP)
    tiles = range(TILES_PER_PASS)
    toks = [slice(j * TM_MOE, (j + 1) * TM_MOE) for j in tiles]

    @pl.when(q == 0)
    def _sort_in():
        slot = lax.broadcasted_iota(jnp.int32, (MOE_ROWS, TM_MOE), 0).astype(F32)
        for j in tiles:
            perm = jnp.where(slot == drow_ref[j, 0:1, :], 1.0, 0.0).astype(BF16)
            xs[j] = _dot(perm, h_ref[toks[j], :]).astype(BF16)
            ch, cl = _split2(comb_ref[toks[j], :])
            cws[j] = _dot(perm, ch) + _dot(perm, cl)
            ys[j] = jnp.zeros((MOE_ROWS, D), F32)

    segs = [(pl.program_id(0) * TILES_PER_PASS + j) * N_GROUPS + g for j in tiles]
    firsts = [start_ref[seg] for seg in segs]

    def visit(windows, n_rows):
        rows = [(j, pl.ds(pl.multiple_of(row0 * SEG, SEG), n_rows)) for j, row0 in windows]
        x = jnp.concatenate([xs[j, r, :] for j, r in rows], axis=0)
        cw_all = jnp.concatenate([cws[j, r, :] for j, r in rows], axis=0)
        lane = lax.broadcasted_iota(jnp.int32, cw_all.shape, 1)
        acc = jnp.concatenate([ys[j, r, :] for j, r in rows], axis=0)
        for k in range(EXP_PER_STEP):
            e_lane = q * EXP_PER_STEP + k + E_LANE0
            cw = jnp.sum(jnp.where(lane == e_lane, cw_all, 0.0), axis=-1, keepdims=True)
            hid = (_silu(_dot(x, wg_ref[k])) * _dot(x, wu_ref[k]) * cw).astype(BF16)
            acc = acc + _dot(hid, wd_ref[k])
        for i, (j, r) in enumerate(rows):
            ys[j, r, :] = acc[i * n_rows:(i + 1) * n_rows]

    visit([(j, jnp.minimum(firsts[j], MOE_ROWS // SEG - MOE_WINDOW)) for j in tiles], MOE_WINDOW * SEG)

    for j in tiles:
        def sub_tile(i, carry, j=j):
            visit([(j, firsts[j] + i)], SEG)
            return carry

        lax.fori_loop(MOE_WINDOW, ntile_ref[segs[j]], sub_tile, 0)

    @pl.when(q == N_EXPERTS // EXP_PER_STEP - 1)
    def _sort_out():
        slot = lax.broadcasted_iota(jnp.int32, (TM_MOE, MOE_ROWS), 1).astype(F32)
        for j in tiles:
            perm_t = jnp.where(slot == dcol_ref[toks[j], 0:1], 1.0, 0.0).astype(BF16)
            o_ref[toks[j], :] = _dot(perm_t, ys[j].astype(BF16)).astype(BF16)


def _moe(h2, comb, drow, dcol, seg_start, seg_ntile, wg, wu, wd):
    rows = TILES_PER_PASS * TM_MOE
    once = dict(pipeline_mode=pl.Buffered(1))
    weights = lambda shape: pl.BlockSpec(shape, lambda p, q, s, n: (q, 0, 0))
    grid_spec = pltpu.PrefetchScalarGridSpec(
        num_scalar_prefetch=2,
        grid=(N_MOE_TILES // TILES_PER_PASS, N_EXPERTS // EXP_PER_STEP),
        in_specs=[pl.BlockSpec((rows, D), lambda p, q, s, n: (p, 0), **once),
                  pl.BlockSpec((rows, ROUTE_LANES), lambda p, q, s, n: (p, 0), **once),
                  pl.BlockSpec((TILES_PER_PASS, SUB, TM_MOE), lambda p, q, s, n: (p, 0, 0), **once),
                  pl.BlockSpec((rows, LANES), lambda p, q, s, n: (p, 0), **once),
                  weights((EXP_PER_STEP, D, D_EXPERT)), weights((EXP_PER_STEP, D, D_EXPERT)),
                  weights((EXP_PER_STEP, D_EXPERT, D))],
        out_specs=pl.BlockSpec((rows, D), lambda p, q, s, n: (p, 0)),
        scratch_shapes=[pltpu.VMEM((TILES_PER_PASS, MOE_ROWS, D), BF16),
                        pltpu.VMEM((TILES_PER_PASS, MOE_ROWS, ROUTE_LANES), F32),
                        pltpu.VMEM((TILES_PER_PASS, MOE_ROWS, D), F32)])
    return pl.pallas_call(
        _moe_kernel,
        grid_spec=grid_spec,
        out_shape=jax.ShapeDtypeStruct((NT, D), BF16),
        compiler_params=_cp(("parallel", "arbitrary")),
        name="moe_experts",
    )(seg_start, seg_ntile, h2, comb, drow, dcol, wg, wu, wd)


def _final_kernel(x1_ref, moe_ref, mod_ref, g_ref, o_ref):
    g2 = mod_ref[:, 5 * D:6 * D]
    x2 = x1_ref[...] + g2 * moe_ref[...].astype(F32)
    o_ref[...] = _rms(x2, g_ref[...])


def _final(x1, moe, mod3, final_g, tile0, n_tiles, out_block, name):
    tile = pl.BlockSpec((TM, D), lambda i: (tile0 + i, 0))
    return pl.pallas_call(
        _final_kernel,
        grid=(n_tiles,),
        in_specs=[tile, tile, pl.BlockSpec((None, 1, N_MOD * D), lambda i: (_mod_row(tile0 + i), 0, 0)),
                  pl.BlockSpec((1, D), lambda i: (0, 0))],
        out_specs=pl.BlockSpec((TM, D), lambda i: (out_block(tile0 + i), 0)),
        out_shape=jax.ShapeDtypeStruct((n_tiles * TM, D), F32),
        compiler_params=_cp(("parallel",)),
        name=name,
    )(x1, moe, mod3, final_g)


def _pos_embed(rows):
    t = np.arange(rows * GRID_W)
    row = (t // GRID_W).astype(np.float32)
    col = (t % GRID_W).astype(np.float32)
    quarter = D // 4
    freqs = (1.0 / (10000.0 ** (np.arange(quarter, dtype=np.float32) / quarter))).astype(np.float32)
    ang_r = row[:, None] * freqs[None, :]
    ang_c = col[:, None] * freqs[None, :]
    pe = np.concatenate([np.sin(ang_r), np.cos(ang_r), np.sin(ang_c), np.cos(ang_c)], axis=-1)
    return jnp.asarray(pe, F32)


def _selection_constants():
    ch = np.arange(D_RWKV)
    ones512 = (ch[:, None] // HEAD == ch[None, :] // HEAD).astype(np.float32)
    t = np.arange(TM)
    same_chunk = t[:, None] // CH == t[None, :] // CH
    cum_f = same_chunk & (t[None, :] <= t[:, None])
    cum_b = same_chunk & (t[None, :] >= t[:, None])
    sel8 = np.arange(NCH)[:, None] == t[None, :] // CH
    col = np.arange(HC)
    hm = col[:, None] // CH == ch[None, :] // HEAD
    bm = col[:, None] // CH == col[None, :] // CH
    tt, jj = np.arange(CH)[:, None], col[None, :] % CH
    masks = np.stack([jj < tt, jj <= tt, jj > tt, jj >= tt])
    eye = jj == tt
    fold = ch[:, None] % HEAD == np.arange(HEAD)[None, :]
    bf = lambda x: jnp.asarray(x, BF16)
    f32 = lambda x: jnp.asarray(x, F32)
    prep_consts = (bf(ones512), bf(np.stack([cum_f, cum_b])), bf(same_chunk), bf(sel8), bf(hm), bf(bm),
                   f32(masks.reshape(2, 2, CH, HC)), f32(eye))
    scan_consts = (bf(hm), f32(ones512), bf(fold), bf(fold.T))
    return prep_consts, scan_consts


def kernel(x_prompt, x_sample, state_rwkv, c, c_ctx, ada_w, ada_b, norm1_g, w_in, tshift_mu, decay_w0, decay_lora_b, iclr_a0, iclr_lora_b, key_k, key_a, bonus_r_k, gate_lora_b, gn_g, gn_b, conv_dw_w, conv_dw_b, conv_ln_g, conv_ln_b, w_out, norm2_g, router_group_w, router_group_b, router_expert_w, router_expert_b, expert_w_gate, expert_w_up, expert_w_down, final_norm_g):
    assert x_prompt.shape == (N_CTX_SEQ, T_CTX, D) and x_sample.shape == (N_LAT_SEQ, T_LAT, D)
    assert ada_w.shape[0] == 1, "one trunk layer"
    prep_consts, scan_consts = _selection_constants()
    ones512 = prep_consts[0]
    xp = x_prompt.reshape(N_CTX_SEQ * T_CTX, D)
    xs = x_sample.reshape(N_LAT_SEQ * T_LAT, D)
    pe = _pos_embed(T_LAT // GRID_W)

    cond8 = jnp.concatenate([c_ctx[None, :], c, jnp.zeros((8 - 1 - N_LAT_SEQ, D), F32)], axis=0)
    mod3 = _adaln(cond8, ada_w[0], ada_b[0][None, :]).reshape(8, 1, N_MOD * D)

    zs, zg, zc = _inproj(xp, xs, pe, mod3, norm1_g, w_in[0].astype(BF16))

    zero = jnp.zeros((2, LORA, D_RWKV), F32)
    lora2 = jnp.concatenate([jnp.concatenate([decay_lora_b[0], zero], axis=2),
                             jnp.concatenate([zero, iclr_lora_b[0]], axis=2)], axis=1)
    vec = lambda p: p.reshape(2, 1, -1)
    prep_out = _prep(zs, vec(tshift_mu[0]), vec(decay_w0[0]), vec(iclr_a0[0]), lora2,
                     vec(key_k[0]), vec(key_a[0]), vec(bonus_r_k[0]), *prep_consts)
    bonus = prep_out[10]

    s0_lat = state_rwkv[:, 0].transpose(1, 0, 2, 4, 3).reshape(2, N_LAT_SEQ, D_RWKV, HEAD)
    yf, yb, s_fin = _scan(prep_out, s0_lat, scan_consts)

    yconv = _conv(zc, jnp.concatenate([conv_dw_w[0], jnp.zeros((1, D_CONV), F32)], axis=0),
                  conv_dw_b, conv_ln_g, conv_ln_b)

    router_w = jnp.concatenate([router_group_w[0], router_expert_w[0],
                                jnp.zeros((D, ROUTE_LANES - N_GROUPS - N_EXPERTS), F32)], axis=1)
    router_b = jnp.concatenate([router_group_b[0], router_expert_b[0],
                                jnp.zeros((ROUTE_LANES - N_GROUPS - N_EXPERTS,), F32)])[None, :]
    x1, h2, logits = _outproj(xp, xs, pe, mod3, yf, yb, bonus, zg, yconv,
                              gn_g, gn_b, gate_lora_b[0].astype(BF16), w_out[0].astype(BF16), norm2_g,
                              router_w, router_b, ones512)

    tok = np.arange(TM_MOE)
    triu = jnp.asarray(tok[:, None] < tok[None, :], BF16)
    comb, drow, dcol, seg_start, seg_ntile = _plan(logits, triu)
    seg_start = seg_start[:, :N_GROUPS, 0].reshape(-1)
    seg_ntile = seg_ntile[:, :N_GROUPS, 0].reshape(-1)
    moe = _moe(h2, comb, drow, dcol, seg_start, seg_ntile,
               expert_w_gate[0].astype(BF16), expert_w_up[0].astype(BF16), expert_w_down[0].astype(BF16))
    fg = final_norm_g[None, :]
    y_prompt = _final(x1, moe, mod3, fg, 0, N_CTX_TILES, lambda i: i, "final_norm_ctx")
    y_sample = _final(x1, moe, mod3, fg, N_CTX_TILES, N_LAT_TILES, _xs_block, "final_norm_lat")
    y_prompt = y_prompt.reshape(N_CTX_SEQ, T_CTX, D)
    y_sample = y_sample.reshape(N_LAT_SEQ, T_LAT, D)
    new_state = s_fin[:, None].astype(state_rwkv.dtype)
    return (y_prompt, y_sample, new_state)
```

```python
import numpy as np
import jax
import jax.numpy as jnp
from jax import lax
from jax.experimental import pallas as pl
from jax.experimental.pallas import tpu as pltpu

F32 = jnp.float32
BF16 = jnp.bfloat16

D = 1024
N_CTX_SEQ = 16
T_CTX = 256
N_LAT_SEQ = 2
T_LAT = 1024
TM = 256
N_CTX_TILES = N_CTX_SEQ * T_CTX // TM
LAT_CHUNKS = T_LAT // TM
N_LAT_TILES = N_LAT_SEQ * LAT_CHUNKS
N_TILES = N_CTX_TILES + N_LAT_TILES
NT = N_TILES * TM
GRID_W = 64
D_RWKV = 512
D_CONV = 512
HEAD = 64
N_HEADS = 8
CONV_W = 31
CONV_PAD = CONV_W // 2
LORA = 64
LORA_G = 128
SHIFT_COLS = 3 * D_RWKV + 2 * LORA
N_GROUPS = 4
N_EXP_PER_GROUP = 8
N_EXPERTS = 32
D_EXPERT = 256
N_MOD = 6
RMS_EPS = 1e-6
LN_EPS = 1e-5
GN_EPS = 64e-5
LANES = 128
SUB = 8
CH = 32
NCH = TM // CH
HC = N_HEADS * CH
DOUBLINGS = 4
ROUTE_LANES = 128
E_LANE0 = N_GROUPS
GROUP_LANE = 36
TM_MOE = 1024
SEG = 64
EXP_PER_STEP = 2
TILES_PER_PASS = 2
MOE_WINDOW = 5
MOE_ROWS = TM_MOE + N_GROUPS * SEG + (MOE_WINDOW - 1) * SEG
N_MOE_TILES = NT // TM_MOE
VMEM_LIMIT = 56 * 1024 * 1024


def _cp(sem, flags=None):
    return pltpu.CompilerParams(dimension_semantics=sem, vmem_limit_bytes=VMEM_LIMIT, flags=flags)


def _split2(a):
    hi = a.astype(BF16)
    lo = (a - hi.astype(F32)).astype(BF16)
    return hi, lo


def _dot(a, b):
    return jnp.dot(a, b, preferred_element_type=F32)


def _dot_hp(a, b):
    ah, al = _split2(a)
    bh, bl = _split2(b)
    return _dot(ah, bh) + _dot(ah, bl) + _dot(al, bh)


def _dot_sel(a, sel):
    h, l = _split2(a)
    return _dot(h, sel) + _dot(l, sel)


def _sel_dot(sel, a):
    h, l = _split2(a)
    return _dot(sel, h) + _dot(sel, l)


def _sigmoid(x):
    return 1.0 / (1.0 + jnp.exp(-x))


def _silu(x):
    return x * _sigmoid(x)


def _lat_js(i):
    il = jnp.maximum(i - N_CTX_TILES, 0)
    return il // N_LAT_SEQ, il % N_LAT_SEQ


def _xp_block(i):
    return jnp.minimum(i, N_CTX_TILES - 1)


def _xs_block(i):
    j, s = _lat_js(i)
    return s * LAT_CHUNKS + j


def _pe_block(i):
    j, _ = _lat_js(i)
    return j


def _mod_row(i):
    _, s = _lat_js(i)
    return jnp.where(i < N_CTX_TILES, 0, 1 + s)


def _first_last(i):
    j, _ = _lat_js(i)
    is_ctx = i < N_CTX_TILES
    first = jnp.logical_or(is_ctx, j == 0)
    last = jnp.logical_or(is_ctx, j == LAT_CHUNKS - 1)
    return first, last


def _prev_tile(i):
    first, _ = _first_last(i)
    return jnp.where(first, i, i - N_LAT_SEQ)


def _next_tile(i):
    _, last = _first_last(i)
    return jnp.where(last, i, i + N_LAT_SEQ)


def _adaln_kernel(c_ref, w_ref, b_ref, o_ref):
    c = c_ref[...]
    o_ref[...] = _dot_hp(_silu(c), w_ref[...]) + b_ref[...]


def _adaln(cond8, ada_w, ada_b):
    tn = 1536
    n = ada_w.shape[1]
    return pl.pallas_call(
        _adaln_kernel,
        grid=(n // tn,),
        in_specs=[pl.BlockSpec((8, D), lambda j: (0, 0)),
                  pl.BlockSpec((D, tn), lambda j: (0, j)),
                  pl.BlockSpec((1, tn), lambda j: (0, j))],
        out_specs=pl.BlockSpec((8, tn), lambda j: (0, j)),
        out_shape=jax.ShapeDtypeStruct((8, n), F32),
        compiler_params=_cp(("parallel",)),
        name="adaln",
    )(cond8, ada_w, ada_b)


def _load_x(i, xp_ref, xs_ref, pe_ref):
    f = (i >= N_CTX_TILES).astype(F32)
    return xp_ref[...] * (1.0 - f) + (xs_ref[...] + pe_ref[...]) * f


def _x_specs():
    return [pl.BlockSpec((TM, D), lambda i: (_xp_block(i), 0)),
            pl.BlockSpec((TM, D), lambda i: (_xs_block(i), 0)),
            pl.BlockSpec((TM, D), lambda i: (_pe_block(i), 0))]


def _rms(x, g):
    return x * lax.rsqrt(jnp.mean(x * x, axis=-1, keepdims=True) + RMS_EPS) * g


def _inproj_kernel(xp_ref, xs_ref, pe_ref, mod_ref, g_ref, w_ref, zs_ref, zg_ref, zc_ref):
    i = pl.program_id(0)
    x = _load_x(i, xp_ref, xs_ref, pe_ref)
    sh1 = mod_ref[:, 0:D]
    sc1 = mod_ref[:, D:2 * D]
    h = (_rms(x, g_ref[...]) * (1.0 + sc1) + sh1).astype(BF16)
    zs_ref[...] = _dot(h, w_ref[:, 0:SHIFT_COLS])
    zg_ref[...] = _dot(h, w_ref[:, SHIFT_COLS:SHIFT_COLS + LORA_G])
    zc_ref[...] = _dot(h, w_ref[:, SHIFT_COLS + LORA_G:])


def _inproj(xp, xs, pe, mod3, norm1_g, w_in_bf):
    in_cols = w_in_bf.shape[1]
    return pl.pallas_call(
        _inproj_kernel,
        grid=(N_TILES,),
        in_specs=_x_specs() + [
            pl.BlockSpec((None, 1, N_MOD * D), lambda i: (_mod_row(i), 0, 0)),
            pl.BlockSpec((1, D), lambda i: (0, 0)),
            pl.BlockSpec((D, in_cols), lambda i: (0, 0))],
        out_specs=[pl.BlockSpec((TM, SHIFT_COLS), lambda i: (i, 0)),
                   pl.BlockSpec((TM, LORA_G), lambda i: (i, 0)),
                   pl.BlockSpec((TM, 2 * D_CONV), lambda i: (i, 0))],
        out_shape=[jax.ShapeDtypeStruct((NT, SHIFT_COLS), F32),
                   jax.ShapeDtypeStruct((NT, LORA_G), F32),
                   jax.ShapeDtypeStruct((NT, 2 * D_CONV), F32)],
        compiler_params=_cp(("parallel",)),
        name="inproj",
    )(xp, xs, pe, mod3, norm1_g, w_in_bf)


NT_DIMS = (((1,), (1,)), ((), ()))


def _tile_rows(x):
    return jnp.concatenate([x] * N_HEADS, axis=0)


def _prep_kernel(zs_ref, prev_ref, next_ref, mu_ref, w0_ref, a0_ref, lora_ref, kk_ref_, ka_ref,
                 rk_ref, ones_ref, cum_ref, tot_ref, sel_ref, hm_ref, bm_ref, msk_ref, eye_ref,
                 at_out, rt_out, bh_out, kh_out, v_out, t_out, aak_out, arb_out, ark_out, gc_out, bon_out):
    i = pl.program_id(0)
    first, last = _first_last(i)
    cur = zs_ref[...]
    prow = prev_ref[7:8, :] * (1.0 - first.astype(F32))
    nrow = next_ref[0:1, :] * (1.0 - last.astype(F32))
    rows = lax.broadcasted_iota(jnp.int32, (TM, 1), 0)
    shifted = (jnp.where(rows == 0, prow, pltpu.roll(cur, 1, axis=0)),
               jnp.where(rows == TM - 1, nrow, pltpu.roll(cur, TM - 1, axis=0)))
    ones = ones_ref[...]
    lane = lax.broadcasted_iota(jnp.int32, (TM, 2 * LORA), 1)
    scaled = []
    for d in range(2):
        xs = cur + (shifted[d] - cur) * mu_ref[d]
        r = xs[:, 0:D_RWKV]
        k = xs[:, D_RWKV:2 * D_RWKV]
        v = xs[:, 2 * D_RWKV:3 * D_RWKV]
        z2 = xs[:, 3 * D_RWKV:SHIFT_COLS]
        lin = jnp.where(lane < LORA, jnp.tanh(z2), z2)
        lo = _dot_hp(lin, lora_ref[d])
        u = -(w0_ref[d] + lo[:, 0:D_RWKV])
        softplus = jnp.maximum(u, 0.0) + jnp.log1p(jnp.exp(-jnp.abs(u)))
        w_log = -softplus - 0.5
        lw = -jnp.exp(w_log)
        a = _sigmoid(a0_ref[d] + lo[:, D_RWKV:])
        kx = k * kk_ref_[d]
        nrm = jnp.sqrt(_dot_sel(kx * kx, ones))
        kk = kx / jnp.maximum(nrm, 1e-12)
        k2 = k * (1.0 + (a - 1.0) * ka_ref[d])
        bvec = kk * a
        bon_out[d] = _dot_sel(r * k2 * rk_ref[d], ones) * v

        lg = _sel_dot(cum_ref[d], lw)
        tot = _sel_dot(tot_ref[...], lw)
        e_tail = jnp.exp(tot - lg)
        e_inv = jnp.exp(-lg)
        at = (-(kk * jnp.exp(lg - lw))).astype(BF16)
        bt = (bvec * e_inv).astype(BF16)
        kt = (k2 * e_inv).astype(BF16)
        rt = (r * jnp.exp(lg)).astype(BF16)
        at_out[d] = at
        rt_out[d] = rt
        bh_out[d] = (bvec * e_tail).astype(BF16)
        kh_out[d] = (k2 * e_tail).astype(BF16)
        v_out[d] = v.astype(BF16)
        gc_out[d] = jnp.exp(_sel_dot(sel_ref[...], lw))
        scaled.append((at, bt, kt, rt))

    hm = hm_ref[...]
    bm = bm_ref[...]

    def expand(z):
        return _tile_rows(z.astype(BF16)) * bm

    chunks = [(d, slice(c * CH, (c + 1) * CH)) for d in range(2) for c in range(NCH)]
    pws = []
    for d, rows in chunks:
        at, bt, kt, rt = scaled[d]
        ar = jnp.concatenate([at[rows], rt[rows]], axis=0)
        bk = jnp.concatenate([_tile_rows(bt[rows]) * hm, _tile_rows(kt[rows]) * hm], axis=0)
        p1 = lax.dot_general(ar, bk, NT_DIMS, preferred_element_type=F32)
        m_strict, m_incl = msk_ref[d, 0], msk_ref[d, 1]
        pws.append(p1[0:CH, 0:HC] * m_strict)
        aak_out[d, rows, :] = (p1[0:CH, HC:] * m_strict).astype(BF16)
        arb_out[d, rows, :] = (p1[CH:, 0:HC] * m_incl).astype(BF16)
        ark_out[d, rows, :] = (p1[CH:, HC:] * m_incl).astype(BF16)
    tms = [eye_ref[...] + pw for pw in pws]
    pws = [_dot(pw.astype(BF16), expand(pw)) for pw in pws]
    for k in range(1, DOUBLINGS + 1):
        if k < DOUBLINGS:
            prods = [_dot(jnp.concatenate([pw, tm], axis=0).astype(BF16), expand(pw))
                     for pw, tm in zip(pws, tms)]
            pws = [p[0:CH] for p in prods]
            tms = [tm + p[CH:] for tm, p in zip(tms, prods)]
        else:
            tms = [tm + _dot(tm.astype(BF16), expand(pw)) for tm, pw in zip(tms, pws)]
    for (d, rows), tm in zip(chunks, tms):
        t_out[d, rows, :] = tm.astype(BF16)


def _prep(zs, mu, w0, a0, lora2, key_k, key_a, r_k, ones512, cum2, tot, sel8, hm, bm, masks, eye):
    full = lambda a: pl.BlockSpec(a.shape, lambda i: (0,) * a.ndim)
    out_spec = pl.BlockSpec((2, TM, D_RWKV), lambda i: (0, i, 0))
    bf_shape = jax.ShapeDtypeStruct((2, NT, D_RWKV), BF16)
    tall_spec = pl.BlockSpec((2, TM, HC), lambda i: (0, i, 0))
    rows8 = TM // 8
    params = (mu, w0, a0, lora2, key_k, key_a, r_k, ones512, cum2, tot, sel8, hm, bm, masks, eye)
    return pl.pallas_call(
        _prep_kernel,
        grid=(N_TILES,),
        in_specs=[pl.BlockSpec((TM, SHIFT_COLS), lambda i: (i, 0)),
                  pl.BlockSpec((8, SHIFT_COLS), lambda i: (_prev_tile(i) * rows8 + rows8 - 1, 0)),
                  pl.BlockSpec((8, SHIFT_COLS), lambda i: (_next_tile(i) * rows8, 0))]
                 + [full(p) for p in params],
        out_specs=[out_spec] * 5 + [tall_spec] * 4 + [
            pl.BlockSpec((2, NCH, D_RWKV), lambda i: (0, i, 0)), out_spec],
        out_shape=[bf_shape] * 5 + [jax.ShapeDtypeStruct((2, NT, HC), BF16)] * 4 + [
            jax.ShapeDtypeStruct((2, N_TILES * NCH, D_RWKV), F32),
            jax.ShapeDtypeStruct((2, NT, D_RWKV), F32)],
        compiler_params=_cp(("parallel",)),
        name="rwkv_prep",
    )(zs, zs, zs, *params)


N_SCAN_STEPS = N_CTX_TILES // 2 + LAT_CHUNKS
LAT_STEP0 = N_CTX_TILES // 2


def _scan_block(step, d):
    lat = step >= LAT_STEP0
    jl = step - LAT_STEP0
    j = jnp.where(d == 0, jl, LAT_CHUNKS - 1 - jl)
    return jnp.where(lat, LAT_STEP0 + j, step)


def _scan_state_block(step):
    return jnp.minimum(step, LAT_STEP0 - 1)


def _chunk_kernel(*refs):
    (atf, rtf, bhf, khf, vf, tf, aakf, arbf, arkf, gcf, atb, rtb, bhb, khb, vb, tb, aakb, arbb, arkb, gcb,
     s0_ref, hm_ref, bd_ref, fold_ref, foldt_ref, yf_ref, yb_ref, sout_ref, mw) = refs
    step = pl.program_id(0)
    in_refs = ((atf, rtf, bhf, khf, vf, tf, aakf, arbf, arkf, gcf),
               (atb, rtb, bhb, khb, vb, tb, aakb, arbb, arkb, gcb))
    y_refs = (yf_ref, yb_ref)

    @pl.when(step < LAT_STEP0)
    def _zero_state():
        mw[...] = jnp.zeros(mw.shape, F32)

    bd = bd_ref[...]

    @pl.when(step == LAT_STEP0)
    def _load_state():
        for dl in range(2):
            for sl in range(2):
                mw[dl, sl] = _dot_sel(s0_ref[dl, sl], foldt_ref[...]) * bd

    hm = hm_ref[...]
    crow = lax.broadcasted_iota(jnp.int32, (NCH, 1), 0)

    def chunk_body(q, carry):
        chains = []
        for dl in range(2):
            cc = q if dl == 0 else NCH - 1 - q
            for sl in range(2):
                chains.append((dl, sl, cc, pl.ds(pl.multiple_of(sl * TM + cc * CH, CH), CH)))
        ld = lambda idx, dl, rows: in_refs[dl][idx][rows, :]
        m0s = [mw[dl, sl] for dl, sl, _, _ in chains]
        xy0s = [_dot(jnp.concatenate([ld(0, dl, rows), ld(1, dl, rows)], axis=0), m0.astype(BF16))
                for (dl, _, _, rows), m0 in zip(chains, m0s)]
        vvs = [ld(4, dl, rows) for dl, _, _, rows in chains]
        avs = [_dot(jnp.concatenate([ld(6, dl, rows), ld(8, dl, rows)], axis=0), _tile_rows(vv) * hm)
               for (dl, _, _, rows), vv in zip(chains, vvs)]
        ubs = [_dot(ld(5, dl, rows), _tile_rows((xy0[0:CH] + av[0:CH]).astype(BF16)) * hm).astype(BF16)
               for (dl, _, _, rows), xy0, av in zip(chains, xy0s, avs)]
        for (dl, _, _, rows), xy0, av, ub in zip(chains, xy0s, avs, ubs):
            y_refs[dl][rows, :] = xy0[CH:] + av[CH:] + _dot(ld(7, dl, rows), _tile_rows(ub) * hm)
        for (dl, sl, cc, rows), m0, ub, vv in zip(chains, m0s, ubs, vvs):
            gcrow = jnp.sum(jnp.where(crow == cc, in_refs[dl][9][sl * NCH:(sl + 1) * NCH, :], 0.0),
                            axis=0, keepdims=True)
            pad = jnp.zeros((LANES - 2 * CH - SUB, D_RWKV), F32)
            stack = jnp.concatenate([ld(2, dl, rows).astype(F32), ld(3, dl, rows).astype(F32),
                                     jnp.broadcast_to(gcrow, (SUB, D_RWKV)), pad], axis=0)
            stack_t = stack.T
            uv = jnp.concatenate([ub, vv, jnp.zeros((LANES - 2 * CH, D_RWKV), BF16)], axis=0)
            upd = _dot(stack_t.astype(BF16), uv)
            mw[dl, sl] = (m0 * stack_t[:, 2 * CH:2 * CH + 1] + upd) * bd
        return carry

    lax.fori_loop(0, NCH, chunk_body, 0)

    @pl.when(step < LAT_STEP0)
    def _final():
        for dl in range(2):
            for sl in range(2):
                compact = _dot_sel(mw[dl, sl], fold_ref[...])
                by_v = jnp.concatenate([compact, jnp.zeros_like(compact)], axis=1).T
                for h in range(N_HEADS):
                    sout_ref[sl, dl, h] = by_v[0:HEAD, h * HEAD:(h + 1) * HEAD]


def _scan(prep_out, s0_lat, consts):
    def row_spec(d, width):
        return pl.BlockSpec((None, 2 * TM, width), lambda s: (d, _scan_block(s, d), 0))

    def gc_spec(d):
        return pl.BlockSpec((None, 2 * NCH, D_RWKV), lambda s: (d, _scan_block(s, d), 0))

    def y_spec(d):
        return pl.BlockSpec((2 * TM, D_RWKV), lambda s: (_scan_block(s, d), 0))

    rows, gc = list(prep_out[:9]), prep_out[9]
    row_specs = lambda d: [row_spec(d, D_RWKV)] * 5 + [row_spec(d, HC)] * 4 + [gc_spec(d)]
    const = lambda a: pl.BlockSpec(a.shape, lambda s: (0,) * a.ndim)
    y_shape = jax.ShapeDtypeStruct((NT, D_RWKV), F32)
    return pl.pallas_call(
        _chunk_kernel,
        grid=(N_SCAN_STEPS,),
        in_specs=row_specs(0) + row_specs(1) + [const(s0_lat)] + [const(a) for a in consts],
        out_specs=[y_spec(0), y_spec(1),
                   pl.BlockSpec((2, 2, N_HEADS, HEAD, HEAD), lambda s: (_scan_state_block(s), 0, 0, 0, 0))],
        out_shape=[y_shape, y_shape,
                   jax.ShapeDtypeStruct((N_CTX_SEQ, 2, N_HEADS, HEAD, HEAD), F32)],
        scratch_shapes=[pltpu.VMEM((2, 2, D_RWKV, D_RWKV), F32)],
        compiler_params=_cp(("arbitrary",)),
        name="rwkv_scan",
    )(*rows, gc, *rows, gc, s0_lat, *consts)


HALO = 16


def _glu(z):
    return z[:, 0:D_CONV] * _sigmoid(z[:, D_CONV:])


def _conv_kernel(cur_ref, prev_ref, next_ref, w_ref, b_ref, g_ref, beta_ref, o_ref, ext):
    i = pl.program_id(0)
    first, last = _first_last(i)
    n_ext = TM + 2 * HALO
    u = jnp.concatenate([_glu(prev_ref[...]) * (1.0 - first.astype(F32)), _glu(cur_ref[...]),
                         _glu(next_ref[...]) * (1.0 - last.astype(F32))], axis=0)
    ext[0] = u
    for b in range(1, SUB):
        ext[b] = pltpu.roll(u, n_ext - b, axis=0)
    acc = jnp.zeros((TM, D_CONV), F32)
    for j in range(CONV_W):
        off = HALO - CONV_PAD + j
        acc = acc + ext[off % SUB, off - off % SUB:off - off % SUB + TM, :] * w_ref[j:j + 1, :]
    h = acc + b_ref[...]
    mu = jnp.mean(h, axis=-1, keepdims=True)
    hc = h - mu
    var = jnp.mean(hc * hc, axis=-1, keepdims=True)
    y = hc * lax.rsqrt(var + LN_EPS) * g_ref[...] + beta_ref[...]
    o_ref[...] = _silu(y)


def _conv(zc, conv_w, conv_b, ln_g, ln_b):
    nh = TM // HALO
    vec = pl.BlockSpec((1, D_CONV), lambda i: (0, 0))
    return pl.pallas_call(
        _conv_kernel,
        grid=(N_TILES,),
        in_specs=[pl.BlockSpec((TM, 2 * D_CONV), lambda i: (i, 0)),
                  pl.BlockSpec((HALO, 2 * D_CONV), lambda i: (_prev_tile(i) * nh + nh - 1, 0)),
                  pl.BlockSpec((HALO, 2 * D_CONV), lambda i: (_next_tile(i) * nh, 0)),
                  pl.BlockSpec((CONV_W + 1, D_CONV), lambda i: (0, 0)), vec, vec, vec],
        out_specs=pl.BlockSpec((TM, D_CONV), lambda i: (i, 0)),
        out_shape=jax.ShapeDtypeStruct((NT, D_CONV), F32),
        scratch_shapes=[pltpu.VMEM((SUB, TM + 2 * HALO, D_CONV), F32)],
        compiler_params=_cp(("parallel",)),
        name="conv_module",
    )(zc, zc, zc, conv_w, conv_b, ln_g, ln_b)


def _outproj_kernel(xp_ref, xs_ref, pe_ref, mod_ref, yf_ref, yb_ref, bf_ref, bb_ref, zg_ref, yc_ref,
                    gng_ref, gnb_ref, gl_ref, wo_ref, n2_ref, rw_ref, rb_ref, ones_ref,
                    x1_ref, h2_ref, logit_ref):
    i = pl.program_id(0)
    x = _load_x(i, xp_ref, xs_ref, pe_ref)
    g1 = mod_ref[:, 2 * D:3 * D]
    sh2 = mod_ref[:, 3 * D:4 * D]
    sc2 = mod_ref[:, 4 * D:5 * D]
    ones = ones_ref[...]
    y = (yf_ref[...] + bf_ref[...]) + (yb_ref[...] + bb_ref[...])
    mu = _dot_sel(y, ones) * (1.0 / HEAD)
    yc = y - mu
    var = _dot_sel(yc * yc, ones) * (1.0 / HEAD)
    yn = yc * lax.rsqrt(var + GN_EPS) * gng_ref[...] + gnb_ref[...]
    gate = _dot(_sigmoid(zg_ref[...]).astype(BF16), gl_ref[...])
    y_rwkv = (yn * gate).astype(BF16)
    mix = _dot(y_rwkv, wo_ref[0:D_RWKV, :]) + _dot(yc_ref[...].astype(BF16), wo_ref[D_RWKV:, :])
    x1 = x + g1 * mix
    x1_ref[...] = x1
    h2 = _rms(x1, n2_ref[...]) * (1.0 + sc2) + sh2
    h2_ref[...] = h2.astype(BF16)

    logit_ref[...] = _dot_hp(h2, rw_ref[...]) + rb_ref[...]


def _route(logits):
    lane = lax.broadcasted_iota(jnp.int32, logits.shape, 1)
    lanef = lane.astype(F32)
    neg = jnp.float32(-1e30)
    big = jnp.float32(1e9)
    gmask = lane < N_GROUPS
    gl = jnp.where(gmask, logits, neg)
    ge = jnp.where(gmask, jnp.exp(gl - jnp.max(gl, axis=-1, keepdims=True)), 0.0)
    gprob = ge / jnp.sum(ge, axis=-1, keepdims=True)
    gp = jnp.max(gprob, axis=-1, keepdims=True)
    gidx = jnp.min(jnp.where(jnp.logical_and(gmask, gprob == gp), lanef, big), axis=-1, keepdims=True)
    egrp = jnp.floor((lanef - float(E_LANE0)) * (1.0 / N_EXP_PER_GROUP))
    emask = jnp.logical_and(jnp.logical_and(lane >= E_LANE0, lane < E_LANE0 + N_EXPERTS), egrp == gidx)
    el = jnp.where(emask, logits, neg)
    ee = jnp.where(emask, jnp.exp(el - jnp.max(el, axis=-1, keepdims=True)), 0.0)
    ep = ee / jnp.sum(ee, axis=-1, keepdims=True)
    m1 = jnp.max(jnp.where(emask, ep, -1.0), axis=-1, keepdims=True)
    i1 = jnp.min(jnp.where(jnp.logical_and(emask, ep == m1), lanef, big), axis=-1, keepdims=True)
    mask2 = jnp.logical_and(emask, lanef != i1)
    m2 = jnp.max(jnp.where(mask2, ep, -1.0), axis=-1, keepdims=True)
    i2 = jnp.min(jnp.where(jnp.logical_and(mask2, ep == m2), lanef, big), axis=-1, keepdims=True)
    den = m1 + m2
    return (jnp.where(lanef == i1, gp * (m1 / den), 0.0)
            + jnp.where(lanef == i2, gp * (m2 / den), 0.0)
            + jnp.where(lane == GROUP_LANE, gidx, 0.0))


def _outproj(xp, xs, pe, mod3, yf, yb, bonus, zg, yconv, gn_g, gn_b, gate_bf, w_out_bf, norm2_g,
             router_w, router_b, ones512):
    tile = lambda n: pl.BlockSpec((TM, n), lambda i: (i, 0))
    const = lambda shape: pl.BlockSpec(shape, lambda i: (0,) * len(shape))
    return pl.pallas_call(
        _outproj_kernel,
        grid=(N_TILES,),
        in_specs=_x_specs() + [
            pl.BlockSpec((None, 1, N_MOD * D), lambda i: (_mod_row(i), 0, 0)),
            tile(D_RWKV), tile(D_RWKV),
            pl.BlockSpec((None, TM, D_RWKV), lambda i: (0, i, 0)),
            pl.BlockSpec((None, TM, D_RWKV), lambda i: (1, i, 0)),
            tile(LORA_G), tile(D_CONV),
            const((1, D_RWKV)), const((1, D_RWKV)), const((LORA_G, D_RWKV)), const((D, D)),
            const((1, D)), const((D, ROUTE_LANES)), const((1, ROUTE_LANES)),
            const((D_RWKV, D_RWKV))],
        out_specs=[tile(D), tile(D), tile(ROUTE_LANES)],
        out_shape=[jax.ShapeDtypeStruct((NT, D), F32), jax.ShapeDtypeStruct((NT, D), BF16),
                   jax.ShapeDtypeStruct((NT, ROUTE_LANES), F32)],
        compiler_params=_cp(("parallel",)),
        name="outproj_router",
    )(xp, xs, pe, mod3, yf, yb, bonus, bonus, zg, yconv, gn_g, gn_b, gate_bf, w_out_bf, norm2_g,
      router_w, router_b, ones512)


def _plan_kernel(logit_ref, triu_ref, comb_ref, drow_ref, dcol_ref, start_ref, ntile_ref):
    comb = _route(logit_ref[...])
    comb_ref[...] = comb
    gidx = comb.T[GROUP_LANE:GROUP_LANE + 1, :]
    grow = lax.broadcasted_iota(jnp.int32, (SUB, 1), 0)
    growf = grow.astype(F32)
    onehot = jnp.where(jnp.logical_and(gidx == growf, grow < N_GROUPS), 1.0, 0.0)
    before = _dot(onehot.astype(BF16), triu_ref[...])
    count = jnp.sum(onehot, axis=-1, keepdims=True)
    padded = jnp.floor((count + (SEG - 1.0)) * (1.0 / SEG)) * SEG
    start = jnp.zeros((SUB, 1), F32)
    for g in range(N_GROUPS - 1):
        start = start + jnp.where(grow > g, padded[g:g + 1, :], 0.0)
    dest = jnp.sum(onehot * (start + before), axis=0, keepdims=True)
    drow_ref[...] = jnp.broadcast_to(dest, (SUB, TM_MOE))
    dcol_ref[...] = jnp.broadcast_to(dest, (LANES, TM_MOE)).T
    start_ref[...] = jnp.broadcast_to(start * (1.0 / SEG), (SUB, LANES)).astype(jnp.int32)
    ntile_ref[...] = jnp.broadcast_to(padded * (1.0 / SEG), (SUB, LANES)).astype(jnp.int32)


def _plan(logits, triu):
    seg_shape = jax.ShapeDtypeStruct((N_MOE_TILES, SUB, LANES), jnp.int32)
    seg_spec = pl.BlockSpec((None, SUB, LANES), lambda t: (t, 0, 0))
    tok_spec = pl.BlockSpec((TM_MOE, LANES), lambda t: (t, 0))
    return pl.pallas_call(
        _plan_kernel,
        grid=(N_MOE_TILES,),
        in_specs=[tok_spec, pl.BlockSpec((TM_MOE, TM_MOE), lambda t: (0, 0))],
        out_specs=[tok_spec, pl.BlockSpec((None, SUB, TM_MOE), lambda t: (t, 0, 0)), tok_spec,
                   seg_spec, seg_spec],
        out_shape=[jax.ShapeDtypeStruct((NT, ROUTE_LANES), F32), jax.ShapeDtypeStruct((N_MOE_TILES, SUB, TM_MOE), F32),
                   jax.ShapeDtypeStruct((NT, LANES), F32), seg_shape, seg_shape],
        compiler_params=_cp(("parallel",)),
        name="moe_plan",
    )(logits, triu)


def _moe_kernel(start_ref, ntile_ref, h_ref, comb_ref, drow_ref, dcol_ref, wg_ref, wu_ref, wd_ref,
                o_ref, xs, cws, ys):
    q = pl.program_id(1)
    g = q // (N_EXP_PER_GROUP // EXP_PER_STEP)
    tiles = range(TILES_PER_PASS)
    toks = [slice(j * TM_MOE, (j + 1) * TM_MOE) for j in tiles]

    @pl.when(q == 0)
    def _sort_in():
        slot = lax.broadcasted_iota(jnp.int32, (MOE_ROWS, TM_MOE), 0).astype(F32)
        for j in tiles:
            perm = jnp.where(slot == drow_ref[j, 0:1, :], 1.0, 0.0).astype(BF16)
            xs[j] = _dot(perm, h_ref[toks[j], :]).astype(BF16)
            ch, cl = _split2(comb_ref[toks[j], :])
            cws[j] = _dot(perm, ch) + _dot(perm, cl)
            ys[j] = jnp.zeros((MOE_ROWS, D), F32)

    segs = [(pl.program_id(0) * TILES_PER_PASS + j) * N_GROUPS + g for j in tiles]
    firsts = [start_ref[seg] for seg in segs]

    def visit(windows, n_rows):
        rows = [(j, pl.ds(pl.multiple_of(row0 * SEG, SEG), n_rows)) for j, row0 in windows]
        x = jnp.concatenate([xs[j, r, :] for j, r in rows], axis=0)
        cw_all = jnp.concatenate([cws[j, r, :] for j, r in rows], axis=0)
        lane = lax.broadcasted_iota(jnp.int32, cw_all.shape, 1)
        acc = jnp.concatenate([ys[j, r, :] for j, r in rows], axis=0)
        for k in range(EXP_PER_STEP):
            e_lane = q * EXP_PER_STEP + k + E_LANE0
            cw = jnp.sum(jnp.where(lane == e_lane, cw_all, 0.0), axis=-1, keepdims=True)
            hid = (_silu(_dot(x, wg_ref[k].astype(BF16))) * _dot(x, wu_ref[k].astype(BF16)) * cw).astype(BF16)
            acc = acc + _dot(hid, wd_ref[k].astype(BF16))
        for i, (j, r) in enumerate(rows):
            ys[j, r, :] = acc[i * n_rows:(i + 1) * n_rows]

    visit([(j, jnp.minimum(firsts[j], MOE_ROWS // SEG - MOE_WINDOW)) for j in tiles], MOE_WINDOW * SEG)

    for j in tiles:
        def sub_tile(i, carry, j=j):
            visit([(j, firsts[j] + i)], SEG)
            return carry

        lax.fori_loop(MOE_WINDOW, ntile_ref[segs[j]], sub_tile, 0)

    @pl.when(q == N_EXPERTS // EXP_PER_STEP - 1)
    def _sort_out():
        slot = lax.broadcasted_iota(jnp.int32, (TM_MOE, MOE_ROWS), 1).astype(F32)
        for j in tiles:
            perm_t = jnp.where(slot == dcol_ref[toks[j], 0:1], 1.0, 0.0).astype(BF16)
            o_ref[toks[j], :] = _dot(perm_t, ys[j].astype(BF16)).astype(BF16)


def _moe(h2, comb, drow, dcol, seg_start, seg_ntile, wg, wu, wd):
    rows = TILES_PER_PASS * TM_MOE
    once = dict(pipeline_mode=pl.Buffered(1))
    weights = lambda shape: pl.BlockSpec(shape, lambda p, q, s, n: (q, 0, 0))
    grid_spec = pltpu.PrefetchScalarGridSpec(
        num_scalar_prefetch=2,
        grid=(N_MOE_TILES // TILES_PER_PASS, N_EXPERTS // EXP_PER_STEP),
        in_specs=[pl.BlockSpec((rows, D), lambda p, q, s, n: (p, 0), **once),
                  pl.BlockSpec((rows, ROUTE_LANES), lambda p, q, s, n: (p, 0), **once),
                  pl.BlockSpec((TILES_PER_PASS, SUB, TM_MOE), lambda p, q, s, n: (p, 0, 0), **once),
                  pl.BlockSpec((rows, LANES), lambda p, q, s, n: (p, 0), **once),
                  weights((EXP_PER_STEP, D, D_EXPERT)), weights((EXP_PER_STEP, D, D_EXPERT)),
                  weights((EXP_PER_STEP, D_EXPERT, D))],
        out_specs=pl.BlockSpec((rows, D), lambda p, q, s, n: (p, 0)),
        scratch_shapes=[pltpu.VMEM((TILES_PER_PASS, MOE_ROWS, D), BF16),
                        pltpu.VMEM((TILES_PER_PASS, MOE_ROWS, ROUTE_LANES), F32),
                        pltpu.VMEM((TILES_PER_PASS, MOE_ROWS, D), F32)])
    return pl.pallas_call(
        _moe_kernel,
        grid_spec=grid_spec,
        out_shape=jax.ShapeDtypeStruct((NT, D), BF16),
        compiler_params=_cp(("parallel", "arbitrary")),
        name="moe_experts",
    )(seg_start, seg_ntile, h2, comb, drow, dcol, wg, wu, wd)


def _final_kernel(x1_ref, moe_ref, mod_ref, g_ref, o_ref):
    g2 = mod_ref[:, 5 * D:6 * D]
    x2 = x1_ref[...] + g2 * moe_ref[...].astype(F32)
    o_ref[...] = _rms(x2, g_ref[...])


def _final(x1, moe, mod3, final_g, tile0, n_tiles, out_block, name):
    tile = pl.BlockSpec((TM, D), lambda i: (tile0 + i, 0))
    return pl.pallas_call(
        _final_kernel,
        grid=(n_tiles,),
        in_specs=[tile, tile, pl.BlockSpec((None, 1, N_MOD * D), lambda i: (_mod_row(tile0 + i), 0, 0)),
                  pl.BlockSpec((1, D), lambda i: (0, 0))],
        out_specs=pl.BlockSpec((TM, D), lambda i: (out_block(tile0 + i), 0)),
        out_shape=jax.ShapeDtypeStruct((n_tiles * TM, D), F32),
        compiler_params=_cp(("parallel",)),
        name=name,
    )(x1, moe, mod3, final_g)


def _pos_embed(rows):
    t = np.arange(rows * GRID_W)
    row = (t // GRID_W).astype(np.float32)
    col = (t % GRID_W).astype(np.float32)
    quarter = D // 4
    freqs = (1.0 / (10000.0 ** (np.arange(quarter, dtype=np.float32) / quarter))).astype(np.float32)
    ang_r = row[:, None] * freqs[None, :]
    ang_c = col[:, None] * freqs[None, :]
    pe = np.concatenate([np.sin(ang_r), np.cos(ang_r), np.sin(ang_c), np.cos(ang_c)], axis=-1)
    return jnp.asarray(pe, F32)


def _selection_constants():
    ch = np.arange(D_RWKV)
    ones512 = (ch[:, None] // HEAD == ch[None, :] // HEAD).astype(np.float32)
    t = np.arange(TM)
    same_chunk = t[:, None] // CH == t[None, :] // CH
    cum_f = same_chunk & (t[None, :] <= t[:, None])
    cum_b = same_chunk & (t[None, :] >= t[:, None])
    sel8 = np.arange(NCH)[:, None] == t[None, :] // CH
    col = np.arange(HC)
    hm = col[:, None] // CH == ch[None, :] // HEAD
    bm = col[:, None] // CH == col[None, :] // CH
    tt, jj = np.arange(CH)[:, None], col[None, :] % CH
    masks = np.stack([jj < tt, jj <= tt, jj > tt, jj >= tt])
    eye = jj == tt
    fold = ch[:, None] % HEAD == np.arange(HEAD)[None, :]
    bf = lambda x: jnp.asarray(x, BF16)
    f32 = lambda x: jnp.asarray(x, F32)
    prep_consts = (bf(ones512), bf(np.stack([cum_f, cum_b])), bf(same_chunk), bf(sel8), bf(hm), bf(bm),
                   f32(masks.reshape(2, 2, CH, HC)), f32(eye))
    scan_consts = (bf(hm), f32(ones512), bf(fold), bf(fold.T))
    return prep_consts, scan_consts


def kernel(x_prompt, x_sample, state_rwkv, c, c_ctx, ada_w, ada_b, norm1_g, w_in, tshift_mu, decay_w0, decay_lora_b, iclr_a0, iclr_lora_b, key_k, key_a, bonus_r_k, gate_lora_b, gn_g, gn_b, conv_dw_w, conv_dw_b, conv_ln_g, conv_ln_b, w_out, norm2_g, router_group_w, router_group_b, router_expert_w, router_expert_b, expert_w_gate, expert_w_up, expert_w_down, final_norm_g):
    assert x_prompt.shape == (N_CTX_SEQ, T_CTX, D) and x_sample.shape == (N_LAT_SEQ, T_LAT, D)
    assert ada_w.shape[0] == 1, "one trunk layer"
    prep_consts, scan_consts = _selection_constants()
    ones512 = prep_consts[0]
    xp = x_prompt.reshape(N_CTX_SEQ * T_CTX, D)
    xs = x_sample.reshape(N_LAT_SEQ * T_LAT, D)
    pe = _pos_embed(T_LAT // GRID_W)

    cond8 = jnp.concatenate([c_ctx[None, :], c, jnp.zeros((8 - 1 - N_LAT_SEQ, D), F32)], axis=0)
    mod3 = _adaln(cond8, ada_w[0], ada_b[0][None, :]).reshape(8, 1, N_MOD * D)

    zs, zg, zc = _inproj(xp, xs, pe, mod3, norm1_g, w_in[0].astype(BF16))

    zero = jnp.zeros((2, LORA, D_RWKV), F32)
    lora2 = jnp.concatenate([jnp.concatenate([decay_lora_b[0], zero], axis=2),
                             jnp.concatenate([zero, iclr_lora_b[0]], axis=2)], axis=1)
    vec = lambda p: p.reshape(2, 1, -1)
    prep_out = _prep(zs, vec(tshift_mu[0]), vec(decay_w0[0]), vec(iclr_a0[0]), lora2,
                     vec(key_k[0]), vec(key_a[0]), vec(bonus_r_k[0]), *prep_consts)
    bonus = prep_out[10]

    s0_lat = state_rwkv[:, 0].transpose(1, 0, 2, 4, 3).reshape(2, N_LAT_SEQ, D_RWKV, HEAD)
    yf, yb, s_fin = _scan(prep_out, s0_lat, scan_consts)

    yconv = _conv(zc, jnp.concatenate([conv_dw_w[0], jnp.zeros((1, D_CONV), F32)], axis=0),
                  conv_dw_b, conv_ln_g, conv_ln_b)

    router_w = jnp.concatenate([router_group_w[0], router_expert_w[0],
                                jnp.zeros((D, ROUTE_LANES - N_GROUPS - N_EXPERTS), F32)], axis=1)
    router_b = jnp.concatenate([router_group_b[0], router_expert_b[0],
                                jnp.zeros((ROUTE_LANES - N_GROUPS - N_EXPERTS,), F32)])[None, :]
    x1, h2, logits = _outproj(xp, xs, pe, mod3, yf, yb, bonus, zg, yconv,
                              gn_g, gn_b, gate_lora_b[0].astype(BF16), w_out[0].astype(BF16), norm2_g,
                              router_w, router_b, ones512)

    tok = np.arange(TM_MOE)
    triu = jnp.asarray(tok[:, None] < tok[None, :], BF16)
    comb, drow, dcol, seg_start, seg_ntile = _plan(logits, triu)
    seg_start = seg_start[:, :N_GROUPS, 0].reshape(-1)
    seg_ntile = seg_ntile[:, :N_GROUPS, 0].reshape(-1)
    moe = _moe(h2, comb, drow, dcol, seg_start, seg_ntile,
               expert_w_gate[0], expert_w_up[0], expert_w_down[0])
    fg = final_norm_g[None, :]
    y_prompt = _final(x1, moe, mod3, fg, 0, N_CTX_TILES, lambda i: i, "final_norm_ctx")
    y_sample = _final(x1, moe, mod3, fg, N_CTX_TILES, N_LAT_TILES, _xs_block, "final_norm_lat")
    y_prompt = y_prompt.reshape(N_CTX_SEQ, T_CTX, D)
    y_sample = y_sample.reshape(N_LAT_SEQ, T_LAT, D)
    new_state = s_fin[:, None].astype(state_rwkv.dtype)
    return (y_prompt, y_sample, new_state)
```

```python
import numpy as np
import jax
import jax.numpy as jnp
from jax import lax
from jax.experimental import pallas as pl
from jax.experimental.pallas import tpu as pltpu

F32 = jnp.float32
BF16 = jnp.bfloat16

D = 1024
N_CTX_SEQ = 16
T_CTX = 256
N_LAT_SEQ = 2
T_LAT = 1024
TM = 256
N_CTX_TILES = N_CTX_SEQ * T_CTX // TM
LAT_CHUNKS = T_LAT // TM
N_LAT_TILES = N_LAT_SEQ * LAT_CHUNKS
N_TILES = N_CTX_TILES + N_LAT_TILES
NT = N_TILES * TM
GRID_W = 64
D_RWKV = 512
D_CONV = 512
HEAD = 64
N_HEADS = 8
CONV_W = 31
CONV_PAD = CONV_W // 2
LORA = 64
LORA_G = 128
SHIFT_COLS = 3 * D_RWKV + 2 * LORA
N_GROUPS = 4
N_EXP_PER_GROUP = 8
N_EXPERTS = 32
D_EXPERT = 256
N_MOD = 6
RMS_EPS = 1e-6
LN_EPS = 1e-5
GN_EPS = 64e-5
LANES = 128
SUB = 8
CH = 32
NCH = TM // CH
HC = N_HEADS * CH
DOUBLINGS = 4
ROUTE_LANES = 128
E_LANE0 = N_GROUPS
GROUP_LANE = 36
TM_MOE = 1024
SEG = 64
EXP_PER_STEP = 2
TILES_PER_PASS = 2
MOE_WINDOW = 5
MOE_ROWS = TM_MOE + N_GROUPS * SEG + (MOE_WINDOW - 1) * SEG
N_MOE_TILES = NT // TM_MOE
VMEM_LIMIT = 56 * 1024 * 1024


def _cp(sem, flags=None):
    return pltpu.CompilerParams(dimension_semantics=sem, vmem_limit_bytes=VMEM_LIMIT, flags=flags)


def _split2(a):
    hi = a.astype(BF16)
    lo = (a - hi.astype(F32)).astype(BF16)
    return hi, lo


def _dot(a, b):
    return jnp.dot(a, b, preferred_element_type=F32)


def _dot_hp(a, b):
    ah, al = _split2(a)
    bh, bl = _split2(b)
    return _dot(ah, bh) + _dot(ah, bl) + _dot(al, bh)


def _dot_sel(a, sel):
    h, l = _split2(a)
    return _dot(h, sel) + _dot(l, sel)


def _sel_dot(sel, a):
    h, l = _split2(a)
    return _dot(sel, h) + _dot(sel, l)


def _sigmoid(x):
    return 1.0 / (1.0 + jnp.exp(-x))


def _silu(x):
    return x * _sigmoid(x)


def _lat_js(i):
    il = jnp.maximum(i - N_CTX_TILES, 0)
    return il // N_LAT_SEQ, il % N_LAT_SEQ


def _xp_block(i):
    return jnp.minimum(i, N_CTX_TILES - 1)


def _xs_block(i):
    j, s = _lat_js(i)
    return s * LAT_CHUNKS + j


def _pe_block(i):
    j, _ = _lat_js(i)
    return j


def _mod_row(i):
    _, s = _lat_js(i)
    return jnp.where(i < N_CTX_TILES, 0, 1 + s)


def _first_last(i):
    j, _ = _lat_js(i)
    is_ctx = i < N_CTX_TILES
    first = jnp.logical_or(is_ctx, j == 0)
    last = jnp.logical_or(is_ctx, j == LAT_CHUNKS - 1)
    return first, last


def _prev_tile(i):
    first, _ = _first_last(i)
    return jnp.where(first, i, i - N_LAT_SEQ)


def _next_tile(i):
    _, last = _first_last(i)
    return jnp.where(last, i, i + N_LAT_SEQ)


def _adaln_kernel(c_ref, w_ref, b_ref, o_ref):
    c = c_ref[...]
    o_ref[...] = _dot_hp(_silu(c), w_ref[...]) + b_ref[...]


def _adaln(cond8, ada_w, ada_b):
    tn = 1536
    n = ada_w.shape[1]
    return pl.pallas_call(
        _adaln_kernel,
        grid=(n // tn,),
        in_specs=[pl.BlockSpec((8, D), lambda j: (0, 0)),
                  pl.BlockSpec((D, tn), lambda j: (0, j)),
                  pl.BlockSpec((1, tn), lambda j: (0, j))],
        out_specs=pl.BlockSpec((8, tn), lambda j: (0, j)),
        out_shape=jax.ShapeDtypeStruct((8, n), F32),
        compiler_params=_cp(("parallel",)),
        name="adaln",
    )(cond8, ada_w, ada_b)


def _load_x(i, xp_ref, xs_ref, pe_ref):
    f = (i >= N_CTX_TILES).astype(F32)
    return xp_ref[...] * (1.0 - f) + (xs_ref[...] + pe_ref[...]) * f


def _x_specs():
    return [pl.BlockSpec((TM, D), lambda i: (_xp_block(i), 0)),
            pl.BlockSpec((TM, D), lambda i: (_xs_block(i), 0)),
            pl.BlockSpec((TM, D), lambda i: (_pe_block(i), 0))]


def _rms(x, g):
    return x * lax.rsqrt(jnp.mean(x * x, axis=-1, keepdims=True) + RMS_EPS) * g


def _inproj_kernel(xp_ref, xs_ref, pe_ref, mod_ref, g_ref, w_ref, zs_ref, zg_ref, zc_ref):
    i = pl.program_id(0)
    x = _load_x(i, xp_ref, xs_ref, pe_ref)
    sh1 = mod_ref[:, 0:D]
    sc1 = mod_ref[:, D:2 * D]
    h = (_rms(x, g_ref[...]) * (1.0 + sc1) + sh1).astype(BF16)
    zs_ref[...] = _dot(h, w_ref[:, 0:SHIFT_COLS])
    zg_ref[...] = _dot(h, w_ref[:, SHIFT_COLS:SHIFT_COLS + LORA_G])
    zc_ref[...] = _dot(h, w_ref[:, SHIFT_COLS + LORA_G:])


def _inproj(xp, xs, pe, mod3, norm1_g, w_in_bf):
    in_cols = w_in_bf.shape[1]
    return pl.pallas_call(
        _inproj_kernel,
        grid=(N_TILES,),
        in_specs=_x_specs() + [
            pl.BlockSpec((None, 1, N_MOD * D), lambda i: (_mod_row(i), 0, 0)),
            pl.BlockSpec((1, D), lambda i: (0, 0)),
            pl.BlockSpec((D, in_cols), lambda i: (0, 0))],
        out_specs=[pl.BlockSpec((TM, SHIFT_COLS), lambda i: (i, 0)),
                   pl.BlockSpec((TM, LORA_G), lambda i: (i, 0)),
                   pl.BlockSpec((TM, 2 * D_CONV), lambda i: (i, 0))],
        out_shape=[jax.ShapeDtypeStruct((NT, SHIFT_COLS), F32),
                   jax.ShapeDtypeStruct((NT, LORA_G), F32),
                   jax.ShapeDtypeStruct((NT, 2 * D_CONV), F32)],
        compiler_params=_cp(("parallel",)),
        name="inproj",
    )(xp, xs, pe, mod3, norm1_g, w_in_bf)


NT_DIMS = (((1,), (1,)), ((), ()))


def _tile_rows(x):
    return jnp.concatenate([x] * N_HEADS, axis=0)


def _prep_kernel(zs_ref, prev_ref, next_ref, mu_ref, w0_ref, a0_ref, lora_ref, kk_ref_, ka_ref,
                 rk_ref, ones_ref, cum_ref, tot_ref, sel_ref, hm_ref, bm_ref, msk_ref, eye_ref,
                 at_out, rt_out, bh_out, kh_out, v_out, t_out, aak_out, arb_out, ark_out, gc_out, bon_out):
    i = pl.program_id(0)
    first, last = _first_last(i)
    cur = zs_ref[...]
    prow = prev_ref[7:8, :] * (1.0 - first.astype(F32))
    nrow = next_ref[0:1, :] * (1.0 - last.astype(F32))
    rows = lax.broadcasted_iota(jnp.int32, (TM, 1), 0)
    shifted = (jnp.where(rows == 0, prow, pltpu.roll(cur, 1, axis=0)),
               jnp.where(rows == TM - 1, nrow, pltpu.roll(cur, TM - 1, axis=0)))
    ones = ones_ref[...]
    lane = lax.broadcasted_iota(jnp.int32, (TM, 2 * LORA), 1)
    scaled = []
    for d in range(2):
        xs = cur + (shifted[d] - cur) * mu_ref[d]
        r = xs[:, 0:D_RWKV]
        k = xs[:, D_RWKV:2 * D_RWKV]
        v = xs[:, 2 * D_RWKV:3 * D_RWKV]
        z2 = xs[:, 3 * D_RWKV:SHIFT_COLS]
        lin = jnp.where(lane < LORA, jnp.tanh(z2), z2)
        lo = _dot_hp(lin, lora_ref[d])
        u = -(w0_ref[d] + lo[:, 0:D_RWKV])
        softplus = jnp.maximum(u, 0.0) + jnp.log1p(jnp.exp(-jnp.abs(u)))
        w_log = -softplus - 0.5
        lw = -jnp.exp(w_log)
        a = _sigmoid(a0_ref[d] + lo[:, D_RWKV:])
        kx = k * kk_ref_[d]
        nrm = jnp.sqrt(_dot_sel(kx * kx, ones))
        kk = kx / jnp.maximum(nrm, 1e-12)
        k2 = k * (1.0 + (a - 1.0) * ka_ref[d])
        bvec = kk * a
        bon_out[d] = _dot_sel(r * k2 * rk_ref[d], ones) * v

        lg = _sel_dot(cum_ref[d], lw)
        tot = _sel_dot(tot_ref[...], lw)
        e_tail = jnp.exp(tot - lg)
        e_inv = jnp.exp(-lg)
        at = (-(kk * jnp.exp(lg - lw))).astype(BF16)
        bt = (bvec * e_inv).astype(BF16)
        kt = (k2 * e_inv).astype(BF16)
        rt = (r * jnp.exp(lg)).astype(BF16)
        at_out[d] = at
        rt_out[d] = rt
        bh_out[d] = (bvec * e_tail).astype(BF16)
        kh_out[d] = (k2 * e_tail).astype(BF16)
        v_out[d] = v.astype(BF16)
        gc_out[d] = jnp.exp(_sel_dot(sel_ref[...], lw))
        scaled.append((at, bt, kt, rt))

    hm = hm_ref[...]
    bm = bm_ref[...]

    def expand(z):
        return _tile_rows(z.astype(BF16)) * bm

    chunks = [(d, slice(c * CH, (c + 1) * CH)) for d in range(2) for c in range(NCH)]
    pws = []
    for d, rows in chunks:
        at, bt, kt, rt = scaled[d]
        ar = jnp.concatenate([at[rows], rt[rows]], axis=0)
        bk = jnp.concatenate([_tile_rows(bt[rows]) * hm, _tile_rows(kt[rows]) * hm], axis=0)
        p1 = lax.dot_general(ar, bk, NT_DIMS, preferred_element_type=F32)
        m_strict, m_incl = msk_ref[d, 0], msk_ref[d, 1]
        pws.append(p1[0:CH, 0:HC] * m_strict)
        aak_out[d, rows, :] = (p1[0:CH, HC:] * m_strict).astype(BF16)
        arb_out[d, rows, :] = (p1[CH:, 0:HC] * m_incl).astype(BF16)
        ark_out[d, rows, :] = (p1[CH:, HC:] * m_incl).astype(BF16)
    tms = [eye_ref[...] + pw for pw in pws]
    pws = [_dot(pw.astype(BF16), expand(pw)) for pw in pws]
    for k in range(1, DOUBLINGS + 1):
        if k < DOUBLINGS:
            prods = [_dot(jnp.concatenate([pw, tm], axis=0).astype(BF16), expand(pw))
                     for pw, tm in zip(pws, tms)]
            pws = [p[0:CH] for p in prods]
            tms = [tm + p[CH:] for tm, p in zip(tms, prods)]
        else:
            tms = [tm + _dot(tm.astype(BF16), expand(pw)) for tm, pw in zip(tms, pws)]
    for (d, rows), tm in zip(chunks, tms):
        t_out[d, rows, :] = tm.astype(BF16)


def _prep(zs, mu, w0, a0, lora2, key_k, key_a, r_k, ones512, cum2, tot, sel8, hm, bm, masks, eye):
    full = lambda a: pl.BlockSpec(a.shape, lambda i: (0,) * a.ndim)
    out_spec = pl.BlockSpec((2, TM, D_RWKV), lambda i: (0, i, 0))
    bf_shape = jax.ShapeDtypeStruct((2, NT, D_RWKV), BF16)
    tall_spec = pl.BlockSpec((2, TM, HC), lambda i: (0, i, 0))
    rows8 = TM // 8
    params = (mu, w0, a0, lora2, key_k, key_a, r_k, ones512, cum2, tot, sel8, hm, bm, masks, eye)
    return pl.pallas_call(
        _prep_kernel,
        grid=(N_TILES,),
        in_specs=[pl.BlockSpec((TM, SHIFT_COLS), lambda i: (i, 0)),
                  pl.BlockSpec((8, SHIFT_COLS), lambda i: (_prev_tile(i) * rows8 + rows8 - 1, 0)),
                  pl.BlockSpec((8, SHIFT_COLS), lambda i: (_next_tile(i) * rows8, 0))]
                 + [full(p) for p in params],
        out_specs=[out_spec] * 5 + [tall_spec] * 4 + [
            pl.BlockSpec((2, NCH, D_RWKV), lambda i: (0, i, 0)), out_spec],
        out_shape=[bf_shape] * 5 + [jax.ShapeDtypeStruct((2, NT, HC), BF16)] * 4 + [
            jax.ShapeDtypeStruct((2, N_TILES * NCH, D_RWKV), F32),
            jax.ShapeDtypeStruct((2, NT, D_RWKV), F32)],
        compiler_params=_cp(("parallel",)),
        name="rwkv_prep",
    )(zs, zs, zs, *params)


SCAN_TILES = 4
CTX_STEPS = N_CTX_TILES // SCAN_TILES
LAT_STEPS = N_LAT_TILES // SCAN_TILES
N_SCAN_STEPS = CTX_STEPS + LAT_STEPS
N_SLOTS = 2 * SCAN_TILES


def _scan_block(step, d):
    jl = step - CTX_STEPS
    j = jnp.where(d == 0, jl, LAT_STEPS - 1 - jl)
    return jnp.where(step >= CTX_STEPS, CTX_STEPS + j, step)


def _scan_state_block(step):
    return jnp.minimum(step, CTX_STEPS - 1)


def _chunk_kernel(*refs):
    (atf, rtf, bhf, khf, vf, tf, aakf, arbf, arkf, gcf, atb, rtb, bhb, khb, vb, tb, aakb, arbb, arkb, gcb,
     s0_ref, hm_ref, bd_ref, yf_ref, yb_ref, sout_ref, mw) = refs
    step = pl.program_id(0)
    in_refs = ((atf, rtf, bhf, khf, vf, tf, aakf, arbf, arkf, gcf),
               (atb, rtb, bhb, khb, vb, tb, aakb, arbb, arkb, gcb))
    y_refs = (yf_ref, yb_ref)
    bd = bd_ref[...]
    hm = hm_ref[...]
    crow = lax.broadcasted_iota(jnp.int32, (NCH, 1), 0)
    heads = range(N_HEADS)

    def advance(chains):
        rows_of = lambda tile, cc: pl.ds(pl.multiple_of(tile * TM + cc * CH, CH), CH)
        chains = [(dl, slot, tile, cc, rows_of(tile, cc)) for dl, slot, tile, cc in chains]
        ld = lambda idx, dl, rows: in_refs[dl][idx][rows, :]
        m0s = [mw[slot] for _, slot, _, _, _ in chains]
        xy0s = [_dot(jnp.concatenate([ld(0, dl, rows), ld(1, dl, rows)], axis=0),
                     _tile_rows(m0.astype(BF16)) * bd)
                for (dl, _, _, _, rows), m0 in zip(chains, m0s)]
        vvs = [ld(4, dl, rows) for dl, _, _, _, rows in chains]
        avs = [_dot(jnp.concatenate([ld(6, dl, rows), ld(8, dl, rows)], axis=0), _tile_rows(vv) * hm)
               for (dl, _, _, _, rows), vv in zip(chains, vvs)]
        ubs = [_dot(ld(5, dl, rows), _tile_rows((xy0[0:CH] + av[0:CH]).astype(BF16)) * hm).astype(BF16)
               for (dl, _, _, _, rows), xy0, av in zip(chains, xy0s, avs)]
        for (dl, _, _, _, rows), xy0, av, ub in zip(chains, xy0s, avs, ubs):
            y_refs[dl][rows, :] = xy0[CH:] + av[CH:] + _dot(ld(7, dl, rows), _tile_rows(ub) * hm)
        for (dl, slot, tile, cc, rows), m0, ub, vv in zip(chains, m0s, ubs, vvs):
            gc_tile = in_refs[dl][9][pl.ds(pl.multiple_of(tile * NCH, NCH), NCH), :]
            gcrow = jnp.sum(jnp.where(crow == cc, gc_tile, 0.0), axis=0, keepdims=True)
            pad = jnp.zeros((LANES - 2 * CH - SUB, D_RWKV), F32)
            stack = jnp.concatenate([ld(2, dl, rows).astype(F32), ld(3, dl, rows).astype(F32),
                                     jnp.broadcast_to(gcrow, (SUB, D_RWKV)), pad], axis=0)
            stack_t = stack.T
            head = lambda h: stack_t[h * HEAD:(h + 1) * HEAD, :]
            bk_t = jnp.concatenate([head(h)[:, 0:2 * CH] for h in heads], axis=1)
            decay = jnp.concatenate([jnp.broadcast_to(head(h)[:, 2 * CH:2 * CH + 1], (HEAD, HEAD))
                                     for h in heads], axis=1)
            uv = _tile_rows(jnp.concatenate([ub, vv], axis=0)) * bd
            mw[slot] = m0 * decay + _dot(bk_t.astype(BF16), uv)

    @pl.when(step < CTX_STEPS)
    def _context():
        mw[...] = jnp.zeros(mw.shape, F32)

        def body(q, carry):
            advance([(dl, dl * SCAN_TILES + sl, jnp.int32(sl), q if dl == 0 else NCH - 1 - q)
                     for dl in range(2) for sl in range(SCAN_TILES)])
            return carry

        lax.fori_loop(0, NCH, body, 0)
        for dl in range(2):
            for sl in range(SCAN_TILES):
                mt = mw[dl * SCAN_TILES + sl]
                by_hv = jnp.concatenate([mt, jnp.zeros_like(mt)], axis=0).T
                for h in range(N_HEADS):
                    sout_ref[sl, dl, h] = by_hv[h * HEAD:(h + 1) * HEAD, 0:HEAD]

    @pl.when(step == CTX_STEPS)
    def _load_state():
        for dl in range(2):
            for sl in range(N_LAT_SEQ):
                mw[dl * SCAN_TILES + sl] = s0_ref[dl, sl]

    @pl.when(step >= CTX_STEPS)
    def _latent():
        n_q = (SCAN_TILES // N_LAT_SEQ) * NCH

        def body(q, carry):
            chains = []
            for dl in range(2):
                qq = q if dl == 0 else n_q - 1 - q
                for sl in range(N_LAT_SEQ):
                    chains.append((dl, dl * SCAN_TILES + sl, (qq // NCH) * N_LAT_SEQ + sl, qq % NCH))
            advance(chains)
            return carry

        lax.fori_loop(0, n_q, body, 0)


def _scan(prep_out, s0_lat, consts):
    rows_per_step = SCAN_TILES * TM

    def row_spec(d, width):
        return pl.BlockSpec((None, rows_per_step, width), lambda s: (d, _scan_block(s, d), 0))

    def gc_spec(d):
        return pl.BlockSpec((None, SCAN_TILES * NCH, D_RWKV), lambda s: (d, _scan_block(s, d), 0))

    def y_spec(d):
        return pl.BlockSpec((rows_per_step, D_RWKV), lambda s: (_scan_block(s, d), 0))

    rows, gc = list(prep_out[:9]), prep_out[9]
    row_specs = lambda d: [row_spec(d, D_RWKV)] * 5 + [row_spec(d, HC)] * 4 + [gc_spec(d)]
    const = lambda a: pl.BlockSpec(a.shape, lambda s: (0,) * a.ndim)
    y_shape = jax.ShapeDtypeStruct((NT, D_RWKV), F32)
    return pl.pallas_call(
        _chunk_kernel,
        grid=(N_SCAN_STEPS,),
        in_specs=row_specs(0) + row_specs(1) + [const(s0_lat)] + [const(a) for a in consts],
        out_specs=[y_spec(0), y_spec(1),
                   pl.BlockSpec((SCAN_TILES, 2, N_HEADS, HEAD, HEAD),
                                lambda s: (_scan_state_block(s), 0, 0, 0, 0))],
        out_shape=[y_shape, y_shape,
                   jax.ShapeDtypeStruct((N_CTX_SEQ, 2, N_HEADS, HEAD, HEAD), F32)],
        scratch_shapes=[pltpu.VMEM((N_SLOTS, HEAD, D_RWKV), F32)],
        compiler_params=_cp(("arbitrary",)),
        name="rwkv_scan",
    )(*rows, gc, *rows, gc, s0_lat, *consts)


HALO = 16


def _glu(z):
    return z[:, 0:D_CONV] * _sigmoid(z[:, D_CONV:])


def _conv_kernel(cur_ref, prev_ref, next_ref, w_ref, b_ref, g_ref, beta_ref, o_ref, ext):
    i = pl.program_id(0)
    first, last = _first_last(i)
    n_ext = TM + 2 * HALO
    u = jnp.concatenate([_glu(prev_ref[...]) * (1.0 - first.astype(F32)), _glu(cur_ref[...]),
                         _glu(next_ref[...]) * (1.0 - last.astype(F32))], axis=0)
    ext[0] = u
    for b in range(1, SUB):
        ext[b] = pltpu.roll(u, n_ext - b, axis=0)
    acc = jnp.zeros((TM, D_CONV), F32)
    for j in range(CONV_W):
        off = HALO - CONV_PAD + j
        acc = acc + ext[off % SUB, off - off % SUB:off - off % SUB + TM, :] * w_ref[j:j + 1, :]
    h = acc + b_ref[...]
    mu = jnp.mean(h, axis=-1, keepdims=True)
    hc = h - mu
    var = jnp.mean(hc * hc, axis=-1, keepdims=True)
    y = hc * lax.rsqrt(var + LN_EPS) * g_ref[...] + beta_ref[...]
    o_ref[...] = _silu(y)


def _conv(zc, conv_w, conv_b, ln_g, ln_b):
    nh = TM // HALO
    vec = pl.BlockSpec((1, D_CONV), lambda i: (0, 0))
    return pl.pallas_call(
        _conv_kernel,
        grid=(N_TILES,),
        in_specs=[pl.BlockSpec((TM, 2 * D_CONV), lambda i: (i, 0)),
                  pl.BlockSpec((HALO, 2 * D_CONV), lambda i: (_prev_tile(i) * nh + nh - 1, 0)),
                  pl.BlockSpec((HALO, 2 * D_CONV), lambda i: (_next_tile(i) * nh, 0)),
                  pl.BlockSpec((CONV_W + 1, D_CONV), lambda i: (0, 0)), vec, vec, vec],
        out_specs=pl.BlockSpec((TM, D_CONV), lambda i: (i, 0)),
        out_shape=jax.ShapeDtypeStruct((NT, D_CONV), F32),
        scratch_shapes=[pltpu.VMEM((SUB, TM + 2 * HALO, D_CONV), F32)],
        compiler_params=_cp(("parallel",)),
        name="conv_module",
    )(zc, zc, zc, conv_w, conv_b, ln_g, ln_b)


def _outproj_kernel(xp_ref, xs_ref, pe_ref, mod_ref, yf_ref, yb_ref, bf_ref, bb_ref, zg_ref, yc_ref,
                    gng_ref, gnb_ref, gl_ref, wo_ref, n2_ref, rw_ref, rb_ref, ones_ref,
                    x1_ref, h2_ref, logit_ref):
    i = pl.program_id(0)
    x = _load_x(i, xp_ref, xs_ref, pe_ref)
    g1 = mod_ref[:, 2 * D:3 * D]
    sh2 = mod_ref[:, 3 * D:4 * D]
    sc2 = mod_ref[:, 4 * D:5 * D]
    ones = ones_ref[...]
    y = (yf_ref[...] + bf_ref[...]) + (yb_ref[...] + bb_ref[...])
    mu = _dot_sel(y, ones) * (1.0 / HEAD)
    yc = y - mu
    var = _dot_sel(yc * yc, ones) * (1.0 / HEAD)
    yn = yc * lax.rsqrt(var + GN_EPS) * gng_ref[...] + gnb_ref[...]
    gate = _dot(_sigmoid(zg_ref[...]).astype(BF16), gl_ref[...])
    y_rwkv = (yn * gate).astype(BF16)
    mix = _dot(y_rwkv, wo_ref[0:D_RWKV, :]) + _dot(yc_ref[...].astype(BF16), wo_ref[D_RWKV:, :])
    x1 = x + g1 * mix
    x1_ref[...] = x1
    h2 = _rms(x1, n2_ref[...]) * (1.0 + sc2) + sh2
    h2_ref[...] = h2.astype(BF16)

    logit_ref[...] = _dot_hp(h2, rw_ref[...]) + rb_ref[...]


def _route(logits):
    lane = lax.broadcasted_iota(jnp.int32, logits.shape, 1)
    lanef = lane.astype(F32)
    neg = jnp.float32(-1e30)
    big = jnp.float32(1e9)
    gmask = lane < N_GROUPS
    gl = jnp.where(gmask, logits, neg)
    ge = jnp.where(gmask, jnp.exp(gl - jnp.max(gl, axis=-1, keepdims=True)), 0.0)
    gprob = ge / jnp.sum(ge, axis=-1, keepdims=True)
    gp = jnp.max(gprob, axis=-1, keepdims=True)
    gidx = jnp.min(jnp.where(jnp.logical_and(gmask, gprob == gp), lanef, big), axis=-1, keepdims=True)
    egrp = jnp.floor((lanef - float(E_LANE0)) * (1.0 / N_EXP_PER_GROUP))
    emask = jnp.logical_and(jnp.logical_and(lane >= E_LANE0, lane < E_LANE0 + N_EXPERTS), egrp == gidx)
    el = jnp.where(emask, logits, neg)
    ee = jnp.where(emask, jnp.exp(el - jnp.max(el, axis=-1, keepdims=True)), 0.0)
    ep = ee / jnp.sum(ee, axis=-1, keepdims=True)
    m1 = jnp.max(jnp.where(emask, ep, -1.0), axis=-1, keepdims=True)
    i1 = jnp.min(jnp.where(jnp.logical_and(emask, ep == m1), lanef, big), axis=-1, keepdims=True)
    mask2 = jnp.logical_and(emask, lanef != i1)
    m2 = jnp.max(jnp.where(mask2, ep, -1.0), axis=-1, keepdims=True)
    i2 = jnp.min(jnp.where(jnp.logical_and(mask2, ep == m2), lanef, big), axis=-1, keepdims=True)
    den = m1 + m2
    return (jnp.where(lanef == i1, gp * (m1 / den), 0.0)
            + jnp.where(lanef == i2, gp * (m2 / den), 0.0)
            + jnp.where(lane == GROUP_LANE, gidx, 0.0))


def _outproj(xp, xs, pe, mod3, yf, yb, bonus, zg, yconv, gn_g, gn_b, gate_bf, w_out_bf, norm2_g,
             router_w, router_b, ones512):
    tile = lambda n: pl.BlockSpec((TM, n), lambda i: (i, 0))
    const = lambda shape: pl.BlockSpec(shape, lambda i: (0,) * len(shape))
    return pl.pallas_call(
        _outproj_kernel,
        grid=(N_TILES,),
        in_specs=_x_specs() + [
            pl.BlockSpec((None, 1, N_MOD * D), lambda i: (_mod_row(i), 0, 0)),
            tile(D_RWKV), tile(D_RWKV),
            pl.BlockSpec((None, TM, D_RWKV), lambda i: (0, i, 0)),
            pl.BlockSpec((None, TM, D_RWKV), lambda i: (1, i, 0)),
            tile(LORA_G), tile(D_CONV),
            const((1, D_RWKV)), const((1, D_RWKV)), const((LORA_G, D_RWKV)), const((D, D)),
            const((1, D)), const((D, ROUTE_LANES)), const((1, ROUTE_LANES)),
            const((D_RWKV, D_RWKV))],
        out_specs=[tile(D), tile(D), tile(ROUTE_LANES)],
        out_shape=[jax.ShapeDtypeStruct((NT, D), F32), jax.ShapeDtypeStruct((NT, D), BF16),
                   jax.ShapeDtypeStruct((NT, ROUTE_LANES), F32)],
        compiler_params=_cp(("parallel",)),
        name="outproj_router",
    )(xp, xs, pe, mod3, yf, yb, bonus, bonus, zg, yconv, gn_g, gn_b, gate_bf, w_out_bf, norm2_g,
      router_w, router_b, ones512)


def _plan_kernel(logit_ref, triu_ref, comb_ref, drow_ref, dcol_ref, start_ref, ntile_ref):
    comb = _route(logit_ref[...])
    comb_ref[...] = comb
    gidx = comb.T[GROUP_LANE:GROUP_LANE + 1, :]
    grow = lax.broadcasted_iota(jnp.int32, (SUB, 1), 0)
    growf = grow.astype(F32)
    onehot = jnp.where(jnp.logical_and(gidx == growf, grow < N_GROUPS), 1.0, 0.0)
    before = _dot(onehot.astype(BF16), triu_ref[...])
    count = jnp.sum(onehot, axis=-1, keepdims=True)
    padded = jnp.floor((count + (SEG - 1.0)) * (1.0 / SEG)) * SEG
    start = jnp.zeros((SUB, 1), F32)
    for g in range(N_GROUPS - 1):
        start = start + jnp.where(grow > g, padded[g:g + 1, :], 0.0)
    dest = jnp.sum(onehot * (start + before), axis=0, keepdims=True)
    drow_ref[...] = jnp.broadcast_to(dest, (SUB, TM_MOE))
    dcol_ref[...] = jnp.broadcast_to(dest, (LANES, TM_MOE)).T
    start_ref[...] = jnp.broadcast_to(start * (1.0 / SEG), (SUB, LANES)).astype(jnp.int32)
    ntile_ref[...] = jnp.broadcast_to(padded * (1.0 / SEG), (SUB, LANES)).astype(jnp.int32)


def _plan(logits, triu):
    seg_shape = jax.ShapeDtypeStruct((N_MOE_TILES, SUB, LANES), jnp.int32)
    seg_spec = pl.BlockSpec((None, SUB, LANES), lambda t: (t, 0, 0))
    tok_spec = pl.BlockSpec((TM_MOE, LANES), lambda t: (t, 0))
    return pl.pallas_call(
        _plan_kernel,
        grid=(N_MOE_TILES,),
        in_specs=[tok_spec, pl.BlockSpec((TM_MOE, TM_MOE), lambda t: (0, 0))],
        out_specs=[tok_spec, pl.BlockSpec((None, SUB, TM_MOE), lambda t: (t, 0, 0)), tok_spec,
                   seg_spec, seg_spec],
        out_shape=[jax.ShapeDtypeStruct((NT, ROUTE_LANES), F32), jax.ShapeDtypeStruct((N_MOE_TILES, SUB, TM_MOE), F32),
                   jax.ShapeDtypeStruct((NT, LANES), F32), seg_shape, seg_shape],
        compiler_params=_cp(("parallel",)),
        name="moe_plan",
    )(logits, triu)


def _moe_kernel(start_ref, ntile_ref, h_ref, comb_ref, drow_ref, dcol_ref, wg_ref, wu_ref, wd_ref,
                o_ref, xs, cws, ys):
    q = pl.program_id(1)
    g = q // (N_EXP_PER_GROUP // EXP_PER_STEP)
    tiles = range(TILES_PER_PASS)
    toks = [slice(j * TM_MOE, (j + 1) * TM_MOE) for j in tiles]

    @pl.when(q == 0)
    def _sort_in():
        slot = lax.broadcasted_iota(jnp.int32, (MOE_ROWS, TM_MOE), 0).astype(F32)
        for j in tiles:
            perm = jnp.where(slot == drow_ref[j, 0:1, :], 1.0, 0.0).astype(BF16)
            xs[j] = _dot(perm, h_ref[toks[j], :]).astype(BF16)
            ch, cl = _split2(comb_ref[toks[j], :])
            cws[j] = _dot(perm, ch) + _dot(perm, cl)
            ys[j] = jnp.zeros((MOE_ROWS, D), F32)

    segs = [(pl.program_id(0) * TILES_PER_PASS + j) * N_GROUPS + g for j in tiles]
    firsts = [start_ref[seg] for seg in segs]

    def visit(windows, n_rows):
        rows = [(j, pl.ds(pl.multiple_of(row0 * SEG, SEG), n_rows)) for j, row0 in windows]
        x = jnp.concatenate([xs[j, r, :] for j, r in rows], axis=0)
        cw_all = jnp.concatenate([cws[j, r, :] for j, r in rows], axis=0)
        lane = lax.broadcasted_iota(jnp.int32, cw_all.shape, 1)
        acc = jnp.concatenate([ys[j, r, :] for j, r in rows], axis=0)
        for k in range(EXP_PER_STEP):
            e_lane = q * EXP_PER_STEP + k + E_LANE0
            cw = jnp.sum(jnp.where(lane == e_lane, cw_all, 0.0), axis=-1, keepdims=True)
            hid = (_silu(_dot(x, wg_ref[k].astype(BF16))) * _dot(x, wu_ref[k].astype(BF16)) * cw).astype(BF16)
            acc = acc + _dot(hid, wd_ref[k].astype(BF16))
        for i, (j, r) in enumerate(rows):
            ys[j, r, :] = acc[i * n_rows:(i + 1) * n_rows]

    visit([(j, jnp.minimum(firsts[j], MOE_ROWS // SEG - MOE_WINDOW)) for j in tiles], MOE_WINDOW * SEG)

    for j in tiles:
        def sub_tile(i, carry, j=j):
            visit([(j, firsts[j] + i)], SEG)
            return carry

        lax.fori_loop(MOE_WINDOW, ntile_ref[segs[j]], sub_tile, 0)

    @pl.when(q == N_EXPERTS // EXP_PER_STEP - 1)
    def _sort_out():
        slot = lax.broadcasted_iota(jnp.int32, (TM_MOE, MOE_ROWS), 1).astype(F32)
        for j in tiles:
            perm_t = jnp.where(slot == dcol_ref[toks[j], 0:1], 1.0, 0.0).astype(BF16)
            o_ref[toks[j], :] = _dot(perm_t, ys[j].astype(BF16)).astype(BF16)


def _moe(h2, comb, drow, dcol, seg_start, seg_ntile, wg, wu, wd):
    rows = TILES_PER_PASS * TM_MOE
    once = dict(pipeline_mode=pl.Buffered(1))
    weights = lambda shape: pl.BlockSpec(shape, lambda p, q, s, n: (q, 0, 0))
    grid_spec = pltpu.PrefetchScalarGridSpec(
        num_scalar_prefetch=2,
        grid=(N_MOE_TILES // TILES_PER_PASS, N_EXPERTS // EXP_PER_STEP),
        in_specs=[pl.BlockSpec((rows, D), lambda p, q, s, n: (p, 0), **once),
                  pl.BlockSpec((rows, ROUTE_LANES), lambda p, q, s, n: (p, 0), **once),
                  pl.BlockSpec((TILES_PER_PASS, SUB, TM_MOE), lambda p, q, s, n: (p, 0, 0), **once),
                  pl.BlockSpec((rows, LANES), lambda p, q, s, n: (p, 0), **once),
                  weights((EXP_PER_STEP, D, D_EXPERT)), weights((EXP_PER_STEP, D, D_EXPERT)),
                  weights((EXP_PER_STEP, D_EXPERT, D))],
        out_specs=pl.BlockSpec((rows, D), lambda p, q, s, n: (p, 0)),
        scratch_shapes=[pltpu.VMEM((TILES_PER_PASS, MOE_ROWS, D), BF16),
                        pltpu.VMEM((TILES_PER_PASS, MOE_ROWS, ROUTE_LANES), F32),
                        pltpu.VMEM((TILES_PER_PASS, MOE_ROWS, D), F32)])
    return pl.pallas_call(
        _moe_kernel,
        grid_spec=grid_spec,
        out_shape=jax.ShapeDtypeStruct((NT, D), BF16),
        compiler_params=_cp(("parallel", "arbitrary")),
        name="moe_experts",
    )(seg_start, seg_ntile, h2, comb, drow, dcol, wg, wu, wd)


def _final_kernel(x1_ref, moe_ref, mod_ref, g_ref, o_ref):
    g2 = mod_ref[:, 5 * D:6 * D]
    x2 = x1_ref[...] + g2 * moe_ref[...].astype(F32)
    o_ref[...] = _rms(x2, g_ref[...])


def _final(x1, moe, mod3, final_g, tile0, n_tiles, out_block, name):
    tile = pl.BlockSpec((TM, D), lambda i: (tile0 + i, 0))
    return pl.pallas_call(
        _final_kernel,
        grid=(n_tiles,),
        in_specs=[tile, tile, pl.BlockSpec((None, 1, N_MOD * D), lambda i: (_mod_row(tile0 + i), 0, 0)),
                  pl.BlockSpec((1, D), lambda i: (0, 0))],
        out_specs=pl.BlockSpec((TM, D), lambda i: (out_block(tile0 + i), 0)),
        out_shape=jax.ShapeDtypeStruct((n_tiles * TM, D), F32),
        compiler_params=_cp(("parallel",)),
        name=name,
    )(x1, moe, mod3, final_g)


def _pos_embed(rows):
    t = np.arange(rows * GRID_W)
    row = (t // GRID_W).astype(np.float32)
    col = (t % GRID_W).astype(np.float32)
    quarter = D // 4
    freqs = (1.0 / (10000.0 ** (np.arange(quarter, dtype=np.float32) / quarter))).astype(np.float32)
    ang_r = row[:, None] * freqs[None, :]
    ang_c = col[:, None] * freqs[None, :]
    pe = np.concatenate([np.sin(ang_r), np.cos(ang_r), np.sin(ang_c), np.cos(ang_c)], axis=-1)
    return jnp.asarray(pe, F32)


def _selection_constants():
    ch = np.arange(D_RWKV)
    ones512 = (ch[:, None] // HEAD == ch[None, :] // HEAD).astype(np.float32)
    t = np.arange(TM)
    same_chunk = t[:, None] // CH == t[None, :] // CH
    cum_f = same_chunk & (t[None, :] <= t[:, None])
    cum_b = same_chunk & (t[None, :] >= t[:, None])
    sel8 = np.arange(NCH)[:, None] == t[None, :] // CH
    col = np.arange(HC)
    hm = col[:, None] // CH == ch[None, :] // HEAD
    bm = col[:, None] // CH == col[None, :] // CH
    tt, jj = np.arange(CH)[:, None], col[None, :] % CH
    masks = np.stack([jj < tt, jj <= tt, jj > tt, jj >= tt])
    eye = jj == tt
    bf = lambda x: jnp.asarray(x, BF16)
    f32 = lambda x: jnp.asarray(x, F32)
    prep_consts = (bf(ones512), bf(np.stack([cum_f, cum_b])), bf(same_chunk), bf(sel8), bf(hm), bf(bm),
                   f32(masks.reshape(2, 2, CH, HC)), f32(eye))
    scan_consts = (bf(hm), bf(ones512))
    return prep_consts, scan_consts


def kernel(x_prompt, x_sample, state_rwkv, c, c_ctx, ada_w, ada_b, norm1_g, w_in, tshift_mu, decay_w0, decay_lora_b, iclr_a0, iclr_lora_b, key_k, key_a, bonus_r_k, gate_lora_b, gn_g, gn_b, conv_dw_w, conv_dw_b, conv_ln_g, conv_ln_b, w_out, norm2_g, router_group_w, router_group_b, router_expert_w, router_expert_b, expert_w_gate, expert_w_up, expert_w_down, final_norm_g):
    assert x_prompt.shape == (N_CTX_SEQ, T_CTX, D) and x_sample.shape == (N_LAT_SEQ, T_LAT, D)
    assert ada_w.shape[0] == 1, "one trunk layer"
    prep_consts, scan_consts = _selection_constants()
    ones512 = prep_consts[0]
    xp = x_prompt.reshape(N_CTX_SEQ * T_CTX, D)
    xs = x_sample.reshape(N_LAT_SEQ * T_LAT, D)
    pe = _pos_embed(T_LAT // GRID_W)

    cond8 = jnp.concatenate([c_ctx[None, :], c, jnp.zeros((8 - 1 - N_LAT_SEQ, D), F32)], axis=0)
    mod3 = _adaln(cond8, ada_w[0], ada_b[0][None, :]).reshape(8, 1, N_MOD * D)

    zs, zg, zc = _inproj(xp, xs, pe, mod3, norm1_g, w_in[0].astype(BF16))

    zero = jnp.zeros((2, LORA, D_RWKV), F32)
    lora2 = jnp.concatenate([jnp.concatenate([decay_lora_b[0], zero], axis=2),
                             jnp.concatenate([zero, iclr_lora_b[0]], axis=2)], axis=1)
    vec = lambda p: p.reshape(2, 1, -1)
    prep_out = _prep(zs, vec(tshift_mu[0]), vec(decay_w0[0]), vec(iclr_a0[0]), lora2,
                     vec(key_k[0]), vec(key_a[0]), vec(bonus_r_k[0]), *prep_consts)
    bonus = prep_out[10]

    s0_lat = state_rwkv[:, 0].transpose(1, 0, 4, 2, 3).reshape(2, N_LAT_SEQ, HEAD, D_RWKV)
    yf, yb, s_fin = _scan(prep_out, s0_lat, scan_consts)

    yconv = _conv(zc, jnp.concatenate([conv_dw_w[0], jnp.zeros((1, D_CONV), F32)], axis=0),
                  conv_dw_b, conv_ln_g, conv_ln_b)

    router_w = jnp.concatenate([router_group_w[0], router_expert_w[0],
                                jnp.zeros((D, ROUTE_LANES - N_GROUPS - N_EXPERTS), F32)], axis=1)
    router_b = jnp.concatenate([router_group_b[0], router_expert_b[0],
                                jnp.zeros((ROUTE_LANES - N_GROUPS - N_EXPERTS,), F32)])[None, :]
    x1, h2, logits = _outproj(xp, xs, pe, mod3, yf, yb, bonus, zg, yconv,
                              gn_g, gn_b, gate_lora_b[0].astype(BF16), w_out[0].astype(BF16), norm2_g,
                              router_w, router_b, ones512)

    tok = np.arange(TM_MOE)
    triu = jnp.asarray(tok[:, None] < tok[None, :], BF16)
    comb, drow, dcol, seg_start, seg_ntile = _plan(logits, triu)
    seg_start = seg_start[:, :N_GROUPS, 0].reshape(-1)
    seg_ntile = seg_ntile[:, :N_GROUPS, 0].reshape(-1)
    moe = _moe(h2, comb, drow, dcol, seg_start, seg_ntile,
               expert_w_gate[0], expert_w_up[0], expert_w_down[0])
    fg = final_norm_g[None, :]
    y_prompt = _final(x1, moe, mod3, fg, 0, N_CTX_TILES, lambda i: i, "final_norm_ctx")
    y_sample = _final(x1, moe, mod3, fg, N_CTX_TILES, N_LAT_TILES, _xs_block, "final_norm_lat")
    y_prompt = y_prompt.reshape(N_CTX_SEQ, T_CTX, D)
    y_sample = y_sample.reshape(N_LAT_SEQ, T_LAT, D)
    new_state = s_fin[:, None].astype(state_rwkv.dtype)
    return (y_prompt, y_sample, new_state)
```

```python
import numpy as np
import jax
import jax.numpy as jnp
from jax import lax
from jax.experimental import pallas as pl
from jax.experimental.pallas import tpu as pltpu

F32 = jnp.float32
BF16 = jnp.bfloat16

D = 1024
N_CTX_SEQ = 16
T_CTX = 256
N_LAT_SEQ = 2
T_LAT = 1024
TM = 256
N_CTX_TILES = N_CTX_SEQ * T_CTX // TM
LAT_CHUNKS = T_LAT // TM
N_LAT_TILES = N_LAT_SEQ * LAT_CHUNKS
N_TILES = N_CTX_TILES + N_LAT_TILES
NT = N_TILES * TM
GRID_W = 64
D_RWKV = 512
D_CONV = 512
HEAD = 64
N_HEADS = 8
CONV_W = 31
CONV_PAD = CONV_W // 2
LORA = 64
LORA_G = 128
SHIFT_COLS = 3 * D_RWKV + 2 * LORA
N_GROUPS = 4
N_EXP_PER_GROUP = 8
N_EXPERTS = 32
D_EXPERT = 256
N_MOD = 6
RMS_EPS = 1e-6
LN_EPS = 1e-5
GN_EPS = 64e-5
LANES = 128
SUB = 8
CH = 32
NCH = TM // CH
HC = N_HEADS * CH
DOUBLINGS = 4
ROUTE_LANES = 128
E_LANE0 = N_GROUPS
GROUP_LANE = 36
TM_MOE = 1024
SEG = 32
EXP_PER_STEP = 2
TILES_PER_PASS = 2
MOE_WINDOW = 9
MOE_ROWS = TM_MOE + N_GROUPS * SEG + (MOE_WINDOW - 1) * SEG
N_MOE_TILES = NT // TM_MOE
VMEM_LIMIT = 56 * 1024 * 1024


def _cp(sem, flags=None):
    return pltpu.CompilerParams(dimension_semantics=sem, vmem_limit_bytes=VMEM_LIMIT, flags=flags)


def _split2(a):
    hi = a.astype(BF16)
    lo = (a - hi.astype(F32)).astype(BF16)
    return hi, lo


def _dot(a, b):
    return jnp.dot(a, b, preferred_element_type=F32)


def _dot_hp(a, b):
    ah, al = _split2(a)
    bh, bl = _split2(b)
    return _dot(ah, bh) + _dot(ah, bl) + _dot(al, bh)


def _dot_sel(a, sel):
    h, l = _split2(a)
    return _dot(h, sel) + _dot(l, sel)


def _sel_dot(sel, a):
    h, l = _split2(a)
    return _dot(sel, h) + _dot(sel, l)


def _sigmoid(x):
    return 1.0 / (1.0 + jnp.exp(-x))


def _silu(x):
    return x * _sigmoid(x)


def _lat_js(i):
    il = jnp.maximum(i - N_CTX_TILES, 0)
    return il // N_LAT_SEQ, il % N_LAT_SEQ


def _xp_block(i):
    return jnp.minimum(i, N_CTX_TILES - 1)


def _xs_block(i):
    j, s = _lat_js(i)
    return s * LAT_CHUNKS + j


def _pe_block(i):
    j, _ = _lat_js(i)
    return j


def _mod_row(i):
    _, s = _lat_js(i)
    return jnp.where(i < N_CTX_TILES, 0, 1 + s)


def _first_last(i):
    j, _ = _lat_js(i)
    is_ctx = i < N_CTX_TILES
    first = jnp.logical_or(is_ctx, j == 0)
    last = jnp.logical_or(is_ctx, j == LAT_CHUNKS - 1)
    return first, last


def _prev_tile(i):
    first, _ = _first_last(i)
    return jnp.where(first, i, i - N_LAT_SEQ)


def _next_tile(i):
    _, last = _first_last(i)
    return jnp.where(last, i, i + N_LAT_SEQ)


def _adaln_kernel(c_ref, w_ref, b_ref, o_ref):
    c = c_ref[...]
    o_ref[...] = _dot_hp(_silu(c), w_ref[...]) + b_ref[...]


def _adaln(cond8, ada_w, ada_b):
    tn = 768
    n = ada_w.shape[1]
    return pl.pallas_call(
        _adaln_kernel,
        grid=(n // tn,),
        in_specs=[pl.BlockSpec((8, D), lambda j: (0, 0)),
                  pl.BlockSpec((D, tn), lambda j: (0, j)),
                  pl.BlockSpec((1, tn), lambda j: (0, j))],
        out_specs=pl.BlockSpec((8, tn), lambda j: (0, j)),
        out_shape=jax.ShapeDtypeStruct((8, n), F32),
        compiler_params=_cp(("parallel",)),
        name="adaln",
    )(cond8, ada_w, ada_b)


def _load_x(i, xp_ref, xs_ref, pe_ref):
    f = (i >= N_CTX_TILES).astype(F32)
    return xp_ref[...] * (1.0 - f) + (xs_ref[...] + pe_ref[...]) * f


def _x_specs():
    return [pl.BlockSpec((TM, D), lambda i: (_xp_block(i), 0)),
            pl.BlockSpec((TM, D), lambda i: (_xs_block(i), 0)),
            pl.BlockSpec((TM, D), lambda i: (_pe_block(i), 0))]


def _rms(x, g):
    return x * lax.rsqrt(jnp.mean(x * x, axis=-1, keepdims=True) + RMS_EPS) * g


def _inproj_kernel(xp_ref, xs_ref, pe_ref, mod_ref, g_ref, w_ref, zs_ref, zg_ref, zc_ref):
    i = pl.program_id(0)
    x = _load_x(i, xp_ref, xs_ref, pe_ref)
    sh1 = mod_ref[:, 0:D]
    sc1 = mod_ref[:, D:2 * D]
    h = (_rms(x, g_ref[...]) * (1.0 + sc1) + sh1).astype(BF16)
    zs_ref[...] = _dot(h, w_ref[:, 0:SHIFT_COLS])
    zg_ref[...] = _dot(h, w_ref[:, SHIFT_COLS:SHIFT_COLS + LORA_G])
    zc_ref[...] = _dot(h, w_ref[:, SHIFT_COLS + LORA_G:])


def _inproj(xp, xs, pe, mod3, norm1_g, w_in_bf):
    in_cols = w_in_bf.shape[1]
    return pl.pallas_call(
        _inproj_kernel,
        grid=(N_TILES,),
        in_specs=_x_specs() + [
            pl.BlockSpec((None, 1, N_MOD * D), lambda i: (_mod_row(i), 0, 0)),
            pl.BlockSpec((1, D), lambda i: (0, 0)),
            pl.BlockSpec((D, in_cols), lambda i: (0, 0))],
        out_specs=[pl.BlockSpec((TM, SHIFT_COLS), lambda i: (i, 0)),
                   pl.BlockSpec((TM, LORA_G), lambda i: (i, 0)),
                   pl.BlockSpec((TM, 2 * D_CONV), lambda i: (i, 0))],
        out_shape=[jax.ShapeDtypeStruct((NT, SHIFT_COLS), F32),
                   jax.ShapeDtypeStruct((NT, LORA_G), F32),
                   jax.ShapeDtypeStruct((NT, 2 * D_CONV), F32)],
        compiler_params=_cp(("parallel",)),
        name="inproj",
    )(xp, xs, pe, mod3, norm1_g, w_in_bf)


NT_DIMS = (((1,), (1,)), ((), ()))


def _tile_rows(x):
    return jnp.concatenate([x] * N_HEADS, axis=0)


def _prep_kernel(zs_ref, prev_ref, next_ref, mu_ref, w0_ref, a0_ref, lora_ref, kk_ref_, ka_ref,
                 rk_ref, ones_ref, cum_ref, hm_ref, bm_ref, msk_ref, eye_ref,
                 at_out, rt_out, bh_out, kh_out, v_out, t_out, aak_out, arb_out, ark_out, gc_out, bon_out):
    i = pl.program_id(0)
    first, last = _first_last(i)
    cur = zs_ref[...]
    prow = prev_ref[7:8, :] * (1.0 - first.astype(F32))
    nrow = next_ref[0:1, :] * (1.0 - last.astype(F32))
    rows = lax.broadcasted_iota(jnp.int32, (TM, 1), 0)
    shifted = (jnp.where(rows == 0, prow, pltpu.roll(cur, 1, axis=0)),
               jnp.where(rows == TM - 1, nrow, pltpu.roll(cur, TM - 1, axis=0)))
    ones = ones_ref[...]
    lane = lax.broadcasted_iota(jnp.int32, (TM, 2 * LORA), 1)
    scaled = []
    for d in range(2):
        xs = cur + (shifted[d] - cur) * mu_ref[d]
        r = xs[:, 0:D_RWKV]
        k = xs[:, D_RWKV:2 * D_RWKV]
        v = xs[:, 2 * D_RWKV:3 * D_RWKV]
        z2 = xs[:, 3 * D_RWKV:SHIFT_COLS]
        lin = jnp.where(lane < LORA, jnp.tanh(z2), z2)
        lo = _dot_hp(lin, lora_ref[d])
        u = -(w0_ref[d] + lo[:, 0:D_RWKV])
        softplus = jnp.maximum(u, 0.0) + jnp.log1p(jnp.exp(-jnp.abs(u)))
        w_log = -softplus - 0.5
        lw = -jnp.exp(w_log)
        a = _sigmoid(a0_ref[d] + lo[:, D_RWKV:])
        kx = k * kk_ref_[d]
        nrm = jnp.sqrt(_dot_sel(kx * kx, ones))
        kk = kx / jnp.maximum(nrm, 1e-12)
        k2 = k * (1.0 + (a - 1.0) * ka_ref[d])
        bvec = kk * a
        bon_out[d] = _dot_sel(r * k2 * rk_ref[d], ones) * v

        sums = _sel_dot(cum_ref[d], lw)
        lg = sums[0:TM]
        tot = sums[TM:2 * TM]
        e_tail = jnp.exp(tot - lg)
        e_inv = jnp.exp(-lg)
        at = (-(kk * jnp.exp(lg - lw))).astype(BF16)
        bt = (bvec * e_inv).astype(BF16)
        kt = (k2 * e_inv).astype(BF16)
        rt = (r * jnp.exp(lg)).astype(BF16)
        at_out[d] = at
        rt_out[d] = rt
        bh_out[d] = (bvec * e_tail).astype(BF16)
        kh_out[d] = (k2 * e_tail).astype(BF16)
        v_out[d] = v.astype(BF16)
        gc_out[d] = jnp.exp(sums[2 * TM:2 * TM + NCH])
        scaled.append((at, bt, kt, rt))

    hm = hm_ref[...]
    bm = bm_ref[...]

    def expand(z):
        return _tile_rows(z.astype(BF16)) * bm

    chunks = [(d, slice(c * CH, (c + 1) * CH)) for d in range(2) for c in range(NCH)]
    pws = []
    for d, rows in chunks:
        at, bt, kt, rt = scaled[d]
        ar = jnp.concatenate([at[rows], rt[rows]], axis=0)
        bk = jnp.concatenate([_tile_rows(bt[rows]) * hm, _tile_rows(kt[rows]) * hm], axis=0)
        p1 = lax.dot_general(ar, bk, NT_DIMS, preferred_element_type=F32)
        m_strict, m_incl = msk_ref[d, 0], msk_ref[d, 1]
        pws.append(p1[0:CH, 0:HC] * m_strict)
        aak_out[d, rows, :] = (p1[0:CH, HC:] * m_strict).astype(BF16)
        arb_out[d, rows, :] = (p1[CH:, 0:HC] * m_incl).astype(BF16)
        ark_out[d, rows, :] = (p1[CH:, HC:] * m_incl).astype(BF16)
    tms = [eye_ref[...] + pw for pw in pws]
    pws = [_dot(pw.astype(BF16), expand(pw)) for pw in pws]
    for k in range(1, DOUBLINGS + 1):
        if k < DOUBLINGS:
            prods = [_dot(jnp.concatenate([pw, tm], axis=0).astype(BF16), expand(pw))
                     for pw, tm in zip(pws, tms)]
            pws = [p[0:CH] for p in prods]
            tms = [tm + p[CH:] for tm, p in zip(tms, prods)]
        else:
            tms = [tm + _dot(tm.astype(BF16), expand(pw)) for tm, pw in zip(tms, pws)]
    for (d, rows), tm in zip(chunks, tms):
        t_out[d, rows, :] = tm.astype(BF16)


def _prep(zs, mu, w0, a0, lora2, key_k, key_a, r_k, ones512, cum2, hm, bm, masks, eye):
    full = lambda a: pl.BlockSpec(a.shape, lambda i: (0,) * a.ndim)
    out_spec = pl.BlockSpec((2, TM, D_RWKV), lambda i: (0, i, 0))
    bf_shape = jax.ShapeDtypeStruct((2, NT, D_RWKV), BF16)
    tall_spec = pl.BlockSpec((2, TM, HC), lambda i: (0, i, 0))
    rows8 = TM // 8
    params = (mu, w0, a0, lora2, key_k, key_a, r_k, ones512, cum2, hm, bm, masks, eye)
    return pl.pallas_call(
        _prep_kernel,
        grid=(N_TILES,),
        in_specs=[pl.BlockSpec((TM, SHIFT_COLS), lambda i: (i, 0)),
                  pl.BlockSpec((8, SHIFT_COLS), lambda i: (_prev_tile(i) * rows8 + rows8 - 1, 0)),
                  pl.BlockSpec((8, SHIFT_COLS), lambda i: (_next_tile(i) * rows8, 0))]
                 + [full(p) for p in params],
        out_specs=[out_spec] * 5 + [tall_spec] * 4 + [
            pl.BlockSpec((2, NCH, D_RWKV), lambda i: (0, i, 0)), out_spec],
        out_shape=[bf_shape] * 5 + [jax.ShapeDtypeStruct((2, NT, HC), BF16)] * 4 + [
            jax.ShapeDtypeStruct((2, N_TILES * NCH, D_RWKV), F32),
            jax.ShapeDtypeStruct((2, NT, D_RWKV), F32)],
        compiler_params=_cp(("parallel",)),
        name="rwkv_prep",
    )(zs, zs, zs, *params)


SCAN_TILES = 4
CTX_STEPS = N_CTX_TILES // SCAN_TILES
LAT_STEPS = N_LAT_TILES // SCAN_TILES
N_SCAN_STEPS = CTX_STEPS + LAT_STEPS
N_SLOTS = 2 * SCAN_TILES


def _scan_block(step, d):
    jl = step - CTX_STEPS
    j = jnp.where(d == 0, jl, LAT_STEPS - 1 - jl)
    return jnp.where(step >= CTX_STEPS, CTX_STEPS + j, step)


def _scan_state_block(step):
    return jnp.minimum(step, CTX_STEPS - 1)


def _chunk_kernel(*refs):
    (atf, rtf, bhf, khf, vf, tf, aakf, arbf, arkf, gcf, atb, rtb, bhb, khb, vb, tb, aakb, arbb, arkb, gcb,
     s0_ref, hm_ref, bd_ref, yf_ref, yb_ref, sout_ref, mw) = refs
    step = pl.program_id(0)
    in_refs = ((atf, rtf, bhf, khf, vf, tf, aakf, arbf, arkf, gcf),
               (atb, rtb, bhb, khb, vb, tb, aakb, arbb, arkb, gcb))
    y_refs = (yf_ref, yb_ref)
    bd = bd_ref[...]
    hm = hm_ref[...]
    crow = lax.broadcasted_iota(jnp.int32, (NCH, 1), 0)
    heads = range(N_HEADS)

    def advance(chains):
        rows_of = lambda tile, cc: pl.ds(pl.multiple_of(tile * TM + cc * CH, CH), CH)
        chains = [(dl, slot, tile, cc, rows_of(tile, cc)) for dl, slot, tile, cc in chains]
        ld = lambda idx, dl, rows: in_refs[dl][idx][rows, :]
        m0s = [mw[slot] for _, slot, _, _, _ in chains]
        xy0s = [_dot(jnp.concatenate([ld(0, dl, rows), ld(1, dl, rows)], axis=0),
                     _tile_rows(m0.astype(BF16)) * bd)
                for (dl, _, _, _, rows), m0 in zip(chains, m0s)]
        vvs = [ld(4, dl, rows) for dl, _, _, _, rows in chains]
        avs = [_dot(jnp.concatenate([ld(6, dl, rows), ld(8, dl, rows)], axis=0), _tile_rows(vv) * hm)
               for (dl, _, _, _, rows), vv in zip(chains, vvs)]
        ubs = [_dot(ld(5, dl, rows), _tile_rows((xy0[0:CH] + av[0:CH]).astype(BF16)) * hm).astype(BF16)
               for (dl, _, _, _, rows), xy0, av in zip(chains, xy0s, avs)]
        for (dl, _, _, _, rows), xy0, av, ub in zip(chains, xy0s, avs, ubs):
            y_refs[dl][rows, :] = xy0[CH:] + av[CH:] + _dot(ld(7, dl, rows), _tile_rows(ub) * hm)
        for (dl, slot, tile, cc, rows), m0, ub, vv in zip(chains, m0s, ubs, vvs):
            gc_tile = in_refs[dl][9][pl.ds(pl.multiple_of(tile * NCH, NCH), NCH), :]
            gcrow = jnp.sum(jnp.where(crow == cc, gc_tile, 0.0), axis=0, keepdims=True)
            pad = jnp.zeros((LANES - 2 * CH - SUB, D_RWKV), F32)
            stack = jnp.concatenate([ld(2, dl, rows).astype(F32), ld(3, dl, rows).astype(F32),
                                     jnp.broadcast_to(gcrow, (SUB, D_RWKV)), pad], axis=0)
            stack_t = stack.T
            head = lambda h: stack_t[h * HEAD:(h + 1) * HEAD, :]
            bk_t = jnp.concatenate([head(h)[:, 0:2 * CH] for h in heads], axis=1)
            decay = jnp.concatenate([jnp.broadcast_to(head(h)[:, 2 * CH:2 * CH + 1], (HEAD, HEAD))
                                     for h in heads], axis=1)
            uv = _tile_rows(jnp.concatenate([ub, vv], axis=0)) * bd
            mw[slot] = m0 * decay + _dot(bk_t.astype(BF16), uv)

    @pl.when(step < CTX_STEPS)
    def _context():
        mw[...] = jnp.zeros(mw.shape, F32)

        def body(q, carry):
            advance([(dl, dl * SCAN_TILES + sl, jnp.int32(sl), q if dl == 0 else NCH - 1 - q)
                     for dl in range(2) for sl in range(SCAN_TILES)])
            return carry

        lax.fori_loop(0, NCH, body, 0)
        for dl in range(2):
            for sl in range(SCAN_TILES):
                mt = mw[dl * SCAN_TILES + sl]
                by_hv = jnp.concatenate([mt, jnp.zeros_like(mt)], axis=0).T
                for h in range(N_HEADS):
                    sout_ref[sl, dl, h] = by_hv[h * HEAD:(h + 1) * HEAD, 0:HEAD]

    @pl.when(step == CTX_STEPS)
    def _load_state():
        for dl in range(2):
            for sl in range(N_LAT_SEQ):
                mw[dl * SCAN_TILES + sl] = s0_ref[dl, sl]

    @pl.when(step >= CTX_STEPS)
    def _latent():
        n_q = (SCAN_TILES // N_LAT_SEQ) * NCH

        def body(q, carry):
            chains = []
            for dl in range(2):
                qq = q if dl == 0 else n_q - 1 - q
                for sl in range(N_LAT_SEQ):
                    chains.append((dl, dl * SCAN_TILES + sl, (qq // NCH) * N_LAT_SEQ + sl, qq % NCH))
            advance(chains)
            return carry

        lax.fori_loop(0, n_q, body, 0)


def _scan(prep_out, s0_lat, consts):
    rows_per_step = SCAN_TILES * TM

    def row_spec(d, width):
        return pl.BlockSpec((None, rows_per_step, width), lambda s: (d, _scan_block(s, d), 0))

    def gc_spec(d):
        return pl.BlockSpec((None, SCAN_TILES * NCH, D_RWKV), lambda s: (d, _scan_block(s, d), 0))

    def y_spec(d):
        return pl.BlockSpec((rows_per_step, D_RWKV), lambda s: (_scan_block(s, d), 0))

    rows, gc = list(prep_out[:9]), prep_out[9]
    row_specs = lambda d: [row_spec(d, D_RWKV)] * 5 + [row_spec(d, HC)] * 4 + [gc_spec(d)]
    const = lambda a: pl.BlockSpec(a.shape, lambda s: (0,) * a.ndim)
    y_shape = jax.ShapeDtypeStruct((NT, D_RWKV), F32)
    return pl.pallas_call(
        _chunk_kernel,
        grid=(N_SCAN_STEPS,),
        in_specs=row_specs(0) + row_specs(1) + [const(s0_lat)] + [const(a) for a in consts],
        out_specs=[y_spec(0), y_spec(1),
                   pl.BlockSpec((SCAN_TILES, 2, N_HEADS, HEAD, HEAD),
                                lambda s: (_scan_state_block(s), 0, 0, 0, 0))],
        out_shape=[y_shape, y_shape,
                   jax.ShapeDtypeStruct((N_CTX_SEQ, 2, N_HEADS, HEAD, HEAD), F32)],
        scratch_shapes=[pltpu.VMEM((N_SLOTS, HEAD, D_RWKV), F32)],
        compiler_params=_cp(("arbitrary",)),
        name="rwkv_scan",
    )(*rows, gc, *rows, gc, s0_lat, *consts)


HALO = 16


def _glu(z):
    return z[:, 0:D_CONV] * _sigmoid(z[:, D_CONV:])


def _conv_kernel(cur_ref, prev_ref, next_ref, w_ref, b_ref, g_ref, beta_ref, o_ref, ext):
    i = pl.program_id(0)
    first, last = _first_last(i)
    n_ext = TM + 2 * HALO
    u = jnp.concatenate([_glu(prev_ref[...]) * (1.0 - first.astype(F32)), _glu(cur_ref[...]),
                         _glu(next_ref[...]) * (1.0 - last.astype(F32))], axis=0)
    ext[0] = u
    for b in range(1, SUB):
        ext[b] = pltpu.roll(u, n_ext - b, axis=0)
    acc = jnp.zeros((TM, D_CONV), F32)
    for j in range(CONV_W):
        off = HALO - CONV_PAD + j
        acc = acc + ext[off % SUB, off - off % SUB:off - off % SUB + TM, :] * w_ref[j:j + 1, :]
    h = acc + b_ref[...]
    mu = jnp.mean(h, axis=-1, keepdims=True)
    hc = h - mu
    var = jnp.mean(hc * hc, axis=-1, keepdims=True)
    y = hc * lax.rsqrt(var + LN_EPS) * g_ref[...] + beta_ref[...]
    o_ref[...] = _silu(y)


def _conv(zc, conv_w, conv_b, ln_g, ln_b):
    nh = TM // HALO
    vec = pl.BlockSpec((1, D_CONV), lambda i: (0, 0))
    return pl.pallas_call(
        _conv_kernel,
        grid=(N_TILES,),
        in_specs=[pl.BlockSpec((TM, 2 * D_CONV), lambda i: (i, 0)),
                  pl.BlockSpec((HALO, 2 * D_CONV), lambda i: (_prev_tile(i) * nh + nh - 1, 0)),
                  pl.BlockSpec((HALO, 2 * D_CONV), lambda i: (_next_tile(i) * nh, 0)),
                  pl.BlockSpec((CONV_W + 1, D_CONV), lambda i: (0, 0)), vec, vec, vec],
        out_specs=pl.BlockSpec((TM, D_CONV), lambda i: (i, 0)),
        out_shape=jax.ShapeDtypeStruct((NT, D_CONV), F32),
        scratch_shapes=[pltpu.VMEM((SUB, TM + 2 * HALO, D_CONV), F32)],
        compiler_params=_cp(("parallel",)),
        name="conv_module",
    )(zc, zc, zc, conv_w, conv_b, ln_g, ln_b)


def _outproj_kernel(xp_ref, xs_ref, pe_ref, mod_ref, yf_ref, yb_ref, bf_ref, bb_ref, zg_ref, yc_ref,
                    gng_ref, gnb_ref, gl_ref, wo_ref, n2_ref, rw_ref, rb_ref, ones_ref,
                    x1_ref, h2_ref, logit_ref):
    i = pl.program_id(0)
    x = _load_x(i, xp_ref, xs_ref, pe_ref)
    g1 = mod_ref[:, 2 * D:3 * D]
    sh2 = mod_ref[:, 3 * D:4 * D]
    sc2 = mod_ref[:, 4 * D:5 * D]
    ones = ones_ref[...]
    y = (yf_ref[...] + bf_ref[...]) + (yb_ref[...] + bb_ref[...])
    mu = _dot_sel(y, ones) * (1.0 / HEAD)
    yc = y - mu
    var = _dot_sel(yc * yc, ones) * (1.0 / HEAD)
    yn = yc * lax.rsqrt(var + GN_EPS) * gng_ref[...] + gnb_ref[...]
    gate = _dot(_sigmoid(zg_ref[...]).astype(BF16), gl_ref[...])
    y_rwkv = (yn * gate).astype(BF16)
    mix = _dot(y_rwkv, wo_ref[0:D_RWKV, :]) + _dot(yc_ref[...].astype(BF16), wo_ref[D_RWKV:, :])
    x1 = x + g1 * mix
    x1_ref[...] = x1
    h2 = _rms(x1, n2_ref[...]) * (1.0 + sc2) + sh2
    h2_ref[...] = h2.astype(BF16)

    logit_ref[...] = _dot_hp(h2, rw_ref[...]) + rb_ref[...]


def _route(logits):
    lane = lax.broadcasted_iota(jnp.int32, logits.shape, 1)
    lanef = lane.astype(F32)
    neg = jnp.float32(-1e30)
    big = jnp.float32(1e9)
    gmask = lane < N_GROUPS
    gl = jnp.where(gmask, logits, neg)
    ge = jnp.where(gmask, jnp.exp(gl - jnp.max(gl, axis=-1, keepdims=True)), 0.0)
    gprob = ge / jnp.sum(ge, axis=-1, keepdims=True)
    gp = jnp.max(gprob, axis=-1, keepdims=True)
    gidx = jnp.min(jnp.where(jnp.logical_and(gmask, gprob == gp), lanef, big), axis=-1, keepdims=True)
    egrp = jnp.floor((lanef - float(E_LANE0)) * (1.0 / N_EXP_PER_GROUP))
    emask = jnp.logical_and(jnp.logical_and(lane >= E_LANE0, lane < E_LANE0 + N_EXPERTS), egrp == gidx)
    el = jnp.where(emask, logits, neg)
    ee = jnp.where(emask, jnp.exp(el - jnp.max(el, axis=-1, keepdims=True)), 0.0)
    ep = ee / jnp.sum(ee, axis=-1, keepdims=True)
    m1 = jnp.max(jnp.where(emask, ep, -1.0), axis=-1, keepdims=True)
    i1 = jnp.min(jnp.where(jnp.logical_and(emask, ep == m1), lanef, big), axis=-1, keepdims=True)
    mask2 = jnp.logical_and(emask, lanef != i1)
    m2 = jnp.max(jnp.where(mask2, ep, -1.0), axis=-1, keepdims=True)
    i2 = jnp.min(jnp.where(jnp.logical_and(mask2, ep == m2), lanef, big), axis=-1, keepdims=True)
    den = m1 + m2
    return (jnp.where(lanef == i1, gp * (m1 / den), 0.0)
            + jnp.where(lanef == i2, gp * (m2 / den), 0.0)
            + jnp.where(lane == GROUP_LANE, gidx, 0.0))


def _outproj(xp, xs, pe, mod3, yf, yb, bonus, zg, yconv, gn_g, gn_b, gate_bf, w_out_bf, norm2_g,
             router_w, router_b, ones512):
    tile = lambda n: pl.BlockSpec((TM, n), lambda i: (i, 0))
    const = lambda shape: pl.BlockSpec(shape, lambda i: (0,) * len(shape))
    return pl.pallas_call(
        _outproj_kernel,
        grid=(N_TILES,),
        in_specs=_x_specs() + [
            pl.BlockSpec((None, 1, N_MOD * D), lambda i: (_mod_row(i), 0, 0)),
            tile(D_RWKV), tile(D_RWKV),
            pl.BlockSpec((None, TM, D_RWKV), lambda i: (0, i, 0)),
            pl.BlockSpec((None, TM, D_RWKV), lambda i: (1, i, 0)),
            tile(LORA_G), tile(D_CONV),
            const((1, D_RWKV)), const((1, D_RWKV)), const((LORA_G, D_RWKV)), const((D, D)),
            const((1, D)), const((D, ROUTE_LANES)), const((1, ROUTE_LANES)),
            const((D_RWKV, D_RWKV))],
        out_specs=[tile(D), tile(D), tile(ROUTE_LANES)],
        out_shape=[jax.ShapeDtypeStruct((NT, D), F32), jax.ShapeDtypeStruct((NT, D), BF16),
                   jax.ShapeDtypeStruct((NT, ROUTE_LANES), F32)],
        compiler_params=_cp(("parallel",)),
        name="outproj_router",
    )(xp, xs, pe, mod3, yf, yb, bonus, bonus, zg, yconv, gn_g, gn_b, gate_bf, w_out_bf, norm2_g,
      router_w, router_b, ones512)


def _plan_kernel(logit_ref, triu_ref, comb_ref, drow_ref, dcol_ref, start_ref, ntile_ref):
    comb = _route(logit_ref[...])
    comb_ref[...] = comb
    gidx = comb.T[GROUP_LANE:GROUP_LANE + 1, :]
    grow = lax.broadcasted_iota(jnp.int32, (SUB, 1), 0)
    growf = grow.astype(F32)
    onehot = jnp.where(jnp.logical_and(gidx == growf, grow < N_GROUPS), 1.0, 0.0)
    before = _dot(onehot.astype(BF16), triu_ref[...])
    count = jnp.sum(onehot, axis=-1, keepdims=True)
    padded = jnp.floor((count + (SEG - 1.0)) * (1.0 / SEG)) * SEG
    start = jnp.zeros((SUB, 1), F32)
    for g in range(N_GROUPS - 1):
        start = start + jnp.where(grow > g, padded[g:g + 1, :], 0.0)
    dest = jnp.sum(onehot * (start + before), axis=0, keepdims=True)
    drow_ref[...] = jnp.broadcast_to(dest, (SUB, TM_MOE))
    dcol_ref[...] = jnp.broadcast_to(dest, (LANES, TM_MOE)).T
    start_ref[...] = jnp.broadcast_to(start * (1.0 / SEG), (SUB, LANES)).astype(jnp.int32)
    ntile_ref[...] = jnp.broadcast_to(padded * (1.0 / SEG), (SUB, LANES)).astype(jnp.int32)


def _plan(logits, triu):
    seg_shape = jax.ShapeDtypeStruct((N_MOE_TILES, SUB, LANES), jnp.int32)
    seg_spec = pl.BlockSpec((None, SUB, LANES), lambda t: (t, 0, 0))
    tok_spec = pl.BlockSpec((TM_MOE, LANES), lambda t: (t, 0))
    return pl.pallas_call(
        _plan_kernel,
        grid=(N_MOE_TILES,),
        in_specs=[tok_spec, pl.BlockSpec((TM_MOE, TM_MOE), lambda t: (0, 0))],
        out_specs=[tok_spec, pl.BlockSpec((None, SUB, TM_MOE), lambda t: (t, 0, 0)), tok_spec,
                   seg_spec, seg_spec],
        out_shape=[jax.ShapeDtypeStruct((NT, ROUTE_LANES), F32), jax.ShapeDtypeStruct((N_MOE_TILES, SUB, TM_MOE), F32),
                   jax.ShapeDtypeStruct((NT, LANES), F32), seg_shape, seg_shape],
        compiler_params=_cp(("parallel",)),
        name="moe_plan",
    )(logits, triu)


def _moe_kernel(start_ref, ntile_ref, h_ref, comb_ref, drow_ref, dcol_ref, wg_ref, wu_ref, wd_ref,
                o_ref, xs, cws, ys):
    q = pl.program_id(1)
    g = q // (N_EXP_PER_GROUP // EXP_PER_STEP)
    tiles = range(TILES_PER_PASS)
    toks = [slice(j * TM_MOE, (j + 1) * TM_MOE) for j in tiles]

    @pl.when(q == 0)
    def _sort_in():
        slot = lax.broadcasted_iota(jnp.int32, (MOE_ROWS, TM_MOE), 0).astype(F32)
        for j in tiles:
            perm = jnp.where(slot == drow_ref[j, 0:1, :], 1.0, 0.0).astype(BF16)
            xs[j] = _dot(perm, h_ref[toks[j], :]).astype(BF16)
            ch, cl = _split2(comb_ref[toks[j], :])
            cws[j] = _dot(perm, ch) + _dot(perm, cl)
            ys[j] = jnp.zeros((MOE_ROWS, D), F32)

    segs = [(pl.program_id(0) * TILES_PER_PASS + j) * N_GROUPS + g for j in tiles]
    firsts = [start_ref[seg] for seg in segs]

    def visit(windows, n_rows):
        rows = [(j, pl.ds(pl.multiple_of(row0 * SEG, SEG), n_rows)) for j, row0 in windows]
        x = jnp.concatenate([xs[j, r, :] for j, r in rows], axis=0)
        cw_all = jnp.concatenate([cws[j, r, :] for j, r in rows], axis=0)
        lane = lax.broadcasted_iota(jnp.int32, cw_all.shape, 1)
        acc = jnp.concatenate([ys[j, r, :] for j, r in rows], axis=0)
        for k in range(EXP_PER_STEP):
            e_lane = q * EXP_PER_STEP + k + E_LANE0
            cw = jnp.sum(jnp.where(lane == e_lane, cw_all, 0.0), axis=-1, keepdims=True)
            hid = (_silu(_dot(x, wg_ref[k].astype(BF16))) * _dot(x, wu_ref[k].astype(BF16)) * cw).astype(BF16)
            acc = acc + _dot(hid, wd_ref[k].astype(BF16))
        for i, (j, r) in enumerate(rows):
            ys[j, r, :] = acc[i * n_rows:(i + 1) * n_rows]

    visit([(j, jnp.minimum(firsts[j], MOE_ROWS // SEG - MOE_WINDOW)) for j in tiles], MOE_WINDOW * SEG)

    for j in tiles:
        def sub_tile(i, carry, j=j):
            visit([(j, firsts[j] + i)], SEG)
            return carry

        lax.fori_loop(MOE_WINDOW, ntile_ref[segs[j]], sub_tile, 0)

    @pl.when(q == N_EXPERTS // EXP_PER_STEP - 1)
    def _sort_out():
        slot = lax.broadcasted_iota(jnp.int32, (TM_MOE, MOE_ROWS), 1).astype(F32)
        for j in tiles:
            perm_t = jnp.where(slot == dcol_ref[toks[j], 0:1], 1.0, 0.0).astype(BF16)
            o_ref[toks[j], :] = _dot(perm_t, ys[j].astype(BF16)).astype(BF16)


def _moe(h2, comb, drow, dcol, seg_start, seg_ntile, wg, wu, wd):
    rows = TILES_PER_PASS * TM_MOE
    once = dict(pipeline_mode=pl.Buffered(1))
    weights = lambda shape: pl.BlockSpec(shape, lambda p, q, s, n: (q, 0, 0))
    grid_spec = pltpu.PrefetchScalarGridSpec(
        num_scalar_prefetch=2,
        grid=(N_MOE_TILES // TILES_PER_PASS, N_EXPERTS // EXP_PER_STEP),
        in_specs=[pl.BlockSpec((rows, D), lambda p, q, s, n: (p, 0), **once),
                  pl.BlockSpec((rows, ROUTE_LANES), lambda p, q, s, n: (p, 0), **once),
                  pl.BlockSpec((TILES_PER_PASS, SUB, TM_MOE), lambda p, q, s, n: (p, 0, 0), **once),
                  pl.BlockSpec((rows, LANES), lambda p, q, s, n: (p, 0), **once),
                  weights((EXP_PER_STEP, D, D_EXPERT)), weights((EXP_PER_STEP, D, D_EXPERT)),
                  weights((EXP_PER_STEP, D_EXPERT, D))],
        out_specs=pl.BlockSpec((rows, D), lambda p, q, s, n: (p, 0)),
        scratch_shapes=[pltpu.VMEM((TILES_PER_PASS, MOE_ROWS, D), BF16),
                        pltpu.VMEM((TILES_PER_PASS, MOE_ROWS, ROUTE_LANES), F32),
                        pltpu.VMEM((TILES_PER_PASS, MOE_ROWS, D), F32)])
    return pl.pallas_call(
        _moe_kernel,
        grid_spec=grid_spec,
        out_shape=jax.ShapeDtypeStruct((NT, D), BF16),
        compiler_params=_cp(("parallel", "arbitrary")),
        name="moe_experts",
    )(seg_start, seg_ntile, h2, comb, drow, dcol, wg, wu, wd)


def _final_kernel(x1_ref, moe_ref, mod_ref, g_ref, o_ref):
    g2 = mod_ref[:, 5 * D:6 * D]
    x2 = x1_ref[...] + g2 * moe_ref[...].astype(F32)
    o_ref[...] = _rms(x2, g_ref[...])


def _final(x1, moe, mod3, final_g, tile0, n_tiles, out_block, name, merge=1):
    rows = merge * TM
    tile = pl.BlockSpec((rows, D), lambda i: (tile0 // merge + i, 0))
    return pl.pallas_call(
        _final_kernel,
        grid=(n_tiles // merge,),
        in_specs=[tile, tile,
                  pl.BlockSpec((None, 1, N_MOD * D), lambda i: (_mod_row(tile0 + i * merge), 0, 0)),
                  pl.BlockSpec((1, D), lambda i: (0, 0))],
        out_specs=pl.BlockSpec((rows, D), lambda i: (out_block(tile0 // merge + i), 0)),
        out_shape=jax.ShapeDtypeStruct((n_tiles * TM, D), F32),
        compiler_params=_cp(("parallel",)),
        name=name,
    )(x1, moe, mod3, final_g)


def _pos_embed(rows):
    t = np.arange(rows * GRID_W)
    row = (t // GRID_W).astype(np.float32)
    col = (t % GRID_W).astype(np.float32)
    quarter = D // 4
    freqs = (1.0 / (10000.0 ** (np.arange(quarter, dtype=np.float32) / quarter))).astype(np.float32)
    ang_r = row[:, None] * freqs[None, :]
    ang_c = col[:, None] * freqs[None, :]
    pe = np.concatenate([np.sin(ang_r), np.cos(ang_r), np.sin(ang_c), np.cos(ang_c)], axis=-1)
    return jnp.asarray(pe, F32)


def _selection_constants():
    ch = np.arange(D_RWKV)
    ones512 = (ch[:, None] // HEAD == ch[None, :] // HEAD).astype(np.float32)
    t = np.arange(TM)
    same_chunk = t[:, None] // CH == t[None, :] // CH
    cum_f = same_chunk & (t[None, :] <= t[:, None])
    cum_b = same_chunk & (t[None, :] >= t[:, None])
    sel8 = np.arange(NCH)[:, None] == t[None, :] // CH
    col = np.arange(HC)
    hm = col[:, None] // CH == ch[None, :] // HEAD
    bm = col[:, None] // CH == col[None, :] // CH
    tt, jj = np.arange(CH)[:, None], col[None, :] % CH
    masks = np.stack([jj < tt, jj <= tt, jj > tt, jj >= tt])
    eye = jj == tt
    bf = lambda x: jnp.asarray(x, BF16)
    f32 = lambda x: jnp.asarray(x, F32)
    pad = np.zeros((2 * SUB - NCH, TM), bool)
    sums = [np.concatenate([cum, same_chunk, sel8, pad]) for cum in (cum_f, cum_b)]
    prep_consts = (bf(ones512), bf(np.stack(sums)), bf(hm), bf(bm),
                   f32(masks.reshape(2, 2, CH, HC)), f32(eye))
    scan_consts = (bf(hm), bf(ones512))
    return prep_consts, scan_consts


def kernel(x_prompt, x_sample, state_rwkv, c, c_ctx, ada_w, ada_b, norm1_g, w_in, tshift_mu, decay_w0, decay_lora_b, iclr_a0, iclr_lora_b, key_k, key_a, bonus_r_k, gate_lora_b, gn_g, gn_b, conv_dw_w, conv_dw_b, conv_ln_g, conv_ln_b, w_out, norm2_g, router_group_w, router_group_b, router_expert_w, router_expert_b, expert_w_gate, expert_w_up, expert_w_down, final_norm_g):
    assert x_prompt.shape == (N_CTX_SEQ, T_CTX, D) and x_sample.shape == (N_LAT_SEQ, T_LAT, D)
    assert ada_w.shape[0] == 1, "one trunk layer"
    prep_consts, scan_consts = _selection_constants()
    ones512 = prep_consts[0]
    xp = x_prompt.reshape(N_CTX_SEQ * T_CTX, D)
    xs = x_sample.reshape(N_LAT_SEQ * T_LAT, D)
    pe = _pos_embed(T_LAT // GRID_W)

    cond8 = jnp.concatenate([c_ctx[None, :], c, jnp.zeros((8 - 1 - N_LAT_SEQ, D), F32)], axis=0)
    mod3 = _adaln(cond8, ada_w[0], ada_b[0][None, :]).reshape(8, 1, N_MOD * D)

    zs, zg, zc = _inproj(xp, xs, pe, mod3, norm1_g, w_in[0].astype(BF16))

    zero = jnp.zeros((2, LORA, D_RWKV), F32)
    lora2 = jnp.concatenate([jnp.concatenate([decay_lora_b[0], zero], axis=2),
                             jnp.concatenate([zero, iclr_lora_b[0]], axis=2)], axis=1)
    vec = lambda p: p.reshape(2, 1, -1)
    prep_out = _prep(zs, vec(tshift_mu[0]), vec(decay_w0[0]), vec(iclr_a0[0]), lora2,
                     vec(key_k[0]), vec(key_a[0]), vec(bonus_r_k[0]), *prep_consts)
    bonus = prep_out[10]

    s0_lat = state_rwkv[:, 0].transpose(1, 0, 4, 2, 3).reshape(2, N_LAT_SEQ, HEAD, D_RWKV)
    yf, yb, s_fin = _scan(prep_out, s0_lat, scan_consts)

    yconv = _conv(zc, jnp.concatenate([conv_dw_w[0], jnp.zeros((1, D_CONV), F32)], axis=0),
                  conv_dw_b, conv_ln_g, conv_ln_b)

    router_w = jnp.concatenate([router_group_w[0], router_expert_w[0],
                                jnp.zeros((D, ROUTE_LANES - N_GROUPS - N_EXPERTS), F32)], axis=1)
    router_b = jnp.concatenate([router_group_b[0], router_expert_b[0],
                                jnp.zeros((ROUTE_LANES - N_GROUPS - N_EXPERTS,), F32)])[None, :]
    x1, h2, logits = _outproj(xp, xs, pe, mod3, yf, yb, bonus, zg, yconv,
                              gn_g, gn_b, gate_lora_b[0].astype(BF16), w_out[0].astype(BF16), norm2_g,
                              router_w, router_b, ones512)

    tok = np.arange(TM_MOE)
    triu = jnp.asarray(tok[:, None] < tok[None, :], BF16)
    comb, drow, dcol, seg_start, seg_ntile = _plan(logits, triu)
    seg_start = seg_start[:, :N_GROUPS, 0].reshape(-1)
    seg_ntile = seg_ntile[:, :N_GROUPS, 0].reshape(-1)
    moe = _moe(h2, comb, drow, dcol, seg_start, seg_ntile,
               expert_w_gate[0], expert_w_up[0], expert_w_down[0])
    fg = final_norm_g[None, :]
    y_prompt = _final(x1, moe, mod3, fg, 0, N_CTX_TILES, lambda i: i, "final_norm_ctx", merge=2)
    y_sample = _final(x1, moe, mod3, fg, N_CTX_TILES, N_LAT_TILES, _xs_block, "final_norm_lat")
    y_prompt = y_prompt.reshape(N_CTX_SEQ, T_CTX, D)
    y_sample = y_sample.reshape(N_LAT_SEQ, T_LAT, D)
    new_state = s_fin[:, None].astype(state_rwkv.dtype)
    return (y_prompt, y_sample, new_state)
```

```python
import functools

import numpy as np
import jax
import jax.numpy as jnp
from jax import lax
from jax.experimental import pallas as pl
from jax.experimental.pallas import tpu as pltpu

F32 = jnp.float32
BF16 = jnp.bfloat16

D = 1024
N_CTX_SEQ = 16
T_CTX = 256
N_LAT_SEQ = 2
T_LAT = 1024
TM = 256
N_CTX_TILES = N_CTX_SEQ * T_CTX // TM
LAT_CHUNKS = T_LAT // TM
N_LAT_TILES = N_LAT_SEQ * LAT_CHUNKS
N_TILES = N_CTX_TILES + N_LAT_TILES
NT = N_TILES * TM
GRID_W = 64
D_RWKV = 512
D_CONV = 512
HEAD = 64
N_HEADS = 8
CONV_W = 31
CONV_PAD = CONV_W // 2
LORA = 64
LORA_G = 128
SHIFT_COLS = 3 * D_RWKV + 2 * LORA
N_GROUPS = 4
N_EXP_PER_GROUP = 8
N_EXPERTS = 32
D_EXPERT = 256
N_MOD = 6
RMS_EPS = 1e-6
LN_EPS = 1e-5
GN_EPS = 64e-5
LANES = 128
SUB = 8
CH = 32
NCH = TM // CH
HC = N_HEADS * CH
DOUBLINGS = 4
ROUTE_LANES = 128
E_LANE0 = N_GROUPS
GROUP_LANE = 36
TM_MOE = 1024
SEG = 64
EXP_PER_STEP = 2
TILES_PER_PASS = 2
MOE_ROWS = TM_MOE + N_GROUPS * SEG
MOE_SHARED_SIZES = (2, 4, 6)
MOE_SIZES = (2, 4, 6, 8, 12, MOE_ROWS // SEG)
N_MOE_TILES = NT // TM_MOE
VMEM_LIMIT = 56 * 1024 * 1024


def _cp(sem, flags=None):
    return pltpu.CompilerParams(dimension_semantics=sem, vmem_limit_bytes=VMEM_LIMIT, flags=flags)


def _split2(a):
    hi = a.astype(BF16)
    lo = (a - hi.astype(F32)).astype(BF16)
    return hi, lo


def _dot(a, b):
    return jnp.dot(a, b, preferred_element_type=F32)


def _dot_hp(a, b):
    ah, al = _split2(a)
    bh, bl = _split2(b)
    return _dot(ah, bh) + _dot(ah, bl) + _dot(al, bh)


def _dot_sel(a, sel):
    h, l = _split2(a)
    return _dot(h, sel) + _dot(l, sel)


def _sel_dot(sel, a):
    h, l = _split2(a)
    return _dot(sel, h) + _dot(sel, l)


def _sigmoid(x):
    return 1.0 / (1.0 + jnp.exp(-x))


def _silu(x):
    return x * _sigmoid(x)


def _lat_js(i):
    il = jnp.maximum(i - N_CTX_TILES, 0)
    return il // N_LAT_SEQ, il % N_LAT_SEQ


def _xp_block(i):
    return jnp.minimum(i, N_CTX_TILES - 1)


def _xs_block(i):
    j, s = _lat_js(i)
    return s * LAT_CHUNKS + j


def _pe_block(i):
    j, _ = _lat_js(i)
    return j


def _mod_row(i):
    _, s = _lat_js(i)
    return jnp.where(i < N_CTX_TILES, 0, 1 + s)


def _first_last(i):
    j, _ = _lat_js(i)
    is_ctx = i < N_CTX_TILES
    first = jnp.logical_or(is_ctx, j == 0)
    last = jnp.logical_or(is_ctx, j == LAT_CHUNKS - 1)
    return first, last


def _prev_tile(i):
    first, _ = _first_last(i)
    return jnp.where(first, i, i - N_LAT_SEQ)


def _next_tile(i):
    _, last = _first_last(i)
    return jnp.where(last, i, i + N_LAT_SEQ)


def _adaln_kernel(c_ref, w_ref, b_ref, o_ref):
    c = c_ref[...]
    o_ref[...] = _dot_hp(_silu(c), w_ref[...]) + b_ref[...]


def _adaln(cond8, ada_w, ada_b):
    tn = 1536
    n = ada_w.shape[1]
    return pl.pallas_call(
        _adaln_kernel,
        grid=(n // tn,),
        in_specs=[pl.BlockSpec((8, D), lambda j: (0, 0)),
                  pl.BlockSpec((D, tn), lambda j: (0, j)),
                  pl.BlockSpec((1, tn), lambda j: (0, j))],
        out_specs=pl.BlockSpec((8, tn), lambda j: (0, j)),
        out_shape=jax.ShapeDtypeStruct((8, n), F32),
        compiler_params=_cp(("parallel",)),
        name="adaln",
    )(cond8, ada_w, ada_b)


def _load_x(i, xp_ref, xs_ref, pe_ref):
    f = (i >= N_CTX_TILES).astype(F32)
    return xp_ref[...] * (1.0 - f) + (xs_ref[...] + pe_ref[...]) * f


def _x_specs():
    return [pl.BlockSpec((TM, D), lambda i: (_xp_block(i), 0)),
            pl.BlockSpec((TM, D), lambda i: (_xs_block(i), 0)),
            pl.BlockSpec((TM, D), lambda i: (_pe_block(i), 0))]


def _rms(x, g):
    return x * lax.rsqrt(jnp.mean(x * x, axis=-1, keepdims=True) + RMS_EPS) * g


def _inproj_kernel(xp_ref, xs_ref, pe_ref, mod_ref, g_ref, w_ref, zs_ref, zg_ref, zc_ref):
    i = pl.program_id(0)
    x = _load_x(i, xp_ref, xs_ref, pe_ref)
    sh1 = mod_ref[:, 0:D]
    sc1 = mod_ref[:, D:2 * D]
    h = (_rms(x, g_ref[...]) * (1.0 + sc1) + sh1).astype(BF16)
    zs_ref[...] = _dot(h, w_ref[:, 0:SHIFT_COLS])
    zg_ref[...] = _dot(h, w_ref[:, SHIFT_COLS:SHIFT_COLS + LORA_G])
    zc_ref[...] = _dot(h, w_ref[:, SHIFT_COLS + LORA_G:])


def _inproj(xp, xs, pe, mod3, norm1_g, w_in_bf):
    in_cols = w_in_bf.shape[1]
    return pl.pallas_call(
        _inproj_kernel,
        grid=(N_TILES,),
        in_specs=_x_specs() + [
            pl.BlockSpec((None, 1, N_MOD * D), lambda i: (_mod_row(i), 0, 0)),
            pl.BlockSpec((1, D), lambda i: (0, 0)),
            pl.BlockSpec((D, in_cols), lambda i: (0, 0))],
        out_specs=[pl.BlockSpec((TM, SHIFT_COLS), lambda i: (i, 0)),
                   pl.BlockSpec((TM, LORA_G), lambda i: (i, 0)),
                   pl.BlockSpec((TM, 2 * D_CONV), lambda i: (i, 0))],
        out_shape=[jax.ShapeDtypeStruct((NT, SHIFT_COLS), F32),
                   jax.ShapeDtypeStruct((NT, LORA_G), F32),
                   jax.ShapeDtypeStruct((NT, 2 * D_CONV), F32)],
        compiler_params=_cp(("parallel",)),
        name="inproj",
    )(xp, xs, pe, mod3, norm1_g, w_in_bf)


NT_DIMS = (((1,), (1,)), ((), ()))


def _tile_rows(x):
    return jnp.concatenate([x] * N_HEADS, axis=0)


def _prep_kernel(zs_ref, prev_ref, next_ref, mu_ref, w0_ref, a0_ref, lora_ref, kk_ref_, ka_ref,
                 rk_ref, ones_ref, cum_ref, hm_ref, bm_ref, msk_ref, eye_ref,
                 at_out, rt_out, bh_out, kh_out, v_out, t_out, aak_out, arb_out, ark_out, gc_out, bon_out):
    i = pl.program_id(0)
    first, last = _first_last(i)
    cur = zs_ref[...]
    prow = prev_ref[7:8, :] * (1.0 - first.astype(F32))
    nrow = next_ref[0:1, :] * (1.0 - last.astype(F32))
    rows = lax.broadcasted_iota(jnp.int32, (TM, 1), 0)
    shifted = (jnp.where(rows == 0, prow, pltpu.roll(cur, 1, axis=0)),
               jnp.where(rows == TM - 1, nrow, pltpu.roll(cur, TM - 1, axis=0)))
    ones = ones_ref[...]
    lane = lax.broadcasted_iota(jnp.int32, (TM, 2 * LORA), 1)
    scaled = []
    for d in range(2):
        xs = cur + (shifted[d] - cur) * mu_ref[d]
        r = xs[:, 0:D_RWKV]
        k = xs[:, D_RWKV:2 * D_RWKV]
        v = xs[:, 2 * D_RWKV:3 * D_RWKV]
        z2 = xs[:, 3 * D_RWKV:SHIFT_COLS]
        lin = jnp.where(lane < LORA, jnp.tanh(z2), z2)
        lo = _dot_hp(lin, lora_ref[d])
        u = -(w0_ref[d] + lo[:, 0:D_RWKV])
        softplus = jnp.maximum(u, 0.0) + jnp.log1p(jnp.exp(-jnp.abs(u)))
        w_log = -softplus - 0.5
        lw = -jnp.exp(w_log)
        a = _sigmoid(a0_ref[d] + lo[:, D_RWKV:])
        kx = k * kk_ref_[d]
        nrm = jnp.sqrt(_dot_sel(kx * kx, ones))
        kk = kx / jnp.maximum(nrm, 1e-12)
        k2 = k * (1.0 + (a - 1.0) * ka_ref[d])
        bvec = kk * a
        bon_out[d] = _dot_sel(r * k2 * rk_ref[d], ones) * v

        sums = _sel_dot(cum_ref[d], lw)
        lg = sums[0:TM]
        tot = sums[TM:2 * TM]
        e_tail = jnp.exp(tot - lg)
        e_inv = jnp.exp(-lg)
        at = (-(kk * jnp.exp(lg - lw))).astype(BF16)
        bt = (bvec * e_inv).astype(BF16)
        kt = (k2 * e_inv).astype(BF16)
        rt = (r * jnp.exp(lg)).astype(BF16)
        at_out[d] = at
        rt_out[d] = rt
        bh_out[d] = (bvec * e_tail).astype(BF16)
        kh_out[d] = (k2 * e_tail).astype(BF16)
        v_out[d] = v.astype(BF16)
        gc_out[d] = jnp.exp(sums[2 * TM:2 * TM + NCH])
        scaled.append((at, bt, kt, rt))

    hm = hm_ref[...]
    bm = bm_ref[...]

    def expand(z):
        return _tile_rows(z.astype(BF16)) * bm

    chunks = [(d, slice(c * CH, (c + 1) * CH)) for d in range(2) for c in range(NCH)]
    pws = []
    for d, rows in chunks:
        at, bt, kt, rt = scaled[d]
        ar = jnp.concatenate([at[rows], rt[rows]], axis=0)
        bk = jnp.concatenate([_tile_rows(bt[rows]) * hm, _tile_rows(kt[rows]) * hm], axis=0)
        p1 = lax.dot_general(ar, bk, NT_DIMS, preferred_element_type=F32)
        m_strict, m_incl = msk_ref[d, 0], msk_ref[d, 1]
        pws.append(p1[0:CH, 0:HC] * m_strict)
        aak_out[d, rows, :] = (p1[0:CH, HC:] * m_strict).astype(BF16)
        arb_out[d, rows, :] = (p1[CH:, 0:HC] * m_incl).astype(BF16)
        ark_out[d, rows, :] = (p1[CH:, HC:] * m_incl).astype(BF16)
    tms = [eye_ref[...] + pw for pw in pws]
    pws = [_dot(pw.astype(BF16), expand(pw)) for pw in pws]
    for k in range(1, DOUBLINGS + 1):
        if k < DOUBLINGS:
            prods = [_dot(jnp.concatenate([pw, tm], axis=0).astype(BF16), expand(pw))
                     for pw, tm in zip(pws, tms)]
            pws = [p[0:CH] for p in prods]
            tms = [tm + p[CH:] for tm, p in zip(tms, prods)]
        else:
            tms = [tm + _dot(tm.astype(BF16), expand(pw)) for tm, pw in zip(tms, pws)]
    for (d, rows), tm in zip(chunks, tms):
        t_out[d, rows, :] = tm.astype(BF16)


def _prep(zs, mu, w0, a0, lora2, key_k, key_a, r_k, ones512, cum2, hm, bm, masks, eye):
    full = lambda a: pl.BlockSpec(a.shape, lambda i: (0,) * a.ndim)
    out_spec = pl.BlockSpec((2, TM, D_RWKV), lambda i: (0, i, 0))
    bf_shape = jax.ShapeDtypeStruct((2, NT, D_RWKV), BF16)
    tall_spec = pl.BlockSpec((2, TM, HC), lambda i: (0, i, 0))
    rows8 = TM // 8
    params = (mu, w0, a0, lora2, key_k, key_a, r_k, ones512, cum2, hm, bm, masks, eye)
    return pl.pallas_call(
        _prep_kernel,
        grid=(N_TILES,),
        in_specs=[pl.BlockSpec((TM, SHIFT_COLS), lambda i: (i, 0)),
                  pl.BlockSpec((8, SHIFT_COLS), lambda i: (_prev_tile(i) * rows8 + rows8 - 1, 0)),
                  pl.BlockSpec((8, SHIFT_COLS), lambda i: (_next_tile(i) * rows8, 0))]
                 + [full(p) for p in params],
        out_specs=[out_spec] * 5 + [tall_spec] * 4 + [
            pl.BlockSpec((2, NCH, D_RWKV), lambda i: (0, i, 0)), out_spec],
        out_shape=[bf_shape] * 5 + [jax.ShapeDtypeStruct((2, NT, HC), BF16)] * 4 + [
            jax.ShapeDtypeStruct((2, N_TILES * NCH, D_RWKV), F32),
            jax.ShapeDtypeStruct((2, NT, D_RWKV), F32)],
        compiler_params=_cp(("parallel",)),
        name="rwkv_prep",
    )(zs, zs, zs, *params)


SCAN_TILES = 4
CTX_STEPS = N_CTX_TILES // SCAN_TILES
LAT_STEPS = N_LAT_TILES // SCAN_TILES
N_SCAN_STEPS = CTX_STEPS + LAT_STEPS
N_SLOTS = 2 * SCAN_TILES


def _scan_block(step, d):
    jl = step - CTX_STEPS
    j = jnp.where(d == 0, jl, LAT_STEPS - 1 - jl)
    return jnp.where(step >= CTX_STEPS, CTX_STEPS + j, step)


def _scan_state_block(step):
    return jnp.minimum(step, CTX_STEPS - 1)


def _chunk_kernel(*refs):
    (atf, rtf, bhf, khf, vf, tf, aakf, arbf, arkf, gcf, atb, rtb, bhb, khb, vb, tb, aakb, arbb, arkb, gcb,
     s0_ref, hm_ref, bd_ref, yf_ref, yb_ref, sout_ref, mw) = refs
    step = pl.program_id(0)
    in_refs = ((atf, rtf, bhf, khf, vf, tf, aakf, arbf, arkf, gcf),
               (atb, rtb, bhb, khb, vb, tb, aakb, arbb, arkb, gcb))
    y_refs = (yf_ref, yb_ref)
    bd = bd_ref[...]
    hm = hm_ref[...]
    crow = lax.broadcasted_iota(jnp.int32, (NCH, 1), 0)
    heads = range(N_HEADS)

    def advance(chains):
        rows_of = lambda tile, cc: pl.ds(pl.multiple_of(tile * TM + cc * CH, CH), CH)
        chains = [(dl, slot, tile, cc, rows_of(tile, cc)) for dl, slot, tile, cc in chains]
        ld = lambda idx, dl, rows: in_refs[dl][idx][rows, :]
        m0s = [mw[slot] for _, slot, _, _, _ in chains]
        xy0s = [_dot(jnp.concatenate([ld(0, dl, rows), ld(1, dl, rows)], axis=0),
                     _tile_rows(m0.astype(BF16)) * bd)
                for (dl, _, _, _, rows), m0 in zip(chains, m0s)]
        vvs = [ld(4, dl, rows) for dl, _, _, _, rows in chains]
        avs = [_dot(jnp.concatenate([ld(6, dl, rows), ld(8, dl, rows)], axis=0), _tile_rows(vv) * hm)
               for (dl, _, _, _, rows), vv in zip(chains, vvs)]
        ubs = [_dot(ld(5, dl, rows), _tile_rows((xy0[0:CH] + av[0:CH]).astype(BF16)) * hm).astype(BF16)
               for (dl, _, _, _, rows), xy0, av in zip(chains, xy0s, avs)]
        for (dl, _, _, _, rows), xy0, av, ub in zip(chains, xy0s, avs, ubs):
            y_refs[dl][rows, :] = xy0[CH:] + av[CH:] + _dot(ld(7, dl, rows), _tile_rows(ub) * hm)
        for (dl, slot, tile, cc, rows), m0, ub, vv in zip(chains, m0s, ubs, vvs):
            gc_tile = in_refs[dl][9][pl.ds(pl.multiple_of(tile * NCH, NCH), NCH), :]
            gcrow = jnp.sum(jnp.where(crow == cc, gc_tile, 0.0), axis=0, keepdims=True)
            pad = jnp.zeros((LANES - 2 * CH - SUB, D_RWKV), F32)
            stack = jnp.concatenate([ld(2, dl, rows).astype(F32), ld(3, dl, rows).astype(F32),
                                     jnp.broadcast_to(gcrow, (SUB, D_RWKV)), pad], axis=0)
            stack_t = stack.T
            head = lambda h: stack_t[h * HEAD:(h + 1) * HEAD, :]
            bk_t = jnp.concatenate([head(h)[:, 0:2 * CH] for h in heads], axis=1)
            decay = jnp.concatenate([jnp.broadcast_to(head(h)[:, 2 * CH:2 * CH + 1], (HEAD, HEAD))
                                     for h in heads], axis=1)
            uv = _tile_rows(jnp.concatenate([ub, vv], axis=0)) * bd
            mw[slot] = m0 * decay + _dot(bk_t.astype(BF16), uv)

    @pl.when(step < CTX_STEPS)
    def _context():
        mw[...] = jnp.zeros(mw.shape, F32)

        def body(q, carry):
            advance([(dl, dl * SCAN_TILES + sl, jnp.int32(sl), q if dl == 0 else NCH - 1 - q)
                     for dl in range(2) for sl in range(SCAN_TILES)])
            return carry

        lax.fori_loop(0, NCH, body, 0)
        for dl in range(2):
            for sl in range(SCAN_TILES):
                mt = mw[dl * SCAN_TILES + sl]
                by_hv = jnp.concatenate([mt, jnp.zeros_like(mt)], axis=0).T
                for h in range(N_HEADS):
                    sout_ref[sl, dl, h] = by_hv[h * HEAD:(h + 1) * HEAD, 0:HEAD]

    @pl.when(step == CTX_STEPS)
    def _load_state():
        for dl in range(2):
            for sl in range(N_LAT_SEQ):
                mw[dl * SCAN_TILES + sl] = s0_ref[dl, sl]

    @pl.when(step >= CTX_STEPS)
    def _latent():
        n_q = (SCAN_TILES // N_LAT_SEQ) * NCH

        def body(q, carry):
            chains = []
            for dl in range(2):
                qq = q if dl == 0 else n_q - 1 - q
                for sl in range(N_LAT_SEQ):
                    chains.append((dl, dl * SCAN_TILES + sl, (qq // NCH) * N_LAT_SEQ + sl, qq % NCH))
            advance(chains)
            return carry

        lax.fori_loop(0, n_q, body, 0)


def _scan(prep_out, s0_lat, consts):
    rows_per_step = SCAN_TILES * TM

    def row_spec(d, width):
        return pl.BlockSpec((None, rows_per_step, width), lambda s: (d, _scan_block(s, d), 0))

    def gc_spec(d):
        return pl.BlockSpec((None, SCAN_TILES * NCH, D_RWKV), lambda s: (d, _scan_block(s, d), 0))

    def y_spec(d):
        return pl.BlockSpec((rows_per_step, D_RWKV), lambda s: (_scan_block(s, d), 0))

    rows, gc = list(prep_out[:9]), prep_out[9]
    row_specs = lambda d: [row_spec(d, D_RWKV)] * 5 + [row_spec(d, HC)] * 4 + [gc_spec(d)]
    const = lambda a: pl.BlockSpec(a.shape, lambda s: (0,) * a.ndim)
    y_shape = jax.ShapeDtypeStruct((NT, D_RWKV), F32)
    return pl.pallas_call(
        _chunk_kernel,
        grid=(N_SCAN_STEPS,),
        in_specs=row_specs(0) + row_specs(1) + [const(s0_lat)] + [const(a) for a in consts],
        out_specs=[y_spec(0), y_spec(1),
                   pl.BlockSpec((SCAN_TILES, 2, N_HEADS, HEAD, HEAD),
                                lambda s: (_scan_state_block(s), 0, 0, 0, 0))],
        out_shape=[y_shape, y_shape,
                   jax.ShapeDtypeStruct((N_CTX_SEQ, 2, N_HEADS, HEAD, HEAD), F32)],
        scratch_shapes=[pltpu.VMEM((N_SLOTS, HEAD, D_RWKV), F32)],
        compiler_params=_cp(("arbitrary",)),
        name="rwkv_scan",
    )(*rows, gc, *rows, gc, s0_lat, *consts)


HALO = 16


def _glu(z):
    return z[:, 0:D_CONV] * _sigmoid(z[:, D_CONV:])


def _conv_kernel(cur_ref, prev_ref, next_ref, w_ref, b_ref, g_ref, beta_ref, o_ref, ext):
    i = pl.program_id(0)
    first, last = _first_last(i)
    n_ext = TM + 2 * HALO
    u = jnp.concatenate([_glu(prev_ref[...]) * (1.0 - first.astype(F32)), _glu(cur_ref[...]),
                         _glu(next_ref[...]) * (1.0 - last.astype(F32))], axis=0)
    ext[0] = u
    for b in range(1, SUB):
        ext[b] = pltpu.roll(u, n_ext - b, axis=0)
    acc = jnp.zeros((TM, D_CONV), F32)
    for j in range(CONV_W):
        off = HALO - CONV_PAD + j
        acc = acc + ext[off % SUB, off - off % SUB:off - off % SUB + TM, :] * w_ref[j:j + 1, :]
    h = acc + b_ref[...]
    mu = jnp.mean(h, axis=-1, keepdims=True)
    hc = h - mu
    var = jnp.mean(hc * hc, axis=-1, keepdims=True)
    y = hc * lax.rsqrt(var + LN_EPS) * g_ref[...] + beta_ref[...]
    o_ref[...] = _silu(y)


def _conv(zc, conv_w, conv_b, ln_g, ln_b):
    nh = TM // HALO
    vec = pl.BlockSpec((1, D_CONV), lambda i: (0, 0))
    return pl.pallas_call(
        _conv_kernel,
        grid=(N_TILES,),
        in_specs=[pl.BlockSpec((TM, 2 * D_CONV), lambda i: (i, 0)),
                  pl.BlockSpec((HALO, 2 * D_CONV), lambda i: (_prev_tile(i) * nh + nh - 1, 0)),
                  pl.BlockSpec((HALO, 2 * D_CONV), lambda i: (_next_tile(i) * nh, 0)),
                  pl.BlockSpec((CONV_W + 1, D_CONV), lambda i: (0, 0)), vec, vec, vec],
        out_specs=pl.BlockSpec((TM, D_CONV), lambda i: (i, 0)),
        out_shape=jax.ShapeDtypeStruct((NT, D_CONV), F32),
        scratch_shapes=[pltpu.VMEM((SUB, TM + 2 * HALO, D_CONV), F32)],
        compiler_params=_cp(("parallel",)),
        name="conv_module",
    )(zc, zc, zc, conv_w, conv_b, ln_g, ln_b)


def _outproj_kernel(xp_ref, xs_ref, pe_ref, mod_ref, yf_ref, yb_ref, bf_ref, bb_ref, zg_ref, yc_ref,
                    gng_ref, gnb_ref, gl_ref, wo_ref, n2_ref, rw_ref, rb_ref, ones_ref,
                    x1_ref, h2_ref, logit_ref):
    i = pl.program_id(0)
    x = _load_x(i, xp_ref, xs_ref, pe_ref)
    g1 = mod_ref[:, 2 * D:3 * D]
    sh2 = mod_ref[:, 3 * D:4 * D]
    sc2 = mod_ref[:, 4 * D:5 * D]
    ones = ones_ref[...]
    y = (yf_ref[...] + bf_ref[...]) + (yb_ref[...] + bb_ref[...])
    mu = _dot_sel(y, ones) * (1.0 / HEAD)
    yc = y - mu
    var = _dot_sel(yc * yc, ones) * (1.0 / HEAD)
    yn = yc * lax.rsqrt(var + GN_EPS) * gng_ref[...] + gnb_ref[...]
    gate = _dot(_sigmoid(zg_ref[...]).astype(BF16), gl_ref[...])
    y_rwkv = (yn * gate).astype(BF16)
    mix = _dot(y_rwkv, wo_ref[0:D_RWKV, :]) + _dot(yc_ref[...].astype(BF16), wo_ref[D_RWKV:, :])
    x1 = x + g1 * mix
    x1_ref[...] = x1
    h2 = _rms(x1, n2_ref[...]) * (1.0 + sc2) + sh2
    h2_ref[...] = h2.astype(BF16)

    logit_ref[...] = _dot_hp(h2, rw_ref[...]) + rb_ref[...]


def _route(logits):
    lane = lax.broadcasted_iota(jnp.int32, logits.shape, 1)
    lanef = lane.astype(F32)
    neg = jnp.float32(-1e30)
    big = jnp.float32(1e9)
    gmask = lane < N_GROUPS
    gl = jnp.where(gmask, logits, neg)
    ge = jnp.where(gmask, jnp.exp(gl - jnp.max(gl, axis=-1, keepdims=True)), 0.0)
    gprob = ge / jnp.sum(ge, axis=-1, keepdims=True)
    gp = jnp.max(gprob, axis=-1, keepdims=True)
    gidx = jnp.min(jnp.where(jnp.logical_and(gmask, gprob == gp), lanef, big), axis=-1, keepdims=True)
    egrp = jnp.floor((lanef - float(E_LANE0)) * (1.0 / N_EXP_PER_GROUP))
    emask = jnp.logical_and(jnp.logical_and(lane >= E_LANE0, lane < E_LANE0 + N_EXPERTS), egrp == gidx)
    el = jnp.where(emask, logits, neg)
    ee = jnp.where(emask, jnp.exp(el - jnp.max(el, axis=-1, keepdims=True)), 0.0)
    ep = ee / jnp.sum(ee, axis=-1, keepdims=True)
    m1 = jnp.max(jnp.where(emask, ep, -1.0), axis=-1, keepdims=True)
    i1 = jnp.min(jnp.where(jnp.logical_and(emask, ep == m1), lanef, big), axis=-1, keepdims=True)
    mask2 = jnp.logical_and(emask, lanef != i1)
    m2 = jnp.max(jnp.where(mask2, ep, -1.0), axis=-1, keepdims=True)
    i2 = jnp.min(jnp.where(jnp.logical_and(mask2, ep == m2), lanef, big), axis=-1, keepdims=True)
    den = m1 + m2
    return (jnp.where(lanef == i1, gp * (m1 / den), 0.0)
            + jnp.where(lanef == i2, gp * (m2 / den), 0.0)
            + jnp.where(lane == GROUP_LANE, gidx, 0.0))


def _outproj(xp, xs, pe, mod3, yf, yb, bonus, zg, yconv, gn_g, gn_b, gate_bf, w_out_bf, norm2_g,
             router_w, router_b, ones512):
    tile = lambda n: pl.BlockSpec((TM, n), lambda i: (i, 0))
    const = lambda shape: pl.BlockSpec(shape, lambda i: (0,) * len(shape))
    return pl.pallas_call(
        _outproj_kernel,
        grid=(N_TILES,),
        in_specs=_x_specs() + [
            pl.BlockSpec((None, 1, N_MOD * D), lambda i: (_mod_row(i), 0, 0)),
            tile(D_RWKV), tile(D_RWKV),
            pl.BlockSpec((None, TM, D_RWKV), lambda i: (0, i, 0)),
            pl.BlockSpec((None, TM, D_RWKV), lambda i: (1, i, 0)),
            tile(LORA_G), tile(D_CONV),
            const((1, D_RWKV)), const((1, D_RWKV)), const((LORA_G, D_RWKV)), const((D, D)),
            const((1, D)), const((D, ROUTE_LANES)), const((1, ROUTE_LANES)),
            const((D_RWKV, D_RWKV))],
        out_specs=[tile(D), tile(D), tile(ROUTE_LANES)],
        out_shape=[jax.ShapeDtypeStruct((NT, D), F32), jax.ShapeDtypeStruct((NT, D), BF16),
                   jax.ShapeDtypeStruct((NT, ROUTE_LANES), F32)],
        compiler_params=_cp(("parallel",)),
        name="outproj_router",
    )(xp, xs, pe, mod3, yf, yb, bonus, bonus, zg, yconv, gn_g, gn_b, gate_bf, w_out_bf, norm2_g,
      router_w, router_b, ones512)


def _plan_kernel(logit_ref, triu_ref, comb_ref, drow_ref, dcol_ref, start_ref, ntile_ref):
    comb = _route(logit_ref[...])
    comb_ref[...] = comb
    gidx = comb.T[GROUP_LANE:GROUP_LANE + 1, :]
    grow = lax.broadcasted_iota(jnp.int32, (SUB, 1), 0)
    growf = grow.astype(F32)
    onehot = jnp.where(jnp.logical_and(gidx == growf, grow < N_GROUPS), 1.0, 0.0)
    before = _dot(onehot.astype(BF16), triu_ref[...])
    count = jnp.sum(onehot, axis=-1, keepdims=True)
    padded = jnp.floor((count + (SEG - 1.0)) * (1.0 / SEG)) * SEG
    start = jnp.zeros((SUB, 1), F32)
    for g in range(N_GROUPS - 1):
        start = start + jnp.where(grow > g, padded[g:g + 1, :], 0.0)
    dest = jnp.sum(onehot * (start + before), axis=0, keepdims=True)
    drow_ref[...] = jnp.broadcast_to(dest, (SUB, TM_MOE))
    dcol_ref[...] = jnp.broadcast_to(dest, (LANES, TM_MOE)).T
    start_ref[...] = jnp.broadcast_to(start * (1.0 / SEG), (SUB, LANES)).astype(jnp.int32)
    ntile_ref[...] = jnp.broadcast_to(padded * (1.0 / SEG), (SUB, LANES)).astype(jnp.int32)


def _plan(logits, triu):
    seg_shape = jax.ShapeDtypeStruct((N_MOE_TILES, SUB, LANES), jnp.int32)
    seg_spec = pl.BlockSpec((None, SUB, LANES), lambda t: (t, 0, 0))
    tok_spec = pl.BlockSpec((TM_MOE, LANES), lambda t: (t, 0))
    return pl.pallas_call(
        _plan_kernel,
        grid=(N_MOE_TILES,),
        in_specs=[tok_spec, pl.BlockSpec((TM_MOE, TM_MOE), lambda t: (0, 0))],
        out_specs=[tok_spec, pl.BlockSpec((None, SUB, TM_MOE), lambda t: (t, 0, 0)), tok_spec,
                   seg_spec, seg_spec],
        out_shape=[jax.ShapeDtypeStruct((NT, ROUTE_LANES), F32), jax.ShapeDtypeStruct((N_MOE_TILES, SUB, TM_MOE), F32),
                   jax.ShapeDtypeStruct((NT, LANES), F32), seg_shape, seg_shape],
        compiler_params=_cp(("parallel",)),
        name="moe_plan",
    )(logits, triu)


def _moe_kernel(start_ref, ntile_ref, h_ref, comb_ref, drow_ref, dcol_ref, wg_ref, wu_ref, wd_ref,
                o_ref, xs, cws, ys):
    q = pl.program_id(1)
    g = q // (N_EXP_PER_GROUP // EXP_PER_STEP)
    tiles = range(TILES_PER_PASS)
    toks = [slice(j * TM_MOE, (j + 1) * TM_MOE) for j in tiles]

    @pl.when(q == 0)
    def _sort_in():
        slot = lax.broadcasted_iota(jnp.int32, (MOE_ROWS, TM_MOE), 0).astype(F32)
        for j in tiles:
            perm = jnp.where(slot == drow_ref[j, 0:1, :], 1.0, 0.0).astype(BF16)
            xs[j] = _dot(perm, h_ref[toks[j], :]).astype(BF16)
            ch, cl = _split2(comb_ref[toks[j], :])
            cws[j] = _dot(perm, ch) + _dot(perm, cl)
            ys[j] = jnp.zeros((MOE_ROWS, D), F32)

    segs = [(pl.program_id(0) * TILES_PER_PASS + j) * N_GROUPS + g for j in tiles]
    firsts = [start_ref[seg] for seg in segs]

    def visit(windows, n_rows):
        rows = [(j, pl.ds(pl.multiple_of(row0 * SEG, SEG), n_rows)) for j, row0 in windows]
        x = jnp.concatenate([xs[j, r, :] for j, r in rows], axis=0)
        cw_all = jnp.concatenate([cws[j, r, :] for j, r in rows], axis=0)
        lane = lax.broadcasted_iota(jnp.int32, cw_all.shape, 1)
        acc = jnp.concatenate([ys[j, r, :] for j, r in rows], axis=0)
        for k in range(EXP_PER_STEP):
            e_lane = q * EXP_PER_STEP + k + E_LANE0
            cw = jnp.sum(jnp.where(lane == e_lane, cw_all, 0.0), axis=-1, keepdims=True)
            hid = (_silu(_dot(x, wg_ref[k].astype(BF16))) * _dot(x, wu_ref[k].astype(BF16)) * cw).astype(BF16)
            acc = acc + _dot(hid, wd_ref[k].astype(BF16))
        for i, (j, r) in enumerate(rows):
            ys[j, r, :] = acc[i * n_rows:(i + 1) * n_rows]

    needs = [ntile_ref[seg] for seg in segs]
    window = lambda j, size: (j, jnp.minimum(firsts[j], MOE_ROWS // SEG - size))

    def size_class(need, sizes, guard, run):
        lo = 0
        for size in sizes:
            pl.when(jnp.logical_and(guard, jnp.logical_and(need > lo, need <= size)))(
                functools.partial(run, size))
            lo = size

    need_all = functools.reduce(jnp.maximum, needs)
    together = need_all <= MOE_SHARED_SIZES[-1]
    size_class(need_all, MOE_SHARED_SIZES, together,
               lambda size: visit([window(j, size) for j in tiles], size * SEG))
    for j in tiles:
        size_class(needs[j], MOE_SIZES, jnp.logical_not(together),
                   lambda size, j=j: visit([window(j, size)], size * SEG))

    @pl.when(q == N_EXPERTS // EXP_PER_STEP - 1)
    def _sort_out():
        slot = lax.broadcasted_iota(jnp.int32, (TM_MOE, MOE_ROWS), 1).astype(F32)
        for j in tiles:
            perm_t = jnp.where(slot == dcol_ref[toks[j], 0:1], 1.0, 0.0).astype(BF16)
            o_ref[toks[j], :] = _dot(perm_t, ys[j].astype(BF16)).astype(BF16)


def _moe(h2, comb, drow, dcol, seg_start, seg_ntile, wg, wu, wd):
    rows = TILES_PER_PASS * TM_MOE
    once = dict(pipeline_mode=pl.Buffered(1))
    weights = lambda shape: pl.BlockSpec(shape, lambda p, q, s, n: (q, 0, 0))
    grid_spec = pltpu.PrefetchScalarGridSpec(
        num_scalar_prefetch=2,
        grid=(N_MOE_TILES // TILES_PER_PASS, N_EXPERTS // EXP_PER_STEP),
        in_specs=[pl.BlockSpec((rows, D), lambda p, q, s, n: (p, 0), **once),
                  pl.BlockSpec((rows, ROUTE_LANES), lambda p, q, s, n: (p, 0), **once),
                  pl.BlockSpec((TILES_PER_PASS, SUB, TM_MOE), lambda p, q, s, n: (p, 0, 0), **once),
                  pl.BlockSpec((rows, LANES), lambda p, q, s, n: (p, 0), **once),
                  weights((EXP_PER_STEP, D, D_EXPERT)), weights((EXP_PER_STEP, D, D_EXPERT)),
                  weights((EXP_PER_STEP, D_EXPERT, D))],
        out_specs=pl.BlockSpec((rows, D), lambda p, q, s, n: (p, 0)),
        scratch_shapes=[pltpu.VMEM((TILES_PER_PASS, MOE_ROWS, D), BF16),
                        pltpu.VMEM((TILES_PER_PASS, MOE_ROWS, ROUTE_LANES), F32),
                        pltpu.VMEM((TILES_PER_PASS, MOE_ROWS, D), F32)])
    return pl.pallas_call(
        _moe_kernel,
        grid_spec=grid_spec,
        out_shape=jax.ShapeDtypeStruct((NT, D), BF16),
        compiler_params=_cp(("parallel", "arbitrary")),
        name="moe_experts",
    )(seg_start, seg_ntile, h2, comb, drow, dcol, wg, wu, wd)


def _final_kernel(x1_ref, moe_ref, mod_ref, g_ref, o_ref):
    g2 = mod_ref[:, 5 * D:6 * D]
    x2 = x1_ref[...] + g2 * moe_ref[...].astype(F32)
    o_ref[...] = _rms(x2, g_ref[...])


def _final(x1, moe, mod3, final_g, tile0, n_tiles, out_block, name, merge=1):
    rows = merge * TM
    tile = pl.BlockSpec((rows, D), lambda i: (tile0 // merge + i, 0))
    return pl.pallas_call(
        _final_kernel,
        grid=(n_tiles // merge,),
        in_specs=[tile, tile,
                  pl.BlockSpec((None, 1, N_MOD * D), lambda i: (_mod_row(tile0 + i * merge), 0, 0)),
                  pl.BlockSpec((1, D), lambda i: (0, 0))],
        out_specs=pl.BlockSpec((rows, D), lambda i: (out_block(tile0 // merge + i), 0)),
        out_shape=jax.ShapeDtypeStruct((n_tiles * TM, D), F32),
        compiler_params=_cp(("parallel",)),
        name=name,
    )(x1, moe, mod3, final_g)


def _pos_embed(rows):
    t = np.arange(rows * GRID_W)
    row = (t // GRID_W).astype(np.float32)
    col = (t % GRID_W).astype(np.float32)
    quarter = D // 4
    freqs = (1.0 / (10000.0 ** (np.arange(quarter, dtype=np.float32) / quarter))).astype(np.float32)
    ang_r = row[:, None] * freqs[None, :]
    ang_c = col[:, None] * freqs[None, :]
    pe = np.concatenate([np.sin(ang_r), np.cos(ang_r), np.sin(ang_c), np.cos(ang_c)], axis=-1)
    return jnp.asarray(pe, F32)


def _selection_constants():
    ch = np.arange(D_RWKV)
    ones512 = (ch[:, None] // HEAD == ch[None, :] // HEAD).astype(np.float32)
    t = np.arange(TM)
    same_chunk = t[:, None] // CH == t[None, :] // CH
    cum_f = same_chunk & (t[None, :] <= t[:, None])
    cum_b = same_chunk & (t[None, :] >= t[:, None])
    sel8 = np.arange(NCH)[:, None] == t[None, :] // CH
    col = np.arange(HC)
    hm = col[:, None] // CH == ch[None, :] // HEAD
    bm = col[:, None] // CH == col[None, :] // CH
    tt, jj = np.arange(CH)[:, None], col[None, :] % CH
    masks = np.stack([jj < tt, jj <= tt, jj > tt, jj >= tt])
    eye = jj == tt
    bf = lambda x: jnp.asarray(x, BF16)
    f32 = lambda x: jnp.asarray(x, F32)
    pad = np.zeros((2 * SUB - NCH, TM), bool)
    sums = [np.concatenate([cum, same_chunk, sel8, pad]) for cum in (cum_f, cum_b)]
    prep_consts = (bf(ones512), bf(np.stack(sums)), bf(hm), bf(bm),
                   f32(masks.reshape(2, 2, CH, HC)), f32(eye))
    scan_consts = (bf(hm), bf(ones512))
    return prep_consts, scan_consts


def kernel(x_prompt, x_sample, state_rwkv, c, c_ctx, ada_w, ada_b, norm1_g, w_in, tshift_mu, decay_w0, decay_lora_b, iclr_a0, iclr_lora_b, key_k, key_a, bonus_r_k, gate_lora_b, gn_g, gn_b, conv_dw_w, conv_dw_b, conv_ln_g, conv_ln_b, w_out, norm2_g, router_group_w, router_group_b, router_expert_w, router_expert_b, expert_w_gate, expert_w_up, expert_w_down, final_norm_g):
    assert x_prompt.shape == (N_CTX_SEQ, T_CTX, D) and x_sample.shape == (N_LAT_SEQ, T_LAT, D)
    assert ada_w.shape[0] == 1, "one trunk layer"
    prep_consts, scan_consts = _selection_constants()
    ones512 = prep_consts[0]
    xp = x_prompt.reshape(N_CTX_SEQ * T_CTX, D)
    xs = x_sample.reshape(N_LAT_SEQ * T_LAT, D)
    pe = _pos_embed(T_LAT // GRID_W)

    cond8 = jnp.concatenate([c_ctx[None, :], c, jnp.zeros((8 - 1 - N_LAT_SEQ, D), F32)], axis=0)
    mod3 = _adaln(cond8, ada_w[0], ada_b[0][None, :]).reshape(8, 1, N_MOD * D)

    zs, zg, zc = _inproj(xp, xs, pe, mod3, norm1_g, w_in[0].astype(BF16))

    zero = jnp.zeros((2, LORA, D_RWKV), F32)
    lora2 = jnp.concatenate([jnp.concatenate([decay_lora_b[0], zero], axis=2),
                             jnp.concatenate([zero, iclr_lora_b[0]], axis=2)], axis=1)
    vec = lambda p: p.reshape(2, 1, -1)
    prep_out = _prep(zs, vec(tshift_mu[0]), vec(decay_w0[0]), vec(iclr_a0[0]), lora2,
                     vec(key_k[0]), vec(key_a[0]), vec(bonus_r_k[0]), *prep_consts)
    bonus = prep_out[10]

    s0_lat = state_rwkv[:, 0].transpose(1, 0, 4, 2, 3).reshape(2, N_LAT_SEQ, HEAD, D_RWKV)
    yf, yb, s_fin = _scan(prep_out, s0_lat, scan_consts)

    yconv = _conv(zc, jnp.concatenate([conv_dw_w[0], jnp.zeros((1, D_CONV), F32)], axis=0),
                  conv_dw_b, conv_ln_g, conv_ln_b)

    router_w = jnp.concatenate([router_group_w[0], router_expert_w[0],
                                jnp.zeros((D, ROUTE_LANES - N_GROUPS - N_EXPERTS), F32)], axis=1)
    router_b = jnp.concatenate([router_group_b[0], router_expert_b[0],
                                jnp.zeros((ROUTE_LANES - N_GROUPS - N_EXPERTS,), F32)])[None, :]
    x1, h2, logits = _outproj(xp, xs, pe, mod3, yf, yb, bonus, zg, yconv,
                              gn_g, gn_b, gate_lora_b[0].astype(BF16), w_out[0].astype(BF16), norm2_g,
                              router_w, router_b, ones512)

    tok = np.arange(TM_MOE)
    triu = jnp.asarray(tok[:, None] < tok[None, :], BF16)
    comb, drow, dcol, seg_start, seg_ntile = _plan(logits, triu)
    seg_start = seg_start[:, :N_GROUPS, 0].reshape(-1)
    seg_ntile = seg_ntile[:, :N_GROUPS, 0].reshape(-1)
    moe = _moe(h2, comb, drow, dcol, seg_start, seg_ntile,
               expert_w_gate[0], expert_w_up[0], expert_w_down[0])
    fg = final_norm_g[None, :]
    y_prompt = _final(x1, moe, mod3, fg, 0, N_CTX_TILES, lambda i: i, "final_norm_ctx", merge=2)
    y_sample = _final(x1, moe, mod3, fg, N_CTX_TILES, N_LAT_TILES, _xs_block, "final_norm_lat")
    y_prompt = y_prompt.reshape(N_CTX_SEQ, T_CTX, D)
    y_sample = y_sample.reshape(N_LAT_SEQ, T_LAT, D)
    new_state = s_fin[:, None].astype(state_rwkv.dtype)
    return (y_prompt, y_sample, new_state)
```

```python
import functools

import numpy as np
import jax
import jax.numpy as jnp
from jax import lax
from jax.experimental import pallas as pl
from jax.experimental.pallas import tpu as pltpu

F32 = jnp.float32
BF16 = jnp.bfloat16

D = 1024
N_CTX_SEQ = 16
T_CTX = 256
N_LAT_SEQ = 2
T_LAT = 1024
TM = 256
N_CTX_TILES = N_CTX_SEQ * T_CTX // TM
LAT_CHUNKS = T_LAT // TM
N_LAT_TILES = N_LAT_SEQ * LAT_CHUNKS
N_TILES = N_CTX_TILES + N_LAT_TILES
NT = N_TILES * TM
GRID_W = 64
D_RWKV = 512
D_CONV = 512
HEAD = 64
N_HEADS = 8
CONV_W = 31
CONV_PAD = CONV_W // 2
LORA = 64
LORA_G = 128
SHIFT_COLS = 3 * D_RWKV + 2 * LORA
N_GROUPS = 4
N_EXP_PER_GROUP = 8
N_EXPERTS = 32
D_EXPERT = 256
N_MOD = 6
RMS_EPS = 1e-6
LN_EPS = 1e-5
GN_EPS = 64e-5
KK_EPS = 1e-12
LANES = 128
SUB = 8
CH = 32
NCH = TM // CH
HC = N_HEADS * CH
DOUBLINGS = 4
ROUTE_LANES = 128
E_LANE0 = N_GROUPS
GROUP_LANE = 36
TM_MOE = 1024
SEG = 64
EXP_PER_STEP = 2
TILES_PER_PASS = 2
MOE_ROWS = TM_MOE + N_GROUPS * SEG
MOE_SHARED_SIZES = (2, 4, 6, 8, 12)
MOE_SIZES = (2, 4, 6, 8, 12, MOE_ROWS // SEG)
N_MOE_TILES = NT // TM_MOE
VMEM_LIMIT = 56 * 1024 * 1024


def _cp(sem, flags=None):
    return pltpu.CompilerParams(dimension_semantics=sem, vmem_limit_bytes=VMEM_LIMIT, flags=flags)


def _split2(a):
    hi = a.astype(BF16)
    lo = (a - hi.astype(F32)).astype(BF16)
    return hi, lo


def _dot(a, b):
    return jnp.dot(a, b, preferred_element_type=F32)


def _dot_hp(a, b):
    ah, al = _split2(a)
    bh, bl = _split2(b)
    return _dot(ah, bh) + _dot(ah, bl) + _dot(al, bh)


def _dot_sel(a, sel):
    h, l = _split2(a)
    return _dot(h, sel) + _dot(l, sel)


def _sel_dot(sel, a):
    h, l = _split2(a)
    return _dot(sel, h) + _dot(sel, l)


def _sigmoid(x):
    return 1.0 / (1.0 + jnp.exp(-x))


def _silu(x):
    return x * _sigmoid(x)


def _lat_js(i):
    il = jnp.maximum(i - N_CTX_TILES, 0)
    return il // N_LAT_SEQ, il % N_LAT_SEQ


def _xp_block(i):
    return jnp.minimum(i, N_CTX_TILES - 1)


def _xs_block(i):
    j, s = _lat_js(i)
    return s * LAT_CHUNKS + j


def _pe_block(i):
    j, _ = _lat_js(i)
    return j


def _mod_row(i):
    _, s = _lat_js(i)
    return jnp.where(i < N_CTX_TILES, 0, 1 + s)


def _first_last(i):
    j, _ = _lat_js(i)
    is_ctx = i < N_CTX_TILES
    first = jnp.logical_or(is_ctx, j == 0)
    last = jnp.logical_or(is_ctx, j == LAT_CHUNKS - 1)
    return first, last


def _prev_tile(i):
    first, _ = _first_last(i)
    return jnp.where(first, i, i - N_LAT_SEQ)


def _next_tile(i):
    _, last = _first_last(i)
    return jnp.where(last, i, i + N_LAT_SEQ)


def _adaln_kernel(c_ref, w_ref, b_ref, o_ref):
    c = c_ref[...]
    o_ref[...] = _dot_hp(_silu(c), w_ref[...]) + b_ref[...]


def _adaln(cond8, ada_w, ada_b):
    tn = 1536
    n = ada_w.shape[1]
    return pl.pallas_call(
        _adaln_kernel,
        grid=(n // tn,),
        in_specs=[pl.BlockSpec((8, D), lambda j: (0, 0)),
                  pl.BlockSpec((D, tn), lambda j: (0, j)),
                  pl.BlockSpec((1, tn), lambda j: (0, j))],
        out_specs=pl.BlockSpec((8, tn), lambda j: (0, j)),
        out_shape=jax.ShapeDtypeStruct((8, n), F32),
        compiler_params=_cp(("parallel",)),
        name="adaln",
    )(cond8, ada_w, ada_b)


def _load_x(i, xp_ref, xs_ref, pe_ref):
    f = (i >= N_CTX_TILES).astype(F32)
    return xp_ref[...] * (1.0 - f) + (xs_ref[...] + pe_ref[...]) * f


def _x_specs():
    return [pl.BlockSpec((TM, D), lambda i: (_xp_block(i), 0)),
            pl.BlockSpec((TM, D), lambda i: (_xs_block(i), 0)),
            pl.BlockSpec((TM, D), lambda i: (_pe_block(i), 0))]


def _rms(x, g):
    return x * lax.rsqrt(jnp.mean(x * x, axis=-1, keepdims=True) + RMS_EPS) * g


def _inproj_kernel(xp_ref, xs_ref, pe_ref, mod_ref, g_ref, w_ref, zs_ref, zg_ref, zc_ref):
    i = pl.program_id(0)
    x = _load_x(i, xp_ref, xs_ref, pe_ref)
    sh1 = mod_ref[:, 0:D]
    sc1 = mod_ref[:, D:2 * D]
    h = (_rms(x, g_ref[...]) * (1.0 + sc1) + sh1).astype(BF16)
    zs_ref[...] = _dot(h, w_ref[:, 0:SHIFT_COLS])
    zg_ref[...] = _dot(h, w_ref[:, SHIFT_COLS:SHIFT_COLS + LORA_G])
    zc_ref[...] = _dot(h, w_ref[:, SHIFT_COLS + LORA_G:])


def _inproj(xp, xs, pe, mod3, norm1_g, w_in_bf):
    in_cols = w_in_bf.shape[1]
    return pl.pallas_call(
        _inproj_kernel,
        grid=(N_TILES,),
        in_specs=_x_specs() + [
            pl.BlockSpec((None, 1, N_MOD * D), lambda i: (_mod_row(i), 0, 0)),
            pl.BlockSpec((1, D), lambda i: (0, 0)),
            pl.BlockSpec((D, in_cols), lambda i: (0, 0))],
        out_specs=[pl.BlockSpec((TM, SHIFT_COLS), lambda i: (i, 0)),
                   pl.BlockSpec((TM, LORA_G), lambda i: (i, 0)),
                   pl.BlockSpec((TM, 2 * D_CONV), lambda i: (i, 0))],
        out_shape=[jax.ShapeDtypeStruct((NT, SHIFT_COLS), F32),
                   jax.ShapeDtypeStruct((NT, LORA_G), F32),
                   jax.ShapeDtypeStruct((NT, 2 * D_CONV), F32)],
        compiler_params=_cp(("parallel",)),
        name="inproj",
    )(xp, xs, pe, mod3, norm1_g, w_in_bf)


NT_DIMS = (((1,), (1,)), ((), ()))


def _tile_rows(x):
    return jnp.concatenate([x] * N_HEADS, axis=0)


def _prep_kernel(zs_ref, prev_ref, next_ref, mu_ref, w0_ref, a0_ref, lora_ref, kk_ref_, ka_ref,
                 rk_ref, ones_ref, cum_ref, hm_ref, bm_ref, msk_ref, eye_ref,
                 at_out, rt_out, bh_out, kh_out, v_out, t_out, aak_out, arb_out, ark_out, gc_out, bon_out):
    i = pl.program_id(0)
    first, last = _first_last(i)
    cur = zs_ref[...]
    prow = prev_ref[7:8, :] * (1.0 - first.astype(F32))
    nrow = next_ref[0:1, :] * (1.0 - last.astype(F32))
    rows = lax.broadcasted_iota(jnp.int32, (TM, 1), 0)
    shifted = (jnp.where(rows == 0, prow, pltpu.roll(cur, 1, axis=0)),
               jnp.where(rows == TM - 1, nrow, pltpu.roll(cur, TM - 1, axis=0)))
    ones = ones_ref[...]
    lane = lax.broadcasted_iota(jnp.int32, (TM, 2 * LORA), 1)
    scaled = []
    for d in range(2):
        xs = cur + (shifted[d] - cur) * mu_ref[d]
        r = xs[:, 0:D_RWKV]
        k = xs[:, D_RWKV:2 * D_RWKV]
        v = xs[:, 2 * D_RWKV:3 * D_RWKV]
        z2 = xs[:, 3 * D_RWKV:SHIFT_COLS]
        lin = jnp.where(lane < LORA, jnp.tanh(z2), z2)
        lo = _dot_hp(lin, lora_ref[d])
        u = -(w0_ref[d] + lo[:, 0:D_RWKV])
        softplus = jnp.maximum(u, 0.0) + jnp.log(1.0 + jnp.exp(-jnp.abs(u)))
        w_log = -softplus - 0.5
        lw = -jnp.exp(w_log)
        a = _sigmoid(a0_ref[d] + lo[:, D_RWKV:])
        kx = k * kk_ref_[d]
        kk = kx * lax.rsqrt(jnp.maximum(_dot_sel(kx * kx, ones), KK_EPS * KK_EPS))
        k2 = k * (1.0 + (a - 1.0) * ka_ref[d])
        bvec = kk * a
        bon_out[d] = _dot_sel(r * k2 * rk_ref[d], ones) * v

        sums = _sel_dot(cum_ref[d], lw)
        lg = sums[0:TM]
        tot = sums[TM:2 * TM]
        e_tail = jnp.exp(tot - lg)
        e_inv = jnp.exp(-lg)
        at = (-(kk * jnp.exp(lg - lw))).astype(BF16)
        bt = (bvec * e_inv).astype(BF16)
        kt = (k2 * e_inv).astype(BF16)
        rt = (r * jnp.exp(lg)).astype(BF16)
        at_out[d] = at
        rt_out[d] = rt
        bh_out[d] = (bvec * e_tail).astype(BF16)
        kh_out[d] = (k2 * e_tail).astype(BF16)
        v_out[d] = v.astype(BF16)
        gc_out[d] = jnp.exp(sums[2 * TM:2 * TM + NCH])
        scaled.append((at, bt, kt, rt))

    hm = hm_ref[...]
    bm = bm_ref[...]

    def expand(z):
        return _tile_rows(z.astype(BF16)) * bm

    chunks = [(d, slice(c * CH, (c + 1) * CH)) for d in range(2) for c in range(NCH)]
    pws = []
    for d, rows in chunks:
        at, bt, kt, rt = scaled[d]
        ar = jnp.concatenate([at[rows], rt[rows]], axis=0)
        bk = jnp.concatenate([_tile_rows(bt[rows]) * hm, _tile_rows(kt[rows]) * hm], axis=0)
        p1 = lax.dot_general(ar, bk, NT_DIMS, preferred_element_type=F32)
        m_strict, m_incl = msk_ref[d, 0], msk_ref[d, 1]
        pws.append(p1[0:CH, 0:HC] * m_strict)
        aak_out[d, rows, :] = (p1[0:CH, HC:] * m_strict).astype(BF16)
        arb_out[d, rows, :] = (p1[CH:, 0:HC] * m_incl).astype(BF16)
        ark_out[d, rows, :] = (p1[CH:, HC:] * m_incl).astype(BF16)
    tms = [eye_ref[...] + pw for pw in pws]
    pws = [_dot(pw.astype(BF16), expand(pw)) for pw in pws]
    for k in range(1, DOUBLINGS + 1):
        if k < DOUBLINGS:
            prods = [_dot(jnp.concatenate([pw, tm], axis=0).astype(BF16), expand(pw))
                     for pw, tm in zip(pws, tms)]
            pws = [p[0:CH] for p in prods]
            tms = [tm + p[CH:] for tm, p in zip(tms, prods)]
        else:
            tms = [tm + _dot(tm.astype(BF16), expand(pw)) for tm, pw in zip(tms, pws)]
    for (d, rows), tm in zip(chunks, tms):
        t_out[d, rows, :] = tm.astype(BF16)


def _prep(zs, mu, w0, a0, lora2, key_k, key_a, r_k, ones512, cum2, hm, bm, masks, eye):
    full = lambda a: pl.BlockSpec(a.shape, lambda i: (0,) * a.ndim)
    out_spec = pl.BlockSpec((2, TM, D_RWKV), lambda i: (0, i, 0))
    bf_shape = jax.ShapeDtypeStruct((2, NT, D_RWKV), BF16)
    tall_spec = pl.BlockSpec((2, TM, HC), lambda i: (0, i, 0))
    rows8 = TM // 8
    params = (mu, w0, a0, lora2, key_k, key_a, r_k, ones512, cum2, hm, bm, masks, eye)
    return pl.pallas_call(
        _prep_kernel,
        grid=(N_TILES,),
        in_specs=[pl.BlockSpec((TM, SHIFT_COLS), lambda i: (i, 0)),
                  pl.BlockSpec((8, SHIFT_COLS), lambda i: (_prev_tile(i) * rows8 + rows8 - 1, 0)),
                  pl.BlockSpec((8, SHIFT_COLS), lambda i: (_next_tile(i) * rows8, 0))]
                 + [full(p) for p in params],
        out_specs=[out_spec] * 5 + [tall_spec] * 4 + [
            pl.BlockSpec((2, NCH, D_RWKV), lambda i: (0, i, 0)), out_spec],
        out_shape=[bf_shape] * 5 + [jax.ShapeDtypeStruct((2, NT, HC), BF16)] * 4 + [
            jax.ShapeDtypeStruct((2, N_TILES * NCH, D_RWKV), F32),
            jax.ShapeDtypeStruct((2, NT, D_RWKV), F32)],
        compiler_params=_cp(("parallel",)),
        name="rwkv_prep",
    )(zs, zs, zs, *params)


SCAN_TILES = 4
CTX_STEPS = N_CTX_TILES // SCAN_TILES
LAT_STEPS = N_LAT_TILES // SCAN_TILES
N_SCAN_STEPS = CTX_STEPS + LAT_STEPS
N_SLOTS = 2 * SCAN_TILES


def _scan_block(step, d):
    jl = step - CTX_STEPS
    j = jnp.where(d == 0, jl, LAT_STEPS - 1 - jl)
    return jnp.where(step >= CTX_STEPS, CTX_STEPS + j, step)


def _scan_state_block(step):
    return jnp.minimum(step, CTX_STEPS - 1)


def _chunk_kernel(*refs):
    (atf, rtf, bhf, khf, vf, tf, aakf, arbf, arkf, gcf, atb, rtb, bhb, khb, vb, tb, aakb, arbb, arkb, gcb,
     s0_ref, hm_ref, bd_ref, yf_ref, yb_ref, sout_ref, mw) = refs
    step = pl.program_id(0)
    in_refs = ((atf, rtf, bhf, khf, vf, tf, aakf, arbf, arkf, gcf),
               (atb, rtb, bhb, khb, vb, tb, aakb, arbb, arkb, gcb))
    y_refs = (yf_ref, yb_ref)
    bd = bd_ref[...]
    hm = hm_ref[...]
    crow = lax.broadcasted_iota(jnp.int32, (NCH, 1), 0)
    heads = range(N_HEADS)

    def advance(chains):
        rows_of = lambda tile, cc: pl.ds(pl.multiple_of(tile * TM + cc * CH, CH), CH)
        chains = [(dl, slot, tile, cc, rows_of(tile, cc)) for dl, slot, tile, cc in chains]
        ld = lambda idx, dl, rows: in_refs[dl][idx][rows, :]
        m0s = [mw[slot] for _, slot, _, _, _ in chains]
        xy0s = [_dot(jnp.concatenate([ld(0, dl, rows), ld(1, dl, rows)], axis=0),
                     _tile_rows(m0.astype(BF16)) * bd)
                for (dl, _, _, _, rows), m0 in zip(chains, m0s)]
        vvs = [ld(4, dl, rows) for dl, _, _, _, rows in chains]
        avs = [_dot(jnp.concatenate([ld(6, dl, rows), ld(8, dl, rows)], axis=0), _tile_rows(vv) * hm)
               for (dl, _, _, _, rows), vv in zip(chains, vvs)]
        ubs = [_dot(ld(5, dl, rows), _tile_rows((xy0[0:CH] + av[0:CH]).astype(BF16)) * hm).astype(BF16)
               for (dl, _, _, _, rows), xy0, av in zip(chains, xy0s, avs)]
        for (dl, _, _, _, rows), xy0, av, ub in zip(chains, xy0s, avs, ubs):
            y_refs[dl][rows, :] = xy0[CH:] + av[CH:] + _dot(ld(7, dl, rows), _tile_rows(ub) * hm)
        for (dl, slot, tile, cc, rows), m0, ub, vv in zip(chains, m0s, ubs, vvs):
            gc_tile = in_refs[dl][9][pl.ds(pl.multiple_of(tile * NCH, NCH), NCH), :]
            gcrow = jnp.sum(jnp.where(crow == cc, gc_tile, 0.0), axis=0, keepdims=True)
            pad = jnp.zeros((LANES - 2 * CH - SUB, D_RWKV), F32)
            stack = jnp.concatenate([ld(2, dl, rows).astype(F32), ld(3, dl, rows).astype(F32),
                                     jnp.broadcast_to(gcrow, (SUB, D_RWKV)), pad], axis=0)
            stack_t = stack.T
            head = lambda h: stack_t[h * HEAD:(h + 1) * HEAD, :]
            bk_t = jnp.concatenate([head(h)[:, 0:2 * CH] for h in heads], axis=1)
            decay = jnp.concatenate([jnp.broadcast_to(head(h)[:, 2 * CH:2 * CH + 1], (HEAD, HEAD))
                                     for h in heads], axis=1)
            uv = _tile_rows(jnp.concatenate([ub, vv], axis=0)) * bd
            mw[slot] = m0 * decay + _dot(bk_t.astype(BF16), uv)

    @pl.when(step < CTX_STEPS)
    def _context():
        mw[...] = jnp.zeros(mw.shape, F32)

        def body(q, carry):
            advance([(dl, dl * SCAN_TILES + sl, jnp.int32(sl), q if dl == 0 else NCH - 1 - q)
                     for dl in range(2) for sl in range(SCAN_TILES)])
            return carry

        lax.fori_loop(0, NCH, body, 0)
        for dl in range(2):
            for sl in range(SCAN_TILES):
                mt = mw[dl * SCAN_TILES + sl]
                by_hv = jnp.concatenate([mt, jnp.zeros_like(mt)], axis=0).T
                for h in range(N_HEADS):
                    sout_ref[sl, dl, h] = by_hv[h * HEAD:(h + 1) * HEAD, 0:HEAD]

    @pl.when(step == CTX_STEPS)
    def _load_state():
        for dl in range(2):
            for sl in range(N_LAT_SEQ):
                mw[dl * SCAN_TILES + sl] = s0_ref[dl, sl]

    @pl.when(step >= CTX_STEPS)
    def _latent():
        n_q = (SCAN_TILES // N_LAT_SEQ) * NCH

        def body(q, carry):
            chains = []
            for dl in range(2):
                qq = q if dl == 0 else n_q - 1 - q
                for sl in range(N_LAT_SEQ):
                    chains.append((dl, dl * SCAN_TILES + sl, (qq // NCH) * N_LAT_SEQ + sl, qq % NCH))
            advance(chains)
            return carry

        lax.fori_loop(0, n_q, body, 0)


def _scan(prep_out, s0_lat, consts):
    rows_per_step = SCAN_TILES * TM

    def row_spec(d, width):
        return pl.BlockSpec((None, rows_per_step, width), lambda s: (d, _scan_block(s, d), 0))

    def gc_spec(d):
        return pl.BlockSpec((None, SCAN_TILES * NCH, D_RWKV), lambda s: (d, _scan_block(s, d), 0))

    def y_spec(d):
        return pl.BlockSpec((rows_per_step, D_RWKV), lambda s: (_scan_block(s, d), 0))

    rows, gc = list(prep_out[:9]), prep_out[9]
    row_specs = lambda d: [row_spec(d, D_RWKV)] * 5 + [row_spec(d, HC)] * 4 + [gc_spec(d)]
    const = lambda a: pl.BlockSpec(a.shape, lambda s: (0,) * a.ndim)
    y_shape = jax.ShapeDtypeStruct((NT, D_RWKV), F32)
    return pl.pallas_call(
        _chunk_kernel,
        grid=(N_SCAN_STEPS,),
        in_specs=row_specs(0) + row_specs(1) + [const(s0_lat)] + [const(a) for a in consts],
        out_specs=[y_spec(0), y_spec(1),
                   pl.BlockSpec((SCAN_TILES, 2, N_HEADS, HEAD, HEAD),
                                lambda s: (_scan_state_block(s), 0, 0, 0, 0))],
        out_shape=[y_shape, y_shape,
                   jax.ShapeDtypeStruct((N_CTX_SEQ, 2, N_HEADS, HEAD, HEAD), F32)],
        scratch_shapes=[pltpu.VMEM((N_SLOTS, HEAD, D_RWKV), F32)],
        compiler_params=_cp(("arbitrary",)),
        name="rwkv_scan",
    )(*rows, gc, *rows, gc, s0_lat, *consts)


HALO = 16


def _glu(z):
    return z[:, 0:D_CONV] * _sigmoid(z[:, D_CONV:])


def _conv_kernel(cur_ref, prev_ref, next_ref, w_ref, b_ref, g_ref, beta_ref, o_ref, ext):
    i = pl.program_id(0)
    first, last = _first_last(i)
    n_ext = TM + 2 * HALO
    u = jnp.concatenate([_glu(prev_ref[...]) * (1.0 - first.astype(F32)), _glu(cur_ref[...]),
                         _glu(next_ref[...]) * (1.0 - last.astype(F32))], axis=0)
    ext[0] = u
    for b in range(1, SUB):
        ext[b] = pltpu.roll(u, n_ext - b, axis=0)
    acc = jnp.zeros((TM, D_CONV), F32)
    for j in range(CONV_W):
        off = HALO - CONV_PAD + j
        acc = acc + ext[off % SUB, off - off % SUB:off - off % SUB + TM, :] * w_ref[j:j + 1, :]
    h = acc + b_ref[...]
    mu = jnp.mean(h, axis=-1, keepdims=True)
    hc = h - mu
    var = jnp.mean(hc * hc, axis=-1, keepdims=True)
    y = hc * lax.rsqrt(var + LN_EPS) * g_ref[...] + beta_ref[...]
    o_ref[...] = _silu(y)


def _conv(zc, conv_w, conv_b, ln_g, ln_b):
    nh = TM // HALO
    vec = pl.BlockSpec((1, D_CONV), lambda i: (0, 0))
    return pl.pallas_call(
        _conv_kernel,
        grid=(N_TILES,),
        in_specs=[pl.BlockSpec((TM, 2 * D_CONV), lambda i: (i, 0)),
                  pl.BlockSpec((HALO, 2 * D_CONV), lambda i: (_prev_tile(i) * nh + nh - 1, 0)),
                  pl.BlockSpec((HALO, 2 * D_CONV), lambda i: (_next_tile(i) * nh, 0)),
                  pl.BlockSpec((CONV_W + 1, D_CONV), lambda i: (0, 0)), vec, vec, vec],
        out_specs=pl.BlockSpec((TM, D_CONV), lambda i: (i, 0)),
        out_shape=jax.ShapeDtypeStruct((NT, D_CONV), F32),
        scratch_shapes=[pltpu.VMEM((SUB, TM + 2 * HALO, D_CONV), F32)],
        compiler_params=_cp(("parallel",)),
        name="conv_module",
    )(zc, zc, zc, conv_w, conv_b, ln_g, ln_b)


def _outproj_kernel(xp_ref, xs_ref, pe_ref, mod_ref, yf_ref, yb_ref, bf_ref, bb_ref, zg_ref, yc_ref,
                    gng_ref, gnb_ref, gl_ref, wo_ref, n2_ref, rw_ref, rb_ref, ones_ref,
                    x1_ref, h2_ref, logit_ref):
    i = pl.program_id(0)
    x = _load_x(i, xp_ref, xs_ref, pe_ref)
    g1 = mod_ref[:, 2 * D:3 * D]
    sh2 = mod_ref[:, 3 * D:4 * D]
    sc2 = mod_ref[:, 4 * D:5 * D]
    ones = ones_ref[...]
    y = (yf_ref[...] + bf_ref[...]) + (yb_ref[...] + bb_ref[...])
    mu = _dot_sel(y, ones) * (1.0 / HEAD)
    yc = y - mu
    var = _dot_sel(yc * yc, ones) * (1.0 / HEAD)
    yn = yc * lax.rsqrt(var + GN_EPS) * gng_ref[...] + gnb_ref[...]
    gate = _dot(_sigmoid(zg_ref[...]).astype(BF16), gl_ref[...])
    y_rwkv = (yn * gate).astype(BF16)
    mix = _dot(y_rwkv, wo_ref[0:D_RWKV, :]) + _dot(yc_ref[...].astype(BF16), wo_ref[D_RWKV:, :])
    x1 = x + g1 * mix
    x1_ref[...] = x1
    h2 = _rms(x1, n2_ref[...]) * (1.0 + sc2) + sh2
    h2_ref[...] = h2.astype(BF16)

    logit_ref[...] = _dot_hp(h2, rw_ref[...]) + rb_ref[...]


def _route(logits):
    lane = lax.broadcasted_iota(jnp.int32, logits.shape, 1)
    lanef = lane.astype(F32)
    neg = jnp.float32(-1e30)
    big = jnp.float32(1e9)
    gmask = lane < N_GROUPS
    gl = jnp.where(gmask, logits, neg)
    ge = jnp.where(gmask, jnp.exp(gl - jnp.max(gl, axis=-1, keepdims=True)), 0.0)
    gprob = ge / jnp.sum(ge, axis=-1, keepdims=True)
    gp = jnp.max(gprob, axis=-1, keepdims=True)
    gidx = jnp.min(jnp.where(jnp.logical_and(gmask, gprob == gp), lanef, big), axis=-1, keepdims=True)
    egrp = jnp.floor((lanef - float(E_LANE0)) * (1.0 / N_EXP_PER_GROUP))
    emask = jnp.logical_and(jnp.logical_and(lane >= E_LANE0, lane < E_LANE0 + N_EXPERTS), egrp == gidx)
    el = jnp.where(emask, logits, neg)
    ee = jnp.where(emask, jnp.exp(el - jnp.max(el, axis=-1, keepdims=True)), 0.0)
    ep = ee / jnp.sum(ee, axis=-1, keepdims=True)
    m1 = jnp.max(jnp.where(emask, ep, -1.0), axis=-1, keepdims=True)
    i1 = jnp.min(jnp.where(jnp.logical_and(emask, ep == m1), lanef, big), axis=-1, keepdims=True)
    mask2 = jnp.logical_and(emask, lanef != i1)
    m2 = jnp.max(jnp.where(mask2, ep, -1.0), axis=-1, keepdims=True)
    i2 = jnp.min(jnp.where(jnp.logical_and(mask2, ep == m2), lanef, big), axis=-1, keepdims=True)
    den = m1 + m2
    return (jnp.where(lanef == i1, gp * (m1 / den), 0.0)
            + jnp.where(lanef == i2, gp * (m2 / den), 0.0)
            + jnp.where(lane == GROUP_LANE, gidx, 0.0))


def _outproj(xp, xs, pe, mod3, yf, yb, bonus, zg, yconv, gn_g, gn_b, gate_bf, w_out_bf, norm2_g,
             router_w, router_b, ones512):
    tile = lambda n: pl.BlockSpec((TM, n), lambda i: (i, 0))
    const = lambda shape: pl.BlockSpec(shape, lambda i: (0,) * len(shape))
    return pl.pallas_call(
        _outproj_kernel,
        grid=(N_TILES,),
        in_specs=_x_specs() + [
            pl.BlockSpec((None, 1, N_MOD * D), lambda i: (_mod_row(i), 0, 0)),
            tile(D_RWKV), tile(D_RWKV),
            pl.BlockSpec((None, TM, D_RWKV), lambda i: (0, i, 0)),
            pl.BlockSpec((None, TM, D_RWKV), lambda i: (1, i, 0)),
            tile(LORA_G), tile(D_CONV),
            const((1, D_RWKV)), const((1, D_RWKV)), const((LORA_G, D_RWKV)), const((D, D)),
            const((1, D)), const((D, ROUTE_LANES)), const((1, ROUTE_LANES)),
            const((D_RWKV, D_RWKV))],
        out_specs=[tile(D), tile(D), tile(ROUTE_LANES)],
        out_shape=[jax.ShapeDtypeStruct((NT, D), F32), jax.ShapeDtypeStruct((NT, D), BF16),
                   jax.ShapeDtypeStruct((NT, ROUTE_LANES), F32)],
        compiler_params=_cp(("parallel",)),
        name="outproj_router",
    )(xp, xs, pe, mod3, yf, yb, bonus, bonus, zg, yconv, gn_g, gn_b, gate_bf, w_out_bf, norm2_g,
      router_w, router_b, ones512)


def _plan_kernel(logit_ref, triu_ref, comb_ref, drow_ref, dcol_ref, start_ref, ntile_ref):
    comb = _route(logit_ref[...])
    comb_ref[...] = comb
    gidx = comb.T[GROUP_LANE:GROUP_LANE + 1, :]
    grow = lax.broadcasted_iota(jnp.int32, (SUB, 1), 0)
    growf = grow.astype(F32)
    onehot = jnp.where(jnp.logical_and(gidx == growf, grow < N_GROUPS), 1.0, 0.0)
    before = _dot(onehot.astype(BF16), triu_ref[...])
    count = jnp.sum(onehot, axis=-1, keepdims=True)
    padded = jnp.floor((count + (SEG - 1.0)) * (1.0 / SEG)) * SEG
    start = jnp.zeros((SUB, 1), F32)
    for g in range(N_GROUPS - 1):
        start = start + jnp.where(grow > g, padded[g:g + 1, :], 0.0)
    dest = jnp.sum(onehot * (start + before), axis=0, keepdims=True)
    drow_ref[...] = jnp.broadcast_to(dest, (SUB, TM_MOE))
    dcol_ref[...] = jnp.broadcast_to(dest, (LANES, TM_MOE)).T
    start_ref[...] = jnp.broadcast_to(start * (1.0 / SEG), (SUB, LANES)).astype(jnp.int32)
    ntile_ref[...] = jnp.broadcast_to(padded * (1.0 / SEG), (SUB, LANES)).astype(jnp.int32)


def _plan(logits, triu):
    seg_shape = jax.ShapeDtypeStruct((N_MOE_TILES, SUB, LANES), jnp.int32)
    seg_spec = pl.BlockSpec((None, SUB, LANES), lambda t: (t, 0, 0))
    tok_spec = pl.BlockSpec((TM_MOE, LANES), lambda t: (t, 0))
    return pl.pallas_call(
        _plan_kernel,
        grid=(N_MOE_TILES,),
        in_specs=[tok_spec, pl.BlockSpec((TM_MOE, TM_MOE), lambda t: (0, 0))],
        out_specs=[tok_spec, pl.BlockSpec((None, SUB, TM_MOE), lambda t: (t, 0, 0)), tok_spec,
                   seg_spec, seg_spec],
        out_shape=[jax.ShapeDtypeStruct((NT, ROUTE_LANES), F32), jax.ShapeDtypeStruct((N_MOE_TILES, SUB, TM_MOE), F32),
                   jax.ShapeDtypeStruct((NT, LANES), F32), seg_shape, seg_shape],
        compiler_params=_cp(("parallel",)),
        name="moe_plan",
    )(logits, triu)


def _moe_kernel(start_ref, ntile_ref, h_ref, comb_ref, drow_ref, dcol_ref, wg_ref, wu_ref, wd_ref,
                o_ref, xs, cws, ys):
    q = pl.program_id(1)
    g = q // (N_EXP_PER_GROUP // EXP_PER_STEP)
    tiles = range(TILES_PER_PASS)
    toks = [slice(j * TM_MOE, (j + 1) * TM_MOE) for j in tiles]

    @pl.when(q == 0)
    def _sort_in():
        slot = lax.broadcasted_iota(jnp.int32, (MOE_ROWS, TM_MOE), 0).astype(F32)
        for j in tiles:
            perm = jnp.where(slot == drow_ref[j, 0:1, :], 1.0, 0.0).astype(BF16)
            xs[j] = _dot(perm, h_ref[toks[j], :]).astype(BF16)
            ch, cl = _split2(comb_ref[toks[j], :])
            cws[j] = _dot(perm, ch) + _dot(perm, cl)
            ys[j] = jnp.zeros((MOE_ROWS, D), F32)

    segs = [(pl.program_id(0) * TILES_PER_PASS + j) * N_GROUPS + g for j in tiles]
    firsts = [start_ref[seg] for seg in segs]

    def visit(windows, n_rows):
        rows = [(j, pl.ds(pl.multiple_of(row0 * SEG, SEG), n_rows)) for j, row0 in windows]
        x = jnp.concatenate([xs[j, r, :] for j, r in rows], axis=0)
        cw_all = jnp.concatenate([cws[j, r, :] for j, r in rows], axis=0)
        lane = lax.broadcasted_iota(jnp.int32, cw_all.shape, 1)
        acc = jnp.concatenate([ys[j, r, :] for j, r in rows], axis=0)
        for k in range(EXP_PER_STEP):
            e_lane = q * EXP_PER_STEP + k + E_LANE0
            cw = jnp.sum(jnp.where(lane == e_lane, cw_all, 0.0), axis=-1, keepdims=True)
            hid = (_silu(_dot(x, wg_ref[k].astype(BF16))) * _dot(x, wu_ref[k].astype(BF16)) * cw).astype(BF16)
            acc = acc + _dot(hid, wd_ref[k].astype(BF16))
        for i, (j, r) in enumerate(rows):
            ys[j, r, :] = acc[i * n_rows:(i + 1) * n_rows]

    needs = [ntile_ref[seg] for seg in segs]
    window = lambda j, size: (j, jnp.minimum(firsts[j], MOE_ROWS // SEG - size))

    def size_class(need, sizes, guard, run):
        lo = 0
        for size in sizes:
            pl.when(jnp.logical_and(guard, jnp.logical_and(need > lo, need <= size)))(
                functools.partial(run, size))
            lo = size

    need_all = functools.reduce(jnp.maximum, needs)
    together = need_all <= MOE_SHARED_SIZES[-1]
    size_class(need_all, MOE_SHARED_SIZES, together,
               lambda size: visit([window(j, size) for j in tiles], size * SEG))
    for j in tiles:
        size_class(needs[j], MOE_SIZES, jnp.logical_not(together),
                   lambda size, j=j: visit([window(j, size)], size * SEG))

    @pl.when(q == N_EXPERTS // EXP_PER_STEP - 1)
    def _sort_out():
        slot = lax.broadcasted_iota(jnp.int32, (TM_MOE, MOE_ROWS), 1).astype(F32)
        for j in tiles:
            perm_t = jnp.where(slot == dcol_ref[toks[j], 0:1], 1.0, 0.0).astype(BF16)
            o_ref[toks[j], :] = _dot(perm_t, ys[j].astype(BF16)).astype(BF16)


def _moe(h2, comb, drow, dcol, seg_start, seg_ntile, wg, wu, wd):
    rows = TILES_PER_PASS * TM_MOE
    once = dict(pipeline_mode=pl.Buffered(1))
    weights = lambda shape: pl.BlockSpec(shape, lambda p, q, s, n: (q, 0, 0))
    grid_spec = pltpu.PrefetchScalarGridSpec(
        num_scalar_prefetch=2,
        grid=(N_MOE_TILES // TILES_PER_PASS, N_EXPERTS // EXP_PER_STEP),
        in_specs=[pl.BlockSpec((rows, D), lambda p, q, s, n: (p, 0), **once),
                  pl.BlockSpec((rows, ROUTE_LANES), lambda p, q, s, n: (p, 0), **once),
                  pl.BlockSpec((TILES_PER_PASS, SUB, TM_MOE), lambda p, q, s, n: (p, 0, 0), **once),
                  pl.BlockSpec((rows, LANES), lambda p, q, s, n: (p, 0), **once),
                  weights((EXP_PER_STEP, D, D_EXPERT)), weights((EXP_PER_STEP, D, D_EXPERT)),
                  weights((EXP_PER_STEP, D_EXPERT, D))],
        out_specs=pl.BlockSpec((rows, D), lambda p, q, s, n: (p, 0)),
        scratch_shapes=[pltpu.VMEM((TILES_PER_PASS, MOE_ROWS, D), BF16),
                        pltpu.VMEM((TILES_PER_PASS, MOE_ROWS, ROUTE_LANES), F32),
                        pltpu.VMEM((TILES_PER_PASS, MOE_ROWS, D), F32)])
    return pl.pallas_call(
        _moe_kernel,
        grid_spec=grid_spec,
        out_shape=jax.ShapeDtypeStruct((NT, D), BF16),
        compiler_params=_cp(("parallel", "arbitrary")),
        name="moe_experts",
    )(seg_start, seg_ntile, h2, comb, drow, dcol, wg, wu, wd)


def _final_kernel(x1_ref, moe_ref, mod_ref, g_ref, o_ref):
    g2 = mod_ref[:, 5 * D:6 * D]
    x2 = x1_ref[...] + g2 * moe_ref[...].astype(F32)
    o_ref[...] = _rms(x2, g_ref[...])


def _final(x1, moe, mod3, final_g, tile0, n_tiles, out_block, name, merge=1):
    rows = merge * TM
    tile = pl.BlockSpec((rows, D), lambda i: (tile0 // merge + i, 0))
    return pl.pallas_call(
        _final_kernel,
        grid=(n_tiles // merge,),
        in_specs=[tile, tile,
                  pl.BlockSpec((None, 1, N_MOD * D), lambda i: (_mod_row(tile0 + i * merge), 0, 0)),
                  pl.BlockSpec((1, D), lambda i: (0, 0))],
        out_specs=pl.BlockSpec((rows, D), lambda i: (out_block(tile0 // merge + i), 0)),
        out_shape=jax.ShapeDtypeStruct((n_tiles * TM, D), F32),
        compiler_params=_cp(("parallel",)),
        name=name,
    )(x1, moe, mod3, final_g)


def _pos_embed(rows):
    t = np.arange(rows * GRID_W)
    row = (t // GRID_W).astype(np.float32)
    col = (t % GRID_W).astype(np.float32)
    quarter = D // 4
    freqs = (1.0 / (10000.0 ** (np.arange(quarter, dtype=np.float32) / quarter))).astype(np.float32)
    ang_r = row[:, None] * freqs[None, :]
    ang_c = col[:, None] * freqs[None, :]
    pe = np.concatenate([np.sin(ang_r), np.cos(ang_r), np.sin(ang_c), np.cos(ang_c)], axis=-1)
    return jnp.asarray(pe, F32)


def _selection_constants():
    ch = np.arange(D_RWKV)
    ones512 = (ch[:, None] // HEAD == ch[None, :] // HEAD).astype(np.float32)
    t = np.arange(TM)
    same_chunk = t[:, None] // CH == t[None, :] // CH
    cum_f = same_chunk & (t[None, :] <= t[:, None])
    cum_b = same_chunk & (t[None, :] >= t[:, None])
    sel8 = np.arange(NCH)[:, None] == t[None, :] // CH
    col = np.arange(HC)
    hm = col[:, None] // CH == ch[None, :] // HEAD
    bm = col[:, None] // CH == col[None, :] // CH
    tt, jj = np.arange(CH)[:, None], col[None, :] % CH
    masks = np.stack([jj < tt, jj <= tt, jj > tt, jj >= tt])
    eye = jj == tt
    bf = lambda x: jnp.asarray(x, BF16)
    f32 = lambda x: jnp.asarray(x, F32)
    pad = np.zeros((2 * SUB - NCH, TM), bool)
    sums = [np.concatenate([cum, same_chunk, sel8, pad]) for cum in (cum_f, cum_b)]
    prep_consts = (bf(ones512), bf(np.stack(sums)), bf(hm), bf(bm),
                   f32(masks.reshape(2, 2, CH, HC)), f32(eye))
    scan_consts = (bf(hm), bf(ones512))
    return prep_consts, scan_consts


def kernel(x_prompt, x_sample, state_rwkv, c, c_ctx, ada_w, ada_b, norm1_g, w_in, tshift_mu, decay_w0, decay_lora_b, iclr_a0, iclr_lora_b, key_k, key_a, bonus_r_k, gate_lora_b, gn_g, gn_b, conv_dw_w, conv_dw_b, conv_ln_g, conv_ln_b, w_out, norm2_g, router_group_w, router_group_b, router_expert_w, router_expert_b, expert_w_gate, expert_w_up, expert_w_down, final_norm_g):
    assert x_prompt.shape == (N_CTX_SEQ, T_CTX, D) and x_sample.shape == (N_LAT_SEQ, T_LAT, D)
    assert ada_w.shape[0] == 1, "one trunk layer"
    prep_consts, scan_consts = _selection_constants()
    ones512 = prep_consts[0]
    xp = x_prompt.reshape(N_CTX_SEQ * T_CTX, D)
    xs = x_sample.reshape(N_LAT_SEQ * T_LAT, D)
    pe = _pos_embed(T_LAT // GRID_W)

    cond8 = jnp.concatenate([c_ctx[None, :], c, jnp.zeros((8 - 1 - N_LAT_SEQ, D), F32)], axis=0)
    mod3 = _adaln(cond8, ada_w[0], ada_b[0][None, :]).reshape(8, 1, N_MOD * D)

    zs, zg, zc = _inproj(xp, xs, pe, mod3, norm1_g, w_in[0].astype(BF16))

    zero = jnp.zeros((2, LORA, D_RWKV), F32)
    lora2 = jnp.concatenate([jnp.concatenate([decay_lora_b[0], zero], axis=2),
                             jnp.concatenate([zero, iclr_lora_b[0]], axis=2)], axis=1)
    vec = lambda p: p.reshape(2, 1, -1)
    prep_out = _prep(zs, vec(tshift_mu[0]), vec(decay_w0[0]), vec(iclr_a0[0]), lora2,
                     vec(key_k[0]), vec(key_a[0]), vec(bonus_r_k[0]), *prep_consts)
    bonus = prep_out[10]

    s0_lat = state_rwkv[:, 0].transpose(1, 0, 4, 2, 3).reshape(2, N_LAT_SEQ, HEAD, D_RWKV)
    yf, yb, s_fin = _scan(prep_out, s0_lat, scan_consts)

    yconv = _conv(zc, jnp.concatenate([conv_dw_w[0], jnp.zeros((1, D_CONV), F32)], axis=0),
                  conv_dw_b, conv_ln_g, conv_ln_b)

    router_w = jnp.concatenate([router_group_w[0], router_expert_w[0],
                                jnp.zeros((D, ROUTE_LANES - N_GROUPS - N_EXPERTS), F32)], axis=1)
    router_b = jnp.concatenate([router_group_b[0], router_expert_b[0],
                                jnp.zeros((ROUTE_LANES - N_GROUPS - N_EXPERTS,), F32)])[None, :]
    x1, h2, logits = _outproj(xp, xs, pe, mod3, yf, yb, bonus, zg, yconv,
                              gn_g, gn_b, gate_lora_b[0].astype(BF16), w_out[0].astype(BF16), norm2_g,
                              router_w, router_b, ones512)

    tok = np.arange(TM_MOE)
    triu = jnp.asarray(tok[:, None] < tok[None, :], BF16)
    comb, drow, dcol, seg_start, seg_ntile = _plan(logits, triu)
    seg_start = seg_start[:, :N_GROUPS, 0].reshape(-1)
    seg_ntile = seg_ntile[:, :N_GROUPS, 0].reshape(-1)
    moe = _moe(h2, comb, drow, dcol, seg_start, seg_ntile,
               expert_w_gate[0], expert_w_up[0], expert_w_down[0])
    fg = final_norm_g[None, :]
    y_prompt = _final(x1, moe, mod3, fg, 0, N_CTX_TILES, lambda i: i, "final_norm_ctx", merge=2)
    y_sample = _final(x1, moe, mod3, fg, N_CTX_TILES, N_LAT_TILES, _xs_block, "final_norm_lat")
    y_prompt = y_prompt.reshape(N_CTX_SEQ, T_CTX, D)
    y_sample = y_sample.reshape(N_LAT_SEQ, T_LAT, D)
    new_state = s_fin[:, None].astype(state_rwkv.dtype)
    return (y_prompt, y_sample, new_state)
```

```python
import functools

import numpy as np
import jax
import jax.numpy as jnp
from jax import lax
from jax.experimental import pallas as pl
from jax.experimental.pallas import tpu as pltpu

F32 = jnp.float32
BF16 = jnp.bfloat16

D = 1024
N_CTX_SEQ = 16
T_CTX = 256
N_LAT_SEQ = 2
T_LAT = 1024
TM = 256
N_CTX_TILES = N_CTX_SEQ * T_CTX // TM
LAT_CHUNKS = T_LAT // TM
N_LAT_TILES = N_LAT_SEQ * LAT_CHUNKS
N_TILES = N_CTX_TILES + N_LAT_TILES
NT = N_TILES * TM
GRID_W = 64
D_RWKV = 512
D_CONV = 512
HEAD = 64
N_HEADS = 8
CONV_W = 31
CONV_PAD = CONV_W // 2
LORA = 64
LORA_G = 128
SHIFT_COLS = 3 * D_RWKV + 2 * LORA
N_GROUPS = 4
N_EXP_PER_GROUP = 8
N_EXPERTS = 32
D_EXPERT = 256
N_MOD = 6
RMS_EPS = 1e-6
LN_EPS = 1e-5
GN_EPS = 64e-5
KK_EPS = 1e-12
LANES = 128
SUB = 8
CH = 32
NCH = TM // CH
HC = N_HEADS * CH
DOUBLINGS = 4
ROUTE_LANES = 128
E_LANE0 = N_GROUPS
GROUP_LANE = 36
TM_MOE = 1024
SEG = 64
EXP_PER_STEP = 2
TILES_PER_PASS = 2
MOE_ROWS = TM_MOE + N_GROUPS * SEG
MOE_SHARED_SIZES = (2, 4, 6)
MOE_SIZES = (2, 4, 6, 8, 12, MOE_ROWS // SEG)
N_MOE_TILES = NT // TM_MOE
VMEM_LIMIT = 56 * 1024 * 1024


def _cp(sem, flags=None):
    return pltpu.CompilerParams(dimension_semantics=sem, vmem_limit_bytes=VMEM_LIMIT, flags=flags)


def _split2(a):
    hi = a.astype(BF16)
    lo = (a - hi.astype(F32)).astype(BF16)
    return hi, lo


def _dot(a, b):
    return jnp.dot(a, b, preferred_element_type=F32)


def _dot_hp(a, b):
    ah, al = _split2(a)
    bh, bl = _split2(b)
    return _dot(ah, bh) + _dot(ah, bl) + _dot(al, bh)


def _dot_sel(a, sel):
    h, l = _split2(a)
    return _dot(h, sel) + _dot(l, sel)


def _sel_dot(sel, a):
    h, l = _split2(a)
    return _dot(sel, h) + _dot(sel, l)


def _sigmoid(x):
    return 1.0 / (1.0 + jnp.exp(-x))


def _silu(x):
    return x * _sigmoid(x)


def _lat_js(i):
    il = jnp.maximum(i - N_CTX_TILES, 0)
    return il // N_LAT_SEQ, il % N_LAT_SEQ


def _xp_block(i):
    return jnp.minimum(i, N_CTX_TILES - 1)


def _xs_block(i):
    j, s = _lat_js(i)
    return s * LAT_CHUNKS + j


def _pe_block(i):
    j, _ = _lat_js(i)
    return j


def _mod_row(i):
    _, s = _lat_js(i)
    return jnp.where(i < N_CTX_TILES, 0, 1 + s)


def _first_last(i):
    j, _ = _lat_js(i)
    is_ctx = i < N_CTX_TILES
    first = jnp.logical_or(is_ctx, j == 0)
    last = jnp.logical_or(is_ctx, j == LAT_CHUNKS - 1)
    return first, last


def _prev_tile(i):
    first, _ = _first_last(i)
    return jnp.where(first, i, i - N_LAT_SEQ)


def _next_tile(i):
    _, last = _first_last(i)
    return jnp.where(last, i, i + N_LAT_SEQ)


def _adaln_kernel(c_ref, w_ref, b_ref, o_ref):
    c = c_ref[...]
    o_ref[...] = _dot_hp(_silu(c), w_ref[...]) + b_ref[...]


def _adaln(cond8, ada_w, ada_b):
    tn = 1536
    n = ada_w.shape[1]
    return pl.pallas_call(
        _adaln_kernel,
        grid=(n // tn,),
        in_specs=[pl.BlockSpec((8, D), lambda j: (0, 0)),
                  pl.BlockSpec((D, tn), lambda j: (0, j)),
                  pl.BlockSpec((1, tn), lambda j: (0, j))],
        out_specs=pl.BlockSpec((8, tn), lambda j: (0, j)),
        out_shape=jax.ShapeDtypeStruct((8, n), F32),
        compiler_params=_cp(("parallel",)),
        name="adaln",
    )(cond8, ada_w, ada_b)


def _load_x(i, xp_ref, xs_ref, pe_ref):
    f = (i >= N_CTX_TILES).astype(F32)
    return xp_ref[...] * (1.0 - f) + (xs_ref[...] + pe_ref[...]) * f


def _x_specs():
    return [pl.BlockSpec((TM, D), lambda i: (_xp_block(i), 0)),
            pl.BlockSpec((TM, D), lambda i: (_xs_block(i), 0)),
            pl.BlockSpec((TM, D), lambda i: (_pe_block(i), 0))]


def _rms(x, g):
    return x * lax.rsqrt(jnp.mean(x * x, axis=-1, keepdims=True) + RMS_EPS) * g


def _inproj_kernel(xp_ref, xs_ref, pe_ref, mod_ref, g_ref, w_ref, zs_ref, zg_ref, zc_ref):
    i = pl.program_id(0)
    x = _load_x(i, xp_ref, xs_ref, pe_ref)
    sh1 = mod_ref[:, 0:D]
    sc1 = mod_ref[:, D:2 * D]
    h = (_rms(x, g_ref[...]) * (1.0 + sc1) + sh1).astype(BF16)
    zs_ref[...] = _dot(h, w_ref[:, 0:SHIFT_COLS])
    zg_ref[...] = _dot(h, w_ref[:, SHIFT_COLS:SHIFT_COLS + LORA_G])
    zc_ref[...] = _dot(h, w_ref[:, SHIFT_COLS + LORA_G:])


def _inproj(xp, xs, pe, mod3, norm1_g, w_in_bf):
    in_cols = w_in_bf.shape[1]
    return pl.pallas_call(
        _inproj_kernel,
        grid=(N_TILES,),
        in_specs=_x_specs() + [
            pl.BlockSpec((None, 1, N_MOD * D), lambda i: (_mod_row(i), 0, 0)),
            pl.BlockSpec((1, D), lambda i: (0, 0)),
            pl.BlockSpec((D, in_cols), lambda i: (0, 0))],
        out_specs=[pl.BlockSpec((TM, SHIFT_COLS), lambda i: (i, 0)),
                   pl.BlockSpec((TM, LORA_G), lambda i: (i, 0)),
                   pl.BlockSpec((TM, 2 * D_CONV), lambda i: (i, 0))],
        out_shape=[jax.ShapeDtypeStruct((NT, SHIFT_COLS), F32),
                   jax.ShapeDtypeStruct((NT, LORA_G), F32),
                   jax.ShapeDtypeStruct((NT, 2 * D_CONV), F32)],
        compiler_params=_cp(("parallel",)),
        name="inproj",
    )(xp, xs, pe, mod3, norm1_g, w_in_bf)


NT_DIMS = (((1,), (1,)), ((), ()))


def _tile_rows(x):
    return jnp.concatenate([x] * N_HEADS, axis=0)


def _prep_kernel(zs_ref, prev_ref, next_ref, mu_ref, w0_ref, a0_ref, lora_ref, kk_ref_, ka_ref,
                 rk_ref, ones_ref, cum_ref, hm_ref, bm_ref, msk_ref, eye_ref,
                 at_out, rt_out, bh_out, kh_out, v_out, t_out, aak_out, arb_out, ark_out, gc_out, bon_out):
    i = pl.program_id(0)
    first, last = _first_last(i)
    cur = zs_ref[...]
    prow = prev_ref[7:8, :] * (1.0 - first.astype(F32))
    nrow = next_ref[0:1, :] * (1.0 - last.astype(F32))
    rows = lax.broadcasted_iota(jnp.int32, (TM, 1), 0)
    shifted = (jnp.where(rows == 0, prow, pltpu.roll(cur, 1, axis=0)),
               jnp.where(rows == TM - 1, nrow, pltpu.roll(cur, TM - 1, axis=0)))
    ones = ones_ref[...]
    lane = lax.broadcasted_iota(jnp.int32, (TM, 2 * LORA), 1)
    scaled = []
    for d in range(2):
        xs = cur + (shifted[d] - cur) * mu_ref[d]
        r = xs[:, 0:D_RWKV]
        k = xs[:, D_RWKV:2 * D_RWKV]
        v = xs[:, 2 * D_RWKV:3 * D_RWKV]
        z2 = xs[:, 3 * D_RWKV:SHIFT_COLS]
        lin = jnp.where(lane < LORA, jnp.tanh(z2), z2)
        lo = _dot_hp(lin, lora_ref[d])
        u = -(w0_ref[d] + lo[:, 0:D_RWKV])
        softplus = jnp.maximum(u, 0.0) + jnp.log(1.0 + jnp.exp(-jnp.abs(u)))
        w_log = -softplus - 0.5
        lw = -jnp.exp(w_log)
        a = _sigmoid(a0_ref[d] + lo[:, D_RWKV:])
        kx = k * kk_ref_[d]
        kk = kx * lax.rsqrt(jnp.maximum(_dot_sel(kx * kx, ones), KK_EPS * KK_EPS))
        k2 = k * (1.0 + (a - 1.0) * ka_ref[d])
        bvec = kk * a
        bon_out[d] = _dot_sel(r * k2 * rk_ref[d], ones) * v

        sums = _sel_dot(cum_ref[d], lw)
        lg = sums[0:TM]
        tot = sums[TM:2 * TM]
        e_tail = jnp.exp(tot - lg)
        e_inv = jnp.exp(-lg)
        at = (-(kk * jnp.exp(lg - lw))).astype(BF16)
        bt = (bvec * e_inv).astype(BF16)
        kt = (k2 * e_inv).astype(BF16)
        rt = (r * jnp.exp(lg)).astype(BF16)
        at_out[d] = at
        rt_out[d] = rt
        bh_out[d] = (bvec * e_tail).astype(BF16)
        kh_out[d] = (k2 * e_tail).astype(BF16)
        v_out[d] = v.astype(BF16)
        gc_out[d] = jnp.exp(sums[2 * TM:2 * TM + NCH])
        scaled.append((at, bt, kt, rt))

    hm = hm_ref[...]
    bm = bm_ref[...]

    def expand(z):
        return _tile_rows(z.astype(BF16)) * bm

    chunks = [(d, slice(c * CH, (c + 1) * CH)) for d in range(2) for c in range(NCH)]
    pws = []
    for d, rows in chunks:
        at, bt, kt, rt = scaled[d]
        ar = jnp.concatenate([at[rows], rt[rows]], axis=0)
        bk = jnp.concatenate([_tile_rows(bt[rows]) * hm, _tile_rows(kt[rows]) * hm], axis=0)
        p1 = lax.dot_general(ar, bk, NT_DIMS, preferred_element_type=F32)
        m_strict, m_incl = msk_ref[d, 0], msk_ref[d, 1]
        pws.append(p1[0:CH, 0:HC] * m_strict)
        aak_out[d, rows, :] = (p1[0:CH, HC:] * m_strict).astype(BF16)
        arb_out[d, rows, :] = (p1[CH:, 0:HC] * m_incl).astype(BF16)
        ark_out[d, rows, :] = (p1[CH:, HC:] * m_incl).astype(BF16)
    tms = [eye_ref[...] + pw for pw in pws]
    pws = [_dot(pw.astype(BF16), expand(pw)) for pw in pws]
    for k in range(1, DOUBLINGS + 1):
        if k < DOUBLINGS:
            prods = [_dot(jnp.concatenate([pw, tm], axis=0).astype(BF16), expand(pw))
                     for pw, tm in zip(pws, tms)]
            pws = [p[0:CH] for p in prods]
            tms = [tm + p[CH:] for tm, p in zip(tms, prods)]
        else:
            tms = [tm + _dot(tm.astype(BF16), expand(pw)) for tm, pw in zip(tms, pws)]
    for (d, rows), tm in zip(chunks, tms):
        t_out[d, rows, :] = tm.astype(BF16)


def _prep(zs, mu, w0, a0, lora2, key_k, key_a, r_k, ones512, cum2, hm, bm, masks, eye):
    full = lambda a: pl.BlockSpec(a.shape, lambda i: (0,) * a.ndim)
    out_spec = pl.BlockSpec((2, TM, D_RWKV), lambda i: (0, i, 0))
    bf_shape = jax.ShapeDtypeStruct((2, NT, D_RWKV), BF16)
    tall_spec = pl.BlockSpec((2, TM, HC), lambda i: (0, i, 0))
    rows8 = TM // 8
    params = (mu, w0, a0, lora2, key_k, key_a, r_k, ones512, cum2, hm, bm, masks, eye)
    return pl.pallas_call(
        _prep_kernel,
        grid=(N_TILES,),
        in_specs=[pl.BlockSpec((TM, SHIFT_COLS), lambda i: (i, 0)),
                  pl.BlockSpec((8, SHIFT_COLS), lambda i: (_prev_tile(i) * rows8 + rows8 - 1, 0)),
                  pl.BlockSpec((8, SHIFT_COLS), lambda i: (_next_tile(i) * rows8, 0))]
                 + [full(p) for p in params],
        out_specs=[out_spec] * 5 + [tall_spec] * 4 + [
            pl.BlockSpec((2, NCH, D_RWKV), lambda i: (0, i, 0)), out_spec],
        out_shape=[bf_shape] * 5 + [jax.ShapeDtypeStruct((2, NT, HC), BF16)] * 4 + [
            jax.ShapeDtypeStruct((2, N_TILES * NCH, D_RWKV), F32),
            jax.ShapeDtypeStruct((2, NT, D_RWKV), F32)],
        compiler_params=_cp(("parallel",)),
        name="rwkv_prep",
    )(zs, zs, zs, *params)


SCAN_TILES = 4
CTX_STEPS = N_CTX_TILES // SCAN_TILES
LAT_STEPS = N_LAT_TILES // SCAN_TILES
N_SCAN_STEPS = CTX_STEPS + LAT_STEPS
N_SLOTS = 2 * SCAN_TILES


def _scan_block(step, d):
    jl = step - CTX_STEPS
    j = jnp.where(d == 0, jl, LAT_STEPS - 1 - jl)
    return jnp.where(step >= CTX_STEPS, CTX_STEPS + j, step)


def _scan_state_block(step):
    return jnp.minimum(step, CTX_STEPS - 1)


def _chunk_kernel(*refs):
    (atf, rtf, bhf, khf, vf, tf, aakf, arbf, arkf, gcf, atb, rtb, bhb, khb, vb, tb, aakb, arbb, arkb, gcb,
     s0_ref, hm_ref, bd_ref, yf_ref, yb_ref, sout_ref, mw) = refs
    step = pl.program_id(0)
    in_refs = ((atf, rtf, bhf, khf, vf, tf, aakf, arbf, arkf, gcf),
               (atb, rtb, bhb, khb, vb, tb, aakb, arbb, arkb, gcb))
    y_refs = (yf_ref, yb_ref)
    bd = bd_ref[...]
    hm = hm_ref[...]
    crow = lax.broadcasted_iota(jnp.int32, (NCH, 1), 0)
    heads = range(N_HEADS)

    def advance(chains):
        rows_of = lambda tile, cc: pl.ds(pl.multiple_of(tile * TM + cc * CH, CH), CH)
        chains = [(dl, slot, tile, cc, rows_of(tile, cc)) for dl, slot, tile, cc in chains]
        ld = lambda idx, dl, rows: in_refs[dl][idx][rows, :]
        m0s = [mw[slot] for _, slot, _, _, _ in chains]
        xy0s = [_dot(jnp.concatenate([ld(0, dl, rows), ld(1, dl, rows)], axis=0),
                     _tile_rows(m0.astype(BF16)) * bd)
                for (dl, _, _, _, rows), m0 in zip(chains, m0s)]
        vvs = [ld(4, dl, rows) for dl, _, _, _, rows in chains]
        avs = [_dot(jnp.concatenate([ld(6, dl, rows), ld(8, dl, rows)], axis=0), _tile_rows(vv) * hm)
               for (dl, _, _, _, rows), vv in zip(chains, vvs)]
        ubs = [_dot(ld(5, dl, rows), _tile_rows((xy0[0:CH] + av[0:CH]).astype(BF16)) * hm).astype(BF16)
               for (dl, _, _, _, rows), xy0, av in zip(chains, xy0s, avs)]
        for (dl, _, _, _, rows), xy0, av, ub in zip(chains, xy0s, avs, ubs):
            y_refs[dl][rows, :] = xy0[CH:] + av[CH:] + _dot(ld(7, dl, rows), _tile_rows(ub) * hm)
        for (dl, slot, tile, cc, rows), m0, ub, vv in zip(chains, m0s, ubs, vvs):
            gc_tile = in_refs[dl][9][pl.ds(pl.multiple_of(tile * NCH, NCH), NCH), :]
            gcrow = jnp.sum(jnp.where(crow == cc, gc_tile, 0.0), axis=0, keepdims=True)
            pad = jnp.zeros((LANES - 2 * CH - SUB, D_RWKV), F32)
            stack = jnp.concatenate([ld(2, dl, rows).astype(F32), ld(3, dl, rows).astype(F32),
                                     jnp.broadcast_to(gcrow, (SUB, D_RWKV)), pad], axis=0)
            stack_t = stack.T
            head = lambda h: stack_t[h * HEAD:(h + 1) * HEAD, :]
            bk_t = jnp.concatenate([head(h)[:, 0:2 * CH] for h in heads], axis=1)
            decay = jnp.concatenate([jnp.broadcast_to(head(h)[:, 2 * CH:2 * CH + 1], (HEAD, HEAD))
                                     for h in heads], axis=1)
            uv = _tile_rows(jnp.concatenate([ub, vv], axis=0)) * bd
            mw[slot] = m0 * decay + _dot(bk_t.astype(BF16), uv)

    @pl.when(step < CTX_STEPS)
    def _context():
        mw[...] = jnp.zeros(mw.shape, F32)

        def body(q, carry):
            advance([(dl, dl * SCAN_TILES + sl, jnp.int32(sl), q if dl == 0 else NCH - 1 - q)
                     for dl in range(2) for sl in range(SCAN_TILES)])
            return carry

        lax.fori_loop(0, NCH, body, 0)
        for dl in range(2):
            for sl in range(SCAN_TILES):
                mt = mw[dl * SCAN_TILES + sl]
                by_hv = jnp.concatenate([mt, jnp.zeros_like(mt)], axis=0).T
                for h in range(N_HEADS):
                    sout_ref[sl, dl, h] = by_hv[h * HEAD:(h + 1) * HEAD, 0:HEAD]

    @pl.when(step == CTX_STEPS)
    def _load_state():
        for dl in range(2):
            for sl in range(N_LAT_SEQ):
                mw[dl * SCAN_TILES + sl] = s0_ref[dl, sl]

    @pl.when(step >= CTX_STEPS)
    def _latent():
        n_q = (SCAN_TILES // N_LAT_SEQ) * NCH

        def body(q, carry):
            chains = []
            for dl in range(2):
                qq = q if dl == 0 else n_q - 1 - q
                for sl in range(N_LAT_SEQ):
                    chains.append((dl, dl * SCAN_TILES + sl, (qq // NCH) * N_LAT_SEQ + sl, qq % NCH))
            advance(chains)
            return carry

        lax.fori_loop(0, n_q, body, 0)


def _scan(prep_out, s0_lat, consts):
    rows_per_step = SCAN_TILES * TM

    def row_spec(d, width):
        return pl.BlockSpec((None, rows_per_step, width), lambda s: (d, _scan_block(s, d), 0))

    def gc_spec(d):
        return pl.BlockSpec((None, SCAN_TILES * NCH, D_RWKV), lambda s: (d, _scan_block(s, d), 0))

    def y_spec(d):
        return pl.BlockSpec((rows_per_step, D_RWKV), lambda s: (_scan_block(s, d), 0))

    rows, gc = list(prep_out[:9]), prep_out[9]
    row_specs = lambda d: [row_spec(d, D_RWKV)] * 5 + [row_spec(d, HC)] * 4 + [gc_spec(d)]
    const = lambda a: pl.BlockSpec(a.shape, lambda s: (0,) * a.ndim)
    y_shape = jax.ShapeDtypeStruct((NT, D_RWKV), F32)
    return pl.pallas_call(
        _chunk_kernel,
        grid=(N_SCAN_STEPS,),
        in_specs=row_specs(0) + row_specs(1) + [const(s0_lat)] + [const(a) for a in consts],
        out_specs=[y_spec(0), y_spec(1),
                   pl.BlockSpec((SCAN_TILES, 2, N_HEADS, HEAD, HEAD),
                                lambda s: (_scan_state_block(s), 0, 0, 0, 0))],
        out_shape=[y_shape, y_shape,
                   jax.ShapeDtypeStruct((N_CTX_SEQ, 2, N_HEADS, HEAD, HEAD), F32)],
        scratch_shapes=[pltpu.VMEM((N_SLOTS, HEAD, D_RWKV), F32)],
        compiler_params=_cp(("arbitrary",)),
        name="rwkv_scan",
    )(*rows, gc, *rows, gc, s0_lat, *consts)


HALO = 16


def _glu(z):
    return z[:, 0:D_CONV] * _sigmoid(z[:, D_CONV:])


def _conv_kernel(cur_ref, prev_ref, next_ref, w_ref, b_ref, g_ref, beta_ref, o_ref, ext):
    i = pl.program_id(0)
    first, last = _first_last(i)
    n_ext = TM + 2 * HALO
    u = jnp.concatenate([_glu(prev_ref[...]) * (1.0 - first.astype(F32)), _glu(cur_ref[...]),
                         _glu(next_ref[...]) * (1.0 - last.astype(F32))], axis=0)
    ext[0] = u
    for b in range(1, SUB):
        ext[b] = pltpu.roll(u, n_ext - b, axis=0)
    acc = jnp.zeros((TM, D_CONV), F32)
    for j in range(CONV_W):
        off = HALO - CONV_PAD + j
        acc = acc + ext[off % SUB, off - off % SUB:off - off % SUB + TM, :] * w_ref[j:j + 1, :]
    h = acc + b_ref[...]
    mu = jnp.mean(h, axis=-1, keepdims=True)
    hc = h - mu
    var = jnp.mean(hc * hc, axis=-1, keepdims=True)
    y = hc * lax.rsqrt(var + LN_EPS) * g_ref[...] + beta_ref[...]
    o_ref[...] = _silu(y)


def _conv(zc, conv_w, conv_b, ln_g, ln_b):
    nh = TM // HALO
    vec = pl.BlockSpec((1, D_CONV), lambda i: (0, 0))
    return pl.pallas_call(
        _conv_kernel,
        grid=(N_TILES,),
        in_specs=[pl.BlockSpec((TM, 2 * D_CONV), lambda i: (i, 0)),
                  pl.BlockSpec((HALO, 2 * D_CONV), lambda i: (_prev_tile(i) * nh + nh - 1, 0)),
                  pl.BlockSpec((HALO, 2 * D_CONV), lambda i: (_next_tile(i) * nh, 0)),
                  pl.BlockSpec((CONV_W + 1, D_CONV), lambda i: (0, 0)), vec, vec, vec],
        out_specs=pl.BlockSpec((TM, D_CONV), lambda i: (i, 0)),
        out_shape=jax.ShapeDtypeStruct((NT, D_CONV), F32),
        scratch_shapes=[pltpu.VMEM((SUB, TM + 2 * HALO, D_CONV), F32)],
        compiler_params=_cp(("parallel",)),
        name="conv_module",
    )(zc, zc, zc, conv_w, conv_b, ln_g, ln_b)


def _outproj_kernel(xp_ref, xs_ref, pe_ref, mod_ref, yf_ref, yb_ref, bf_ref, bb_ref, zg_ref, yc_ref,
                    gng_ref, gnb_ref, gl_ref, wo_ref, n2_ref, rw_ref, rb_ref, ones_ref,
                    x1_ref, h2_ref, logit_ref):
    i = pl.program_id(0)
    x = _load_x(i, xp_ref, xs_ref, pe_ref)
    g1 = mod_ref[:, 2 * D:3 * D]
    sh2 = mod_ref[:, 3 * D:4 * D]
    sc2 = mod_ref[:, 4 * D:5 * D]
    ones = ones_ref[...]
    y = (yf_ref[...] + bf_ref[...]) + (yb_ref[...] + bb_ref[...])
    mu = _dot_sel(y, ones) * (1.0 / HEAD)
    yc = y - mu
    var = _dot_sel(yc * yc, ones) * (1.0 / HEAD)
    yn = yc * lax.rsqrt(var + GN_EPS) * gng_ref[...] + gnb_ref[...]
    gate = _dot(_sigmoid(zg_ref[...]).astype(BF16), gl_ref[...])
    y_rwkv = (yn * gate).astype(BF16)
    mix = _dot(y_rwkv, wo_ref[0:D_RWKV, :]) + _dot(yc_ref[...].astype(BF16), wo_ref[D_RWKV:, :])
    x1 = x + g1 * mix
    x1_ref[...] = x1
    h2 = _rms(x1, n2_ref[...]) * (1.0 + sc2) + sh2
    h2_ref[...] = h2.astype(BF16)

    logit_ref[...] = _dot_hp(h2, rw_ref[...]) + rb_ref[...]


def _route(logits):
    lane = lax.broadcasted_iota(jnp.int32, logits.shape, 1)
    lanef = lane.astype(F32)
    neg = jnp.float32(-1e30)
    big = jnp.float32(1e9)
    gmask = lane < N_GROUPS
    gl = jnp.where(gmask, logits, neg)
    ge = jnp.where(gmask, jnp.exp(gl - jnp.max(gl, axis=-1, keepdims=True)), 0.0)
    gprob = ge / jnp.sum(ge, axis=-1, keepdims=True)
    gp = jnp.max(gprob, axis=-1, keepdims=True)
    gidx = jnp.min(jnp.where(jnp.logical_and(gmask, gprob == gp), lanef, big), axis=-1, keepdims=True)
    egrp = jnp.floor((lanef - float(E_LANE0)) * (1.0 / N_EXP_PER_GROUP))
    emask = jnp.logical_and(jnp.logical_and(lane >= E_LANE0, lane < E_LANE0 + N_EXPERTS), egrp == gidx)
    el = jnp.where(emask, logits, neg)
    ee = jnp.where(emask, jnp.exp(el - jnp.max(el, axis=-1, keepdims=True)), 0.0)
    ep = ee / jnp.sum(ee, axis=-1, keepdims=True)
    m1 = jnp.max(jnp.where(emask, ep, -1.0), axis=-1, keepdims=True)
    i1 = jnp.min(jnp.where(jnp.logical_and(emask, ep == m1), lanef, big), axis=-1, keepdims=True)
    mask2 = jnp.logical_and(emask, lanef != i1)
    m2 = jnp.max(jnp.where(mask2, ep, -1.0), axis=-1, keepdims=True)
    i2 = jnp.min(jnp.where(jnp.logical_and(mask2, ep == m2), lanef, big), axis=-1, keepdims=True)
    den = m1 + m2
    return (jnp.where(lanef == i1, gp * (m1 / den), 0.0)
            + jnp.where(lanef == i2, gp * (m2 / den), 0.0)
            + jnp.where(lane == GROUP_LANE, gidx, 0.0))


def _outproj(xp, xs, pe, mod3, yf, yb, bonus, zg, yconv, gn_g, gn_b, gate_bf, w_out_bf, norm2_g,
             router_w, router_b, ones512):
    tile = lambda n: pl.BlockSpec((TM, n), lambda i: (i, 0))
    const = lambda shape: pl.BlockSpec(shape, lambda i: (0,) * len(shape))
    return pl.pallas_call(
        _outproj_kernel,
        grid=(N_TILES,),
        in_specs=_x_specs() + [
            pl.BlockSpec((None, 1, N_MOD * D), lambda i: (_mod_row(i), 0, 0)),
            tile(D_RWKV), tile(D_RWKV),
            pl.BlockSpec((None, TM, D_RWKV), lambda i: (0, i, 0)),
            pl.BlockSpec((None, TM, D_RWKV), lambda i: (1, i, 0)),
            tile(LORA_G), tile(D_CONV),
            const((1, D_RWKV)), const((1, D_RWKV)), const((LORA_G, D_RWKV)), const((D, D)),
            const((1, D)), const((D, ROUTE_LANES)), const((1, ROUTE_LANES)),
            const((D_RWKV, D_RWKV))],
        out_specs=[tile(D), tile(D), tile(ROUTE_LANES)],
        out_shape=[jax.ShapeDtypeStruct((NT, D), F32), jax.ShapeDtypeStruct((NT, D), BF16),
                   jax.ShapeDtypeStruct((NT, ROUTE_LANES), F32)],
        compiler_params=_cp(("parallel",)),
        name="outproj_router",
    )(xp, xs, pe, mod3, yf, yb, bonus, bonus, zg, yconv, gn_g, gn_b, gate_bf, w_out_bf, norm2_g,
      router_w, router_b, ones512)


def _plan_kernel(logit_ref, triu_ref, comb_ref, drow_ref, dcol_ref, start_ref, ntile_ref):
    comb = _route(logit_ref[...])
    comb_ref[...] = comb
    gidx = comb.T[GROUP_LANE:GROUP_LANE + 1, :]
    grow = lax.broadcasted_iota(jnp.int32, (SUB, 1), 0)
    growf = grow.astype(F32)
    onehot = jnp.where(jnp.logical_and(gidx == growf, grow < N_GROUPS), 1.0, 0.0)
    before = _dot(onehot.astype(BF16), triu_ref[...])
    count = jnp.sum(onehot, axis=-1, keepdims=True)
    padded = jnp.floor((count + (SEG - 1.0)) * (1.0 / SEG)) * SEG
    start = jnp.zeros((SUB, 1), F32)
    for g in range(N_GROUPS - 1):
        start = start + jnp.where(grow > g, padded[g:g + 1, :], 0.0)
    dest = jnp.sum(onehot * (start + before), axis=0, keepdims=True)
    drow_ref[...] = jnp.broadcast_to(dest, (SUB, TM_MOE))
    dcol_ref[...] = jnp.broadcast_to(dest, (LANES, TM_MOE)).T
    start_ref[...] = jnp.broadcast_to(start * (1.0 / SEG), (SUB, LANES)).astype(jnp.int32)
    ntile_ref[...] = jnp.broadcast_to(padded * (1.0 / SEG), (SUB, LANES)).astype(jnp.int32)


def _plan(logits, triu):
    seg_shape = jax.ShapeDtypeStruct((N_MOE_TILES, SUB, LANES), jnp.int32)
    seg_spec = pl.BlockSpec((None, SUB, LANES), lambda t: (t, 0, 0))
    tok_spec = pl.BlockSpec((TM_MOE, LANES), lambda t: (t, 0))
    return pl.pallas_call(
        _plan_kernel,
        grid=(N_MOE_TILES,),
        in_specs=[tok_spec, pl.BlockSpec((TM_MOE, TM_MOE), lambda t: (0, 0))],
        out_specs=[tok_spec, pl.BlockSpec((None, SUB, TM_MOE), lambda t: (t, 0, 0)), tok_spec,
                   seg_spec, seg_spec],
        out_shape=[jax.ShapeDtypeStruct((NT, ROUTE_LANES), F32), jax.ShapeDtypeStruct((N_MOE_TILES, SUB, TM_MOE), F32),
                   jax.ShapeDtypeStruct((NT, LANES), F32), seg_shape, seg_shape],
        compiler_params=_cp(("parallel",)),
        name="moe_plan",
    )(logits, triu)


def _moe_kernel(start_ref, ntile_ref, h_ref, comb_ref, drow_ref, dcol_ref, wg_ref, wu_ref, wd_ref,
                o_ref, xs, cws, ys):
    q = pl.program_id(1)
    g = q // (N_EXP_PER_GROUP // EXP_PER_STEP)
    tiles = range(TILES_PER_PASS)
    toks = [slice(j * TM_MOE, (j + 1) * TM_MOE) for j in tiles]

    @pl.when(q == 0)
    def _sort_in():
        slot = lax.broadcasted_iota(jnp.int32, (MOE_ROWS, TM_MOE), 0).astype(F32)
        for j in tiles:
            perm = jnp.where(slot == drow_ref[j, 0:1, :], 1.0, 0.0).astype(BF16)
            xs[j] = _dot(perm, h_ref[toks[j], :]).astype(BF16)
            ch, cl = _split2(comb_ref[toks[j], :])
            cws[j] = _dot(perm, ch) + _dot(perm, cl)
            ys[j] = jnp.zeros((MOE_ROWS, D), F32)

    segs = [(pl.program_id(0) * TILES_PER_PASS + j) * N_GROUPS + g for j in tiles]
    firsts = [start_ref[seg] for seg in segs]

    def visit(windows, n_rows):
        rows = [(j, pl.ds(pl.multiple_of(row0 * SEG, SEG), n_rows)) for j, row0 in windows]
        x = jnp.concatenate([xs[j, r, :] for j, r in rows], axis=0)
        cw_all = jnp.concatenate([cws[j, r, :] for j, r in rows], axis=0)
        lane = lax.broadcasted_iota(jnp.int32, cw_all.shape, 1)
        acc = jnp.concatenate([ys[j, r, :] for j, r in rows], axis=0)
        for k in range(EXP_PER_STEP):
            e_lane = q * EXP_PER_STEP + k + E_LANE0
            cw = jnp.sum(jnp.where(lane == e_lane, cw_all, 0.0), axis=-1, keepdims=True)
            hid = (_silu(_dot(x, wg_ref[k].astype(BF16))) * _dot(x, wu_ref[k].astype(BF16)) * cw).astype(BF16)
            acc = acc + _dot(hid, wd_ref[k].astype(BF16))
        for i, (j, r) in enumerate(rows):
            ys[j, r, :] = acc[i * n_rows:(i + 1) * n_rows]

    needs = [ntile_ref[seg] for seg in segs]
    window = lambda j, size: (j, jnp.minimum(firsts[j], MOE_ROWS // SEG - size))

    def size_class(need, sizes, guard, run):
        lo = 0
        for size in sizes:
            pl.when(jnp.logical_and(guard, jnp.logical_and(need > lo, need <= size)))(
                functools.partial(run, size))
            lo = size

    need_all = functools.reduce(jnp.maximum, needs)
    together = need_all <= MOE_SHARED_SIZES[-1]
    size_class(need_all, MOE_SHARED_SIZES, together,
               lambda size: visit([window(j, size) for j in tiles], size * SEG))
    for j in tiles:
        size_class(needs[j], MOE_SIZES, jnp.logical_not(together),
                   lambda size, j=j: visit([window(j, size)], size * SEG))

    @pl.when(q == N_EXPERTS // EXP_PER_STEP - 1)
    def _sort_out():
        slot = lax.broadcasted_iota(jnp.int32, (TM_MOE, MOE_ROWS), 1).astype(F32)
        for j in tiles:
            perm_t = jnp.where(slot == dcol_ref[toks[j], 0:1], 1.0, 0.0).astype(BF16)
            o_ref[toks[j], :] = _dot(perm_t, ys[j].astype(BF16)).astype(BF16)


def _moe(h2, comb, drow, dcol, seg_start, seg_ntile, wg, wu, wd):
    rows = TILES_PER_PASS * TM_MOE
    once = dict(pipeline_mode=pl.Buffered(1))
    weights = lambda shape: pl.BlockSpec(shape, lambda p, q, s, n: (q, 0, 0))
    grid_spec = pltpu.PrefetchScalarGridSpec(
        num_scalar_prefetch=2,
        grid=(N_MOE_TILES // TILES_PER_PASS, N_EXPERTS // EXP_PER_STEP),
        in_specs=[pl.BlockSpec((rows, D), lambda p, q, s, n: (p, 0), **once),
                  pl.BlockSpec((rows, ROUTE_LANES), lambda p, q, s, n: (p, 0), **once),
                  pl.BlockSpec((TILES_PER_PASS, SUB, TM_MOE), lambda p, q, s, n: (p, 0, 0), **once),
                  pl.BlockSpec((rows, LANES), lambda p, q, s, n: (p, 0), **once),
                  weights((EXP_PER_STEP, D, D_EXPERT)), weights((EXP_PER_STEP, D, D_EXPERT)),
                  weights((EXP_PER_STEP, D_EXPERT, D))],
        out_specs=pl.BlockSpec((rows, D), lambda p, q, s, n: (p, 0)),
        scratch_shapes=[pltpu.VMEM((TILES_PER_PASS, MOE_ROWS, D), BF16),
                        pltpu.VMEM((TILES_PER_PASS, MOE_ROWS, ROUTE_LANES), F32),
                        pltpu.VMEM((TILES_PER_PASS, MOE_ROWS, D), F32)])
    return pl.pallas_call(
        _moe_kernel,
        grid_spec=grid_spec,
        out_shape=jax.ShapeDtypeStruct((NT, D), BF16),
        compiler_params=_cp(("parallel", "arbitrary")),
        name="moe_experts",
    )(seg_start, seg_ntile, h2, comb, drow, dcol, wg, wu, wd)


def _final_kernel(x1_ref, moe_ref, mod_ref, g_ref, o_ref):
    g2 = mod_ref[:, 5 * D:6 * D]
    x2 = x1_ref[...] + g2 * moe_ref[...].astype(F32)
    o_ref[...] = _rms(x2, g_ref[...])


def _final(x1, moe, mod3, final_g, tile0, n_tiles, out_block, name, merge=1):
    rows = merge * TM
    tile = pl.BlockSpec((rows, D), lambda i: (tile0 // merge + i, 0))
    return pl.pallas_call(
        _final_kernel,
        grid=(n_tiles // merge,),
        in_specs=[tile, tile,
                  pl.BlockSpec((None, 1, N_MOD * D), lambda i: (_mod_row(tile0 + i * merge), 0, 0)),
                  pl.BlockSpec((1, D), lambda i: (0, 0))],
        out_specs=pl.BlockSpec((rows, D), lambda i: (out_block(tile0 // merge + i), 0)),
        out_shape=jax.ShapeDtypeStruct((n_tiles * TM, D), F32),
        compiler_params=_cp(("parallel",)),
        name=name,
    )(x1, moe, mod3, final_g)


def _pos_embed(rows):
    t = np.arange(rows * GRID_W)
    row = (t // GRID_W).astype(np.float32)
    col = (t % GRID_W).astype(np.float32)
    quarter = D // 4
    freqs = (1.0 / (10000.0 ** (np.arange(quarter, dtype=np.float32) / quarter))).astype(np.float32)
    ang_r = row[:, None] * freqs[None, :]
    ang_c = col[:, None] * freqs[None, :]
    pe = np.concatenate([np.sin(ang_r), np.cos(ang_r), np.sin(ang_c), np.cos(ang_c)], axis=-1)
    return jnp.asarray(pe, F32)


def _selection_constants():
    ch = np.arange(D_RWKV)
    ones512 = (ch[:, None] // HEAD == ch[None, :] // HEAD).astype(np.float32)
    t = np.arange(TM)
    same_chunk = t[:, None] // CH == t[None, :] // CH
    cum_f = same_chunk & (t[None, :] <= t[:, None])
    cum_b = same_chunk & (t[None, :] >= t[:, None])
    sel8 = np.arange(NCH)[:, None] == t[None, :] // CH
    col = np.arange(HC)
    hm = col[:, None] // CH == ch[None, :] // HEAD
    bm = col[:, None] // CH == col[None, :] // CH
    tt, jj = np.arange(CH)[:, None], col[None, :] % CH
    masks = np.stack([jj < tt, jj <= tt, jj > tt, jj >= tt])
    eye = jj == tt
    bf = lambda x: jnp.asarray(x, BF16)
    f32 = lambda x: jnp.asarray(x, F32)
    pad = np.zeros((2 * SUB - NCH, TM), bool)
    sums = [np.concatenate([cum, same_chunk, sel8, pad]) for cum in (cum_f, cum_b)]
    prep_consts = (bf(ones512), bf(np.stack(sums)), bf(hm), bf(bm),
                   f32(masks.reshape(2, 2, CH, HC)), f32(eye))
    scan_consts = (bf(hm), bf(ones512))
    return prep_consts, scan_consts


def kernel(x_prompt, x_sample, state_rwkv, c, c_ctx, ada_w, ada_b, norm1_g, w_in, tshift_mu, decay_w0, decay_lora_b, iclr_a0, iclr_lora_b, key_k, key_a, bonus_r_k, gate_lora_b, gn_g, gn_b, conv_dw_w, conv_dw_b, conv_ln_g, conv_ln_b, w_out, norm2_g, router_group_w, router_group_b, router_expert_w, router_expert_b, expert_w_gate, expert_w_up, expert_w_down, final_norm_g):
    assert x_prompt.shape == (N_CTX_SEQ, T_CTX, D) and x_sample.shape == (N_LAT_SEQ, T_LAT, D)
    assert ada_w.shape[0] == 1, "one trunk layer"
    prep_consts, scan_consts = _selection_constants()
    ones512 = prep_consts[0]
    xp = x_prompt.reshape(N_CTX_SEQ * T_CTX, D)
    xs = x_sample.reshape(N_LAT_SEQ * T_LAT, D)
    pe = _pos_embed(T_LAT // GRID_W)

    cond8 = jnp.concatenate([c_ctx[None, :], c, jnp.zeros((8 - 1 - N_LAT_SEQ, D), F32)], axis=0)
    mod3 = _adaln(cond8, ada_w[0], ada_b[0][None, :]).reshape(8, 1, N_MOD * D)

    zs, zg, zc = _inproj(xp, xs, pe, mod3, norm1_g, w_in[0].astype(BF16))

    zero = jnp.zeros((2, LORA, D_RWKV), F32)
    lora2 = jnp.concatenate([jnp.concatenate([decay_lora_b[0], zero], axis=2),
                             jnp.concatenate([zero, iclr_lora_b[0]], axis=2)], axis=1)
    vec = lambda p: p.reshape(2, 1, -1)
    prep_out = _prep(zs, vec(tshift_mu[0]), vec(decay_w0[0]), vec(iclr_a0[0]), lora2,
                     vec(key_k[0]), vec(key_a[0]), vec(bonus_r_k[0]), *prep_consts)
    bonus = prep_out[10]

    s0_lat = state_rwkv[:, 0].transpose(1, 0, 4, 2, 3).reshape(2, N_LAT_SEQ, HEAD, D_RWKV)
    yf, yb, s_fin = _scan(prep_out, s0_lat, scan_consts)

    yconv = _conv(zc, jnp.concatenate([conv_dw_w[0], jnp.zeros((1, D_CONV), F32)], axis=0),
                  conv_dw_b, conv_ln_g, conv_ln_b)

    router_w = jnp.concatenate([router_group_w[0], router_expert_w[0],
                                jnp.zeros((D, ROUTE_LANES - N_GROUPS - N_EXPERTS), F32)], axis=1)
    router_b = jnp.concatenate([router_group_b[0], router_expert_b[0],
                                jnp.zeros((ROUTE_LANES - N_GROUPS - N_EXPERTS,), F32)])[None, :]
    x1, h2, logits = _outproj(xp, xs, pe, mod3, yf, yb, bonus, zg, yconv,
                              gn_g, gn_b, gate_lora_b[0].astype(BF16), w_out[0].astype(BF16), norm2_g,
                              router_w, router_b, ones512)

    tok = np.arange(TM_MOE)
    triu = jnp.asarray(tok[:, None] < tok[None, :], BF16)
    comb, drow, dcol, seg_start, seg_ntile = _plan(logits, triu)
    seg_start = seg_start[:, :N_GROUPS, 0].reshape(-1)
    seg_ntile = seg_ntile[:, :N_GROUPS, 0].reshape(-1)
    moe = _moe(h2, comb, drow, dcol, seg_start, seg_ntile,
               expert_w_gate[0], expert_w_up[0], expert_w_down[0])
    fg = final_norm_g[None, :]
    y_prompt = _final(x1, moe, mod3, fg, 0, N_CTX_TILES, lambda i: i, "final_norm_ctx", merge=2)
    y_sample = _final(x1, moe, mod3, fg, N_CTX_TILES, N_LAT_TILES, _xs_block, "final_norm_lat")
    y_prompt = y_prompt.reshape(N_CTX_SEQ, T_CTX, D)
    y_sample = y_sample.reshape(N_LAT_SEQ, T_LAT, D)
    new_state = s_fin[:, None].astype(state_rwkv.dtype)
    return (y_prompt, y_sample, new_state)
```

```python
import functools

import numpy as np
import jax
import jax.numpy as jnp
from jax import lax
from jax.experimental import pallas as pl
from jax.experimental.pallas import tpu as pltpu

F32 = jnp.float32
BF16 = jnp.bfloat16

D = 1024
N_CTX_SEQ = 16
T_CTX = 256
N_LAT_SEQ = 2
T_LAT = 1024
TM = 256
N_CTX_TILES = N_CTX_SEQ * T_CTX // TM
LAT_CHUNKS = T_LAT // TM
N_LAT_TILES = N_LAT_SEQ * LAT_CHUNKS
N_TILES = N_CTX_TILES + N_LAT_TILES
NT = N_TILES * TM
GRID_W = 64
D_RWKV = 512
D_CONV = 512
HEAD = 64
N_HEADS = 8
CONV_W = 31
CONV_PAD = CONV_W // 2
LORA = 64
LORA_G = 128
SHIFT_COLS = 3 * D_RWKV + 2 * LORA
N_GROUPS = 4
N_EXP_PER_GROUP = 8
N_EXPERTS = 32
D_EXPERT = 256
N_MOD = 6
RMS_EPS = 1e-6
LN_EPS = 1e-5
GN_EPS = 64e-5
KK_EPS = 1e-12
LANES = 128
SUB = 8
CH = 32
NCH = TM // CH
HC = N_HEADS * CH
DOUBLINGS = 4
ROUTE_LANES = 128
E_LANE0 = N_GROUPS
GROUP_LANE = 36
TM_MOE = 1024
SEG = 64
EXP_PER_STEP = 2
TILES_PER_PASS = 2
MOE_ROWS = TM_MOE + N_GROUPS * SEG
MOE_SHARED_SIZES = (2, 4, 6)
MOE_SIZES = (8, 12, MOE_ROWS // SEG)
N_MOE_TILES = NT // TM_MOE
VMEM_LIMIT = 56 * 1024 * 1024


def _cp(sem, flags=None):
    return pltpu.CompilerParams(dimension_semantics=sem, vmem_limit_bytes=VMEM_LIMIT, flags=flags)


def _split2(a):
    hi = a.astype(BF16)
    lo = (a - hi.astype(F32)).astype(BF16)
    return hi, lo


def _dot(a, b):
    return jnp.dot(a, b, preferred_element_type=F32)


def _dot_hp(a, b):
    ah, al = _split2(a)
    bh, bl = _split2(b)
    return _dot(ah, bh) + _dot(ah, bl) + _dot(al, bh)


def _dot_sel(a, sel):
    h, l = _split2(a)
    return _dot(h, sel) + _dot(l, sel)


def _sel_dot(sel, a):
    h, l = _split2(a)
    return _dot(sel, h) + _dot(sel, l)


def _sigmoid(x):
    return 1.0 / (1.0 + jnp.exp(-x))


def _silu(x):
    return x * _sigmoid(x)


def _lat_js(i):
    il = jnp.maximum(i - N_CTX_TILES, 0)
    return il // N_LAT_SEQ, il % N_LAT_SEQ


def _xp_block(i):
    return jnp.minimum(i, N_CTX_TILES - 1)


def _xs_block(i):
    j, s = _lat_js(i)
    return s * LAT_CHUNKS + j


def _pe_block(i):
    j, _ = _lat_js(i)
    return j


def _mod_row(i):
    _, s = _lat_js(i)
    return jnp.where(i < N_CTX_TILES, 0, 1 + s)


def _first_last(i):
    j, _ = _lat_js(i)
    is_ctx = i < N_CTX_TILES
    first = jnp.logical_or(is_ctx, j == 0)
    last = jnp.logical_or(is_ctx, j == LAT_CHUNKS - 1)
    return first, last


def _prev_tile(i):
    first, _ = _first_last(i)
    return jnp.where(first, i, i - N_LAT_SEQ)


def _next_tile(i):
    _, last = _first_last(i)
    return jnp.where(last, i, i + N_LAT_SEQ)


def _adaln_kernel(c_ref, w_ref, b_ref, o_ref):
    c = c_ref[...]
    o_ref[...] = _dot_hp(_silu(c), w_ref[...]) + b_ref[...]


def _adaln(cond8, ada_w, ada_b):
    tn = 1536
    n = ada_w.shape[1]
    return pl.pallas_call(
        _adaln_kernel,
        grid=(n // tn,),
        in_specs=[pl.BlockSpec((8, D), lambda j: (0, 0)),
                  pl.BlockSpec((D, tn), lambda j: (0, j)),
                  pl.BlockSpec((1, tn), lambda j: (0, j))],
        out_specs=pl.BlockSpec((8, tn), lambda j: (0, j)),
        out_shape=jax.ShapeDtypeStruct((8, n), F32),
        compiler_params=_cp(("parallel",)),
        name="adaln",
    )(cond8, ada_w, ada_b)


def _load_x(i, xp_ref, xs_ref, pe_ref):
    f = (i >= N_CTX_TILES).astype(F32)
    return xp_ref[...] * (1.0 - f) + (xs_ref[...] + pe_ref[...]) * f


def _x_specs():
    return [pl.BlockSpec((TM, D), lambda i: (_xp_block(i), 0)),
            pl.BlockSpec((TM, D), lambda i: (_xs_block(i), 0)),
            pl.BlockSpec((TM, D), lambda i: (_pe_block(i), 0))]


def _rms(x, g):
    return x * lax.rsqrt(jnp.mean(x * x, axis=-1, keepdims=True) + RMS_EPS) * g


def _inproj_kernel(xp_ref, xs_ref, pe_ref, mod_ref, g_ref, w_ref, zs_ref, zg_ref, zc_ref):
    i = pl.program_id(0)
    x = _load_x(i, xp_ref, xs_ref, pe_ref)
    sh1 = mod_ref[:, 0:D]
    sc1 = mod_ref[:, D:2 * D]
    h = (_rms(x, g_ref[...]) * (1.0 + sc1) + sh1).astype(BF16)
    zs_ref[...] = _dot(h, w_ref[:, 0:SHIFT_COLS])
    zg_ref[...] = _dot(h, w_ref[:, SHIFT_COLS:SHIFT_COLS + LORA_G])
    zc_ref[...] = _dot(h, w_ref[:, SHIFT_COLS + LORA_G:])


def _inproj(xp, xs, pe, mod3, norm1_g, w_in_bf):
    in_cols = w_in_bf.shape[1]
    return pl.pallas_call(
        _inproj_kernel,
        grid=(N_TILES,),
        in_specs=_x_specs() + [
            pl.BlockSpec((None, 1, N_MOD * D), lambda i: (_mod_row(i), 0, 0)),
            pl.BlockSpec((1, D), lambda i: (0, 0)),
            pl.BlockSpec((D, in_cols), lambda i: (0, 0))],
        out_specs=[pl.BlockSpec((TM, SHIFT_COLS), lambda i: (i, 0)),
                   pl.BlockSpec((TM, LORA_G), lambda i: (i, 0)),
                   pl.BlockSpec((TM, 2 * D_CONV), lambda i: (i, 0))],
        out_shape=[jax.ShapeDtypeStruct((NT, SHIFT_COLS), F32),
                   jax.ShapeDtypeStruct((NT, LORA_G), F32),
                   jax.ShapeDtypeStruct((NT, 2 * D_CONV), F32)],
        compiler_params=_cp(("parallel",)),
        name="inproj",
    )(xp, xs, pe, mod3, norm1_g, w_in_bf)


NT_DIMS = (((1,), (1,)), ((), ()))


def _tile_rows(x):
    return jnp.concatenate([x] * N_HEADS, axis=0)


def _prep_kernel(zs_ref, prev_ref, next_ref, mu_ref, w0_ref, a0_ref, lora_ref, kk_ref_, ka_ref,
                 rk_ref, ones_ref, cum_ref, hm_ref, bm_ref, msk_ref, eye_ref,
                 at_out, rt_out, bh_out, kh_out, v_out, t_out, aak_out, arb_out, ark_out, gc_out, bon_out):
    i = pl.program_id(0)
    first, last = _first_last(i)
    cur = zs_ref[...]
    prow = prev_ref[7:8, :] * (1.0 - first.astype(F32))
    nrow = next_ref[0:1, :] * (1.0 - last.astype(F32))
    rows = lax.broadcasted_iota(jnp.int32, (TM, 1), 0)
    shifted = (jnp.where(rows == 0, prow, pltpu.roll(cur, 1, axis=0)),
               jnp.where(rows == TM - 1, nrow, pltpu.roll(cur, TM - 1, axis=0)))
    ones = ones_ref[...]
    lane = lax.broadcasted_iota(jnp.int32, (TM, 2 * LORA), 1)
    scaled = []
    for d in range(2):
        xs = cur + (shifted[d] - cur) * mu_ref[d]
        r = xs[:, 0:D_RWKV]
        k = xs[:, D_RWKV:2 * D_RWKV]
        v = xs[:, 2 * D_RWKV:3 * D_RWKV]
        z2 = xs[:, 3 * D_RWKV:SHIFT_COLS]
        lin = jnp.where(lane < LORA, jnp.tanh(z2), z2)
        lo = _dot_hp(lin, lora_ref[d])
        u = -(w0_ref[d] + lo[:, 0:D_RWKV])
        softplus = jnp.maximum(u, 0.0) + jnp.log(1.0 + jnp.exp(-jnp.abs(u)))
        w_log = -softplus - 0.5
        lw = -jnp.exp(w_log)
        a = _sigmoid(a0_ref[d] + lo[:, D_RWKV:])
        kx = k * kk_ref_[d]
        kk = kx * lax.rsqrt(jnp.maximum(_dot_sel(kx * kx, ones), KK_EPS * KK_EPS))
        k2 = k * (1.0 + (a - 1.0) * ka_ref[d])
        bvec = kk * a
        bon_out[d] = _dot_sel(r * k2 * rk_ref[d], ones) * v

        sums = _sel_dot(cum_ref[d], lw)
        lg = sums[0:TM]
        tot = sums[TM:2 * TM]
        e_tail = jnp.exp(tot - lg)
        e_inv = jnp.exp(-lg)
        at = (-(kk * jnp.exp(lg - lw))).astype(BF16)
        bt = (bvec * e_inv).astype(BF16)
        kt = (k2 * e_inv).astype(BF16)
        rt = (r * jnp.exp(lg)).astype(BF16)
        at_out[d] = at
        rt_out[d] = rt
        bh_out[d] = (bvec * e_tail).astype(BF16)
        kh_out[d] = (k2 * e_tail).astype(BF16)
        v_out[d] = v.astype(BF16)
        gc_out[d] = jnp.exp(sums[2 * TM:2 * TM + NCH])
        scaled.append((at, bt, kt, rt))

    hm = hm_ref[...]
    bm = bm_ref[...]

    def expand(z):
        return _tile_rows(z.astype(BF16)) * bm

    chunks = [(d, slice(c * CH, (c + 1) * CH)) for d in range(2) for c in range(NCH)]
    pws = []
    for d, rows in chunks:
        at, bt, kt, rt = scaled[d]
        ar = jnp.concatenate([at[rows], rt[rows]], axis=0)
        bk = jnp.concatenate([_tile_rows(bt[rows]) * hm, _tile_rows(kt[rows]) * hm], axis=0)
        p1 = lax.dot_general(ar, bk, NT_DIMS, preferred_element_type=F32)
        m_strict, m_incl = msk_ref[d, 0], msk_ref[d, 1]
        pws.append(p1[0:CH, 0:HC] * m_strict)
        aak_out[d, rows, :] = (p1[0:CH, HC:] * m_strict).astype(BF16)
        arb_out[d, rows, :] = (p1[CH:, 0:HC] * m_incl).astype(BF16)
        ark_out[d, rows, :] = (p1[CH:, HC:] * m_incl).astype(BF16)
    tms = [eye_ref[...] + pw for pw in pws]
    pws = [_dot(pw.astype(BF16), expand(pw)) for pw in pws]
    for k in range(1, DOUBLINGS + 1):
        if k < DOUBLINGS:
            prods = [_dot(jnp.concatenate([pw, tm], axis=0).astype(BF16), expand(pw))
                     for pw, tm in zip(pws, tms)]
            pws = [p[0:CH] for p in prods]
            tms = [tm + p[CH:] for tm, p in zip(tms, prods)]
        else:
            tms = [tm + _dot(tm.astype(BF16), expand(pw)) for tm, pw in zip(tms, pws)]
    for (d, rows), tm in zip(chunks, tms):
        t_out[d, rows, :] = tm.astype(BF16)


def _prep(zs, mu, w0, a0, lora2, key_k, key_a, r_k, ones512, cum2, hm, bm, masks, eye):
    full = lambda a: pl.BlockSpec(a.shape, lambda i: (0,) * a.ndim)
    out_spec = pl.BlockSpec((2, TM, D_RWKV), lambda i: (0, i, 0))
    bf_shape = jax.ShapeDtypeStruct((2, NT, D_RWKV), BF16)
    tall_spec = pl.BlockSpec((2, TM, HC), lambda i: (0, i, 0))
    rows8 = TM // 8
    params = (mu, w0, a0, lora2, key_k, key_a, r_k, ones512, cum2, hm, bm, masks, eye)
    return pl.pallas_call(
        _prep_kernel,
        grid=(N_TILES,),
        in_specs=[pl.BlockSpec((TM, SHIFT_COLS), lambda i: (i, 0)),
                  pl.BlockSpec((8, SHIFT_COLS), lambda i: (_prev_tile(i) * rows8 + rows8 - 1, 0)),
                  pl.BlockSpec((8, SHIFT_COLS), lambda i: (_next_tile(i) * rows8, 0))]
                 + [full(p) for p in params],
        out_specs=[out_spec] * 5 + [tall_spec] * 4 + [
            pl.BlockSpec((2, NCH, D_RWKV), lambda i: (0, i, 0)), out_spec],
        out_shape=[bf_shape] * 5 + [jax.ShapeDtypeStruct((2, NT, HC), BF16)] * 4 + [
            jax.ShapeDtypeStruct((2, N_TILES * NCH, D_RWKV), F32),
            jax.ShapeDtypeStruct((2, NT, D_RWKV), F32)],
        compiler_params=_cp(("parallel",)),
        name="rwkv_prep",
    )(zs, zs, zs, *params)


SCAN_TILES = 4
CTX_STEPS = N_CTX_TILES // SCAN_TILES
LAT_STEPS = N_LAT_TILES // SCAN_TILES
N_SCAN_STEPS = CTX_STEPS + LAT_STEPS
N_SLOTS = 2 * SCAN_TILES


def _scan_block(step, d):
    jl = step - CTX_STEPS
    j = jnp.where(d == 0, jl, LAT_STEPS - 1 - jl)
    return jnp.where(step >= CTX_STEPS, CTX_STEPS + j, step)


def _scan_state_block(step):
    return jnp.minimum(step, CTX_STEPS - 1)


def _chunk_kernel(*refs):
    (atf, rtf, bhf, khf, vf, tf, aakf, arbf, arkf, gcf, atb, rtb, bhb, khb, vb, tb, aakb, arbb, arkb, gcb,
     s0_ref, hm_ref, bd_ref, yf_ref, yb_ref, sout_ref, mw) = refs
    step = pl.program_id(0)
    in_refs = ((atf, rtf, bhf, khf, vf, tf, aakf, arbf, arkf, gcf),
               (atb, rtb, bhb, khb, vb, tb, aakb, arbb, arkb, gcb))
    y_refs = (yf_ref, yb_ref)
    bd = bd_ref[...]
    hm = hm_ref[...]
    crow = lax.broadcasted_iota(jnp.int32, (NCH, 1), 0)
    heads = range(N_HEADS)

    def advance(chains):
        rows_of = lambda tile, cc: pl.ds(pl.multiple_of(tile * TM + cc * CH, CH), CH)
        chains = [(dl, slot, tile, cc, rows_of(tile, cc)) for dl, slot, tile, cc in chains]
        ld = lambda idx, dl, rows: in_refs[dl][idx][rows, :]
        m0s = [mw[slot] for _, slot, _, _, _ in chains]
        xy0s = [_dot(jnp.concatenate([ld(0, dl, rows), ld(1, dl, rows)], axis=0),
                     _tile_rows(m0.astype(BF16)) * bd)
                for (dl, _, _, _, rows), m0 in zip(chains, m0s)]
        vvs = [ld(4, dl, rows) for dl, _, _, _, rows in chains]
        avs = [_dot(jnp.concatenate([ld(6, dl, rows), ld(8, dl, rows)], axis=0), _tile_rows(vv) * hm)
               for (dl, _, _, _, rows), vv in zip(chains, vvs)]
        ubs = [_dot(ld(5, dl, rows), _tile_rows((xy0[0:CH] + av[0:CH]).astype(BF16)) * hm).astype(BF16)
               for (dl, _, _, _, rows), xy0, av in zip(chains, xy0s, avs)]
        for (dl, _, _, _, rows), xy0, av, ub in zip(chains, xy0s, avs, ubs):
            y_refs[dl][rows, :] = xy0[CH:] + av[CH:] + _dot(ld(7, dl, rows), _tile_rows(ub) * hm)
        for (dl, slot, tile, cc, rows), m0, ub, vv in zip(chains, m0s, ubs, vvs):
            gc_tile = in_refs[dl][9][pl.ds(pl.multiple_of(tile * NCH, NCH), NCH), :]
            gcrow = jnp.sum(jnp.where(crow == cc, gc_tile, 0.0), axis=0, keepdims=True)
            pad = jnp.zeros((LANES - 2 * CH - SUB, D_RWKV), F32)
            stack = jnp.concatenate([ld(2, dl, rows).astype(F32), ld(3, dl, rows).astype(F32),
                                     jnp.broadcast_to(gcrow, (SUB, D_RWKV)), pad], axis=0)
            stack_t = stack.T
            head = lambda h: stack_t[h * HEAD:(h + 1) * HEAD, :]
            bk_t = jnp.concatenate([head(h)[:, 0:2 * CH] for h in heads], axis=1)
            decay = jnp.concatenate([jnp.broadcast_to(head(h)[:, 2 * CH:2 * CH + 1], (HEAD, HEAD))
                                     for h in heads], axis=1)
            uv = _tile_rows(jnp.concatenate([ub, vv], axis=0)) * bd
            mw[slot] = m0 * decay + _dot(bk_t.astype(BF16), uv)

    @pl.when(step < CTX_STEPS)
    def _context():
        mw[...] = jnp.zeros(mw.shape, F32)

        def body(q, carry):
            advance([(dl, dl * SCAN_TILES + sl, jnp.int32(sl), q if dl == 0 else NCH - 1 - q)
                     for dl in range(2) for sl in range(SCAN_TILES)])
            return carry

        lax.fori_loop(0, NCH, body, 0)
        for dl in range(2):
            for sl in range(SCAN_TILES):
                mt = mw[dl * SCAN_TILES + sl]
                by_hv = jnp.concatenate([mt, jnp.zeros_like(mt)], axis=0).T
                for h in range(N_HEADS):
                    sout_ref[sl, dl, h] = by_hv[h * HEAD:(h + 1) * HEAD, 0:HEAD]

    @pl.when(step == CTX_STEPS)
    def _load_state():
        for dl in range(2):
            for sl in range(N_LAT_SEQ):
                mw[dl * SCAN_TILES + sl] = s0_ref[dl, sl]

    @pl.when(step >= CTX_STEPS)
    def _latent():
        n_q = (SCAN_TILES // N_LAT_SEQ) * NCH

        def body(q, carry):
            chains = []
            for dl in range(2):
                qq = q if dl == 0 else n_q - 1 - q
                for sl in range(N_LAT_SEQ):
                    chains.append((dl, dl * SCAN_TILES + sl, (qq // NCH) * N_LAT_SEQ + sl, qq % NCH))
            advance(chains)
            return carry

        lax.fori_loop(0, n_q, body, 0)


def _scan(prep_out, s0_lat, consts):
    rows_per_step = SCAN_TILES * TM

    def row_spec(d, width):
        return pl.BlockSpec((None, rows_per_step, width), lambda s: (d, _scan_block(s, d), 0))

    def gc_spec(d):
        return pl.BlockSpec((None, SCAN_TILES * NCH, D_RWKV), lambda s: (d, _scan_block(s, d), 0))

    def y_spec(d):
        return pl.BlockSpec((rows_per_step, D_RWKV), lambda s: (_scan_block(s, d), 0))

    rows, gc = list(prep_out[:9]), prep_out[9]
    row_specs = lambda d: [row_spec(d, D_RWKV)] * 5 + [row_spec(d, HC)] * 4 + [gc_spec(d)]
    const = lambda a: pl.BlockSpec(a.shape, lambda s: (0,) * a.ndim)
    y_shape = jax.ShapeDtypeStruct((NT, D_RWKV), F32)
    return pl.pallas_call(
        _chunk_kernel,
        grid=(N_SCAN_STEPS,),
        in_specs=row_specs(0) + row_specs(1) + [const(s0_lat)] + [const(a) for a in consts],
        out_specs=[y_spec(0), y_spec(1),
                   pl.BlockSpec((SCAN_TILES, 2, N_HEADS, HEAD, HEAD),
                                lambda s: (_scan_state_block(s), 0, 0, 0, 0))],
        out_shape=[y_shape, y_shape,
                   jax.ShapeDtypeStruct((N_CTX_SEQ, 2, N_HEADS, HEAD, HEAD), F32)],
        scratch_shapes=[pltpu.VMEM((N_SLOTS, HEAD, D_RWKV), F32)],
        compiler_params=_cp(("arbitrary",)),
        name="rwkv_scan",
    )(*rows, gc, *rows, gc, s0_lat, *consts)


HALO = 16


def _glu(z):
    return z[:, 0:D_CONV] * _sigmoid(z[:, D_CONV:])


def _conv_kernel(cur_ref, prev_ref, next_ref, w_ref, b_ref, g_ref, beta_ref, o_ref, ext):
    i = pl.program_id(0)
    first, last = _first_last(i)
    n_ext = TM + 2 * HALO
    u = jnp.concatenate([_glu(prev_ref[...]) * (1.0 - first.astype(F32)), _glu(cur_ref[...]),
                         _glu(next_ref[...]) * (1.0 - last.astype(F32))], axis=0)
    ext[0] = u
    for b in range(1, SUB):
        ext[b] = pltpu.roll(u, n_ext - b, axis=0)
    acc = jnp.zeros((TM, D_CONV), F32)
    for j in range(CONV_W):
        off = HALO - CONV_PAD + j
        acc = acc + ext[off % SUB, off - off % SUB:off - off % SUB + TM, :] * w_ref[j:j + 1, :]
    h = acc + b_ref[...]
    mu = jnp.mean(h, axis=-1, keepdims=True)
    hc = h - mu
    var = jnp.mean(hc * hc, axis=-1, keepdims=True)
    y = hc * lax.rsqrt(var + LN_EPS) * g_ref[...] + beta_ref[...]
    o_ref[...] = _silu(y)


def _conv(zc, conv_w, conv_b, ln_g, ln_b):
    nh = TM // HALO
    vec = pl.BlockSpec((1, D_CONV), lambda i: (0, 0))
    return pl.pallas_call(
        _conv_kernel,
        grid=(N_TILES,),
        in_specs=[pl.BlockSpec((TM, 2 * D_CONV), lambda i: (i, 0)),
                  pl.BlockSpec((HALO, 2 * D_CONV), lambda i: (_prev_tile(i) * nh + nh - 1, 0)),
                  pl.BlockSpec((HALO, 2 * D_CONV), lambda i: (_next_tile(i) * nh, 0)),
                  pl.BlockSpec((CONV_W + 1, D_CONV), lambda i: (0, 0)), vec, vec, vec],
        out_specs=pl.BlockSpec((TM, D_CONV), lambda i: (i, 0)),
        out_shape=jax.ShapeDtypeStruct((NT, D_CONV), F32),
        scratch_shapes=[pltpu.VMEM((SUB, TM + 2 * HALO, D_CONV), F32)],
        compiler_params=_cp(("parallel",)),
        name="conv_module",
    )(zc, zc, zc, conv_w, conv_b, ln_g, ln_b)


def _outproj_kernel(xp_ref, xs_ref, pe_ref, mod_ref, yf_ref, yb_ref, bf_ref, bb_ref, zg_ref, yc_ref,
                    gng_ref, gnb_ref, gl_ref, wo_ref, n2_ref, rw_ref, rb_ref, ones_ref,
                    x1_ref, h2_ref, logit_ref):
    i = pl.program_id(0)
    x = _load_x(i, xp_ref, xs_ref, pe_ref)
    g1 = mod_ref[:, 2 * D:3 * D]
    sh2 = mod_ref[:, 3 * D:4 * D]
    sc2 = mod_ref[:, 4 * D:5 * D]
    ones = ones_ref[...]
    y = (yf_ref[...] + bf_ref[...]) + (yb_ref[...] + bb_ref[...])
    mu = _dot_sel(y, ones) * (1.0 / HEAD)
    yc = y - mu
    var = _dot_sel(yc * yc, ones) * (1.0 / HEAD)
    yn = yc * lax.rsqrt(var + GN_EPS) * gng_ref[...] + gnb_ref[...]
    gate = _dot(_sigmoid(zg_ref[...]).astype(BF16), gl_ref[...])
    y_rwkv = (yn * gate).astype(BF16)
    mix = _dot(y_rwkv, wo_ref[0:D_RWKV, :]) + _dot(yc_ref[...].astype(BF16), wo_ref[D_RWKV:, :])
    x1 = x + g1 * mix
    x1_ref[...] = x1
    h2 = _rms(x1, n2_ref[...]) * (1.0 + sc2) + sh2
    h2_ref[...] = h2.astype(BF16)

    logit_ref[...] = _dot_hp(h2, rw_ref[...]) + rb_ref[...]


def _route(logits):
    lane = lax.broadcasted_iota(jnp.int32, logits.shape, 1)
    lanef = lane.astype(F32)
    neg = jnp.float32(-1e30)
    big = jnp.float32(1e9)
    gmask = lane < N_GROUPS
    gl = jnp.where(gmask, logits, neg)
    ge = jnp.where(gmask, jnp.exp(gl - jnp.max(gl, axis=-1, keepdims=True)), 0.0)
    gprob = ge / jnp.sum(ge, axis=-1, keepdims=True)
    gp = jnp.max(gprob, axis=-1, keepdims=True)
    gidx = jnp.min(jnp.where(jnp.logical_and(gmask, gprob == gp), lanef, big), axis=-1, keepdims=True)
    egrp = jnp.floor((lanef - float(E_LANE0)) * (1.0 / N_EXP_PER_GROUP))
    emask = jnp.logical_and(jnp.logical_and(lane >= E_LANE0, lane < E_LANE0 + N_EXPERTS), egrp == gidx)
    el = jnp.where(emask, logits, neg)
    ee = jnp.where(emask, jnp.exp(el - jnp.max(el, axis=-1, keepdims=True)), 0.0)
    ep = ee / jnp.sum(ee, axis=-1, keepdims=True)
    m1 = jnp.max(jnp.where(emask, ep, -1.0), axis=-1, keepdims=True)
    i1 = jnp.min(jnp.where(jnp.logical_and(emask, ep == m1), lanef, big), axis=-1, keepdims=True)
    mask2 = jnp.logical_and(emask, lanef != i1)
    m2 = jnp.max(jnp.where(mask2, ep, -1.0), axis=-1, keepdims=True)
    i2 = jnp.min(jnp.where(jnp.logical_and(mask2, ep == m2), lanef, big), axis=-1, keepdims=True)
    den = m1 + m2
    return (jnp.where(lanef == i1, gp * (m1 / den), 0.0)
            + jnp.where(lanef == i2, gp * (m2 / den), 0.0)
            + jnp.where(lane == GROUP_LANE, gidx, 0.0))


def _outproj(xp, xs, pe, mod3, yf, yb, bonus, zg, yconv, gn_g, gn_b, gate_bf, w_out_bf, norm2_g,
             router_w, router_b, ones512):
    tile = lambda n: pl.BlockSpec((TM, n), lambda i: (i, 0))
    const = lambda shape: pl.BlockSpec(shape, lambda i: (0,) * len(shape))
    return pl.pallas_call(
        _outproj_kernel,
        grid=(N_TILES,),
        in_specs=_x_specs() + [
            pl.BlockSpec((None, 1, N_MOD * D), lambda i: (_mod_row(i), 0, 0)),
            tile(D_RWKV), tile(D_RWKV),
            pl.BlockSpec((None, TM, D_RWKV), lambda i: (0, i, 0)),
            pl.BlockSpec((None, TM, D_RWKV), lambda i: (1, i, 0)),
            tile(LORA_G), tile(D_CONV),
            const((1, D_RWKV)), const((1, D_RWKV)), const((LORA_G, D_RWKV)), const((D, D)),
            const((1, D)), const((D, ROUTE_LANES)), const((1, ROUTE_LANES)),
            const((D_RWKV, D_RWKV))],
        out_specs=[tile(D), tile(D), tile(ROUTE_LANES)],
        out_shape=[jax.ShapeDtypeStruct((NT, D), F32), jax.ShapeDtypeStruct((NT, D), BF16),
                   jax.ShapeDtypeStruct((NT, ROUTE_LANES), F32)],
        compiler_params=_cp(("parallel",)),
        name="outproj_router",
    )(xp, xs, pe, mod3, yf, yb, bonus, bonus, zg, yconv, gn_g, gn_b, gate_bf, w_out_bf, norm2_g,
      router_w, router_b, ones512)


def _plan_kernel(logit_ref, triu_ref, comb_ref, drow_ref, dcol_ref, start_ref, ntile_ref):
    comb = _route(logit_ref[...])
    comb_ref[...] = comb
    gidx = comb.T[GROUP_LANE:GROUP_LANE + 1, :]
    grow = lax.broadcasted_iota(jnp.int32, (SUB, 1), 0)
    growf = grow.astype(F32)
    onehot = jnp.where(jnp.logical_and(gidx == growf, grow < N_GROUPS), 1.0, 0.0)
    before = _dot(onehot.astype(BF16), triu_ref[...])
    count = jnp.sum(onehot, axis=-1, keepdims=True)
    padded = jnp.floor((count + (SEG - 1.0)) * (1.0 / SEG)) * SEG
    start = jnp.zeros((SUB, 1), F32)
    for g in range(N_GROUPS - 1):
        start = start + jnp.where(grow > g, padded[g:g + 1, :], 0.0)
    dest = jnp.sum(onehot * (start + before), axis=0, keepdims=True)
    drow_ref[...] = jnp.broadcast_to(dest, (SUB, TM_MOE))
    dcol_ref[...] = jnp.broadcast_to(dest, (LANES, TM_MOE)).T
    start_ref[...] = jnp.broadcast_to(start * (1.0 / SEG), (SUB, LANES)).astype(jnp.int32)
    ntile_ref[...] = jnp.broadcast_to(padded * (1.0 / SEG), (SUB, LANES)).astype(jnp.int32)


def _plan(logits, triu):
    seg_shape = jax.ShapeDtypeStruct((N_MOE_TILES, SUB, LANES), jnp.int32)
    seg_spec = pl.BlockSpec((None, SUB, LANES), lambda t: (t, 0, 0))
    tok_spec = pl.BlockSpec((TM_MOE, LANES), lambda t: (t, 0))
    return pl.pallas_call(
        _plan_kernel,
        grid=(N_MOE_TILES,),
        in_specs=[tok_spec, pl.BlockSpec((TM_MOE, TM_MOE), lambda t: (0, 0))],
        out_specs=[tok_spec, pl.BlockSpec((None, SUB, TM_MOE), lambda t: (t, 0, 0)), tok_spec,
                   seg_spec, seg_spec],
        out_shape=[jax.ShapeDtypeStruct((NT, ROUTE_LANES), F32), jax.ShapeDtypeStruct((N_MOE_TILES, SUB, TM_MOE), F32),
                   jax.ShapeDtypeStruct((NT, LANES), F32), seg_shape, seg_shape],
        compiler_params=_cp(("parallel",)),
        name="moe_plan",
    )(logits, triu)


def _moe_kernel(start_ref, ntile_ref, h_ref, comb_ref, drow_ref, dcol_ref, wg_ref, wu_ref, wd_ref,
                o_ref, xs, cws, ys):
    q = pl.program_id(1)
    g = q // (N_EXP_PER_GROUP // EXP_PER_STEP)
    tiles = range(TILES_PER_PASS)
    toks = [slice(j * TM_MOE, (j + 1) * TM_MOE) for j in tiles]

    @pl.when(q == 0)
    def _sort_in():
        slot = lax.broadcasted_iota(jnp.int32, (MOE_ROWS, TM_MOE), 0).astype(F32)
        for j in tiles:
            perm = jnp.where(slot == drow_ref[j, 0:1, :], 1.0, 0.0).astype(BF16)
            xs[j] = _dot(perm, h_ref[toks[j], :]).astype(BF16)
            ch, cl = _split2(comb_ref[toks[j], :])
            cws[j] = _dot(perm, ch) + _dot(perm, cl)
            ys[j] = jnp.zeros((MOE_ROWS, D), F32)

    segs = [(pl.program_id(0) * TILES_PER_PASS + j) * N_GROUPS + g for j in tiles]
    firsts = [start_ref[seg] for seg in segs]

    def visit(windows, n_rows):
        rows = [(j, pl.ds(pl.multiple_of(row0 * SEG, SEG), n_rows)) for j, row0 in windows]
        x = jnp.concatenate([xs[j, r, :] for j, r in rows], axis=0)
        cw_all = jnp.concatenate([cws[j, r, :] for j, r in rows], axis=0)
        lane = lax.broadcasted_iota(jnp.int32, cw_all.shape, 1)
        acc = jnp.concatenate([ys[j, r, :] for j, r in rows], axis=0)
        for k in range(EXP_PER_STEP):
            e_lane = q * EXP_PER_STEP + k + E_LANE0
            cw = jnp.sum(jnp.where(lane == e_lane, cw_all, 0.0), axis=-1, keepdims=True)
            hid = (_silu(_dot(x, wg_ref[k].astype(BF16))) * _dot(x, wu_ref[k].astype(BF16)) * cw).astype(BF16)
            acc = acc + _dot(hid, wd_ref[k].astype(BF16))
        for i, (j, r) in enumerate(rows):
            ys[j, r, :] = acc[i * n_rows:(i + 1) * n_rows]

    needs = [ntile_ref[seg] for seg in segs]
    window = lambda j, size: (j, jnp.minimum(firsts[j], MOE_ROWS // SEG - size))

    def size_class(need, sizes, guard, run):
        lo = 0
        for size in sizes:
            pl.when(jnp.logical_and(guard, jnp.logical_and(need > lo, need <= size)))(
                functools.partial(run, size))
            lo = size

    need_all = functools.reduce(jnp.maximum, needs)
    together = need_all <= MOE_SHARED_SIZES[-1]
    size_class(need_all, MOE_SHARED_SIZES, together,
               lambda size: visit([window(j, size) for j in tiles], size * SEG))
    for j in tiles:
        size_class(needs[j], MOE_SIZES, jnp.logical_not(together),
                   lambda size, j=j: visit([window(j, size)], size * SEG))

    @pl.when(q == N_EXPERTS // EXP_PER_STEP - 1)
    def _sort_out():
        slot = lax.broadcasted_iota(jnp.int32, (TM_MOE, MOE_ROWS), 1).astype(F32)
        for j in tiles:
            perm_t = jnp.where(slot == dcol_ref[toks[j], 0:1], 1.0, 0.0).astype(BF16)
            o_ref[toks[j], :] = _dot(perm_t, ys[j].astype(BF16)).astype(BF16)


def _moe(h2, comb, drow, dcol, seg_start, seg_ntile, wg, wu, wd):
    rows = TILES_PER_PASS * TM_MOE
    once = dict(pipeline_mode=pl.Buffered(1))
    weights = lambda shape: pl.BlockSpec(shape, lambda p, q, s, n: (q, 0, 0))
    grid_spec = pltpu.PrefetchScalarGridSpec(
        num_scalar_prefetch=2,
        grid=(N_MOE_TILES // TILES_PER_PASS, N_EXPERTS // EXP_PER_STEP),
        in_specs=[pl.BlockSpec((rows, D), lambda p, q, s, n: (p, 0), **once),
                  pl.BlockSpec((rows, ROUTE_LANES), lambda p, q, s, n: (p, 0), **once),
                  pl.BlockSpec((TILES_PER_PASS, SUB, TM_MOE), lambda p, q, s, n: (p, 0, 0), **once),
                  pl.BlockSpec((rows, LANES), lambda p, q, s, n: (p, 0), **once),
                  weights((EXP_PER_STEP, D, D_EXPERT)), weights((EXP_PER_STEP, D, D_EXPERT)),
                  weights((EXP_PER_STEP, D_EXPERT, D))],
        out_specs=pl.BlockSpec((rows, D), lambda p, q, s, n: (p, 0)),
        scratch_shapes=[pltpu.VMEM((TILES_PER_PASS, MOE_ROWS, D), BF16),
                        pltpu.VMEM((TILES_PER_PASS, MOE_ROWS, ROUTE_LANES), F32),
                        pltpu.VMEM((TILES_PER_PASS, MOE_ROWS, D), F32)])
    return pl.pallas_call(
        _moe_kernel,
        grid_spec=grid_spec,
        out_shape=jax.ShapeDtypeStruct((NT, D), BF16),
        compiler_params=_cp(("parallel", "arbitrary")),
        name="moe_experts",
    )(seg_start, seg_ntile, h2, comb, drow, dcol, wg, wu, wd)


def _final_kernel(x1_ref, moe_ref, mod_ref, g_ref, o_ref):
    g2 = mod_ref[:, 5 * D:6 * D]
    x2 = x1_ref[...] + g2 * moe_ref[...].astype(F32)
    o_ref[...] = _rms(x2, g_ref[...])


def _final(x1, moe, mod3, final_g, tile0, n_tiles, out_block, name, merge=1):
    rows = merge * TM
    tile = pl.BlockSpec((rows, D), lambda i: (tile0 // merge + i, 0))
    return pl.pallas_call(
        _final_kernel,
        grid=(n_tiles // merge,),
        in_specs=[tile, tile,
                  pl.BlockSpec((None, 1, N_MOD * D), lambda i: (_mod_row(tile0 + i * merge), 0, 0)),
                  pl.BlockSpec((1, D), lambda i: (0, 0))],
        out_specs=pl.BlockSpec((rows, D), lambda i: (out_block(tile0 // merge + i), 0)),
        out_shape=jax.ShapeDtypeStruct((n_tiles * TM, D), F32),
        compiler_params=_cp(("parallel",)),
        name=name,
    )(x1, moe, mod3, final_g)


def _pos_embed(rows):
    t = np.arange(rows * GRID_W)
    row = (t // GRID_W).astype(np.float32)
    col = (t % GRID_W).astype(np.float32)
    quarter = D // 4
    freqs = (1.0 / (10000.0 ** (np.arange(quarter, dtype=np.float32) / quarter))).astype(np.float32)
    ang_r = row[:, None] * freqs[None, :]
    ang_c = col[:, None] * freqs[None, :]
    pe = np.concatenate([np.sin(ang_r), np.cos(ang_r), np.sin(ang_c), np.cos(ang_c)], axis=-1)
    return jnp.asarray(pe, F32)


def _selection_constants():
    ch = np.arange(D_RWKV)
    ones512 = (ch[:, None] // HEAD == ch[None, :] // HEAD).astype(np.float32)
    t = np.arange(TM)
    same_chunk = t[:, None] // CH == t[None, :] // CH
    cum_f = same_chunk & (t[None, :] <= t[:, None])
    cum_b = same_chunk & (t[None, :] >= t[:, None])
    sel8 = np.arange(NCH)[:, None] == t[None, :] // CH
    col = np.arange(HC)
    hm = col[:, None] // CH == ch[None, :] // HEAD
    bm = col[:, None] // CH == col[None, :] // CH
    tt, jj = np.arange(CH)[:, None], col[None, :] % CH
    masks = np.stack([jj < tt, jj <= tt, jj > tt, jj >= tt])
    eye = jj == tt
    bf = lambda x: jnp.asarray(x, BF16)
    f32 = lambda x: jnp.asarray(x, F32)
    pad = np.zeros((2 * SUB - NCH, TM), bool)
    sums = [np.concatenate([cum, same_chunk, sel8, pad]) for cum in (cum_f, cum_b)]
    prep_consts = (bf(ones512), bf(np.stack(sums)), bf(hm), bf(bm),
                   f32(masks.reshape(2, 2, CH, HC)), f32(eye))
    scan_consts = (bf(hm), bf(ones512))
    return prep_consts, scan_consts


def kernel(x_prompt, x_sample, state_rwkv, c, c_ctx, ada_w, ada_b, norm1_g, w_in, tshift_mu, decay_w0, decay_lora_b, iclr_a0, iclr_lora_b, key_k, key_a, bonus_r_k, gate_lora_b, gn_g, gn_b, conv_dw_w, conv_dw_b, conv_ln_g, conv_ln_b, w_out, norm2_g, router_group_w, router_group_b, router_expert_w, router_expert_b, expert_w_gate, expert_w_up, expert_w_down, final_norm_g):
    assert x_prompt.shape == (N_CTX_SEQ, T_CTX, D) and x_sample.shape == (N_LAT_SEQ, T_LAT, D)
    assert ada_w.shape[0] == 1, "one trunk layer"
    prep_consts, scan_consts = _selection_constants()
    ones512 = prep_consts[0]
    xp = x_prompt.reshape(N_CTX_SEQ * T_CTX, D)
    xs = x_sample.reshape(N_LAT_SEQ * T_LAT, D)
    pe = _pos_embed(T_LAT // GRID_W)

    cond8 = jnp.concatenate([c_ctx[None, :], c, jnp.zeros((8 - 1 - N_LAT_SEQ, D), F32)], axis=0)
    mod3 = _adaln(cond8, ada_w[0], ada_b[0][None, :]).reshape(8, 1, N_MOD * D)

    zs, zg, zc = _inproj(xp, xs, pe, mod3, norm1_g, w_in[0].astype(BF16))

    zero = jnp.zeros((2, LORA, D_RWKV), F32)
    lora2 = jnp.concatenate([jnp.concatenate([decay_lora_b[0], zero], axis=2),
                             jnp.concatenate([zero, iclr_lora_b[0]], axis=2)], axis=1)
    vec = lambda p: p.reshape(2, 1, -1)
    prep_out = _prep(zs, vec(tshift_mu[0]), vec(decay_w0[0]), vec(iclr_a0[0]), lora2,
                     vec(key_k[0]), vec(key_a[0]), vec(bonus_r_k[0]), *prep_consts)
    bonus = prep_out[10]

    s0_lat = state_rwkv[:, 0].transpose(1, 0, 4, 2, 3).reshape(2, N_LAT_SEQ, HEAD, D_RWKV)
    yf, yb, s_fin = _scan(prep_out, s0_lat, scan_consts)

    yconv = _conv(zc, jnp.concatenate([conv_dw_w[0], jnp.zeros((1, D_CONV), F32)], axis=0),
                  conv_dw_b, conv_ln_g, conv_ln_b)

    router_w = jnp.concatenate([router_group_w[0], router_expert_w[0],
                                jnp.zeros((D, ROUTE_LANES - N_GROUPS - N_EXPERTS), F32)], axis=1)
    router_b = jnp.concatenate([router_group_b[0], router_expert_b[0],
                                jnp.zeros((ROUTE_LANES - N_GROUPS - N_EXPERTS,), F32)])[None, :]
    x1, h2, logits = _outproj(xp, xs, pe, mod3, yf, yb, bonus, zg, yconv,
                              gn_g, gn_b, gate_lora_b[0].astype(BF16), w_out[0].astype(BF16), norm2_g,
                              router_w, router_b, ones512)

    tok = np.arange(TM_MOE)
    triu = jnp.asarray(tok[:, None] < tok[None, :], BF16)
    comb, drow, dcol, seg_start, seg_ntile = _plan(logits, triu)
    seg_start = seg_start[:, :N_GROUPS, 0].reshape(-1)
    seg_ntile = seg_ntile[:, :N_GROUPS, 0].reshape(-1)
    moe = _moe(h2, comb, drow, dcol, seg_start, seg_ntile,
               expert_w_gate[0], expert_w_up[0], expert_w_down[0])
    fg = final_norm_g[None, :]
    y_prompt = _final(x1, moe, mod3, fg, 0, N_CTX_TILES, lambda i: i, "final_norm_ctx", merge=2)
    y_sample = _final(x1, moe, mod3, fg, N_CTX_TILES, N_LAT_TILES, _xs_block, "final_norm_lat")
    y_prompt = y_prompt.reshape(N_CTX_SEQ, T_CTX, D)
    y_sample = y_sample.reshape(N_LAT_SEQ, T_LAT, D)
    new_state = s_fin[:, None].astype(state_rwkv.dtype)
    return (y_prompt, y_sample, new_state)
```

```python
import functools

import numpy as np
import jax
import jax.numpy as jnp
from jax import lax
from jax.experimental import pallas as pl
from jax.experimental.pallas import tpu as pltpu

F32 = jnp.float32
BF16 = jnp.bfloat16

D = 1024
N_CTX_SEQ = 16
T_CTX = 256
N_LAT_SEQ = 2
T_LAT = 1024
TM = 256
N_CTX_TILES = N_CTX_SEQ * T_CTX // TM
LAT_CHUNKS = T_LAT // TM
N_LAT_TILES = N_LAT_SEQ * LAT_CHUNKS
N_TILES = N_CTX_TILES + N_LAT_TILES
NT = N_TILES * TM
GRID_W = 64
D_RWKV = 512
D_CONV = 512
HEAD = 64
N_HEADS = 8
CONV_W = 31
CONV_PAD = CONV_W // 2
LORA = 64
LORA_G = 128
SHIFT_COLS = 3 * D_RWKV + 2 * LORA
N_GROUPS = 4
N_EXP_PER_GROUP = 8
N_EXPERTS = 32
D_EXPERT = 256
N_MOD = 6
RMS_EPS = 1e-6
LN_EPS = 1e-5
GN_EPS = 64e-5
KK_EPS = 1e-12
LANES = 128
SUB = 8
CH = 32
NCH = TM // CH
HC = N_HEADS * CH
DOUBLINGS = 4
ROUTE_LANES = 128
E_LANE0 = N_GROUPS
GROUP_LANE = 36
TM_MOE = 1024
SEG = 64
EXP_PER_STEP = 2
TILES_PER_PASS = 2
MOE_ROWS = TM_MOE + N_GROUPS * SEG
MOE_SHARED_SIZES = (2, 4, 6)
MOE_SIZES = (8, 12, MOE_ROWS // SEG)
N_MOE_TILES = NT // TM_MOE
VMEM_LIMIT = 56 * 1024 * 1024


def _cp(sem, flags=None):
    return pltpu.CompilerParams(dimension_semantics=sem, vmem_limit_bytes=VMEM_LIMIT, flags=flags)


def _split2(a):
    hi = a.astype(BF16)
    lo = (a - hi.astype(F32)).astype(BF16)
    return hi, lo


def _dot(a, b):
    return jnp.dot(a, b, preferred_element_type=F32)


def _dot_hp(a, b):
    ah, al = _split2(a)
    bh, bl = _split2(b)
    return _dot(ah, bh) + _dot(ah, bl) + _dot(al, bh)


def _dot_sel(a, sel):
    h, l = _split2(a)
    return _dot(h, sel) + _dot(l, sel)


def _sel_dot(sel, a):
    h, l = _split2(a)
    return _dot(sel, h) + _dot(sel, l)


def _sigmoid(x):
    return 1.0 / (1.0 + jnp.exp(-x))


def _silu(x):
    return x * _sigmoid(x)


def _lat_js(i):
    il = jnp.maximum(i - N_CTX_TILES, 0)
    return il // N_LAT_SEQ, il % N_LAT_SEQ


def _xp_block(i):
    return jnp.minimum(i, N_CTX_TILES - 1)


def _xs_block(i):
    j, s = _lat_js(i)
    return s * LAT_CHUNKS + j


def _pe_block(i):
    j, _ = _lat_js(i)
    return j


def _mod_row(i):
    _, s = _lat_js(i)
    return jnp.where(i < N_CTX_TILES, 0, 1 + s)


def _first_last(i):
    j, _ = _lat_js(i)
    is_ctx = i < N_CTX_TILES
    first = jnp.logical_or(is_ctx, j == 0)
    last = jnp.logical_or(is_ctx, j == LAT_CHUNKS - 1)
    return first, last


def _prev_tile(i):
    first, _ = _first_last(i)
    return jnp.where(first, i, i - N_LAT_SEQ)


def _next_tile(i):
    _, last = _first_last(i)
    return jnp.where(last, i, i + N_LAT_SEQ)


def _adaln_kernel(c_ref, w_ref, b_ref, o_ref):
    c = c_ref[...]
    o_ref[...] = _dot_hp(_silu(c), w_ref[...]) + b_ref[...]


def _adaln(cond8, ada_w, ada_b):
    tn = 1536
    n = ada_w.shape[1]
    return pl.pallas_call(
        _adaln_kernel,
        grid=(n // tn,),
        in_specs=[pl.BlockSpec((8, D), lambda j: (0, 0)),
                  pl.BlockSpec((D, tn), lambda j: (0, j)),
                  pl.BlockSpec((1, tn), lambda j: (0, j))],
        out_specs=pl.BlockSpec((8, tn), lambda j: (0, j)),
        out_shape=jax.ShapeDtypeStruct((8, n), F32),
        compiler_params=_cp(("parallel",)),
        name="adaln",
    )(cond8, ada_w, ada_b)


def _load_x(i, xp_ref, xs_ref, pe_ref):
    f = (i >= N_CTX_TILES).astype(F32)
    return xp_ref[...] * (1.0 - f) + (xs_ref[...] + pe_ref[...]) * f


def _x_specs():
    return [pl.BlockSpec((TM, D), lambda i: (_xp_block(i), 0)),
            pl.BlockSpec((TM, D), lambda i: (_xs_block(i), 0)),
            pl.BlockSpec((TM, D), lambda i: (_pe_block(i), 0))]


def _rms(x, g):
    return x * lax.rsqrt(jnp.mean(x * x, axis=-1, keepdims=True) + RMS_EPS) * g


def _inproj_kernel(xp_ref, xs_ref, pe_ref, mod_ref, g_ref, w_ref, zs_ref, zg_ref, zc_ref):
    i = pl.program_id(0)
    x = _load_x(i, xp_ref, xs_ref, pe_ref)
    sh1 = mod_ref[:, 0:D]
    sc1 = mod_ref[:, D:2 * D]
    h = (_rms(x, g_ref[...]) * (1.0 + sc1) + sh1).astype(BF16)
    zs_ref[...] = _dot(h, w_ref[:, 0:SHIFT_COLS])
    zg_ref[...] = _dot(h, w_ref[:, SHIFT_COLS:SHIFT_COLS + LORA_G])
    zc_ref[...] = _dot(h, w_ref[:, SHIFT_COLS + LORA_G:])


def _inproj(xp, xs, pe, mod3, norm1_g, w_in_bf):
    in_cols = w_in_bf.shape[1]
    return pl.pallas_call(
        _inproj_kernel,
        grid=(N_TILES,),
        in_specs=_x_specs() + [
            pl.BlockSpec((None, 1, N_MOD * D), lambda i: (_mod_row(i), 0, 0)),
            pl.BlockSpec((1, D), lambda i: (0, 0)),
            pl.BlockSpec((D, in_cols), lambda i: (0, 0))],
        out_specs=[pl.BlockSpec((TM, SHIFT_COLS), lambda i: (i, 0)),
                   pl.BlockSpec((TM, LORA_G), lambda i: (i, 0)),
                   pl.BlockSpec((TM, 2 * D_CONV), lambda i: (i, 0))],
        out_shape=[jax.ShapeDtypeStruct((NT, SHIFT_COLS), F32),
                   jax.ShapeDtypeStruct((NT, LORA_G), F32),
                   jax.ShapeDtypeStruct((NT, 2 * D_CONV), F32)],
        compiler_params=_cp(("parallel",)),
        name="inproj",
    )(xp, xs, pe, mod3, norm1_g, w_in_bf)


NT_DIMS = (((1,), (1,)), ((), ()))


def _tile_rows(x):
    return jnp.concatenate([x] * N_HEADS, axis=0)


def _prep_kernel(zs_ref, prev_ref, next_ref, mu_ref, w0_ref, a0_ref, lora_ref, kk_ref_, ka_ref,
                 rk_ref, ones_ref, cum_ref, hm_ref, bm_ref, msk_ref, eye_ref,
                 at_out, rt_out, bh_out, kh_out, v_out, t_out, aak_out, arb_out, ark_out, gc_out, bon_out):
    i = pl.program_id(0)
    first, last = _first_last(i)
    cur = zs_ref[...]
    prow = prev_ref[7:8, :] * (1.0 - first.astype(F32))
    nrow = next_ref[0:1, :] * (1.0 - last.astype(F32))
    rows = lax.broadcasted_iota(jnp.int32, (TM, 1), 0)
    shifted = (jnp.where(rows == 0, prow, pltpu.roll(cur, 1, axis=0)),
               jnp.where(rows == TM - 1, nrow, pltpu.roll(cur, TM - 1, axis=0)))
    ones = ones_ref[...]
    lane = lax.broadcasted_iota(jnp.int32, (TM, 2 * LORA), 1)
    scaled = []
    for d in range(2):
        xs = cur + (shifted[d] - cur) * mu_ref[d]
        r = xs[:, 0:D_RWKV]
        k = xs[:, D_RWKV:2 * D_RWKV]
        v = xs[:, 2 * D_RWKV:3 * D_RWKV]
        z2 = xs[:, 3 * D_RWKV:SHIFT_COLS]
        lin = jnp.where(lane < LORA, jnp.tanh(z2), z2)
        lo = _dot_hp(lin, lora_ref[d])
        u = -(w0_ref[d] + lo[:, 0:D_RWKV])
        softplus = jnp.maximum(u, 0.0) + jnp.log(1.0 + jnp.exp(-jnp.abs(u)))
        w_log = -softplus - 0.5
        lw = -jnp.exp(w_log)
        a = _sigmoid(a0_ref[d] + lo[:, D_RWKV:])
        kx = k * kk_ref_[d]
        kk = kx * lax.rsqrt(jnp.maximum(_dot_sel(kx * kx, ones), KK_EPS * KK_EPS))
        k2 = k * (1.0 + (a - 1.0) * ka_ref[d])
        bvec = kk * a
        bon_out[d] = _dot_sel(r * k2 * rk_ref[d], ones) * v

        sums = _sel_dot(cum_ref[d], lw)
        lg = sums[0:TM]
        tot = sums[TM:2 * TM]
        e_tail = jnp.exp(tot - lg)
        e_inv = jnp.exp(-lg)
        at = (-(kk * jnp.exp(lg - lw))).astype(BF16)
        bt = (bvec * e_inv).astype(BF16)
        kt = (k2 * e_inv).astype(BF16)
        rt = (r * jnp.exp(lg)).astype(BF16)
        at_out[d] = at
        rt_out[d] = rt
        bh_out[d] = (bvec * e_tail).astype(BF16)
        kh_out[d] = (k2 * e_tail).astype(BF16)
        v_out[d] = v.astype(BF16)
        gc_out[d] = jnp.exp(sums[2 * TM:2 * TM + NCH])
        scaled.append((at, bt, kt, rt))

    hm = hm_ref[...]
    bm = bm_ref[...]

    def expand(z):
        return _tile_rows(z.astype(BF16)) * bm

    chunks = [(d, slice(c * CH, (c + 1) * CH)) for d in range(2) for c in range(NCH)]
    pws = []
    for d, rows in chunks:
        at, bt, kt, rt = scaled[d]
        ar = jnp.concatenate([at[rows], rt[rows]], axis=0)
        bk = jnp.concatenate([_tile_rows(bt[rows]) * hm, _tile_rows(kt[rows]) * hm], axis=0)
        p1 = lax.dot_general(ar, bk, NT_DIMS, preferred_element_type=F32)
        m_strict, m_incl = msk_ref[d, 0], msk_ref[d, 1]
        pws.append(p1[0:CH, 0:HC] * m_strict)
        aak_out[d, rows, :] = (p1[0:CH, HC:] * m_strict).astype(BF16)
        arb_out[d, rows, :] = (p1[CH:, 0:HC] * m_incl).astype(BF16)
        ark_out[d, rows, :] = (p1[CH:, HC:] * m_incl).astype(BF16)
    tms = [eye_ref[...] + pw for pw in pws]
    pws = [_dot(pw.astype(BF16), expand(pw)) for pw in pws]
    for k in range(1, DOUBLINGS + 1):
        if k < DOUBLINGS:
            prods = [_dot(jnp.concatenate([pw, tm], axis=0).astype(BF16), expand(pw))
                     for pw, tm in zip(pws, tms)]
            pws = [p[0:CH] for p in prods]
            tms = [tm + p[CH:] for tm, p in zip(tms, prods)]
        else:
            tms = [tm + _dot(tm.astype(BF16), expand(pw)) for tm, pw in zip(tms, pws)]
    for (d, rows), tm in zip(chunks, tms):
        t_out[d, rows, :] = tm.astype(BF16)


def _prep(zs, mu, w0, a0, lora2, key_k, key_a, r_k, ones512, cum2, hm, bm, masks, eye):
    full = lambda a: pl.BlockSpec(a.shape, lambda i: (0,) * a.ndim)
    out_spec = pl.BlockSpec((2, TM, D_RWKV), lambda i: (0, i, 0))
    bf_shape = jax.ShapeDtypeStruct((2, NT, D_RWKV), BF16)
    tall_spec = pl.BlockSpec((2, TM, HC), lambda i: (0, i, 0))
    rows8 = TM // 8
    params = (mu, w0, a0, lora2, key_k, key_a, r_k, ones512, cum2, hm, bm, masks, eye)
    return pl.pallas_call(
        _prep_kernel,
        grid=(N_TILES,),
        in_specs=[pl.BlockSpec((TM, SHIFT_COLS), lambda i: (i, 0)),
                  pl.BlockSpec((8, SHIFT_COLS), lambda i: (_prev_tile(i) * rows8 + rows8 - 1, 0)),
                  pl.BlockSpec((8, SHIFT_COLS), lambda i: (_next_tile(i) * rows8, 0))]
                 + [full(p) for p in params],
        out_specs=[out_spec] * 5 + [tall_spec] * 4 + [
            pl.BlockSpec((2, NCH, D_RWKV), lambda i: (0, i, 0)), out_spec],
        out_shape=[bf_shape] * 5 + [jax.ShapeDtypeStruct((2, NT, HC), BF16)] * 4 + [
            jax.ShapeDtypeStruct((2, N_TILES * NCH, D_RWKV), F32),
            jax.ShapeDtypeStruct((2, NT, D_RWKV), F32)],
        compiler_params=_cp(("parallel",)),
        name="rwkv_prep",
    )(zs, zs, zs, *params)


SCAN_TILES = 4
CTX_STEPS = N_CTX_TILES // SCAN_TILES
LAT_STEPS = N_LAT_TILES // SCAN_TILES
N_SCAN_STEPS = CTX_STEPS + LAT_STEPS
N_SLOTS = 2 * SCAN_TILES


def _scan_block(step, d):
    jl = step - CTX_STEPS
    j = jnp.where(d == 0, jl, LAT_STEPS - 1 - jl)
    return jnp.where(step >= CTX_STEPS, CTX_STEPS + j, step)


def _scan_state_block(step):
    return jnp.minimum(step, CTX_STEPS - 1)


def _chunk_kernel(*refs):
    (atf, rtf, bhf, khf, vf, tf, aakf, arbf, arkf, gcf, atb, rtb, bhb, khb, vb, tb, aakb, arbb, arkb, gcb,
     s0_ref, hm_ref, bd_ref, yf_ref, yb_ref, sout_ref, mw) = refs
    step = pl.program_id(0)
    in_refs = ((atf, rtf, bhf, khf, vf, tf, aakf, arbf, arkf, gcf),
               (atb, rtb, bhb, khb, vb, tb, aakb, arbb, arkb, gcb))
    y_refs = (yf_ref, yb_ref)
    bd = bd_ref[...]
    hm = hm_ref[...]
    crow = lax.broadcasted_iota(jnp.int32, (NCH, 1), 0)
    heads = range(N_HEADS)

    def advance(chains):
        rows_of = lambda tile, cc: pl.ds(pl.multiple_of(tile * TM + cc * CH, CH), CH)
        chains = [(dl, slot, tile, cc, rows_of(tile, cc)) for dl, slot, tile, cc in chains]
        ld = lambda idx, dl, rows: in_refs[dl][idx][rows, :]
        m0s = [mw[slot] for _, slot, _, _, _ in chains]
        xy0s = [_dot(jnp.concatenate([ld(0, dl, rows), ld(1, dl, rows)], axis=0),
                     _tile_rows(m0.astype(BF16)) * bd)
                for (dl, _, _, _, rows), m0 in zip(chains, m0s)]
        vvs = [ld(4, dl, rows) for dl, _, _, _, rows in chains]
        avs = [_dot(jnp.concatenate([ld(6, dl, rows), ld(8, dl, rows)], axis=0), _tile_rows(vv) * hm)
               for (dl, _, _, _, rows), vv in zip(chains, vvs)]
        ubs = [_dot(ld(5, dl, rows), _tile_rows((xy0[0:CH] + av[0:CH]).astype(BF16)) * hm).astype(BF16)
               for (dl, _, _, _, rows), xy0, av in zip(chains, xy0s, avs)]
        for (dl, _, _, _, rows), xy0, av, ub in zip(chains, xy0s, avs, ubs):
            y_refs[dl][rows, :] = xy0[CH:] + av[CH:] + _dot(ld(7, dl, rows), _tile_rows(ub) * hm)
        for (dl, slot, tile, cc, rows), m0, ub, vv in zip(chains, m0s, ubs, vvs):
            gc_tile = in_refs[dl][9][pl.ds(pl.multiple_of(tile * NCH, NCH), NCH), :]
            gcrow = jnp.sum(jnp.where(crow == cc, gc_tile, 0.0), axis=0, keepdims=True)
            pad = jnp.zeros((LANES - 2 * CH - SUB, D_RWKV), F32)
            stack = jnp.concatenate([ld(2, dl, rows).astype(F32), ld(3, dl, rows).astype(F32),
                                     jnp.broadcast_to(gcrow, (SUB, D_RWKV)), pad], axis=0)
            stack_t = stack.T
            head = lambda h: stack_t[h * HEAD:(h + 1) * HEAD, :]
            bk_t = jnp.concatenate([head(h)[:, 0:2 * CH] for h in heads], axis=1)
            decay = jnp.concatenate([jnp.broadcast_to(head(h)[:, 2 * CH:2 * CH + 1], (HEAD, HEAD))
                                     for h in heads], axis=1)
            uv = _tile_rows(jnp.concatenate([ub, vv], axis=0)) * bd
            mw[slot] = m0 * decay + _dot(bk_t.astype(BF16), uv)

    @pl.when(step < CTX_STEPS)
    def _context():
        mw[...] = jnp.zeros(mw.shape, F32)

        def body(q, carry):
            advance([(dl, dl * SCAN_TILES + sl, jnp.int32(sl), q if dl == 0 else NCH - 1 - q)
                     for dl in range(2) for sl in range(SCAN_TILES)])
            return carry

        lax.fori_loop(0, NCH, body, 0)
        for dl in range(2):
            for sl in range(SCAN_TILES):
                mt = mw[dl * SCAN_TILES + sl]
                by_hv = jnp.concatenate([mt, jnp.zeros_like(mt)], axis=0).T
                for h in range(N_HEADS):
                    sout_ref[sl, dl, h] = by_hv[h * HEAD:(h + 1) * HEAD, 0:HEAD]

    @pl.when(step == CTX_STEPS)
    def _load_state():
        for dl in range(2):
            for sl in range(N_LAT_SEQ):
                mw[dl * SCAN_TILES + sl] = s0_ref[dl, sl]

    @pl.when(step >= CTX_STEPS)
    def _latent():
        n_q = (SCAN_TILES // N_LAT_SEQ) * NCH

        def body(q, carry):
            chains = []
            for dl in range(2):
                qq = q if dl == 0 else n_q - 1 - q
                for sl in range(N_LAT_SEQ):
                    chains.append((dl, dl * SCAN_TILES + sl, (qq // NCH) * N_LAT_SEQ + sl, qq % NCH))
            advance(chains)
            return carry

        lax.fori_loop(0, n_q, body, 0)


def _scan(prep_out, s0_lat, consts):
    rows_per_step = SCAN_TILES * TM

    def row_spec(d, width):
        return pl.BlockSpec((None, rows_per_step, width), lambda s: (d, _scan_block(s, d), 0))

    def gc_spec(d):
        return pl.BlockSpec((None, SCAN_TILES * NCH, D_RWKV), lambda s: (d, _scan_block(s, d), 0))

    def y_spec(d):
        return pl.BlockSpec((rows_per_step, D_RWKV), lambda s: (_scan_block(s, d), 0))

    rows, gc = list(prep_out[:9]), prep_out[9]
    row_specs = lambda d: [row_spec(d, D_RWKV)] * 5 + [row_spec(d, HC)] * 4 + [gc_spec(d)]
    const = lambda a: pl.BlockSpec(a.shape, lambda s: (0,) * a.ndim)
    y_shape = jax.ShapeDtypeStruct((NT, D_RWKV), F32)
    return pl.pallas_call(
        _chunk_kernel,
        grid=(N_SCAN_STEPS,),
        in_specs=row_specs(0) + row_specs(1) + [const(s0_lat)] + [const(a) for a in consts],
        out_specs=[y_spec(0), y_spec(1),
                   pl.BlockSpec((SCAN_TILES, 2, N_HEADS, HEAD, HEAD),
                                lambda s: (_scan_state_block(s), 0, 0, 0, 0))],
        out_shape=[y_shape, y_shape,
                   jax.ShapeDtypeStruct((N_CTX_SEQ, 2, N_HEADS, HEAD, HEAD), F32)],
        scratch_shapes=[pltpu.VMEM((N_SLOTS, HEAD, D_RWKV), F32)],
        compiler_params=_cp(("arbitrary",)),
        name="rwkv_scan",
    )(*rows, gc, *rows, gc, s0_lat, *consts)


HALO = 16


def _glu(z):
    return z[:, 0:D_CONV] * _sigmoid(z[:, D_CONV:])


def _conv_kernel(cur_ref, prev_ref, next_ref, w_ref, b_ref, g_ref, beta_ref, o_ref, ext):
    i = pl.program_id(0)
    first, last = _first_last(i)
    n_ext = TM + 2 * HALO
    u = jnp.concatenate([_glu(prev_ref[...]) * (1.0 - first.astype(F32)), _glu(cur_ref[...]),
                         _glu(next_ref[...]) * (1.0 - last.astype(F32))], axis=0)
    ext[0] = u
    for b in range(1, SUB):
        ext[b] = pltpu.roll(u, n_ext - b, axis=0)
    acc = jnp.zeros((TM, D_CONV), F32)
    for j in range(CONV_W):
        off = HALO - CONV_PAD + j
        acc = acc + ext[off % SUB, off - off % SUB:off - off % SUB + TM, :] * w_ref[j:j + 1, :]
    h = acc + b_ref[...]
    mu = jnp.mean(h, axis=-1, keepdims=True)
    hc = h - mu
    var = jnp.mean(hc * hc, axis=-1, keepdims=True)
    y = hc * lax.rsqrt(var + LN_EPS) * g_ref[...] + beta_ref[...]
    o_ref[...] = _silu(y)


def _conv(zc, conv_w, conv_b, ln_g, ln_b):
    nh = TM // HALO
    vec = pl.BlockSpec((1, D_CONV), lambda i: (0, 0))
    return pl.pallas_call(
        _conv_kernel,
        grid=(N_TILES,),
        in_specs=[pl.BlockSpec((TM, 2 * D_CONV), lambda i: (i, 0)),
                  pl.BlockSpec((HALO, 2 * D_CONV), lambda i: (_prev_tile(i) * nh + nh - 1, 0)),
                  pl.BlockSpec((HALO, 2 * D_CONV), lambda i: (_next_tile(i) * nh, 0)),
                  pl.BlockSpec((CONV_W + 1, D_CONV), lambda i: (0, 0)), vec, vec, vec],
        out_specs=pl.BlockSpec((TM, D_CONV), lambda i: (i, 0)),
        out_shape=jax.ShapeDtypeStruct((NT, D_CONV), F32),
        scratch_shapes=[pltpu.VMEM((SUB, TM + 2 * HALO, D_CONV), F32)],
        compiler_params=_cp(("parallel",)),
        name="conv_module",
    )(zc, zc, zc, conv_w, conv_b, ln_g, ln_b)


def _outproj_kernel(xp_ref, xs_ref, pe_ref, mod_ref, yf_ref, yb_ref, bf_ref, bb_ref, zg_ref, yc_ref,
                    gng_ref, gnb_ref, gl_ref, wo_ref, n2_ref, rw_ref, rb_ref, ones_ref,
                    x1_ref, h2_ref, logit_ref):
    i = pl.program_id(0)
    x = _load_x(i, xp_ref, xs_ref, pe_ref)
    g1 = mod_ref[:, 2 * D:3 * D]
    sh2 = mod_ref[:, 3 * D:4 * D]
    sc2 = mod_ref[:, 4 * D:5 * D]
    ones = ones_ref[...]
    y = (yf_ref[...] + bf_ref[...]) + (yb_ref[...] + bb_ref[...])
    mu = _dot_sel(y, ones) * (1.0 / HEAD)
    yc = y - mu
    var = _dot_sel(yc * yc, ones) * (1.0 / HEAD)
    yn = yc * lax.rsqrt(var + GN_EPS) * gng_ref[...] + gnb_ref[...]
    gate = _dot(_sigmoid(zg_ref[...]).astype(BF16), gl_ref[...])
    y_rwkv = (yn * gate).astype(BF16)
    mix = _dot(y_rwkv, wo_ref[0:D_RWKV, :]) + _dot(yc_ref[...].astype(BF16), wo_ref[D_RWKV:, :])
    x1 = x + g1 * mix
    x1_ref[...] = x1
    h2 = _rms(x1, n2_ref[...]) * (1.0 + sc2) + sh2
    h2_ref[...] = h2.astype(BF16)

    logit_ref[...] = _dot_hp(h2, rw_ref[...]) + rb_ref[...]


def _route(logits):
    lane = lax.broadcasted_iota(jnp.int32, logits.shape, 1)
    lanef = lane.astype(F32)
    neg = jnp.float32(-1e30)
    big = jnp.float32(1e9)
    gmask = lane < N_GROUPS
    gl = jnp.where(gmask, logits, neg)
    ge = jnp.where(gmask, jnp.exp(gl - jnp.max(gl, axis=-1, keepdims=True)), 0.0)
    gprob = ge / jnp.sum(ge, axis=-1, keepdims=True)
    gp = jnp.max(gprob, axis=-1, keepdims=True)
    gidx = jnp.min(jnp.where(jnp.logical_and(gmask, gprob == gp), lanef, big), axis=-1, keepdims=True)
    egrp = jnp.floor((lanef - float(E_LANE0)) * (1.0 / N_EXP_PER_GROUP))
    emask = jnp.logical_and(jnp.logical_and(lane >= E_LANE0, lane < E_LANE0 + N_EXPERTS), egrp == gidx)
    el = jnp.where(emask, logits, neg)
    ee = jnp.where(emask, jnp.exp(el - jnp.max(el, axis=-1, keepdims=True)), 0.0)
    ep = ee / jnp.sum(ee, axis=-1, keepdims=True)
    m1 = jnp.max(jnp.where(emask, ep, -1.0), axis=-1, keepdims=True)
    i1 = jnp.min(jnp.where(jnp.logical_and(emask, ep == m1), lanef, big), axis=-1, keepdims=True)
    mask2 = jnp.logical_and(emask, lanef != i1)
    m2 = jnp.max(jnp.where(mask2, ep, -1.0), axis=-1, keepdims=True)
    i2 = jnp.min(jnp.where(jnp.logical_and(mask2, ep == m2), lanef, big), axis=-1, keepdims=True)
    den = m1 + m2
    return (jnp.where(lanef == i1, gp * (m1 / den), 0.0)
            + jnp.where(lanef == i2, gp * (m2 / den), 0.0)
            + jnp.where(lane == GROUP_LANE, gidx, 0.0))


def _outproj(xp, xs, pe, mod3, yf, yb, bonus, zg, yconv, gn_g, gn_b, gate_bf, w_out_bf, norm2_g,
             router_w, router_b, ones512):
    tile = lambda n: pl.BlockSpec((TM, n), lambda i: (i, 0))
    const = lambda shape: pl.BlockSpec(shape, lambda i: (0,) * len(shape))
    return pl.pallas_call(
        _outproj_kernel,
        grid=(N_TILES,),
        in_specs=_x_specs() + [
            pl.BlockSpec((None, 1, N_MOD * D), lambda i: (_mod_row(i), 0, 0)),
            tile(D_RWKV), tile(D_RWKV),
            pl.BlockSpec((None, TM, D_RWKV), lambda i: (0, i, 0)),
            pl.BlockSpec((None, TM, D_RWKV), lambda i: (1, i, 0)),
            tile(LORA_G), tile(D_CONV),
            const((1, D_RWKV)), const((1, D_RWKV)), const((LORA_G, D_RWKV)), const((D, D)),
            const((1, D)), const((D, ROUTE_LANES)), const((1, ROUTE_LANES)),
            const((D_RWKV, D_RWKV))],
        out_specs=[tile(D), tile(D), tile(ROUTE_LANES)],
        out_shape=[jax.ShapeDtypeStruct((NT, D), F32), jax.ShapeDtypeStruct((NT, D), BF16),
                   jax.ShapeDtypeStruct((NT, ROUTE_LANES), F32)],
        compiler_params=_cp(("parallel",)),
        name="outproj_router",
    )(xp, xs, pe, mod3, yf, yb, bonus, bonus, zg, yconv, gn_g, gn_b, gate_bf, w_out_bf, norm2_g,
      router_w, router_b, ones512)


def _plan_kernel(logit_ref, triu_ref, comb_ref, drow_ref, dcol_ref, start_ref, ntile_ref):
    comb = _route(logit_ref[...])
    comb_ref[...] = comb
    gidx = comb.T[GROUP_LANE:GROUP_LANE + 1, :]
    grow = lax.broadcasted_iota(jnp.int32, (SUB, 1), 0)
    growf = grow.astype(F32)
    onehot = jnp.where(jnp.logical_and(gidx == growf, grow < N_GROUPS), 1.0, 0.0)
    before = _dot(onehot.astype(BF16), triu_ref[...])
    count = jnp.sum(onehot, axis=-1, keepdims=True)
    padded = jnp.floor((count + (SEG - 1.0)) * (1.0 / SEG)) * SEG
    start = jnp.zeros((SUB, 1), F32)
    for g in range(N_GROUPS - 1):
        start = start + jnp.where(grow > g, padded[g:g + 1, :], 0.0)
    dest = jnp.sum(onehot * (start + before), axis=0, keepdims=True)
    drow_ref[...] = jnp.broadcast_to(dest, (SUB, TM_MOE))
    dcol_ref[...] = jnp.broadcast_to(dest, (LANES, TM_MOE)).T
    start_ref[...] = jnp.broadcast_to(start * (1.0 / SEG), (SUB, LANES)).astype(jnp.int32)
    ntile_ref[...] = jnp.broadcast_to(padded * (1.0 / SEG), (SUB, LANES)).astype(jnp.int32)


def _plan(logits, triu):
    seg_shape = jax.ShapeDtypeStruct((N_MOE_TILES, SUB, LANES), jnp.int32)
    seg_spec = pl.BlockSpec((None, SUB, LANES), lambda t: (t, 0, 0))
    tok_spec = pl.BlockSpec((TM_MOE, LANES), lambda t: (t, 0))
    return pl.pallas_call(
        _plan_kernel,
        grid=(N_MOE_TILES,),
        in_specs=[tok_spec, pl.BlockSpec((TM_MOE, TM_MOE), lambda t: (0, 0))],
        out_specs=[tok_spec, pl.BlockSpec((None, SUB, TM_MOE), lambda t: (t, 0, 0)), tok_spec,
                   seg_spec, seg_spec],
        out_shape=[jax.ShapeDtypeStruct((NT, ROUTE_LANES), F32), jax.ShapeDtypeStruct((N_MOE_TILES, SUB, TM_MOE), F32),
                   jax.ShapeDtypeStruct((NT, LANES), F32), seg_shape, seg_shape],
        compiler_params=_cp(("parallel",)),
        name="moe_plan",
    )(logits, triu)


def _moe_kernel(start_ref, ntile_ref, h_ref, comb_ref, drow_ref, dcol_ref, wg_ref, wu_ref, wd_ref,
                o_ref, xs, cws, ys):
    q = pl.program_id(1)
    g = q // (N_EXP_PER_GROUP // EXP_PER_STEP)
    tiles = range(TILES_PER_PASS)

    @pl.when(q == 0)
    def _sort_in():
        slot = lax.broadcasted_iota(jnp.int32, (MOE_ROWS, TM_MOE), 0).astype(F32)

        def sort_tile(j, carry):
            tok = pl.ds(pl.multiple_of(j * TM_MOE, TM_MOE), TM_MOE)
            perm = jnp.where(slot == drow_ref[j, 0:1, :], 1.0, 0.0).astype(BF16)
            xs[j] = _dot(perm, h_ref[tok, :]).astype(BF16)
            ch, cl = _split2(comb_ref[tok, :])
            cws[j] = _dot(perm, ch) + _dot(perm, cl)
            ys[j] = jnp.zeros((MOE_ROWS, D), F32)
            return carry

        lax.fori_loop(0, TILES_PER_PASS, sort_tile, 0)

    segs = [(pl.program_id(0) * TILES_PER_PASS + j) * N_GROUPS + g for j in tiles]
    firsts = [start_ref[seg] for seg in segs]

    def visit(windows, n_rows):
        rows = [(j, pl.ds(pl.multiple_of(row0 * SEG, SEG), n_rows)) for j, row0 in windows]
        x = jnp.concatenate([xs[j, r, :] for j, r in rows], axis=0)
        cw_all = jnp.concatenate([cws[j, r, :] for j, r in rows], axis=0)
        lane = lax.broadcasted_iota(jnp.int32, cw_all.shape, 1)
        acc = jnp.concatenate([ys[j, r, :] for j, r in rows], axis=0)
        for k in range(EXP_PER_STEP):
            e_lane = q * EXP_PER_STEP + k + E_LANE0
            cw = jnp.sum(jnp.where(lane == e_lane, cw_all, 0.0), axis=-1, keepdims=True)
            hid = (_silu(_dot(x, wg_ref[k].astype(BF16))) * _dot(x, wu_ref[k].astype(BF16)) * cw).astype(BF16)
            acc = acc + _dot(hid, wd_ref[k].astype(BF16))
        for i, (j, r) in enumerate(rows):
            ys[j, r, :] = acc[i * n_rows:(i + 1) * n_rows]

    needs = [ntile_ref[seg] for seg in segs]
    window = lambda j, size: (j, jnp.minimum(firsts[j], MOE_ROWS // SEG - size))

    def size_class(need, sizes, guard, run):
        lo = 0
        for size in sizes:
            pl.when(jnp.logical_and(guard, jnp.logical_and(need > lo, need <= size)))(
                functools.partial(run, size))
            lo = size

    need_all = functools.reduce(jnp.maximum, needs)
    together = need_all <= MOE_SHARED_SIZES[-1]
    size_class(need_all, MOE_SHARED_SIZES, together,
               lambda size: visit([window(j, size) for j in tiles], size * SEG))
    for j in tiles:
        size_class(needs[j], MOE_SIZES, jnp.logical_not(together),
                   lambda size, j=j: visit([window(j, size)], size * SEG))

    @pl.when(q == N_EXPERTS // EXP_PER_STEP - 1)
    def _sort_out():
        slot = lax.broadcasted_iota(jnp.int32, (TM_MOE, MOE_ROWS), 1).astype(F32)

        def unsort_tile(j, carry):
            tok = pl.ds(pl.multiple_of(j * TM_MOE, TM_MOE), TM_MOE)
            perm_t = jnp.where(slot == dcol_ref[tok, 0:1], 1.0, 0.0).astype(BF16)
            o_ref[tok, :] = _dot(perm_t, ys[j].astype(BF16)).astype(BF16)
            return carry

        lax.fori_loop(0, TILES_PER_PASS, unsort_tile, 0)


def _moe(h2, comb, drow, dcol, seg_start, seg_ntile, wg, wu, wd):
    rows = TILES_PER_PASS * TM_MOE
    once = dict(pipeline_mode=pl.Buffered(1))
    weights = lambda shape: pl.BlockSpec(shape, lambda p, q, s, n: (q, 0, 0))
    grid_spec = pltpu.PrefetchScalarGridSpec(
        num_scalar_prefetch=2,
        grid=(N_MOE_TILES // TILES_PER_PASS, N_EXPERTS // EXP_PER_STEP),
        in_specs=[pl.BlockSpec((rows, D), lambda p, q, s, n: (p, 0), **once),
                  pl.BlockSpec((rows, ROUTE_LANES), lambda p, q, s, n: (p, 0), **once),
                  pl.BlockSpec((TILES_PER_PASS, SUB, TM_MOE), lambda p, q, s, n: (p, 0, 0), **once),
                  pl.BlockSpec((rows, LANES), lambda p, q, s, n: (p, 0), **once),
                  weights((EXP_PER_STEP, D, D_EXPERT)), weights((EXP_PER_STEP, D, D_EXPERT)),
                  weights((EXP_PER_STEP, D_EXPERT, D))],
        out_specs=pl.BlockSpec((rows, D), lambda p, q, s, n: (p, 0)),
        scratch_shapes=[pltpu.VMEM((TILES_PER_PASS, MOE_ROWS, D), BF16),
                        pltpu.VMEM((TILES_PER_PASS, MOE_ROWS, ROUTE_LANES), F32),
                        pltpu.VMEM((TILES_PER_PASS, MOE_ROWS, D), F32)])
    return pl.pallas_call(
        _moe_kernel,
        grid_spec=grid_spec,
        out_shape=jax.ShapeDtypeStruct((NT, D), BF16),
        compiler_params=_cp(("parallel", "arbitrary")),
        name="moe_experts",
    )(seg_start, seg_ntile, h2, comb, drow, dcol, wg, wu, wd)


def _final_kernel(x1_ref, moe_ref, mod_ref, g_ref, o_ref):
    g2 = mod_ref[:, 5 * D:6 * D]
    x2 = x1_ref[...] + g2 * moe_ref[...].astype(F32)
    o_ref[...] = _rms(x2, g_ref[...])


def _final(x1, moe, mod3, final_g, tile0, n_tiles, out_block, name, merge=1):
    rows = merge * TM
    tile = pl.BlockSpec((rows, D), lambda i: (tile0 // merge + i, 0))
    return pl.pallas_call(
        _final_kernel,
        grid=(n_tiles // merge,),
        in_specs=[tile, tile,
                  pl.BlockSpec((None, 1, N_MOD * D), lambda i: (_mod_row(tile0 + i * merge), 0, 0)),
                  pl.BlockSpec((1, D), lambda i: (0, 0))],
        out_specs=pl.BlockSpec((rows, D), lambda i: (out_block(tile0 // merge + i), 0)),
        out_shape=jax.ShapeDtypeStruct((n_tiles * TM, D), F32),
        compiler_params=_cp(("parallel",)),
        name=name,
    )(x1, moe, mod3, final_g)


def _pos_embed(rows):
    t = np.arange(rows * GRID_W)
    row = (t // GRID_W).astype(np.float32)
    col = (t % GRID_W).astype(np.float32)
    quarter = D // 4
    freqs = (1.0 / (10000.0 ** (np.arange(quarter, dtype=np.float32) / quarter))).astype(np.float32)
    ang_r = row[:, None] * freqs[None, :]
    ang_c = col[:, None] * freqs[None, :]
    pe = np.concatenate([np.sin(ang_r), np.cos(ang_r), np.sin(ang_c), np.cos(ang_c)], axis=-1)
    return jnp.asarray(pe, F32)


def _selection_constants():
    ch = np.arange(D_RWKV)
    ones512 = (ch[:, None] // HEAD == ch[None, :] // HEAD).astype(np.float32)
    t = np.arange(TM)
    same_chunk = t[:, None] // CH == t[None, :] // CH
    cum_f = same_chunk & (t[None, :] <= t[:, None])
    cum_b = same_chunk & (t[None, :] >= t[:, None])
    sel8 = np.arange(NCH)[:, None] == t[None, :] // CH
    col = np.arange(HC)
    hm = col[:, None] // CH == ch[None, :] // HEAD
    bm = col[:, None] // CH == col[None, :] // CH
    tt, jj = np.arange(CH)[:, None], col[None, :] % CH
    masks = np.stack([jj < tt, jj <= tt, jj > tt, jj >= tt])
    eye = jj == tt
    bf = lambda x: jnp.asarray(x, BF16)
    f32 = lambda x: jnp.asarray(x, F32)
    pad = np.zeros((2 * SUB - NCH, TM), bool)
    sums = [np.concatenate([cum, same_chunk, sel8, pad]) for cum in (cum_f, cum_b)]
    prep_consts = (bf(ones512), bf(np.stack(sums)), bf(hm), bf(bm),
                   f32(masks.reshape(2, 2, CH, HC)), f32(eye))
    scan_consts = (bf(hm), bf(ones512))
    return prep_consts, scan_consts


def kernel(x_prompt, x_sample, state_rwkv, c, c_ctx, ada_w, ada_b, norm1_g, w_in, tshift_mu, decay_w0, decay_lora_b, iclr_a0, iclr_lora_b, key_k, key_a, bonus_r_k, gate_lora_b, gn_g, gn_b, conv_dw_w, conv_dw_b, conv_ln_g, conv_ln_b, w_out, norm2_g, router_group_w, router_group_b, router_expert_w, router_expert_b, expert_w_gate, expert_w_up, expert_w_down, final_norm_g):
    assert x_prompt.shape == (N_CTX_SEQ, T_CTX, D) and x_sample.shape == (N_LAT_SEQ, T_LAT, D)
    assert ada_w.shape[0] == 1, "one trunk layer"
    prep_consts, scan_consts = _selection_constants()
    ones512 = prep_consts[0]
    xp = x_prompt.reshape(N_CTX_SEQ * T_CTX, D)
    xs = x_sample.reshape(N_LAT_SEQ * T_LAT, D)
    pe = _pos_embed(T_LAT // GRID_W)

    cond8 = jnp.concatenate([c_ctx[None, :], c, jnp.zeros((8 - 1 - N_LAT_SEQ, D), F32)], axis=0)
    mod3 = _adaln(cond8, ada_w[0], ada_b[0][None, :]).reshape(8, 1, N_MOD * D)

    zs, zg, zc = _inproj(xp, xs, pe, mod3, norm1_g, w_in[0].astype(BF16))

    zero = jnp.zeros((2, LORA, D_RWKV), F32)
    lora2 = jnp.concatenate([jnp.concatenate([decay_lora_b[0], zero], axis=2),
                             jnp.concatenate([zero, iclr_lora_b[0]], axis=2)], axis=1)
    vec = lambda p: p.reshape(2, 1, -1)
    prep_out = _prep(zs, vec(tshift_mu[0]), vec(decay_w0[0]), vec(iclr_a0[0]), lora2,
                     vec(key_k[0]), vec(key_a[0]), vec(bonus_r_k[0]), *prep_consts)
    bonus = prep_out[10]

    s0_lat = state_rwkv[:, 0].transpose(1, 0, 4, 2, 3).reshape(2, N_LAT_SEQ, HEAD, D_RWKV)
    yf, yb, s_fin = _scan(prep_out, s0_lat, scan_consts)

    yconv = _conv(zc, jnp.concatenate([conv_dw_w[0], jnp.zeros((1, D_CONV), F32)], axis=0),
                  conv_dw_b, conv_ln_g, conv_ln_b)

    router_w = jnp.concatenate([router_group_w[0], router_expert_w[0],
                                jnp.zeros((D, ROUTE_LANES - N_GROUPS - N_EXPERTS), F32)], axis=1)
    router_b = jnp.concatenate([router_group_b[0], router_expert_b[0],
                                jnp.zeros((ROUTE_LANES - N_GROUPS - N_EXPERTS,), F32)])[None, :]
    x1, h2, logits = _outproj(xp, xs, pe, mod3, yf, yb, bonus, zg, yconv,
                              gn_g, gn_b, gate_lora_b[0].astype(BF16), w_out[0].astype(BF16), norm2_g,
                              router_w, router_b, ones512)

    tok = np.arange(TM_MOE)
    triu = jnp.asarray(tok[:, None] < tok[None, :], BF16)
    comb, drow, dcol, seg_start, seg_ntile = _plan(logits, triu)
    seg_start = seg_start[:, :N_GROUPS, 0].reshape(-1)
    seg_ntile = seg_ntile[:, :N_GROUPS, 0].reshape(-1)
    moe = _moe(h2, comb, drow, dcol, seg_start, seg_ntile,
               expert_w_gate[0], expert_w_up[0], expert_w_down[0])
    fg = final_norm_g[None, :]
    y_prompt = _final(x1, moe, mod3, fg, 0, N_CTX_TILES, lambda i: i, "final_norm_ctx", merge=2)
    y_sample = _final(x1, moe, mod3, fg, N_CTX_TILES, N_LAT_TILES, _xs_block, "final_norm_lat")
    y_prompt = y_prompt.reshape(N_CTX_SEQ, T_CTX, D)
    y_sample = y_sample.reshape(N_LAT_SEQ, T_LAT, D)
    new_state = s_fin[:, None].astype(state_rwkv.dtype)
    return (y_prompt, y_sample, new_state)
```

```python
import functools

import numpy as np
import jax
import jax.numpy as jnp
from jax import lax
from jax.experimental import pallas as pl
from jax.experimental.pallas import tpu as pltpu

F32 = jnp.float32
BF16 = jnp.bfloat16

D = 1024
N_CTX_SEQ = 16
T_CTX = 256
N_LAT_SEQ = 2
T_LAT = 1024
TM = 256
N_CTX_TILES = N_CTX_SEQ * T_CTX // TM
LAT_CHUNKS = T_LAT // TM
N_LAT_TILES = N_LAT_SEQ * LAT_CHUNKS
N_TILES = N_CTX_TILES + N_LAT_TILES
NT = N_TILES * TM
GRID_W = 64
D_RWKV = 512
D_CONV = 512
HEAD = 64
N_HEADS = 8
CONV_W = 31
CONV_PAD = CONV_W // 2
LORA = 64
LORA_G = 128
SHIFT_COLS = 3 * D_RWKV + 2 * LORA
N_GROUPS = 4
N_EXP_PER_GROUP = 8
N_EXPERTS = 32
D_EXPERT = 256
N_MOD = 6
RMS_EPS = 1e-6
LN_EPS = 1e-5
GN_EPS = 64e-5
KK_EPS = 1e-12
LANES = 128
SUB = 8
CH = 32
NCH = TM // CH
HC = N_HEADS * CH
DOUBLINGS = 4
ROUTE_LANES = 128
E_LANE0 = N_GROUPS
GROUP_LANE = 36
TM_MOE = 1024
SEG = 64
EXP_PER_STEP = 2
TILES_PER_PASS = 2
MOE_ROWS = TM_MOE + N_GROUPS * SEG
MOE_SHARED_SIZES = (2, 4, 6)
MOE_SIZES = (8, 12, MOE_ROWS // SEG)
N_MOE_TILES = NT // TM_MOE
VMEM_LIMIT = 56 * 1024 * 1024


def _cp(sem, flags=None):
    return pltpu.CompilerParams(dimension_semantics=sem, vmem_limit_bytes=VMEM_LIMIT, flags=flags)


def _split2(a):
    hi = a.astype(BF16)
    lo = (a - hi.astype(F32)).astype(BF16)
    return hi, lo


def _dot(a, b):
    return jnp.dot(a, b, preferred_element_type=F32)


def _dot_hp(a, b):
    ah, al = _split2(a)
    bh, bl = _split2(b)
    return _dot(ah, bh) + _dot(ah, bl) + _dot(al, bh)


def _dot_sel(a, sel):
    h, l = _split2(a)
    return _dot(h, sel) + _dot(l, sel)


def _sel_dot(sel, a):
    h, l = _split2(a)
    return _dot(sel, h) + _dot(sel, l)


def _sigmoid(x):
    return 1.0 / (1.0 + jnp.exp(-x))


def _silu(x):
    return x * _sigmoid(x)


def _lat_js(i):
    il = jnp.maximum(i - N_CTX_TILES, 0)
    return il // N_LAT_SEQ, il % N_LAT_SEQ


def _xp_block(i):
    return jnp.minimum(i, N_CTX_TILES - 1)


def _xs_block(i):
    j, s = _lat_js(i)
    return s * LAT_CHUNKS + j


def _pe_block(i):
    j, _ = _lat_js(i)
    return j


def _mod_row(i):
    _, s = _lat_js(i)
    return jnp.where(i < N_CTX_TILES, 0, 1 + s)


def _first_last(i):
    j, _ = _lat_js(i)
    is_ctx = i < N_CTX_TILES
    first = jnp.logical_or(is_ctx, j == 0)
    last = jnp.logical_or(is_ctx, j == LAT_CHUNKS - 1)
    return first, last


def _prev_tile(i):
    first, _ = _first_last(i)
    return jnp.where(first, i, i - N_LAT_SEQ)


def _next_tile(i):
    _, last = _first_last(i)
    return jnp.where(last, i, i + N_LAT_SEQ)


def _adaln_kernel(c_ref, w_ref, b_ref, o_ref):
    c = c_ref[...]
    o_ref[...] = _dot_hp(_silu(c), w_ref[...]) + b_ref[...]


def _adaln(cond8, ada_w, ada_b):
    tn = 1536
    n = ada_w.shape[1]
    return pl.pallas_call(
        _adaln_kernel,
        grid=(n // tn,),
        in_specs=[pl.BlockSpec((8, D), lambda j: (0, 0)),
                  pl.BlockSpec((D, tn), lambda j: (0, j)),
                  pl.BlockSpec((1, tn), lambda j: (0, j))],
        out_specs=pl.BlockSpec((8, tn), lambda j: (0, j)),
        out_shape=jax.ShapeDtypeStruct((8, n), F32),
        compiler_params=_cp(("parallel",)),
        name="adaln",
    )(cond8, ada_w, ada_b)


def _load_x(i, xp_ref, xs_ref, pe_ref):
    f = (i >= N_CTX_TILES).astype(F32)
    return xp_ref[...] * (1.0 - f) + (xs_ref[...] + pe_ref[...]) * f


def _x_specs():
    return [pl.BlockSpec((TM, D), lambda i: (_xp_block(i), 0)),
            pl.BlockSpec((TM, D), lambda i: (_xs_block(i), 0)),
            pl.BlockSpec((TM, D), lambda i: (_pe_block(i), 0))]


def _rms(x, g):
    return x * lax.rsqrt(jnp.mean(x * x, axis=-1, keepdims=True) + RMS_EPS) * g


def _inproj_kernel(xp_ref, xs_ref, pe_ref, mod_ref, g_ref, w_ref, zs_ref, zg_ref, zc_ref):
    i = pl.program_id(0)
    x = _load_x(i, xp_ref, xs_ref, pe_ref)
    sh1 = mod_ref[:, 0:D]
    sc1 = mod_ref[:, D:2 * D]
    h = (_rms(x, g_ref[...]) * (1.0 + sc1) + sh1).astype(BF16)
    zs_ref[...] = _dot(h, w_ref[:, 0:SHIFT_COLS])
    zg_ref[...] = _dot(h, w_ref[:, SHIFT_COLS:SHIFT_COLS + LORA_G])
    zc_ref[...] = _dot(h, w_ref[:, SHIFT_COLS + LORA_G:])


def _inproj(xp, xs, pe, mod3, norm1_g, w_in_bf):
    in_cols = w_in_bf.shape[1]
    return pl.pallas_call(
        _inproj_kernel,
        grid=(N_TILES,),
        in_specs=_x_specs() + [
            pl.BlockSpec((None, 1, N_MOD * D), lambda i: (_mod_row(i), 0, 0)),
            pl.BlockSpec((1, D), lambda i: (0, 0)),
            pl.BlockSpec((D, in_cols), lambda i: (0, 0))],
        out_specs=[pl.BlockSpec((TM, SHIFT_COLS), lambda i: (i, 0)),
                   pl.BlockSpec((TM, LORA_G), lambda i: (i, 0)),
                   pl.BlockSpec((TM, 2 * D_CONV), lambda i: (i, 0))],
        out_shape=[jax.ShapeDtypeStruct((NT, SHIFT_COLS), F32),
                   jax.ShapeDtypeStruct((NT, LORA_G), F32),
                   jax.ShapeDtypeStruct((NT, 2 * D_CONV), F32)],
        compiler_params=_cp(("parallel",)),
        name="inproj",
    )(xp, xs, pe, mod3, norm1_g, w_in_bf)


NT_DIMS = (((1,), (1,)), ((), ()))


def _tile_rows(x):
    return jnp.concatenate([x] * N_HEADS, axis=0)


def _prep_kernel(zs_ref, prev_ref, next_ref, mu_ref, w0_ref, a0_ref, lora_ref, kk_ref_, ka_ref,
                 rk_ref, ones_ref, cum_ref,
                 at_out, rt_out, bh_out, kh_out, v_out, bt_out, kt_out, gc_out, bon_out):
    i = pl.program_id(0)
    first, last = _first_last(i)
    cur = zs_ref[...]
    prow = prev_ref[7:8, :] * (1.0 - first.astype(F32))
    nrow = next_ref[0:1, :] * (1.0 - last.astype(F32))
    rows = lax.broadcasted_iota(jnp.int32, (TM, 1), 0)
    shifted = (jnp.where(rows == 0, prow, pltpu.roll(cur, 1, axis=0)),
               jnp.where(rows == TM - 1, nrow, pltpu.roll(cur, TM - 1, axis=0)))
    ones = ones_ref[...]
    lane = lax.broadcasted_iota(jnp.int32, (TM, 2 * LORA), 1)
    for d in range(2):
        xs = cur + (shifted[d] - cur) * mu_ref[d]
        r = xs[:, 0:D_RWKV]
        k = xs[:, D_RWKV:2 * D_RWKV]
        v = xs[:, 2 * D_RWKV:3 * D_RWKV]
        z2 = xs[:, 3 * D_RWKV:SHIFT_COLS]
        lin = jnp.where(lane < LORA, jnp.tanh(z2), z2)
        lo = _dot_hp(lin, lora_ref[d])
        u = -(w0_ref[d] + lo[:, 0:D_RWKV])
        softplus = jnp.maximum(u, 0.0) + jnp.log(1.0 + jnp.exp(-jnp.abs(u)))
        w_log = -softplus - 0.5
        lw = -jnp.exp(w_log)
        a = _sigmoid(a0_ref[d] + lo[:, D_RWKV:])
        kx = k * kk_ref_[d]
        kk = kx * lax.rsqrt(jnp.maximum(_dot_sel(kx * kx, ones), KK_EPS * KK_EPS))
        k2 = k * (1.0 + (a - 1.0) * ka_ref[d])
        bvec = kk * a
        bon_out[d] = _dot_sel(r * k2 * rk_ref[d], ones) * v

        sums = _sel_dot(cum_ref[d], lw)
        lg = sums[0:TM]
        tot = sums[TM:2 * TM]
        e_tail = jnp.exp(tot - lg)
        e_inv = jnp.exp(-lg)
        at = (-(kk * jnp.exp(lg - lw))).astype(BF16)
        bt = (bvec * e_inv).astype(BF16)
        kt = (k2 * e_inv).astype(BF16)
        rt = (r * jnp.exp(lg)).astype(BF16)
        at_out[d] = at
        rt_out[d] = rt
        bh_out[d] = (bvec * e_tail).astype(BF16)
        kh_out[d] = (k2 * e_tail).astype(BF16)
        v_out[d] = v.astype(BF16)
        gc_out[d] = jnp.exp(sums[2 * TM:2 * TM + NCH])
        bt_out[d] = bt
        kt_out[d] = kt


def _chunkmat_kernel(at_ref, rt_ref, bt_ref, kt_ref, hm_ref, bm_ref, msk_ref, eye_ref,
                     t_out, aak_out, arb_out, ark_out):
    scaled = [(at_ref[d], bt_ref[d], kt_ref[d], rt_ref[d]) for d in range(2)]
    hm = hm_ref[...]
    bm = bm_ref[...]

    def expand(z):
        return _tile_rows(z.astype(BF16)) * bm

    chunks = [(d, slice(c * CH, (c + 1) * CH)) for d in range(2) for c in range(NCH)]
    pws = []
    for d, rows in chunks:
        at, bt, kt, rt = scaled[d]
        ar = jnp.concatenate([at[rows], rt[rows]], axis=0)
        bk = jnp.concatenate([_tile_rows(bt[rows]) * hm, _tile_rows(kt[rows]) * hm], axis=0)
        p1 = lax.dot_general(ar, bk, NT_DIMS, preferred_element_type=F32)
        m_strict, m_incl = msk_ref[d, 0], msk_ref[d, 1]
        pws.append(p1[0:CH, 0:HC] * m_strict)
        aak_out[d, rows, :] = (p1[0:CH, HC:] * m_strict).astype(BF16)
        arb_out[d, rows, :] = (p1[CH:, 0:HC] * m_incl).astype(BF16)
        ark_out[d, rows, :] = (p1[CH:, HC:] * m_incl).astype(BF16)
    tms = [eye_ref[...] + pw for pw in pws]
    pws = [_dot(pw.astype(BF16), expand(pw)) for pw in pws]
    for k in range(1, DOUBLINGS + 1):
        if k < DOUBLINGS:
            prods = [_dot(jnp.concatenate([pw, tm], axis=0).astype(BF16), expand(pw))
                     for pw, tm in zip(pws, tms)]
            pws = [p[0:CH] for p in prods]
            tms = [tm + p[CH:] for tm, p in zip(tms, prods)]
        else:
            tms = [tm + _dot(tm.astype(BF16), expand(pw)) for tm, pw in zip(tms, pws)]
    for (d, rows), tm in zip(chunks, tms):
        t_out[d, rows, :] = tm.astype(BF16)


def _prep(zs, mu, w0, a0, lora2, key_k, key_a, r_k, ones512, cum2, hm, bm, masks, eye):
    full = lambda a: pl.BlockSpec(a.shape, lambda i: (0,) * a.ndim)
    out_spec = pl.BlockSpec((2, TM, D_RWKV), lambda i: (0, i, 0))
    bf_shape = jax.ShapeDtypeStruct((2, NT, D_RWKV), BF16)
    tall_spec = pl.BlockSpec((2, TM, HC), lambda i: (0, i, 0))
    rows8 = TM // 8
    params = (mu, w0, a0, lora2, key_k, key_a, r_k, ones512, cum2)
    at, rt, bh, kh, v, bt, kt, gc, bonus = pl.pallas_call(
        _prep_kernel,
        grid=(N_TILES,),
        in_specs=[pl.BlockSpec((TM, SHIFT_COLS), lambda i: (i, 0)),
                  pl.BlockSpec((8, SHIFT_COLS), lambda i: (_prev_tile(i) * rows8 + rows8 - 1, 0)),
                  pl.BlockSpec((8, SHIFT_COLS), lambda i: (_next_tile(i) * rows8, 0))]
                 + [full(p) for p in params],
        out_specs=[out_spec] * 7 + [pl.BlockSpec((2, NCH, D_RWKV), lambda i: (0, i, 0)), out_spec],
        out_shape=[bf_shape] * 7 + [jax.ShapeDtypeStruct((2, N_TILES * NCH, D_RWKV), F32),
                                    jax.ShapeDtypeStruct((2, NT, D_RWKV), F32)],
        compiler_params=_cp(("parallel",)),
        name="rwkv_prep",
    )(zs, zs, zs, *params)
    consts = (hm, bm, masks, eye)
    t, aak, arb, ark = pl.pallas_call(
        _chunkmat_kernel,
        grid=(N_TILES,),
        in_specs=[out_spec] * 4 + [full(p) for p in consts],
        out_specs=[tall_spec] * 4,
        out_shape=[jax.ShapeDtypeStruct((2, NT, HC), BF16)] * 4,
        compiler_params=_cp(("parallel",)),
        name="rwkv_chunk_mats",
    )(at, rt, bt, kt, *consts)
    return at, rt, bh, kh, v, t, aak, arb, ark, gc, bonus


SCAN_TILES = 4
CTX_STEPS = N_CTX_TILES // SCAN_TILES
LAT_STEPS = N_LAT_TILES // SCAN_TILES
N_SCAN_STEPS = CTX_STEPS + LAT_STEPS
N_SLOTS = 2 * SCAN_TILES


def _scan_block(step, d):
    jl = step - CTX_STEPS
    j = jnp.where(d == 0, jl, LAT_STEPS - 1 - jl)
    return jnp.where(step >= CTX_STEPS, CTX_STEPS + j, step)


def _scan_state_block(step):
    return jnp.minimum(step, CTX_STEPS - 1)


def _chunk_kernel(*refs):
    (atf, rtf, bhf, khf, vf, tf, aakf, arbf, arkf, gcf, atb, rtb, bhb, khb, vb, tb, aakb, arbb, arkb, gcb,
     s0_ref, hm_ref, bd_ref, yf_ref, yb_ref, sout_ref, mw) = refs
    step = pl.program_id(0)
    in_refs = ((atf, rtf, bhf, khf, vf, tf, aakf, arbf, arkf, gcf),
               (atb, rtb, bhb, khb, vb, tb, aakb, arbb, arkb, gcb))
    y_refs = (yf_ref, yb_ref)
    bd = bd_ref[...]
    hm = hm_ref[...]
    crow = lax.broadcasted_iota(jnp.int32, (NCH, 1), 0)
    heads = range(N_HEADS)

    def advance(chains):
        rows_of = lambda tile, cc: pl.ds(pl.multiple_of(tile * TM + cc * CH, CH), CH)
        chains = [(dl, slot, tile, cc, rows_of(tile, cc)) for dl, slot, tile, cc in chains]
        ld = lambda idx, dl, rows: in_refs[dl][idx][rows, :]
        m0s = [mw[slot] for _, slot, _, _, _ in chains]
        xy0s = [_dot(jnp.concatenate([ld(0, dl, rows), ld(1, dl, rows)], axis=0),
                     _tile_rows(m0.astype(BF16)) * bd)
                for (dl, _, _, _, rows), m0 in zip(chains, m0s)]
        vvs = [ld(4, dl, rows) for dl, _, _, _, rows in chains]
        avs = [_dot(jnp.concatenate([ld(6, dl, rows), ld(8, dl, rows)], axis=0), _tile_rows(vv) * hm)
               for (dl, _, _, _, rows), vv in zip(chains, vvs)]
        ubs = [_dot(ld(5, dl, rows), _tile_rows((xy0[0:CH] + av[0:CH]).astype(BF16)) * hm).astype(BF16)
               for (dl, _, _, _, rows), xy0, av in zip(chains, xy0s, avs)]
        for (dl, _, _, _, rows), xy0, av, ub in zip(chains, xy0s, avs, ubs):
            y_refs[dl][rows, :] = xy0[CH:] + av[CH:] + _dot(ld(7, dl, rows), _tile_rows(ub) * hm)
        for (dl, slot, tile, cc, rows), m0, ub, vv in zip(chains, m0s, ubs, vvs):
            gc_tile = in_refs[dl][9][pl.ds(pl.multiple_of(tile * NCH, NCH), NCH), :]
            gcrow = jnp.sum(jnp.where(crow == cc, gc_tile, 0.0), axis=0, keepdims=True)
            pad = jnp.zeros((LANES - 2 * CH - SUB, D_RWKV), F32)
            stack = jnp.concatenate([ld(2, dl, rows).astype(F32), ld(3, dl, rows).astype(F32),
                                     jnp.broadcast_to(gcrow, (SUB, D_RWKV)), pad], axis=0)
            stack_t = stack.T
            head = lambda h: stack_t[h * HEAD:(h + 1) * HEAD, :]
            bk_t = jnp.concatenate([head(h)[:, 0:2 * CH] for h in heads], axis=1)
            decay = jnp.concatenate([jnp.broadcast_to(head(h)[:, 2 * CH:2 * CH + 1], (HEAD, HEAD))
                                     for h in heads], axis=1)
            uv = _tile_rows(jnp.concatenate([ub, vv], axis=0)) * bd
            mw[slot] = m0 * decay + _dot(bk_t.astype(BF16), uv)

    @pl.when(step < CTX_STEPS)
    def _context():
        mw[...] = jnp.zeros(mw.shape, F32)

        def body(q, carry):
            advance([(dl, dl * SCAN_TILES + sl, jnp.int32(sl), q if dl == 0 else NCH - 1 - q)
                     for dl in range(2) for sl in range(SCAN_TILES)])
            return carry

        lax.fori_loop(0, NCH, body, 0)
        for dl in range(2):
            for sl in range(SCAN_TILES):
                mt = mw[dl * SCAN_TILES + sl]
                by_hv = jnp.concatenate([mt, jnp.zeros_like(mt)], axis=0).T
                for h in range(N_HEADS):
                    sout_ref[sl, dl, h] = by_hv[h * HEAD:(h + 1) * HEAD, 0:HEAD]

    @pl.when(step == CTX_STEPS)
    def _load_state():
        for dl in range(2):
            for sl in range(N_LAT_SEQ):
                mw[dl * SCAN_TILES + sl] = s0_ref[dl, sl]

    @pl.when(step >= CTX_STEPS)
    def _latent():
        n_q = (SCAN_TILES // N_LAT_SEQ) * NCH

        def body(q, carry):
            chains = []
            for dl in range(2):
                qq = q if dl == 0 else n_q - 1 - q
                for sl in range(N_LAT_SEQ):
                    chains.append((dl, dl * SCAN_TILES + sl, (qq // NCH) * N_LAT_SEQ + sl, qq % NCH))
            advance(chains)
            return carry

        lax.fori_loop(0, n_q, body, 0)


def _scan(prep_out, s0_lat, consts):
    rows_per_step = SCAN_TILES * TM

    def row_spec(d, width):
        return pl.BlockSpec((None, rows_per_step, width), lambda s: (d, _scan_block(s, d), 0))

    def gc_spec(d):
        return pl.BlockSpec((None, SCAN_TILES * NCH, D_RWKV), lambda s: (d, _scan_block(s, d), 0))

    def y_spec(d):
        return pl.BlockSpec((rows_per_step, D_RWKV), lambda s: (_scan_block(s, d), 0))

    rows, gc = list(prep_out[:9]), prep_out[9]
    row_specs = lambda d: [row_spec(d, D_RWKV)] * 5 + [row_spec(d, HC)] * 4 + [gc_spec(d)]
    const = lambda a: pl.BlockSpec(a.shape, lambda s: (0,) * a.ndim)
    y_shape = jax.ShapeDtypeStruct((NT, D_RWKV), F32)
    return pl.pallas_call(
        _chunk_kernel,
        grid=(N_SCAN_STEPS,),
        in_specs=row_specs(0) + row_specs(1) + [const(s0_lat)] + [const(a) for a in consts],
        out_specs=[y_spec(0), y_spec(1),
                   pl.BlockSpec((SCAN_TILES, 2, N_HEADS, HEAD, HEAD),
                                lambda s: (_scan_state_block(s), 0, 0, 0, 0))],
        out_shape=[y_shape, y_shape,
                   jax.ShapeDtypeStruct((N_CTX_SEQ, 2, N_HEADS, HEAD, HEAD), F32)],
        scratch_shapes=[pltpu.VMEM((N_SLOTS, HEAD, D_RWKV), F32)],
        compiler_params=_cp(("arbitrary",)),
        name="rwkv_scan",
    )(*rows, gc, *rows, gc, s0_lat, *consts)


HALO = 16


def _glu(z):
    return z[:, 0:D_CONV] * _sigmoid(z[:, D_CONV:])


def _conv_kernel(cur_ref, prev_ref, next_ref, w_ref, b_ref, g_ref, beta_ref, o_ref, ext):
    i = pl.program_id(0)
    first, last = _first_last(i)
    n_ext = TM + 2 * HALO
    u = jnp.concatenate([_glu(prev_ref[...]) * (1.0 - first.astype(F32)), _glu(cur_ref[...]),
                         _glu(next_ref[...]) * (1.0 - last.astype(F32))], axis=0)
    ext[0] = u
    for b in range(1, SUB):
        ext[b] = pltpu.roll(u, n_ext - b, axis=0)
    acc = jnp.zeros((TM, D_CONV), F32)
    for j in range(CONV_W):
        off = HALO - CONV_PAD + j
        acc = acc + ext[off % SUB, off - off % SUB:off - off % SUB + TM, :] * w_ref[j:j + 1, :]
    h = acc + b_ref[...]
    mu = jnp.mean(h, axis=-1, keepdims=True)
    hc = h - mu
    var = jnp.mean(hc * hc, axis=-1, keepdims=True)
    y = hc * lax.rsqrt(var + LN_EPS) * g_ref[...] + beta_ref[...]
    o_ref[...] = _silu(y)


def _conv(zc, conv_w, conv_b, ln_g, ln_b):
    nh = TM // HALO
    vec = pl.BlockSpec((1, D_CONV), lambda i: (0, 0))
    return pl.pallas_call(
        _conv_kernel,
        grid=(N_TILES,),
        in_specs=[pl.BlockSpec((TM, 2 * D_CONV), lambda i: (i, 0)),
                  pl.BlockSpec((HALO, 2 * D_CONV), lambda i: (_prev_tile(i) * nh + nh - 1, 0)),
                  pl.BlockSpec((HALO, 2 * D_CONV), lambda i: (_next_tile(i) * nh, 0)),
                  pl.BlockSpec((CONV_W + 1, D_CONV), lambda i: (0, 0)), vec, vec, vec],
        out_specs=pl.BlockSpec((TM, D_CONV), lambda i: (i, 0)),
        out_shape=jax.ShapeDtypeStruct((NT, D_CONV), F32),
        scratch_shapes=[pltpu.VMEM((SUB, TM + 2 * HALO, D_CONV), F32)],
        compiler_params=_cp(("parallel",)),
        name="conv_module",
    )(zc, zc, zc, conv_w, conv_b, ln_g, ln_b)


def _outproj_kernel(xp_ref, xs_ref, pe_ref, mod_ref, yf_ref, yb_ref, bf_ref, bb_ref, zg_ref, yc_ref,
                    gng_ref, gnb_ref, gl_ref, wo_ref, n2_ref, rw_ref, rb_ref, ones_ref,
                    x1_ref, h2_ref, logit_ref):
    i = pl.program_id(0)
    x = _load_x(i, xp_ref, xs_ref, pe_ref)
    g1 = mod_ref[:, 2 * D:3 * D]
    sh2 = mod_ref[:, 3 * D:4 * D]
    sc2 = mod_ref[:, 4 * D:5 * D]
    ones = ones_ref[...]
    y = (yf_ref[...] + bf_ref[...]) + (yb_ref[...] + bb_ref[...])
    mu = _dot_sel(y, ones) * (1.0 / HEAD)
    yc = y - mu
    var = _dot_sel(yc * yc, ones) * (1.0 / HEAD)
    yn = yc * lax.rsqrt(var + GN_EPS) * gng_ref[...] + gnb_ref[...]
    gate = _dot(_sigmoid(zg_ref[...]).astype(BF16), gl_ref[...])
    y_rwkv = (yn * gate).astype(BF16)
    mix = _dot(y_rwkv, wo_ref[0:D_RWKV, :]) + _dot(yc_ref[...].astype(BF16), wo_ref[D_RWKV:, :])
    x1 = x + g1 * mix
    x1_ref[...] = x1
    h2 = _rms(x1, n2_ref[...]) * (1.0 + sc2) + sh2
    h2_ref[...] = h2.astype(BF16)

    logit_ref[...] = _dot_hp(h2, rw_ref[...]) + rb_ref[...]


def _route(logits):
    lane = lax.broadcasted_iota(jnp.int32, logits.shape, 1)
    lanef = lane.astype(F32)
    neg = jnp.float32(-1e30)
    big = jnp.float32(1e9)
    gmask = lane < N_GROUPS
    gl = jnp.where(gmask, logits, neg)
    ge = jnp.where(gmask, jnp.exp(gl - jnp.max(gl, axis=-1, keepdims=True)), 0.0)
    gprob = ge / jnp.sum(ge, axis=-1, keepdims=True)
    gp = jnp.max(gprob, axis=-1, keepdims=True)
    gidx = jnp.min(jnp.where(jnp.logical_and(gmask, gprob == gp), lanef, big), axis=-1, keepdims=True)
    egrp = jnp.floor((lanef - float(E_LANE0)) * (1.0 / N_EXP_PER_GROUP))
    emask = jnp.logical_and(jnp.logical_and(lane >= E_LANE0, lane < E_LANE0 + N_EXPERTS), egrp == gidx)
    el = jnp.where(emask, logits, neg)
    ee = jnp.where(emask, jnp.exp(el - jnp.max(el, axis=-1, keepdims=True)), 0.0)
    ep = ee / jnp.sum(ee, axis=-1, keepdims=True)
    m1 = jnp.max(jnp.where(emask, ep, -1.0), axis=-1, keepdims=True)
    i1 = jnp.min(jnp.where(jnp.logical_and(emask, ep == m1), lanef, big), axis=-1, keepdims=True)
    mask2 = jnp.logical_and(emask, lanef != i1)
    m2 = jnp.max(jnp.where(mask2, ep, -1.0), axis=-1, keepdims=True)
    i2 = jnp.min(jnp.where(jnp.logical_and(mask2, ep == m2), lanef, big), axis=-1, keepdims=True)
    den = m1 + m2
    return (jnp.where(lanef == i1, gp * (m1 / den), 0.0)
            + jnp.where(lanef == i2, gp * (m2 / den), 0.0)
            + jnp.where(lane == GROUP_LANE, gidx, 0.0))


def _outproj(xp, xs, pe, mod3, yf, yb, bonus, zg, yconv, gn_g, gn_b, gate_bf, w_out_bf, norm2_g,
             router_w, router_b, ones512):
    tile = lambda n: pl.BlockSpec((TM, n), lambda i: (i, 0))
    const = lambda shape: pl.BlockSpec(shape, lambda i: (0,) * len(shape))
    return pl.pallas_call(
        _outproj_kernel,
        grid=(N_TILES,),
        in_specs=_x_specs() + [
            pl.BlockSpec((None, 1, N_MOD * D), lambda i: (_mod_row(i), 0, 0)),
            tile(D_RWKV), tile(D_RWKV),
            pl.BlockSpec((None, TM, D_RWKV), lambda i: (0, i, 0)),
            pl.BlockSpec((None, TM, D_RWKV), lambda i: (1, i, 0)),
            tile(LORA_G), tile(D_CONV),
            const((1, D_RWKV)), const((1, D_RWKV)), const((LORA_G, D_RWKV)), const((D, D)),
            const((1, D)), const((D, ROUTE_LANES)), const((1, ROUTE_LANES)),
            const((D_RWKV, D_RWKV))],
        out_specs=[tile(D), tile(D), tile(ROUTE_LANES)],
        out_shape=[jax.ShapeDtypeStruct((NT, D), F32), jax.ShapeDtypeStruct((NT, D), BF16),
                   jax.ShapeDtypeStruct((NT, ROUTE_LANES), F32)],
        compiler_params=_cp(("parallel",)),
        name="outproj_router",
    )(xp, xs, pe, mod3, yf, yb, bonus, bonus, zg, yconv, gn_g, gn_b, gate_bf, w_out_bf, norm2_g,
      router_w, router_b, ones512)


def _plan_kernel(logit_ref, triu_ref, comb_ref, drow_ref, dcol_ref, start_ref, ntile_ref):
    comb = _route(logit_ref[...])
    comb_ref[...] = comb
    gidx = comb.T[GROUP_LANE:GROUP_LANE + 1, :]
    grow = lax.broadcasted_iota(jnp.int32, (SUB, 1), 0)
    growf = grow.astype(F32)
    onehot = jnp.where(jnp.logical_and(gidx == growf, grow < N_GROUPS), 1.0, 0.0)
    before = _dot(onehot.astype(BF16), triu_ref[...])
    count = jnp.sum(onehot, axis=-1, keepdims=True)
    padded = jnp.floor((count + (SEG - 1.0)) * (1.0 / SEG)) * SEG
    start = jnp.zeros((SUB, 1), F32)
    for g in range(N_GROUPS - 1):
        start = start + jnp.where(grow > g, padded[g:g + 1, :], 0.0)
    dest = jnp.sum(onehot * (start + before), axis=0, keepdims=True)
    drow_ref[...] = jnp.broadcast_to(dest, (SUB, TM_MOE))
    dcol_ref[...] = jnp.broadcast_to(dest, (LANES, TM_MOE)).T
    start_ref[...] = jnp.broadcast_to(start * (1.0 / SEG), (SUB, LANES)).astype(jnp.int32)
    ntile_ref[...] = jnp.broadcast_to(padded * (1.0 / SEG), (SUB, LANES)).astype(jnp.int32)


def _plan(logits, triu):
    seg_shape = jax.ShapeDtypeStruct((N_MOE_TILES, SUB, LANES), jnp.int32)
    seg_spec = pl.BlockSpec((None, SUB, LANES), lambda t: (t, 0, 0))
    tok_spec = pl.BlockSpec((TM_MOE, LANES), lambda t: (t, 0))
    return pl.pallas_call(
        _plan_kernel,
        grid=(N_MOE_TILES,),
        in_specs=[tok_spec, pl.BlockSpec((TM_MOE, TM_MOE), lambda t: (0, 0))],
        out_specs=[tok_spec, pl.BlockSpec((None, SUB, TM_MOE), lambda t: (t, 0, 0)), tok_spec,
                   seg_spec, seg_spec],
        out_shape=[jax.ShapeDtypeStruct((NT, ROUTE_LANES), F32), jax.ShapeDtypeStruct((N_MOE_TILES, SUB, TM_MOE), F32),
                   jax.ShapeDtypeStruct((NT, LANES), F32), seg_shape, seg_shape],
        compiler_params=_cp(("parallel",)),
        name="moe_plan",
    )(logits, triu)


def _moe_kernel(start_ref, ntile_ref, h_ref, comb_ref, drow_ref, dcol_ref, wg_ref, wu_ref, wd_ref,
                o_ref, xs, cws, ys):
    q = pl.program_id(1)
    g = q // (N_EXP_PER_GROUP // EXP_PER_STEP)
    tiles = range(TILES_PER_PASS)
    toks = [slice(j * TM_MOE, (j + 1) * TM_MOE) for j in tiles]

    @pl.when(q == 0)
    def _sort_in():
        slot = lax.broadcasted_iota(jnp.int32, (MOE_ROWS, TM_MOE), 0).astype(F32)
        for j in tiles:
            perm = jnp.where(slot == drow_ref[j, 0:1, :], 1.0, 0.0).astype(BF16)
            xs[j] = _dot(perm, h_ref[toks[j], :]).astype(BF16)
            ch, cl = _split2(comb_ref[toks[j], :])
            cws[j] = _dot(perm, ch) + _dot(perm, cl)
            ys[j] = jnp.zeros((MOE_ROWS, D), F32)

    segs = [(pl.program_id(0) * TILES_PER_PASS + j) * N_GROUPS + g for j in tiles]
    firsts = [start_ref[seg] for seg in segs]

    def visit(windows, n_rows):
        rows = [(j, pl.ds(pl.multiple_of(row0 * SEG, SEG), n_rows)) for j, row0 in windows]
        x = jnp.concatenate([xs[j, r, :] for j, r in rows], axis=0)
        cw_all = jnp.concatenate([cws[j, r, :] for j, r in rows], axis=0)
        lane = lax.broadcasted_iota(jnp.int32, cw_all.shape, 1)
        acc = jnp.concatenate([ys[j, r, :] for j, r in rows], axis=0)
        for k in range(EXP_PER_STEP):
            e_lane = q * EXP_PER_STEP + k + E_LANE0
            cw = jnp.sum(jnp.where(lane == e_lane, cw_all, 0.0), axis=-1, keepdims=True)
            hid = (_silu(_dot(x, wg_ref[k].astype(BF16))) * _dot(x, wu_ref[k].astype(BF16)) * cw).astype(BF16)
            acc = acc + _dot(hid, wd_ref[k].astype(BF16))
        for i, (j, r) in enumerate(rows):
            ys[j, r, :] = acc[i * n_rows:(i + 1) * n_rows]

    needs = [ntile_ref[seg] for seg in segs]
    window = lambda j, size: (j, jnp.minimum(firsts[j], MOE_ROWS // SEG - size))

    def size_class(need, sizes, guard, run):
        lo = 0
        for size in sizes:
            pl.when(jnp.logical_and(guard, jnp.logical_and(need > lo, need <= size)))(
                functools.partial(run, size))
            lo = size

    need_all = functools.reduce(jnp.maximum, needs)
    together = need_all <= MOE_SHARED_SIZES[-1]
    size_class(need_all, MOE_SHARED_SIZES, together,
               lambda size: visit([window(j, size) for j in tiles], size * SEG))
    for j in tiles:
        size_class(needs[j], MOE_SIZES, jnp.logical_not(together),
                   lambda size, j=j: visit([window(j, size)], size * SEG))

    @pl.when(q == N_EXPERTS // EXP_PER_STEP - 1)
    def _sort_out():
        slot = lax.broadcasted_iota(jnp.int32, (TM_MOE, MOE_ROWS), 1).astype(F32)
        for j in tiles:
            perm_t = jnp.where(slot == dcol_ref[toks[j], 0:1], 1.0, 0.0).astype(BF16)
            o_ref[toks[j], :] = _dot(perm_t, ys[j].astype(BF16)).astype(BF16)


def _moe(h2, comb, drow, dcol, seg_start, seg_ntile, wg, wu, wd):
    rows = TILES_PER_PASS * TM_MOE
    once = dict(pipeline_mode=pl.Buffered(1))
    weights = lambda shape: pl.BlockSpec(shape, lambda p, q, s, n: (q, 0, 0))
    grid_spec = pltpu.PrefetchScalarGridSpec(
        num_scalar_prefetch=2,
        grid=(N_MOE_TILES // TILES_PER_PASS, N_EXPERTS // EXP_PER_STEP),
        in_specs=[pl.BlockSpec((rows, D), lambda p, q, s, n: (p, 0), **once),
                  pl.BlockSpec((rows, ROUTE_LANES), lambda p, q, s, n: (p, 0), **once),
                  pl.BlockSpec((TILES_PER_PASS, SUB, TM_MOE), lambda p, q, s, n: (p, 0, 0), **once),
                  pl.BlockSpec((rows, LANES), lambda p, q, s, n: (p, 0), **once),
                  weights((EXP_PER_STEP, D, D_EXPERT)), weights((EXP_PER_STEP, D, D_EXPERT)),
                  weights((EXP_PER_STEP, D_EXPERT, D))],
        out_specs=pl.BlockSpec((rows, D), lambda p, q, s, n: (p, 0)),
        scratch_shapes=[pltpu.VMEM((TILES_PER_PASS, MOE_ROWS, D), BF16),
                        pltpu.VMEM((TILES_PER_PASS, MOE_ROWS, ROUTE_LANES), F32),
                        pltpu.VMEM((TILES_PER_PASS, MOE_ROWS, D), F32)])
    return pl.pallas_call(
        _moe_kernel,
        grid_spec=grid_spec,
        out_shape=jax.ShapeDtypeStruct((NT, D), BF16),
        compiler_params=_cp(("parallel", "arbitrary")),
        name="moe_experts",
    )(seg_start, seg_ntile, h2, comb, drow, dcol, wg, wu, wd)


def _final_kernel(x1_ref, moe_ref, mod_ref, g_ref, o_ref):
    g2 = mod_ref[:, 5 * D:6 * D]
    x2 = x1_ref[...] + g2 * moe_ref[...].astype(F32)
    o_ref[...] = _rms(x2, g_ref[...])


def _final(x1, moe, mod3, final_g, tile0, n_tiles, out_block, name, merge=1):
    rows = merge * TM
    tile = pl.BlockSpec((rows, D), lambda i: (tile0 // merge + i, 0))
    return pl.pallas_call(
        _final_kernel,
        grid=(n_tiles // merge,),
        in_specs=[tile, tile,
                  pl.BlockSpec((None, 1, N_MOD * D), lambda i: (_mod_row(tile0 + i * merge), 0, 0)),
                  pl.BlockSpec((1, D), lambda i: (0, 0))],
        out_specs=pl.BlockSpec((rows, D), lambda i: (out_block(tile0 // merge + i), 0)),
        out_shape=jax.ShapeDtypeStruct((n_tiles * TM, D), F32),
        compiler_params=_cp(("parallel",)),
        name=name,
    )(x1, moe, mod3, final_g)


def _pos_embed(rows):
    t = np.arange(rows * GRID_W)
    row = (t // GRID_W).astype(np.float32)
    col = (t % GRID_W).astype(np.float32)
    quarter = D // 4
    freqs = (1.0 / (10000.0 ** (np.arange(quarter, dtype=np.float32) / quarter))).astype(np.float32)
    ang_r = row[:, None] * freqs[None, :]
    ang_c = col[:, None] * freqs[None, :]
    pe = np.concatenate([np.sin(ang_r), np.cos(ang_r), np.sin(ang_c), np.cos(ang_c)], axis=-1)
    return jnp.asarray(pe, F32)


def _selection_constants():
    ch = np.arange(D_RWKV)
    ones512 = (ch[:, None] // HEAD == ch[None, :] // HEAD).astype(np.float32)
    t = np.arange(TM)
    same_chunk = t[:, None] // CH == t[None, :] // CH
    cum_f = same_chunk & (t[None, :] <= t[:, None])
    cum_b = same_chunk & (t[None, :] >= t[:, None])
    sel8 = np.arange(NCH)[:, None] == t[None, :] // CH
    col = np.arange(HC)
    hm = col[:, None] // CH == ch[None, :] // HEAD
    bm = col[:, None] // CH == col[None, :] // CH
    tt, jj = np.arange(CH)[:, None], col[None, :] % CH
    masks = np.stack([jj < tt, jj <= tt, jj > tt, jj >= tt])
    eye = jj == tt
    bf = lambda x: jnp.asarray(x, BF16)
    f32 = lambda x: jnp.asarray(x, F32)
    pad = np.zeros((2 * SUB - NCH, TM), bool)
    sums = [np.concatenate([cum, same_chunk, sel8, pad]) for cum in (cum_f, cum_b)]
    prep_consts = (bf(ones512), bf(np.stack(sums)), bf(hm), bf(bm),
                   f32(masks.reshape(2, 2, CH, HC)), f32(eye))
    scan_consts = (bf(hm), bf(ones512))
    return prep_consts, scan_consts


def kernel(x_prompt, x_sample, state_rwkv, c, c_ctx, ada_w, ada_b, norm1_g, w_in, tshift_mu, decay_w0, decay_lora_b, iclr_a0, iclr_lora_b, key_k, key_a, bonus_r_k, gate_lora_b, gn_g, gn_b, conv_dw_w, conv_dw_b, conv_ln_g, conv_ln_b, w_out, norm2_g, router_group_w, router_group_b, router_expert_w, router_expert_b, expert_w_gate, expert_w_up, expert_w_down, final_norm_g):
    assert x_prompt.shape == (N_CTX_SEQ, T_CTX, D) and x_sample.shape == (N_LAT_SEQ, T_LAT, D)
    assert ada_w.shape[0] == 1, "one trunk layer"
    prep_consts, scan_consts = _selection_constants()
    ones512 = prep_consts[0]
    xp = x_prompt.reshape(N_CTX_SEQ * T_CTX, D)
    xs = x_sample.reshape(N_LAT_SEQ * T_LAT, D)
    pe = _pos_embed(T_LAT // GRID_W)

    cond8 = jnp.concatenate([c_ctx[None, :], c, jnp.zeros((8 - 1 - N_LAT_SEQ, D), F32)], axis=0)
    mod3 = _adaln(cond8, ada_w[0], ada_b[0][None, :]).reshape(8, 1, N_MOD * D)

    zs, zg, zc = _inproj(xp, xs, pe, mod3, norm1_g, w_in[0].astype(BF16))

    zero = jnp.zeros((2, LORA, D_RWKV), F32)
    lora2 = jnp.concatenate([jnp.concatenate([decay_lora_b[0], zero], axis=2),
                             jnp.concatenate([zero, iclr_lora_b[0]], axis=2)], axis=1)
    vec = lambda p: p.reshape(2, 1, -1)
    prep_out = _prep(zs, vec(tshift_mu[0]), vec(decay_w0[0]), vec(iclr_a0[0]), lora2,
                     vec(key_k[0]), vec(key_a[0]), vec(bonus_r_k[0]), *prep_consts)
    bonus = prep_out[10]

    s0_lat = state_rwkv[:, 0].transpose(1, 0, 4, 2, 3).reshape(2, N_LAT_SEQ, HEAD, D_RWKV)
    yf, yb, s_fin = _scan(prep_out, s0_lat, scan_consts)

    yconv = _conv(zc, jnp.concatenate([conv_dw_w[0], jnp.zeros((1, D_CONV), F32)], axis=0),
                  conv_dw_b, conv_ln_g, conv_ln_b)

    router_w = jnp.concatenate([router_group_w[0], router_expert_w[0],
                                jnp.zeros((D, ROUTE_LANES - N_GROUPS - N_EXPERTS), F32)], axis=1)
    router_b = jnp.concatenate([router_group_b[0], router_expert_b[0],
                                jnp.zeros((ROUTE_LANES - N_GROUPS - N_EXPERTS,), F32)])[None, :]
    x1, h2, logits = _outproj(xp, xs, pe, mod3, yf, yb, bonus, zg, yconv,
                              gn_g, gn_b, gate_lora_b[0].astype(BF16), w_out[0].astype(BF16), norm2_g,
                              router_w, router_b, ones512)

    tok = np.arange(TM_MOE)
    triu = jnp.asarray(tok[:, None] < tok[None, :], BF16)
    comb, drow, dcol, seg_start, seg_ntile = _plan(logits, triu)
    seg_start = seg_start[:, :N_GROUPS, 0].reshape(-1)
    seg_ntile = seg_ntile[:, :N_GROUPS, 0].reshape(-1)
    moe = _moe(h2, comb, drow, dcol, seg_start, seg_ntile,
               expert_w_gate[0], expert_w_up[0], expert_w_down[0])
    fg = final_norm_g[None, :]
    y_prompt = _final(x1, moe, mod3, fg, 0, N_CTX_TILES, lambda i: i, "final_norm_ctx", merge=2)
    y_sample = _final(x1, moe, mod3, fg, N_CTX_TILES, N_LAT_TILES, _xs_block, "final_norm_lat")
    y_prompt = y_prompt.reshape(N_CTX_SEQ, T_CTX, D)
    y_sample = y_sample.reshape(N_LAT_SEQ, T_LAT, D)
    new_state = s_fin[:, None].astype(state_rwkv.dtype)
    return (y_prompt, y_sample, new_state)
```
